```python
import math
import jax, jax.numpy as jnp
from jax import lax
import numpy as np

D_MODEL = 2048
BATCH = 4
SEQ = 2048
DEPTH = 1
DEC_BATCH = 128
DEC_SEQ = 1
PAST_LEN = 16384
PAGE_SIZE = 128

W_LRU = D_MODEL // 2
LRU_HEADS = 8
LRU_HEAD_DIM = W_LRU // LRU_HEADS
CONV_W = 4
LRU_C = 8.0
W_SSM = D_MODEL // 2
SSM_GROUP = 16
SSM_GROUPS = W_SSM // SSM_GROUP
SSM_STATE = 64
D_FF = 4 * D_MODEL
N_MOD = 6
EPS = 1e-6

kernel_name = "hawk_s5_gated_hybrid_step"

F32 = jnp.float32


def rmsnorm(x, g):
    xf = x.astype(F32)
    y = xf * lax.rsqrt(jnp.mean(xf * xf, axis=-1, keepdims=True) + EPS)
    return (y * g.astype(F32)).astype(x.dtype)


def causal_conv(u, buf, w, b):
    L = u.shape[1]
    up = jnp.concatenate([buf.astype(u.dtype), u], axis=1)
    out = b + up[:, 0:L] * w[0]
    for k in range(1, CONV_W):
        out = out + up[:, k:k + L] * w[k]
    return out, up[:, -(CONV_W - 1):]


def _lin_combine(e1, e2):
    a1, b1 = e1
    a2, b2 = e2
    return a2 * a1, a2 * b1 + b2


def _cplx_combine(e1, e2):
    a1r, a1i, b1r, b1i = e1
    a2r, a2i, b2r, b2i = e2
    return (a2r * a1r - a2i * a1i,
            a2r * a1i + a2i * a1r,
            a2r * b1r - a2i * b1i + b2r,
            a2r * b1i + a2i * b1r + b2i)


def rg_lru(u, h0, w_a, b_a, w_x, b_x, lam):
    B, L, W = u.shape
    uf = u.astype(F32)
    uh = uf.reshape(B, L, LRU_HEADS, LRU_HEAD_DIM)
    r = jax.nn.sigmoid(jnp.einsum('blhi,hij->blhj', uh, w_a.astype(F32)).reshape(B, L, W) + b_a.astype(F32))
    i = jax.nn.sigmoid(jnp.einsum('blhi,hij->blhj', uh, w_x.astype(F32)).reshape(B, L, W) + b_x.astype(F32))
    log_a = -LRU_C * r * jax.nn.softplus(-lam.astype(F32))
    a = jnp.exp(log_a)
    mult = jnp.sqrt(-jnp.expm1(2.0 * log_a))
    bx = mult * i * uf
    a_cum, h = lax.associative_scan(_lin_combine, (a, bx), axis=1)
    h = h + a_cum * h0.astype(F32)[:, None]
    return h, h[:, -1]


def s5(u, h0_re, h0_im, a_re, a_im, log_dt, b_re, b_im, c_re, c_im, d):
    Bsz, L, _ = u.shape
    uf = u.astype(F32).reshape(Bsz, L, SSM_GROUPS, SSM_GROUP)
    ar, ai = a_re.astype(F32), a_im.astype(F32)
    dt = jnp.exp(log_dt.astype(F32))[:, None]
    mag = jnp.exp(dt * ar)
    ang = dt * ai
    abar_re, abar_im = mag * jnp.cos(ang), mag * jnp.sin(ang)
    den = ar * ar + ai * ai
    nr, ni = abar_re - 1.0, abar_im
    q_re = (nr * ar + ni * ai) / den
    q_im = (ni * ar - nr * ai) / den
    br, bi = b_re.astype(F32), b_im.astype(F32)
    bb_re = q_re[..., None] * br - q_im[..., None] * bi
    bb_im = q_re[..., None] * bi + q_im[..., None] * br
    bu_re = jnp.einsum('gpk,blgk->blgp', bb_re, uf)
    bu_im = jnp.einsum('gpk,blgk->blgp', bb_im, uf)
    shp = bu_re.shape
    a_r = jnp.broadcast_to(abar_re, shp)
    a_i = jnp.broadcast_to(abar_im, shp)
    cr, ci, hr, hi = lax.associative_scan(_cplx_combine, (a_r, a_i, bu_re, bu_im), axis=1)
    h0r = h0_re.astype(F32)[:, None]
    h0i = h0_im.astype(F32)[:, None]
    h_re = hr + cr * h0r - ci * h0i
    h_im = hi + cr * h0i + ci * h0r
    y = (jnp.einsum('gkp,blgp->blgk', c_re.astype(F32), h_re)
         - jnp.einsum('gkp,blgp->blgk', c_im.astype(F32), h_im))
    y = y.reshape(Bsz, L, W_SSM) + d.astype(F32) * u.astype(F32)
    return y, h_re[:, -1], h_im[:, -1]


def block(x, c, conv_buf, h_lru, h_re, h_im,
          w_ada, b_ada, g1, g2, w_in, conv_w, conv_b,
          w_rg_a, b_rg_a, w_rg_x, b_rg_x, lru_lambda, w_proj_a,
          a_re, a_im, log_dt, b_re, b_im, c_re, c_im, ssm_d, w_glu,
          w_out, w_up, w_down):
    mod = jax.nn.silu(c) @ w_ada + b_ada
    sh1, sc1, gt1, sh2, sc2, gt2 = jnp.split(mod[:, None, :], N_MOD, axis=-1)
    h = rmsnorm(x, g1) * (1.0 + sc1) + sh1
    proj = h @ w_in
    u_a, u_b, z_a, z_b = jnp.split(proj, [W_LRU, W_LRU + W_SSM, W_LRU + W_SSM + D_MODEL], axis=-1)
    u_conv, new_conv = causal_conv(u_a, conv_buf, conv_w, conv_b)
    h_seq, new_lru = rg_lru(u_conv, h_lru, w_rg_a, b_rg_a, w_rg_x, b_rg_x, lru_lambda)
    y_a = h_seq.astype(x.dtype) @ w_proj_a
    y_s, new_re, new_im = s5(u_b, h_re, h_im, a_re, a_im, log_dt, b_re, b_im, c_re, c_im, ssm_d)
    glu = jax.nn.gelu(y_s).astype(x.dtype) @ w_glu
    y_b = glu[..., :D_MODEL] * jax.nn.sigmoid(glu[..., D_MODEL:])
    merged = jax.nn.sigmoid(z_a) * y_a + jax.nn.sigmoid(z_b) * y_b
    x = x + gt1 * (merged @ w_out)
    h2 = rmsnorm(x, g2) * (1.0 + sc2) + sh2
    x = x + gt2 * (jnp.square(jax.nn.relu(h2 @ w_up)) @ w_down)
    return x, new_conv, new_lru, new_re, new_im


def setup_inputs(seed: int = 0) -> dict:
    key = jax.random.key(seed)
    ks = iter(jax.random.split(key, 40))
    nrm = lambda shape, s: jax.random.normal(next(ks), shape, F32) * s
    Lr = DEPTH
    n = jnp.arange(SSM_STATE, dtype=F32)
    a_pow = jax.random.uniform(next(ks), (Lr, W_LRU), F32, 0.9, 0.999)
    s = a_pow ** (1.0 / LRU_C)
    lru_lambda = jnp.log(s) - jnp.log1p(-s)
    log_dt = jax.random.uniform(next(ks), (Lr, SSM_GROUPS), F32, math.log(1e-3), math.log(1e-1))
    return {
        "x_prompt": nrm((BATCH, SEQ, D_MODEL), 1.0),
        "x_sample": nrm((DEC_BATCH, DEC_SEQ, D_MODEL), 1.0),
        "state_conv": nrm((Lr, DEC_BATCH, CONV_W - 1, W_LRU), 1.0),
        "state_lru": nrm((Lr, DEC_BATCH, W_LRU), 0.5),
        "state_ssm_re": nrm((Lr, DEC_BATCH, SSM_GROUPS, SSM_STATE), 0.5),
        "state_ssm_im": nrm((Lr, DEC_BATCH, SSM_GROUPS, SSM_STATE), 0.5),
        "c_prompt": nrm((BATCH, D_MODEL), 1.0),
        "c_sample": nrm((DEC_BATCH, D_MODEL), 1.0),
        "w_ada": nrm((Lr, D_MODEL, N_MOD * D_MODEL), D_MODEL ** -0.5),
        "b_ada": nrm((Lr, N_MOD * D_MODEL), 0.01),
        "g_norm1": 1.0 + nrm((Lr, D_MODEL), 0.01),
        "g_norm2": 1.0 + nrm((Lr, D_MODEL), 0.01),
        "w_in": nrm((Lr, D_MODEL, W_LRU + W_SSM + 2 * D_MODEL), D_MODEL ** -0.5),
        "conv_w": nrm((Lr, CONV_W, W_LRU), CONV_W ** -0.5),
        "conv_b": nrm((Lr, W_LRU), 0.01),
        "w_rg_a": nrm((Lr, LRU_HEADS, LRU_HEAD_DIM, LRU_HEAD_DIM), LRU_HEAD_DIM ** -0.5),
        "b_rg_a": nrm((Lr, W_LRU), 0.1),
        "w_rg_x": nrm((Lr, LRU_HEADS, LRU_HEAD_DIM, LRU_HEAD_DIM), LRU_HEAD_DIM ** -0.5),
        "b_rg_x": nrm((Lr, W_LRU), 0.1),
        "lru_lambda": lru_lambda,
        "w_proj_a": nrm((Lr, W_LRU, D_MODEL), W_LRU ** -0.5),
        "ssm_a_re": -0.5 + nrm((Lr, SSM_GROUPS, SSM_STATE), 0.01),
        "ssm_a_im": math.pi * n + nrm((Lr, SSM_GROUPS, SSM_STATE), 0.01),
        "ssm_log_dt": log_dt,
        "ssm_b_re": nrm((Lr, SSM_GROUPS, SSM_STATE, SSM_GROUP), (2 * SSM_GROUP) ** -0.5),
        "ssm_b_im": nrm((Lr, SSM_GROUPS, SSM_STATE, SSM_GROUP), (2 * SSM_GROUP) ** -0.5),
        "ssm_c_re": nrm((Lr, SSM_GROUPS, SSM_GROUP, SSM_STATE), (2 * SSM_STATE) ** -0.5),
        "ssm_c_im": nrm((Lr, SSM_GROUPS, SSM_GROUP, SSM_STATE), (2 * SSM_STATE) ** -0.5),
        "ssm_d": nrm((Lr, W_SSM), 1.0),
        "w_glu": nrm((Lr, W_SSM, 2 * D_MODEL), W_SSM ** -0.5),
        "w_out": nrm((Lr, D_MODEL, D_MODEL), D_MODEL ** -0.5),
        "w_up": nrm((Lr, D_MODEL, D_FF), D_MODEL ** -0.5),
        "w_down": nrm((Lr, D_FF, D_MODEL), D_FF ** -0.5),
        "g_final": 1.0 + nrm((D_MODEL,), 0.01),
    }


def reference(x_prompt, x_sample, state_conv, state_lru, state_ssm_re, state_ssm_im,
              c_prompt, c_sample, w_ada, b_ada, g_norm1, g_norm2, w_in, conv_w, conv_b,
              w_rg_a, b_rg_a, w_rg_x, b_rg_x, lru_lambda, w_proj_a,
              ssm_a_re, ssm_a_im, ssm_log_dt, ssm_b_re, ssm_b_im, ssm_c_re, ssm_c_im, ssm_d,
              w_glu, w_out, w_up, w_down, g_final):
    nb = x_prompt.shape[0]
    xp, xs = x_prompt, x_sample
    pc, pl, pr, pi_ = [], [], [], []
    sc, sl, sr, si = [], [], [], []
    for l in range(DEPTH):
        w = (w_ada[l], b_ada[l], g_norm1[l], g_norm2[l], w_in[l], conv_w[l], conv_b[l],
             w_rg_a[l], b_rg_a[l], w_rg_x[l], b_rg_x[l], lru_lambda[l], w_proj_a[l],
             ssm_a_re[l], ssm_a_im[l], ssm_log_dt[l], ssm_b_re[l], ssm_b_im[l],
             ssm_c_re[l], ssm_c_im[l], ssm_d[l], w_glu[l], w_out[l], w_up[l], w_down[l])
        z_conv = jnp.zeros((nb, CONV_W - 1, W_LRU), xp.dtype)
        z_lru = jnp.zeros((nb, W_LRU), F32)
        z_ssm = jnp.zeros((nb, SSM_GROUPS, SSM_STATE), F32)
        xp, c1, l1, r1, i1 = block(xp, c_prompt, z_conv, z_lru, z_ssm, z_ssm, *w)
        xs, c2, l2, r2, i2 = block(xs, c_sample, state_conv[l], state_lru[l],
                                   state_ssm_re[l], state_ssm_im[l], *w)
        pc.append(c1); pl.append(l1); pr.append(r1); pi_.append(i1)
        sc.append(c2); sl.append(l2); sr.append(r2); si.append(i2)
    y_prompt = rmsnorm(xp, g_final)
    y_sample = rmsnorm(xs, g_final)
    return (y_prompt, y_sample,
            jnp.stack(pc), jnp.stack(pl), jnp.stack(pr), jnp.stack(pi_),
            jnp.stack(sc), jnp.stack(sl), jnp.stack(sr), jnp.stack(si))
```

```python
import functools

import jax
import jax.numpy as jnp
from jax import lax
from jax.experimental import pallas as pl
from jax.experimental.pallas import tpu as pltpu

F32 = jnp.float32
BF16 = jnp.bfloat16

LANES = 128
SUBLANES = 8
VMEM_PHYSICAL_BYTES = 64 * 1024 * 1024
VMEM_LIMIT_CAP_BYTES = VMEM_PHYSICAL_BYTES - 6 * 1024 * 1024

LRU_C = 8.0
EPS = 1e-6
N_MOD = 6
SSM_GROUP = 16
SSM_STATE = 64
GROUPS_PER_BLOCK = LANES // SSM_GROUP
STATES_PER_BLOCK = GROUPS_PER_BLOCK * SSM_STATE


def _vmem_limit(nbytes):
    return int(min(VMEM_LIMIT_CAP_BYTES, max(32 * 1024 * 1024, nbytes * 3 // 2)))


ROW_CHUNK = 256


def _rows(ref, rows=None):
    if len(ref.shape) == 3:
        return ref[0]
    return ref[...] if rows is None else ref[rows, :]


def _for_row_chunks(n_rows, fn):
    rc = min(n_rows, ROW_CHUNK)
    assert n_rows % rc == 0
    if n_rows == rc:
        fn(slice(0, rc))
        return

    def body(i, carry):
        fn(pl.ds(pl.multiple_of(i * rc, rc), rc))
        return carry

    lax.fori_loop(0, n_rows // rc, body, 0)


def _dot(a, b):
    return jnp.dot(a, b, preferred_element_type=F32)


def _rmsnorm(x, g):
    ms = jnp.mean(x * x, axis=-1, keepdims=True)
    return (x * lax.rsqrt(ms + EPS)) * g


def _mod_kernel(c_ref, w_ref, b_ref, o_ref):
    c = c_ref[...]
    cs = (c * jax.nn.sigmoid(c)).astype(BF16)
    o_ref[...] = _dot(cs, w_ref[...].astype(BF16)) + b_ref[...]


def _mod_call(c_all, w_ada, b_ada, tn=1024):
    n_rows, d = c_all.shape
    n = w_ada.shape[1]
    est = 2 * (d * tn * 4) + 2 * n_rows * tn * 4 + n_rows * d * 4 * 2 + d * tn * 2
    return pl.pallas_call(
        _mod_kernel,
        grid=(n // tn,),
        in_specs=[pl.BlockSpec((n_rows, d), lambda j: (0, 0)),
                  pl.BlockSpec((d, tn), lambda j: (0, j)),
                  pl.BlockSpec((1, tn), lambda j: (0, j))],
        out_specs=pl.BlockSpec((n_rows, tn), lambda j: (0, j)),
        out_shape=jax.ShapeDtypeStruct((n_rows, n), F32),
        compiler_params=pltpu.CompilerParams(dimension_semantics=("parallel",),
                                             vmem_limit_bytes=_vmem_limit(est)),
        name="mod",
    )(c_all, w_ada, b_ada)


def _mod_spec(mod, piece, d, tiles_per_batch, ngrid):
    if mod.ndim == 3:
        if ngrid == 1:
            return pl.BlockSpec((1, 1, d), lambda i: (i // tiles_per_batch, 0, piece))
        return pl.BlockSpec((1, 1, d), lambda i, j: (i // tiles_per_batch, 0, piece))
    rows = mod.shape[0]
    if ngrid == 1:
        return pl.BlockSpec((rows, d), lambda i: (0, piece))
    return pl.BlockSpec((rows, d), lambda i, j: (0, piece))


def _inproj_kernel(x_ref, sc_ref, sh_ref, g_ref, w_ref, u_ref, sz_ref, h_scr, *, n_u_tiles):
    j = pl.program_id(1)

    @pl.when(j == 0)
    def _():
        def chunk(rows):
            h = _rmsnorm(x_ref[rows, :], g_ref[...])
            h_scr[rows, :] = (h * (1.0 + _rows(sc_ref, rows)) + _rows(sh_ref, rows)).astype(BF16)

        _for_row_chunks(x_ref.shape[0], chunk)

    r = _dot(h_scr[...], w_ref[...].astype(BF16))

    @pl.when(j < n_u_tiles)
    def _():
        u_ref[...] = r

    @pl.when(j >= n_u_tiles)
    def _():
        sz_ref[...] = jax.nn.sigmoid(r).astype(BF16)


def _inproj_call(x2d, mod, g1, w_in, *, tm, tiles_per_batch, w_mix, tn=512):
    m, d = x2d.shape
    n = w_in.shape[1]
    n_u = 2 * w_mix
    n_u_tiles = n_u // tn
    est = 2 * tm * d * 4 + 2 * d * tn * 4 + 2 * tm * tn * 4 + 2 * tm * tn * 2 + tm * d * 2 + d * tn * 2 + tm * tn * 4
    return pl.pallas_call(
        functools.partial(_inproj_kernel, n_u_tiles=n_u_tiles),
        grid=(m // tm, n // tn),
        in_specs=[pl.BlockSpec((tm, d), lambda i, j: (i, 0)),
                  _mod_spec(mod, 1, d, tiles_per_batch, 2),
                  _mod_spec(mod, 0, d, tiles_per_batch, 2),
                  pl.BlockSpec((1, d), lambda i, j: (0, 0)),
                  pl.BlockSpec((d, tn), lambda i, j: (0, j))],
        out_specs=[pl.BlockSpec((tm, tn), lambda i, j: (i, jnp.minimum(j, n_u_tiles - 1))),
                   pl.BlockSpec((tm, tn), lambda i, j: (i, jnp.maximum(j - n_u_tiles, 0)))],
        out_shape=[jax.ShapeDtypeStruct((m, n_u), F32),
                   jax.ShapeDtypeStruct((m, n - n_u), BF16)],
        scratch_shapes=[pltpu.VMEM((tm, d), BF16)],
        compiler_params=pltpu.CompilerParams(dimension_semantics=("parallel", "arbitrary"),
                                             vmem_limit_bytes=_vmem_limit(est)),
        name="inproj",
    )(x2d, mod, mod, g1, w_in)


def _segments_to_sublanes(slab_ref, dst_ref, seg):
    nblk = slab_ref.shape[0]

    def body(tau, carry):
        r = pl.multiple_of(tau * SUBLANES, SUBLANES)
        for c in range(nblk):
            dst_ref[pl.ds(r, SUBLANES), c * LANES:(c + 1) * LANES] = (
                slab_ref[c, pl.ds(tau, SUBLANES, stride=seg), :])
        return carry

    lax.fori_loop(0, seg, body, 0)


def _sublanes_to_segments(src_ref, slab_ref, seg):
    nblk = slab_ref.shape[0]

    def body(tau, carry):
        r = pl.multiple_of(tau * SUBLANES, SUBLANES)
        for c in range(nblk):
            slab_ref[c, pl.ds(tau, SUBLANES, stride=seg), :] = (
                src_ref[pl.ds(r, SUBLANES), c * LANES:(c + 1) * LANES])
        return carry

    lax.fori_loop(0, seg, body, 0)


def _lru_gate_block(uc, g, ba, bx, sp):
    r = jax.nn.sigmoid(g[:, :LANES] + ba)
    i = jax.nn.sigmoid(g[:, LANES:] + bx)
    log_a = (-LRU_C * r) * sp
    a = jnp.exp(log_a)
    mult = jnp.sqrt(1.0 - jnp.exp(2.0 * log_a))
    return a, (mult * i) * uc


def _lru_kernel(u_ref, cw_ref, cb_ref, wg_ref, ba_ref, bx_ref, lam_ref,
                hs_ref, nconv_ref, nlru_ref,
                ext, slab, ucp, a_s, b_s, carry, e_s, p_s, i_s):
    t = pl.program_id(1)
    nt = pl.num_programs(1)
    tl, w = ucp.shape
    seg = tl // SUBLANES
    nblk = w // LANES
    hist = SUBLANES
    kw = cw_ref.shape[0]

    @pl.when(t == 0)
    def _():
        ext[0:hist, :] = jnp.zeros((hist, w), F32)
        carry[...] = jnp.zeros_like(carry)

    ext[hist:hist + tl, :] = u_ref[0]

    for c in range(nblk):
        cols = slice(c * LANES, (c + 1) * LANES)
        acc = cb_ref[:, cols] + ext[hist - kw + 1:hist - kw + 1 + tl, cols] * cw_ref[0:1, cols]
        for k in range(1, kw):
            acc = acc + ext[hist - kw + 1 + k:hist - kw + 1 + k + tl, cols] * cw_ref[k:k + 1, cols]
        slab[c] = acc

    tail = ext[hist + tl - (kw - 1):hist + tl, :]
    ext[hist - (kw - 1):hist, :] = tail

    @pl.when(t == nt - 1)
    def _():
        nconv_ref[0] = tail

    _segments_to_sublanes(slab, ucp, seg)

    sp = jax.nn.softplus(-lam_ref[...])
    for h in range(nblk):
        cols = slice(h * LANES, (h + 1) * LANES)
        uc = ucp[:, cols]
        g = _dot(uc.astype(BF16), wg_ref[h])
        a, b = _lru_gate_block(uc, g, ba_ref[:, cols], bx_ref[:, cols], sp[:, cols])
        a_s[:, cols] = a
        b_s[:, cols] = b

    def p1(tau, c):
        p, e = c
        r = pl.multiple_of(tau * SUBLANES, SUBLANES)
        at = a_s[pl.ds(r, SUBLANES), :]
        return at * p, at * e + b_s[pl.ds(r, SUBLANES), :]

    p_end, e_end = lax.fori_loop(0, seg, p1, (jnp.ones((SUBLANES, w), F32), jnp.zeros((SUBLANES, w), F32)))
    p_s[...] = p_end
    e_s[...] = e_end

    row = carry[...]
    for j in range(SUBLANES):
        i_s[j:j + 1, :] = row
        row = p_s[j:j + 1, :] * row + e_s[j:j + 1, :]
    carry[...] = row

    def p2(tau, hcur):
        r = pl.multiple_of(tau * SUBLANES, SUBLANES)
        hcur = a_s[pl.ds(r, SUBLANES), :] * hcur + b_s[pl.ds(r, SUBLANES), :]
        b_s[pl.ds(r, SUBLANES), :] = hcur
        return hcur

    lax.fori_loop(0, seg, p2, i_s[...])

    _sublanes_to_segments(b_s, slab, seg)
    for c in range(nblk):
        hs_ref[0, :, c * LANES:(c + 1) * LANES] = slab[c].astype(BF16)

    @pl.when(t == nt - 1)
    def _():
        nlru_ref[0] = row


def _lru_call(u3d, cw, cb, wg, ba, bx, lam, *, tl):
    bsz, seq, _ = u3d.shape
    w = cw.shape[1]
    kw = cw.shape[0]
    nblk = w // LANES
    est = (2 * tl * w * 4 + 2 * tl * w * 2 + (tl + SUBLANES) * w * 4 + 4 * tl * w * 4
           + nblk * LANES * 2 * LANES * 2 * 2)
    const2 = lambda b, t: (0, 0)
    return pl.pallas_call(
        _lru_kernel,
        grid=(bsz, seq // tl),
        in_specs=[pl.BlockSpec((1, tl, w), lambda b, t: (b, t, 0)),
                  pl.BlockSpec((kw, w), const2),
                  pl.BlockSpec((1, w), const2),
                  pl.BlockSpec((nblk, LANES, 2 * LANES), lambda b, t: (0, 0, 0)),
                  pl.BlockSpec((1, w), const2),
                  pl.BlockSpec((1, w), const2),
                  pl.BlockSpec((1, w), const2)],
        out_specs=[pl.BlockSpec((1, tl, w), lambda b, t: (b, t, 0)),
                   pl.BlockSpec((1, kw - 1, w), lambda b, t: (b, 0, 0)),
                   pl.BlockSpec((1, 1, w), lambda b, t: (b, 0, 0))],
        out_shape=[jax.ShapeDtypeStruct((bsz, seq, w), BF16),
                   jax.ShapeDtypeStruct((bsz, kw - 1, w), F32),
                   jax.ShapeDtypeStruct((bsz, 1, w), F32)],
        scratch_shapes=[pltpu.VMEM((tl + SUBLANES, w), F32),
                        pltpu.VMEM((nblk, tl, LANES), F32),
                        pltpu.VMEM((tl, w), F32),
                        pltpu.VMEM((tl, w), F32),
                        pltpu.VMEM((tl, w), F32),
                        pltpu.VMEM((1, w), F32),
                        pltpu.VMEM((SUBLANES, w), F32),
                        pltpu.VMEM((SUBLANES, w), F32),
                        pltpu.VMEM((SUBLANES, w), F32)],
        compiler_params=pltpu.CompilerParams(dimension_semantics=("parallel", "arbitrary"),
                                             vmem_limit_bytes=_vmem_limit(est)),
        name="lru_scan",
    )(u3d, cw, cb, wg, ba, bx, lam)


def _lru_step_kernel(u_ref, buf_ref, h0_ref, cw_ref, cb_ref, wg_ref, ba_ref, bx_ref, lam_ref,
                     hs_ref, nconv_ref, nlru_ref):
    w = cw_ref.shape[1]
    kw = cw_ref.shape[0]
    nblk = w // LANES
    u = u_ref[...]
    uc = cb_ref[...] + buf_ref[0] * cw_ref[0:1, :]
    for k in range(1, kw - 1):
        uc = uc + buf_ref[k] * cw_ref[k:k + 1, :]
    uc = uc + u * cw_ref[kw - 1:kw, :]
    for k in range(kw - 2):
        nconv_ref[k] = buf_ref[k + 1]
    nconv_ref[kw - 2] = u
    sp = jax.nn.softplus(-lam_ref[...])
    for h in range(nblk):
        cols = slice(h * LANES, (h + 1) * LANES)
        uch = uc[:, cols]
        g = _dot(uch.astype(BF16), wg_ref[h])
        a, b = _lru_gate_block(uch, g, ba_ref[:, cols], bx_ref[:, cols], sp[:, cols])
        hn = a * h0_ref[:, cols] + b
        nlru_ref[:, cols] = hn
        hs_ref[:, cols] = hn.astype(BF16)


def _lru_step_call(u2d, buf, h0, cw, cb, wg, ba, bx, lam):
    n = h0.shape[0]
    w = cw.shape[1]
    kw = cw.shape[0]
    nblk = w // LANES
    c2 = lambda i: (0, 0)
    c3 = lambda i: (0, 0, 0)
    return pl.pallas_call(
        _lru_step_kernel,
        grid=(1,),
        in_specs=[pl.BlockSpec((n, w), c2),
                  pl.BlockSpec((kw - 1, n, w), c3),
                  pl.BlockSpec((n, w), c2),
                  pl.BlockSpec((kw, w), c2),
                  pl.BlockSpec((1, w), c2),
                  pl.BlockSpec((nblk, LANES, 2 * LANES), c3),
                  pl.BlockSpec((1, w), c2),
                  pl.BlockSpec((1, w), c2),
                  pl.BlockSpec((1, w), c2)],
        out_specs=[pl.BlockSpec((n, w), c2),
                   pl.BlockSpec((kw - 1, n, w), c3),
                   pl.BlockSpec((n, w), c2)],
        out_shape=[jax.ShapeDtypeStruct((n, w), BF16),
                   jax.ShapeDtypeStruct((kw - 1, n, w), F32),
                   jax.ShapeDtypeStruct((n, w), F32)],
        name="lru_step",
    )(u2d, buf, h0, cw, cb, wg, ba, bx, lam)


def _s5_prep_kernel(arx_ref, aix_ref, dt_ref, br_ref, bi_ref, ar_ref, ai_ref,
                    bbr_ref, bbi_ref, abr_ref, abi_ref, asr_ref, asi_ref, *, log2_seg):
    dt = jnp.exp(dt_ref[...])

    def abar(ar, ai):
        mag = jnp.exp(dt * ar)
        ang = dt * ai
        return mag * jnp.cos(ang), mag * jnp.sin(ang)

    ar, ai = arx_ref[...], aix_ref[...]
    abr, abi = abar(ar, ai)
    den = ar * ar + ai * ai
    nr, ni = abr - 1.0, abi
    q_re = (nr * ar + ni * ai) / den
    q_im = (ni * ar - nr * ai) / den
    br, bi = br_ref[...], bi_ref[...]
    bbr_ref[...] = q_re * br - q_im * bi
    bbi_ref[...] = q_re * bi + q_im * br

    pr, pi = abar(ar_ref[...], ai_ref[...])
    abr_ref[...] = pr
    abi_ref[...] = pi
    for _ in range(log2_seg):
        pr, pi = pr * pr - pi * pi, 2.0 * (pr * pi)
    asr_ref[...] = pr
    asi_ref[...] = pi


def _s5_prep_call(a_re, a_im, log_dt, b_re, b_im, seg):
    g, p, k = b_re.shape
    log2_seg = seg.bit_length() - 1
    assert 1 << log2_seg == seg
    arx = jnp.repeat(a_re, k, axis=1)
    aix = jnp.repeat(a_im, k, axis=1)
    outs = pl.pallas_call(
        functools.partial(_s5_prep_kernel, log2_seg=log2_seg),
        out_shape=[jax.ShapeDtypeStruct((g, p * k), F32), jax.ShapeDtypeStruct((g, p * k), F32),
                   jax.ShapeDtypeStruct((g, p), F32), jax.ShapeDtypeStruct((g, p), F32),
                   jax.ShapeDtypeStruct((g, p), F32), jax.ShapeDtypeStruct((g, p), F32)],
        name="s5_prep",
    )(arx, aix, log_dt.reshape(g, 1), b_re.reshape(g, p * k), b_im.reshape(g, p * k), a_re, a_im)
    bbr, bbi, abr, abi, asr, asi = outs
    return bbr.reshape(g, p, k), bbi.reshape(g, p, k), abr, abi, asr, asi


def _s5_block_weights(bb_re, bb_im, c_re, c_im):
    g, p, k = bb_re.shape
    gb = GROUPS_PER_BLOCK
    nb = g // gb
    eye = jnp.eye(gb, dtype=F32)

    def in_blocks(bb):
        t = bb.reshape(nb, gb, p, k).transpose(0, 1, 3, 2)
        return jnp.einsum('cgkp,gh->cgkhp', t, eye).reshape(nb, gb * k, gb * p)

    def out_blocks(cc):
        t = cc.reshape(nb, gb, k, p)
        return jnp.einsum('cgkp,hg->chpgk', t, eye).reshape(nb, gb * p, gb * k)

    wb = jnp.concatenate([in_blocks(bb_re), in_blocks(bb_im)], axis=2).astype(BF16)
    wc = jnp.concatenate([out_blocks(c_re), -out_blocks(c_im)], axis=1).astype(BF16)
    return wb, wc


def _cplx_step(ar, ai, hr, hi, br, bi):
    return ar * hr - ai * hi + br, ar * hi + ai * hr + bi


def _s5_kernel(u_ref, wb_ref, wc_ref, are_ref, aim_ref, asr_ref, asi_ref, d_ref,
               gy_ref, nre_ref, nim_ref,
               slab, up, bre, bim, e_re, e_im, i_re, i_im, c_re, c_im, *, lane_chunk):
    t = pl.program_id(1)
    nt = pl.num_programs(1)
    tl, w = up.shape
    n_state = bre.shape[1]
    seg = tl // SUBLANES
    nblk = w // LANES
    spb = STATES_PER_BLOCK

    @pl.when(t == 0)
    def _():
        c_re[...] = jnp.zeros_like(c_re)
        c_im[...] = jnp.zeros_like(c_im)

    for c in range(nblk):
        slab[c] = u_ref[0, :, c * LANES:(c + 1) * LANES]
    _segments_to_sublanes(slab, up, seg)

    for c in range(nblk):
        r = _dot(up[:, c * LANES:(c + 1) * LANES].astype(BF16), wb_ref[c])
        bre[:, c * spb:(c + 1) * spb] = r[:, :spb]
        bim[:, c * spb:(c + 1) * spb] = r[:, spb:]

    n_chunks = n_state // lane_chunk

    for k in range(n_chunks):
        cols = slice(k * lane_chunk, (k + 1) * lane_chunk)
        ar = jnp.broadcast_to(are_ref[:, cols], (SUBLANES, lane_chunk))
        ai = jnp.broadcast_to(aim_ref[:, cols], (SUBLANES, lane_chunk))

        def p1(tau, hc, cols=cols, ar=ar, ai=ai):
            r = pl.multiple_of(tau * SUBLANES, SUBLANES)
            return _cplx_step(ar, ai, hc[0], hc[1], bre[pl.ds(r, SUBLANES), cols], bim[pl.ds(r, SUBLANES), cols])

        z = jnp.zeros((SUBLANES, lane_chunk), F32)
        er, ei = lax.fori_loop(0, seg, p1, (z, z))
        e_re[:, cols] = er
        e_im[:, cols] = ei

    rr, ri = c_re[...], c_im[...]
    asr, asi = asr_ref[...], asi_ref[...]
    for j in range(SUBLANES):
        i_re[j:j + 1, :] = rr
        i_im[j:j + 1, :] = ri
        rr, ri = _cplx_step(asr, asi, rr, ri, e_re[j:j + 1, :], e_im[j:j + 1, :])
    c_re[...] = rr
    c_im[...] = ri

    for k in range(n_chunks):
        cols = slice(k * lane_chunk, (k + 1) * lane_chunk)
        ar = jnp.broadcast_to(are_ref[:, cols], (SUBLANES, lane_chunk))
        ai = jnp.broadcast_to(aim_ref[:, cols], (SUBLANES, lane_chunk))

        def p2(tau, hc, cols=cols, ar=ar, ai=ai):
            r = pl.multiple_of(tau * SUBLANES, SUBLANES)
            hr, hi = _cplx_step(ar, ai, hc[0], hc[1], bre[pl.ds(r, SUBLANES), cols], bim[pl.ds(r, SUBLANES), cols])
            bre[pl.ds(r, SUBLANES), cols] = hr
            bim[pl.ds(r, SUBLANES), cols] = hi
            return hr, hi

        lax.fori_loop(0, seg, p2, (i_re[:, cols], i_im[:, cols]))

    for c in range(nblk):
        cols = slice(c * LANES, (c + 1) * LANES)
        scols = slice(c * spb, (c + 1) * spb)
        y = (_dot(bre[:, scols].astype(BF16), wc_ref[c, :spb, :])
             + _dot(bim[:, scols].astype(BF16), wc_ref[c, spb:, :]))
        ys = y + d_ref[:, cols] * up[:, cols]
        up[:, cols] = jax.nn.gelu(ys)

    _sublanes_to_segments(up, slab, seg)
    for c in range(nblk):
        gy_ref[0, :, c * LANES:(c + 1) * LANES] = slab[c].astype(BF16)

    @pl.when(t == nt - 1)
    def _():
        nre_ref[0] = rr
        nim_ref[0] = ri


def _s5_call(u3d, wb, wc, abr, abi, asr, asi, d, *, tl, w, lane_chunk=512):
    bsz, seq, _ = u3d.shape
    nblk = w // LANES
    n_state = nblk * STATES_PER_BLOCK
    est = (2 * tl * w * 4 + 2 * tl * w * 2 + 2 * tl * w * 4 + 2 * tl * n_state * 4
           + 2 * (wb.size + wc.size) * 2 + tl * 2 * STATES_PER_BLOCK * 4)
    c2 = lambda b, t: (0, 0)
    c3 = lambda b, t: (0, 0, 0)
    return pl.pallas_call(
        functools.partial(_s5_kernel, lane_chunk=lane_chunk),
        grid=(bsz, seq // tl),
        in_specs=[pl.BlockSpec((1, tl, w), lambda b, t: (b, t, 1)),
                  pl.BlockSpec(wb.shape, c3),
                  pl.BlockSpec(wc.shape, c3),
                  pl.BlockSpec((1, n_state), c2),
                  pl.BlockSpec((1, n_state), c2),
                  pl.BlockSpec((1, n_state), c2),
                  pl.BlockSpec((1, n_state), c2),
                  pl.BlockSpec((1, w), c2)],
        out_specs=[pl.BlockSpec((1, tl, w), lambda b, t: (b, t, 0)),
                   pl.BlockSpec((1, 1, n_state), lambda b, t: (b, 0, 0)),
                   pl.BlockSpec((1, 1, n_state), lambda b, t: (b, 0, 0))],
        out_shape=[jax.ShapeDtypeStruct((bsz, seq, w), BF16),
                   jax.ShapeDtypeStruct((bsz, 1, n_state), F32),
                   jax.ShapeDtypeStruct((bsz, 1, n_state), F32)],
        scratch_shapes=[pltpu.VMEM((nblk, tl, LANES), F32),
                        pltpu.VMEM((tl, w), F32),
                        pltpu.VMEM((tl, n_state), F32),
                        pltpu.VMEM((tl, n_state), F32),
                        pltpu.VMEM((SUBLANES, n_state), F32),
                        pltpu.VMEM((SUBLANES, n_state), F32),
                        pltpu.VMEM((SUBLANES, n_state), F32),
                        pltpu.VMEM((SUBLANES, n_state), F32),
                        pltpu.VMEM((1, n_state), F32),
                        pltpu.VMEM((1, n_state), F32)],
        compiler_params=pltpu.CompilerParams(dimension_semantics=("parallel", "arbitrary"),
                                             vmem_limit_bytes=_vmem_limit(est)),
        name="s5_scan",
    )(u3d, wb, wc, abr, abi, asr, asi, d)


def _s5_step_kernel(u_ref, h0r_ref, h0i_ref, wb_ref, wc_ref, are_ref, aim_ref, d_ref,
                    gy_ref, nre_ref, nim_ref):
    w = u_ref.shape[1]
    nblk = w // LANES
    spb = STATES_PER_BLOCK
    for c in range(nblk):
        cols = slice(c * LANES, (c + 1) * LANES)
        scols = slice(c * spb, (c + 1) * spb)
        u = u_ref[:, cols]
        r = _dot(u.astype(BF16), wb_ref[c])
        hr, hi = _cplx_step(are_ref[:, scols], aim_ref[:, scols], h0r_ref[:, scols], h0i_ref[:, scols],
                            r[:, :spb], r[:, spb:])
        nre_ref[:, scols] = hr
        nim_ref[:, scols] = hi
        y = _dot(hr.astype(BF16), wc_ref[c, :spb, :]) + _dot(hi.astype(BF16), wc_ref[c, spb:, :])
        gy_ref[:, cols] = jax.nn.gelu(y + d_ref[:, cols] * u).astype(BF16)


def _s5_step_call(u2d, h0r, h0i, wb, wc, abr, abi, d, *, w):
    n, n_state = h0r.shape
    c2 = lambda i: (0, 0)
    c3 = lambda i: (0, 0, 0)
    return pl.pallas_call(
        _s5_step_kernel,
        grid=(1,),
        in_specs=[pl.BlockSpec((n, w), lambda i: (0, 1)),
                  pl.BlockSpec((n, n_state), c2),
                  pl.BlockSpec((n, n_state), c2),
                  pl.BlockSpec(wb.shape, c3),
                  pl.BlockSpec(wc.shape, c3),
                  pl.BlockSpec((1, n_state), c2),
                  pl.BlockSpec((1, n_state), c2),
                  pl.BlockSpec((1, w), c2)],
        out_specs=[pl.BlockSpec((n, w), c2),
                   pl.BlockSpec((n, n_state), c2),
                   pl.BlockSpec((n, n_state), c2)],
        out_shape=[jax.ShapeDtypeStruct((n, w), BF16),
                   jax.ShapeDtypeStruct((n, n_state), F32),
                   jax.ShapeDtypeStruct((n, n_state), F32)],
        name="s5_step",
    )(u2d, h0r, h0i, wb, wc, abr, abi, d)


def _merge_kernel(hs_ref, gy_ref, sz_ref, x_ref, gt_ref, wp_ref, wg_ref, wo_ref, o_ref):
    d = x_ref.shape[1]
    ya = _dot(hs_ref[...], wp_ref[...])
    merged = sz_ref[:, :d].astype(F32) * ya
    glu = _dot(gy_ref[...], wg_ref[...])
    yb = glu[:, :d] * jax.nn.sigmoid(glu[:, d:])
    merged = merged + sz_ref[:, d:].astype(F32) * yb
    o = _dot(merged.astype(BF16), wo_ref[...])
    o_ref[...] = x_ref[...] + _rows(gt_ref) * o


def _merge_call(hs, gy, sz, x2d, mod, wp, wg, wo, *, tm, tiles_per_batch):
    m, d = x2d.shape
    w = hs.shape[1]
    est = (2 * (2 * tm * w * 2 + tm * 2 * d * 2 + 2 * tm * d * 4) + (wp.size + wg.size + wo.size) * 2
           + tm * d * 4 * 5)
    one = pl.Buffered(1)
    return pl.pallas_call(
        _merge_kernel,
        grid=(m // tm,),
        in_specs=[pl.BlockSpec((tm, w), lambda i: (i, 0)),
                  pl.BlockSpec((tm, w), lambda i: (i, 0)),
                  pl.BlockSpec((tm, 2 * d), lambda i: (i, 0)),
                  pl.BlockSpec((tm, d), lambda i: (i, 0)),
                  _mod_spec(mod, 2, d, tiles_per_batch, 1),
                  pl.BlockSpec(wp.shape, lambda i: (0, 0), pipeline_mode=one),
                  pl.BlockSpec(wg.shape, lambda i: (0, 0), pipeline_mode=one),
                  pl.BlockSpec(wo.shape, lambda i: (0, 0), pipeline_mode=one)],
        out_specs=pl.BlockSpec((tm, d), lambda i: (i, 0)),
        out_shape=jax.ShapeDtypeStruct((m, d), F32),
        compiler_params=pltpu.CompilerParams(dimension_semantics=("parallel",),
                                             vmem_limit_bytes=_vmem_limit(est)),
        name="merge",
    )(hs, gy, sz, x2d, mod, wp, wg, wo)


def _mlp_kernel(x_ref, sc_ref, sh_ref, gt_ref, g2_ref, gf_ref, wu_ref, wd_ref, o_ref, h_scr, *, final_norm):
    j = pl.program_id(1)
    nj = pl.num_programs(1)

    n_rows = x_ref.shape[0]

    @pl.when(j == 0)
    def _():
        def chunk(rows):
            h = _rmsnorm(x_ref[rows, :], g2_ref[...])
            h_scr[rows, :] = (h * (1.0 + _rows(sc_ref, rows)) + _rows(sh_ref, rows)).astype(BF16)
            o_ref[rows, :] = jnp.zeros((h.shape[0], h.shape[1]), F32)

        _for_row_chunks(n_rows, chunk)

    wu = wu_ref[...].astype(BF16)
    wd = wd_ref[...].astype(BF16)

    def accumulate(rows):
        up = _dot(h_scr[rows, :], wu)
        act = jnp.square(jnp.maximum(up, 0.0)).astype(BF16)
        o_ref[rows, :] += _dot(act, wd)

    _for_row_chunks(n_rows, accumulate)

    @pl.when(j == nj - 1)
    def _():
        def chunk(rows):
            x2 = x_ref[rows, :] + _rows(gt_ref, rows) * o_ref[rows, :]
            o_ref[rows, :] = _rmsnorm(x2, gf_ref[...]) if final_norm else x2

        _for_row_chunks(n_rows, chunk)


def _mlp_call(x2d, mod, g2, gf, w_up, w_down, *, tm, tiles_per_batch, final_norm, tf=256):
    m, d = x2d.shape
    dff = w_up.shape[1]
    est = 4 * tm * d * 4 + tm * d * 2 + 4 * d * tf * 4 + 2 * d * tf * 2 + tm * tf * 6 + tm * d * 4
    return pl.pallas_call(
        functools.partial(_mlp_kernel, final_norm=final_norm),
        grid=(m // tm, dff // tf),
        in_specs=[pl.BlockSpec((tm, d), lambda i, j: (i, 0)),
                  _mod_spec(mod, 4, d, tiles_per_batch, 2),
                  _mod_spec(mod, 3, d, tiles_per_batch, 2),
                  _mod_spec(mod, 5, d, tiles_per_batch, 2),
                  pl.BlockSpec((1, d), lambda i, j: (0, 0)),
                  pl.BlockSpec((1, d), lambda i, j: (0, 0)),
                  pl.BlockSpec((d, tf), lambda i, j: (0, j)),
                  pl.BlockSpec((tf, d), lambda i, j: (j, 0))],
        out_specs=pl.BlockSpec((tm, d), lambda i, j: (i, 0)),
        out_shape=jax.ShapeDtypeStruct((m, d), F32),
        scratch_shapes=[pltpu.VMEM((tm, d), BF16)],
        compiler_params=pltpu.CompilerParams(dimension_semantics=("parallel", "arbitrary"),
                                             vmem_limit_bytes=_vmem_limit(est)),
        name="mlp",
    )(x2d, mod, mod, mod, g2, gf, w_up, w_down)


def _pick_tile(n, pref):
    t = min(n, pref)
    assert n % t == 0, (n, t)
    return t


def kernel(x_prompt, x_sample, state_conv, state_lru, state_ssm_re, state_ssm_im, c_prompt, c_sample, w_ada, b_ada, g_norm1, g_norm2, w_in, conv_w, conv_b, w_rg_a, b_rg_a, w_rg_x, b_rg_x, lru_lambda, w_proj_a, ssm_a_re, ssm_a_im, ssm_log_dt, ssm_b_re, ssm_b_im, ssm_c_re, ssm_c_im, ssm_d, w_glu, w_out, w_up, w_down, g_final):
    depth = w_ada.shape[0]
    nb, seq, d = x_prompt.shape
    ns = x_sample.shape[0]
    assert x_sample.shape[1] == 1
    w = conv_w.shape[2]
    n_state = ssm_a_re.shape[1] * ssm_a_re.shape[2]
    assert ssm_b_re.shape[2:] == (SSM_STATE, SSM_GROUP) and w % LANES == 0
    assert w_in.shape[2] == 2 * w + 2 * d

    tl = _pick_tile(seq, 512)
    seg = tl // SUBLANES
    tm_in = _pick_tile(seq, 1024)
    tm_mg = _pick_tile(seq, 256)
    tm_mlp = _pick_tile(seq, 1024)

    xp = x_prompt.reshape(nb * seq, d)
    xs = x_sample.reshape(ns, d)
    pad = (-nb) % SUBLANES
    c_all = jnp.concatenate([c_prompt, jnp.zeros((pad, d), F32), c_sample], axis=0)

    outs_p = [[] for _ in range(4)]
    outs_s = [[] for _ in range(4)]
    for l in range(depth):
        last = l == depth - 1
        mod = _mod_call(c_all, w_ada[l], b_ada[l].reshape(1, -1))
        mod_p = mod[:nb].reshape(nb, 1, N_MOD * d)
        mod_s = mod[nb + pad:]

        g1 = g_norm1[l].reshape(1, d)
        g2 = g_norm2[l].reshape(1, d)
        gf = g_final.reshape(1, d)
        cw, cb = conv_w[l], conv_b[l].reshape(1, w)
        wg = jnp.concatenate([w_rg_a[l], w_rg_x[l]], axis=2).astype(BF16)
        ba, bx, lam = b_rg_a[l].reshape(1, w), b_rg_x[l].reshape(1, w), lru_lambda[l].reshape(1, w)
        bbr, bbi, abr, abi, asr, asi = _s5_prep_call(ssm_a_re[l], ssm_a_im[l], ssm_log_dt[l],
                                                     ssm_b_re[l], ssm_b_im[l], seg)
        wb, wc = _s5_block_weights(bbr, bbi, ssm_c_re[l], ssm_c_im[l])
        abr, abi, asr, asi = (v.reshape(1, n_state) for v in (abr, abi, asr, asi))
        dskip = ssm_d[l].reshape(1, w)
        wp, wgl, wo = w_proj_a[l].astype(BF16), w_glu[l].astype(BF16), w_out[l].astype(BF16)

        u_p, sz_p = _inproj_call(xp, mod_p, g1, w_in[l], tm=tm_in, tiles_per_batch=seq // tm_in, w_mix=w)
        u3 = u_p.reshape(nb, seq, 2 * w)
        hs_p, nconv_p, nlru_p = _lru_call(u3, cw, cb, wg, ba, bx, lam, tl=tl)
        gy_p, nre_p, nim_p = _s5_call(u3, wb, wc, abr, abi, asr, asi, dskip, tl=tl, w=w)
        x1_p = _merge_call(hs_p.reshape(nb * seq, w), gy_p.reshape(nb * seq, w), sz_p, xp, mod_p, wp, wgl, wo,
                           tm=tm_mg, tiles_per_batch=seq // tm_mg)
        xp = _mlp_call(x1_p, mod_p, g2, gf, w_up[l], w_down[l], tm=tm_mlp, tiles_per_batch=seq // tm_mlp,
                       final_norm=last)

        u_s, sz_s = _inproj_call(xs, mod_s, g1, w_in[l], tm=ns, tiles_per_batch=1, w_mix=w)
        buf = jnp.transpose(state_conv[l], (1, 0, 2))
        hs_s, nconv_s, nlru_s = _lru_step_call(u_s, buf, state_lru[l], cw, cb, wg, ba, bx, lam)
        gy_s, nre_s, nim_s = _s5_step_call(u_s, state_ssm_re[l].reshape(ns, n_state),
                                           state_ssm_im[l].reshape(ns, n_state), wb, wc, abr, abi, dskip, w=w)
        x1_s = _merge_call(hs_s, gy_s, sz_s, xs, mod_s, wp, wgl, wo, tm=ns, tiles_per_batch=1)
        xs = _mlp_call(x1_s, mod_s, g2, gf, w_up[l], w_down[l], tm=ns, tiles_per_batch=1, final_norm=last)

        gshape = ssm_a_re.shape[1:]
        for acc, v in zip(outs_p, (nconv_p, nlru_p.reshape(nb, w),
                                   nre_p.reshape((nb,) + gshape), nim_p.reshape((nb,) + gshape))):
            acc.append(v)
        for acc, v in zip(outs_s, (jnp.transpose(nconv_s, (1, 0, 2)), nlru_s,
                                   nre_s.reshape((ns,) + gshape), nim_s.reshape((ns,) + gshape))):
            acc.append(v)

    y_prompt = xp.reshape(nb, seq, d)
    y_sample = xs.reshape(ns, 1, d)
    return (y_prompt, y_sample) + tuple(jnp.stack(a) for a in outs_p) + tuple(jnp.stack(a) for a in outs_s)
```

```python
import functools

import jax
import jax.numpy as jnp
from jax import lax
from jax.experimental import pallas as pl
from jax.experimental.pallas import tpu as pltpu

F32 = jnp.float32
BF16 = jnp.bfloat16

LANES = 128
SUBLANES = 8
VMEM_PHYSICAL_BYTES = 64 * 1024 * 1024
VMEM_LIMIT_CAP_BYTES = VMEM_PHYSICAL_BYTES - 6 * 1024 * 1024

LRU_C = 8.0
EPS = 1e-6
N_MOD = 6
SSM_GROUP = 16
SSM_STATE = 64
GROUPS_PER_BLOCK = LANES // SSM_GROUP
STATES_PER_BLOCK = GROUPS_PER_BLOCK * SSM_STATE


def _vmem_limit(nbytes):
    return int(min(VMEM_LIMIT_CAP_BYTES, max(32 * 1024 * 1024, nbytes * 3 // 2)))


ROW_CHUNK = 256
MLP_OUT_CHUNK = 512
INPROJ_CHUNK = 256


def _rows(ref, rows=None):
    if len(ref.shape) == 3:
        return ref[0]
    return ref[...] if rows is None else ref[rows, :]


def _for_row_chunks(n_rows, fn):
    rc = min(n_rows, ROW_CHUNK)
    assert n_rows % rc == 0
    if n_rows == rc:
        fn(slice(0, rc))
        return

    def body(i, carry):
        fn(pl.ds(pl.multiple_of(i * rc, rc), rc))
        return carry

    lax.fori_loop(0, n_rows // rc, body, 0)


def _dot(a, b):
    return jnp.dot(a, b, preferred_element_type=F32)


def _rmsnorm(x, g):
    ms = jnp.mean(x * x, axis=-1, keepdims=True)
    return (x * lax.rsqrt(ms + EPS)) * g


def _mod_kernel(c_ref, w_ref, b_ref, o_ref):
    c = c_ref[...]
    cs = (c * jax.nn.sigmoid(c)).astype(BF16)
    o_ref[...] = _dot(cs, w_ref[...].astype(BF16)) + b_ref[...]


def _mod_call(c_all, w_ada, b_ada, tn=1024):
    n_rows, d = c_all.shape
    n = w_ada.shape[1]
    est = 2 * (d * tn * 4) + 2 * n_rows * tn * 4 + n_rows * d * 4 * 2 + d * tn * 2
    return pl.pallas_call(
        _mod_kernel,
        grid=(n // tn,),
        in_specs=[pl.BlockSpec((n_rows, d), lambda j: (0, 0)),
                  pl.BlockSpec((d, tn), lambda j: (0, j)),
                  pl.BlockSpec((1, tn), lambda j: (0, j))],
        out_specs=pl.BlockSpec((n_rows, tn), lambda j: (0, j)),
        out_shape=jax.ShapeDtypeStruct((n_rows, n), F32),
        compiler_params=pltpu.CompilerParams(dimension_semantics=("parallel",),
                                             vmem_limit_bytes=_vmem_limit(est)),
        name="mod",
    )(c_all, w_ada, b_ada)


def _mod_spec(mod, piece, d, tiles_per_batch, ngrid):
    if mod.ndim == 3:
        if ngrid == 1:
            return pl.BlockSpec((1, 1, d), lambda i: (i // tiles_per_batch, 0, piece))
        return pl.BlockSpec((1, 1, d), lambda i, j: (i // tiles_per_batch, 0, piece))
    rows = mod.shape[0]
    if ngrid == 1:
        return pl.BlockSpec((rows, d), lambda i: (0, piece))
    return pl.BlockSpec((rows, d), lambda i, j: (0, piece))


def _inproj_kernel(x_ref, sc_ref, sh_ref, g_ref, w_ref, u_ref, sz_ref, h_scr, *, n_u_tiles):
    j = pl.program_id(1)

    @pl.when(j == 0)
    def _():
        def chunk(rows):
            h = _rmsnorm(x_ref[rows, :], g_ref[...])
            h_scr[rows, :] = (h * (1.0 + _rows(sc_ref, rows)) + _rows(sh_ref, rows)).astype(BF16)

        _for_row_chunks(x_ref.shape[0], chunk)

    tn = w_ref.shape[1]
    nc = min(tn, INPROJ_CHUNK)

    @pl.when(j < n_u_tiles)
    def _():
        u_ref[...] = _dot(h_scr[...], w_ref[...])

    @pl.when(j >= n_u_tiles)
    def _():
        for c in range(tn // nc):
            cols = slice(c * nc, (c + 1) * nc)
            sz_ref[:, cols] = jax.nn.sigmoid(_dot(h_scr[...], w_ref[:, cols])).astype(BF16)


def _inproj_call(x2d, mod, g1, w_in, *, tm, tiles_per_batch, w_mix, tn=1024):
    m, d = x2d.shape
    n = w_in.shape[1]
    n_u = 2 * w_mix
    n_u_tiles = n_u // tn
    est = (2 * tm * d * 4 + 2 * d * tn * 2 + 2 * tm * tn * 4 + 2 * tm * tn * 2 + tm * d * 2 + tm * tn * 4) * 2 // 3 + (4 << 20)
    return pl.pallas_call(
        functools.partial(_inproj_kernel, n_u_tiles=n_u_tiles),
        grid=(m // tm, n // tn),
        in_specs=[pl.BlockSpec((tm, d), lambda i, j: (i, 0)),
                  _mod_spec(mod, 1, d, tiles_per_batch, 2),
                  _mod_spec(mod, 0, d, tiles_per_batch, 2),
                  pl.BlockSpec((1, d), lambda i, j: (0, 0)),
                  pl.BlockSpec((d, tn), lambda i, j: (0, j))],
        out_specs=[pl.BlockSpec((tm, tn), lambda i, j: (i, jnp.minimum(j, n_u_tiles - 1))),
                   pl.BlockSpec((tm, tn), lambda i, j: (i, jnp.maximum(j - n_u_tiles, 0)))],
        out_shape=[jax.ShapeDtypeStruct((m, n_u), F32),
                   jax.ShapeDtypeStruct((m, n - n_u), BF16)],
        scratch_shapes=[pltpu.VMEM((tm, d), BF16)],
        compiler_params=pltpu.CompilerParams(dimension_semantics=("parallel", "arbitrary"),
                                             vmem_limit_bytes=_vmem_limit(est)),
        name="inproj",
    )(x2d, mod, mod, g1, w_in)


def _segments_to_sublanes(slab_ref, dst_ref, seg):
    nblk = slab_ref.shape[0]

    def body(tau, carry):
        r = pl.multiple_of(tau * SUBLANES, SUBLANES)
        for c in range(nblk):
            dst_ref[pl.ds(r, SUBLANES), c * LANES:(c + 1) * LANES] = (
                slab_ref[c, pl.ds(tau, SUBLANES, stride=seg), :])
        return carry

    lax.fori_loop(0, seg, body, 0)


def _sublanes_to_segments(src_ref, slab_ref, seg):
    nblk = slab_ref.shape[0]

    def body(tau, carry):
        r = pl.multiple_of(tau * SUBLANES, SUBLANES)
        for c in range(nblk):
            slab_ref[c, pl.ds(tau, SUBLANES, stride=seg), :] = (
                src_ref[pl.ds(r, SUBLANES), c * LANES:(c + 1) * LANES])
        return carry

    lax.fori_loop(0, seg, body, 0)


def _lru_gate_block(uc, g, ba, bx, sp):
    r = jax.nn.sigmoid(g[:, :LANES] + ba)
    i = jax.nn.sigmoid(g[:, LANES:] + bx)
    log_a = (-LRU_C * r) * sp
    a = jnp.exp(log_a)
    mult = jnp.sqrt(1.0 - jnp.exp(2.0 * log_a))
    return a, (mult * i) * uc


def _lru_kernel(u_ref, cw_ref, cb_ref, wg_ref, ba_ref, bx_ref, lam_ref,
                hs_ref, nconv_ref, nlru_ref,
                ext, slab, ucp, a_s, b_s, carry, e_s, p_s, i_s):
    t = pl.program_id(1)
    nt = pl.num_programs(1)
    tl, w = ucp.shape
    seg = tl // SUBLANES
    nblk = w // LANES
    hist = SUBLANES
    kw = cw_ref.shape[0]

    @pl.when(t == 0)
    def _():
        ext[0:hist, :] = jnp.zeros((hist, w), F32)
        carry[...] = jnp.zeros_like(carry)

    ext[hist:hist + tl, :] = u_ref[0]

    for c in range(nblk):
        cols = slice(c * LANES, (c + 1) * LANES)
        acc = cb_ref[:, cols] + ext[hist - kw + 1:hist - kw + 1 + tl, cols] * cw_ref[0:1, cols]
        for k in range(1, kw):
            acc = acc + ext[hist - kw + 1 + k:hist - kw + 1 + k + tl, cols] * cw_ref[k:k + 1, cols]
        slab[c] = acc

    tail = ext[hist + tl - (kw - 1):hist + tl, :]
    ext[hist - (kw - 1):hist, :] = tail

    @pl.when(t == nt - 1)
    def _():
        nconv_ref[0] = tail

    _segments_to_sublanes(slab, ucp, seg)

    sp = jax.nn.softplus(-lam_ref[...])
    for h in range(nblk):
        cols = slice(h * LANES, (h + 1) * LANES)
        uc = ucp[:, cols]
        g = _dot(uc.astype(BF16), wg_ref[h])
        a, b = _lru_gate_block(uc, g, ba_ref[:, cols], bx_ref[:, cols], sp[:, cols])
        a_s[:, cols] = a
        b_s[:, cols] = b

    def p1(tau, c):
        p, e = c
        r = pl.multiple_of(tau * SUBLANES, SUBLANES)
        at = a_s[pl.ds(r, SUBLANES), :]
        return at * p, at * e + b_s[pl.ds(r, SUBLANES), :]

    p_end, e_end = lax.fori_loop(0, seg, p1, (jnp.ones((SUBLANES, w), F32), jnp.zeros((SUBLANES, w), F32)))
    p_s[...] = p_end
    e_s[...] = e_end

    row = carry[...]
    for j in range(SUBLANES):
        i_s[j:j + 1, :] = row
        row = p_s[j:j + 1, :] * row + e_s[j:j + 1, :]
    carry[...] = row

    def p2(tau, hcur):
        r = pl.multiple_of(tau * SUBLANES, SUBLANES)
        hcur = a_s[pl.ds(r, SUBLANES), :] * hcur + b_s[pl.ds(r, SUBLANES), :]
        b_s[pl.ds(r, SUBLANES), :] = hcur
        return hcur

    lax.fori_loop(0, seg, p2, i_s[...])

    _sublanes_to_segments(b_s, slab, seg)
    for c in range(nblk):
        hs_ref[0, :, c * LANES:(c + 1) * LANES] = slab[c].astype(BF16)

    @pl.when(t == nt - 1)
    def _():
        nlru_ref[0] = row


def _lru_call(u3d, cw, cb, wg, ba, bx, lam, *, tl):
    bsz, seq, _ = u3d.shape
    w = cw.shape[1]
    kw = cw.shape[0]
    nblk = w // LANES
    est = (2 * tl * w * 4 + 2 * tl * w * 2 + (tl + SUBLANES) * w * 4 + 4 * tl * w * 4
           + nblk * LANES * 2 * LANES * 2 * 2)
    const2 = lambda b, t: (0, 0)
    return pl.pallas_call(
        _lru_kernel,
        grid=(bsz, seq // tl),
        in_specs=[pl.BlockSpec((1, tl, w), lambda b, t: (b, t, 0)),
                  pl.BlockSpec((kw, w), const2),
                  pl.BlockSpec((1, w), const2),
                  pl.BlockSpec((nblk, LANES, 2 * LANES), lambda b, t: (0, 0, 0)),
                  pl.BlockSpec((1, w), const2),
                  pl.BlockSpec((1, w), const2),
                  pl.BlockSpec((1, w), const2)],
        out_specs=[pl.BlockSpec((1, tl, w), lambda b, t: (b, t, 0)),
                   pl.BlockSpec((1, kw - 1, w), lambda b, t: (b, 0, 0)),
                   pl.BlockSpec((1, 1, w), lambda b, t: (b, 0, 0))],
        out_shape=[jax.ShapeDtypeStruct((bsz, seq, w), BF16),
                   jax.ShapeDtypeStruct((bsz, kw - 1, w), F32),
                   jax.ShapeDtypeStruct((bsz, 1, w), F32)],
        scratch_shapes=[pltpu.VMEM((tl + SUBLANES, w), F32),
                        pltpu.VMEM((nblk, tl, LANES), F32),
                        pltpu.VMEM((tl, w), F32),
                        pltpu.VMEM((tl, w), F32),
                        pltpu.VMEM((tl, w), F32),
                        pltpu.VMEM((1, w), F32),
                        pltpu.VMEM((SUBLANES, w), F32),
                        pltpu.VMEM((SUBLANES, w), F32),
                        pltpu.VMEM((SUBLANES, w), F32)],
        compiler_params=pltpu.CompilerParams(dimension_semantics=("parallel", "arbitrary"),
                                             vmem_limit_bytes=_vmem_limit(est)),
        name="lru_scan",
    )(u3d, cw, cb, wg, ba, bx, lam)


def _lru_step_kernel(u_ref, buf_ref, h0_ref, cw_ref, cb_ref, wg_ref, ba_ref, bx_ref, lam_ref,
                     hs_ref, nconv_ref, nlru_ref):
    w = cw_ref.shape[1]
    kw = cw_ref.shape[0]
    nblk = w // LANES
    u = u_ref[...]
    uc = cb_ref[...] + buf_ref[0] * cw_ref[0:1, :]
    for k in range(1, kw - 1):
        uc = uc + buf_ref[k] * cw_ref[k:k + 1, :]
    uc = uc + u * cw_ref[kw - 1:kw, :]
    for k in range(kw - 2):
        nconv_ref[k] = buf_ref[k + 1]
    nconv_ref[kw - 2] = u
    sp = jax.nn.softplus(-lam_ref[...])
    for h in range(nblk):
        cols = slice(h * LANES, (h + 1) * LANES)
        uch = uc[:, cols]
        g = _dot(uch.astype(BF16), wg_ref[h])
        a, b = _lru_gate_block(uch, g, ba_ref[:, cols], bx_ref[:, cols], sp[:, cols])
        hn = a * h0_ref[:, cols] + b
        nlru_ref[:, cols] = hn
        hs_ref[:, cols] = hn.astype(BF16)


def _lru_step_call(u2d, buf, h0, cw, cb, wg, ba, bx, lam):
    n = h0.shape[0]
    w = cw.shape[1]
    kw = cw.shape[0]
    nblk = w // LANES
    c2 = lambda i: (0, 0)
    c3 = lambda i: (0, 0, 0)
    return pl.pallas_call(
        _lru_step_kernel,
        grid=(1,),
        in_specs=[pl.BlockSpec((n, w), c2),
                  pl.BlockSpec((kw - 1, n, w), c3),
                  pl.BlockSpec((n, w), c2),
                  pl.BlockSpec((kw, w), c2),
                  pl.BlockSpec((1, w), c2),
                  pl.BlockSpec((nblk, LANES, 2 * LANES), c3),
                  pl.BlockSpec((1, w), c2),
                  pl.BlockSpec((1, w), c2),
                  pl.BlockSpec((1, w), c2)],
        out_specs=[pl.BlockSpec((n, w), c2),
                   pl.BlockSpec((kw - 1, n, w), c3),
                   pl.BlockSpec((n, w), c2)],
        out_shape=[jax.ShapeDtypeStruct((n, w), BF16),
                   jax.ShapeDtypeStruct((kw - 1, n, w), F32),
                   jax.ShapeDtypeStruct((n, w), F32)],
        name="lru_step",
    )(u2d, buf, h0, cw, cb, wg, ba, bx, lam)


def _s5_prep_kernel(arx_ref, aix_ref, dt_ref, br_ref, bi_ref, ar_ref, ai_ref,
                    bbr_ref, bbi_ref, abr_ref, abi_ref, asr_ref, asi_ref, *, log2_seg):
    dt = jnp.exp(dt_ref[...])

    def abar(ar, ai):
        mag = jnp.exp(dt * ar)
        ang = dt * ai
        return mag * jnp.cos(ang), mag * jnp.sin(ang)

    ar, ai = arx_ref[...], aix_ref[...]
    abr, abi = abar(ar, ai)
    den = ar * ar + ai * ai
    nr, ni = abr - 1.0, abi
    q_re = (nr * ar + ni * ai) / den
    q_im = (ni * ar - nr * ai) / den
    br, bi = br_ref[...], bi_ref[...]
    bbr_ref[...] = q_re * br - q_im * bi
    bbi_ref[...] = q_re * bi + q_im * br

    pr, pi = abar(ar_ref[...], ai_ref[...])
    abr_ref[...] = pr
    abi_ref[...] = pi
    for _ in range(log2_seg):
        pr, pi = pr * pr - pi * pi, 2.0 * (pr * pi)
    asr_ref[...] = pr
    asi_ref[...] = pi


def _s5_prep_call(a_re, a_im, log_dt, b_re, b_im, seg):
    g, p, k = b_re.shape
    log2_seg = seg.bit_length() - 1
    assert 1 << log2_seg == seg
    arx = jnp.repeat(a_re, k, axis=1)
    aix = jnp.repeat(a_im, k, axis=1)
    outs = pl.pallas_call(
        functools.partial(_s5_prep_kernel, log2_seg=log2_seg),
        out_shape=[jax.ShapeDtypeStruct((g, p * k), F32), jax.ShapeDtypeStruct((g, p * k), F32),
                   jax.ShapeDtypeStruct((g, p), F32), jax.ShapeDtypeStruct((g, p), F32),
                   jax.ShapeDtypeStruct((g, p), F32), jax.ShapeDtypeStruct((g, p), F32)],
        name="s5_prep",
    )(arx, aix, log_dt.reshape(g, 1), b_re.reshape(g, p * k), b_im.reshape(g, p * k), a_re, a_im)
    bbr, bbi, abr, abi, asr, asi = outs
    return bbr.reshape(g, p, k), bbi.reshape(g, p, k), abr, abi, asr, asi


def _s5_block_weights(bb_re, bb_im, c_re, c_im):
    g, p, k = bb_re.shape
    gb = GROUPS_PER_BLOCK
    nb = g // gb
    eye = jnp.eye(gb, dtype=F32)

    def in_blocks(bb):
        t = bb.reshape(nb, gb, p, k).transpose(0, 1, 3, 2)
        return jnp.einsum('cgkp,gh->cgkhp', t, eye).reshape(nb, gb * k, gb * p)

    def out_blocks(cc):
        t = cc.reshape(nb, gb, k, p)
        return jnp.einsum('cgkp,hg->chpgk', t, eye).reshape(nb, gb * p, gb * k)

    wb = jnp.concatenate([in_blocks(bb_re), in_blocks(bb_im)], axis=2).astype(BF16)
    wc = jnp.concatenate([out_blocks(c_re), -out_blocks(c_im)], axis=1).astype(BF16)
    return wb, wc


def _cplx_step(ar, ai, hr, hi, br, bi):
    return ar * hr - ai * hi + br, ar * hi + ai * hr + bi


def _s5_kernel(u_ref, wb_ref, wc_ref, are_ref, aim_ref, asr_ref, asi_ref, d_ref,
               gy_ref, nre_ref, nim_ref,
               slab, up, bre, bim, e_re, e_im, i_re, i_im, c_re, c_im, *, lane_chunk):
    t = pl.program_id(1)
    nt = pl.num_programs(1)
    tl, w = up.shape
    n_state = bre.shape[1]
    seg = tl // SUBLANES
    nblk = w // LANES
    spb = STATES_PER_BLOCK

    @pl.when(t == 0)
    def _():
        c_re[...] = jnp.zeros_like(c_re)
        c_im[...] = jnp.zeros_like(c_im)

    for c in range(nblk):
        slab[c] = u_ref[0, :, c * LANES:(c + 1) * LANES]
    _segments_to_sublanes(slab, up, seg)

    for c in range(nblk):
        r = _dot(up[:, c * LANES:(c + 1) * LANES].astype(BF16), wb_ref[c])
        bre[:, c * spb:(c + 1) * spb] = r[:, :spb]
        bim[:, c * spb:(c + 1) * spb] = r[:, spb:]

    n_chunks = n_state // lane_chunk

    for k in range(n_chunks):
        cols = slice(k * lane_chunk, (k + 1) * lane_chunk)
        ar = jnp.broadcast_to(are_ref[:, cols], (SUBLANES, lane_chunk))
        ai = jnp.broadcast_to(aim_ref[:, cols], (SUBLANES, lane_chunk))

        def p1(tau, hc, cols=cols, ar=ar, ai=ai):
            r = pl.multiple_of(tau * SUBLANES, SUBLANES)
            return _cplx_step(ar, ai, hc[0], hc[1], bre[pl.ds(r, SUBLANES), cols], bim[pl.ds(r, SUBLANES), cols])

        z = jnp.zeros((SUBLANES, lane_chunk), F32)
        er, ei = lax.fori_loop(0, seg, p1, (z, z))
        e_re[:, cols] = er
        e_im[:, cols] = ei

    rr, ri = c_re[...], c_im[...]
    asr, asi = asr_ref[...], asi_ref[...]
    for j in range(SUBLANES):
        i_re[j:j + 1, :] = rr
        i_im[j:j + 1, :] = ri
        rr, ri = _cplx_step(asr, asi, rr, ri, e_re[j:j + 1, :], e_im[j:j + 1, :])
    c_re[...] = rr
    c_im[...] = ri

    for k in range(n_chunks):
        cols = slice(k * lane_chunk, (k + 1) * lane_chunk)
        ar = jnp.broadcast_to(are_ref[:, cols], (SUBLANES, lane_chunk))
        ai = jnp.broadcast_to(aim_ref[:, cols], (SUBLANES, lane_chunk))

        def p2(tau, hc, cols=cols, ar=ar, ai=ai):
            r = pl.multiple_of(tau * SUBLANES, SUBLANES)
            hr, hi = _cplx_step(ar, ai, hc[0], hc[1], bre[pl.ds(r, SUBLANES), cols], bim[pl.ds(r, SUBLANES), cols])
            bre[pl.ds(r, SUBLANES), cols] = hr
            bim[pl.ds(r, SUBLANES), cols] = hi
            return hr, hi

        lax.fori_loop(0, seg, p2, (i_re[:, cols], i_im[:, cols]))

    for c in range(nblk):
        cols = slice(c * LANES, (c + 1) * LANES)
        scols = slice(c * spb, (c + 1) * spb)
        y = (_dot(bre[:, scols].astype(BF16), wc_ref[c, :spb, :])
             + _dot(bim[:, scols].astype(BF16), wc_ref[c, spb:, :]))
        ys = y + d_ref[:, cols] * up[:, cols]
        up[:, cols] = jax.nn.gelu(ys)

    _sublanes_to_segments(up, slab, seg)
    for c in range(nblk):
        gy_ref[0, :, c * LANES:(c + 1) * LANES] = slab[c].astype(BF16)

    @pl.when(t == nt - 1)
    def _():
        nre_ref[0] = rr
        nim_ref[0] = ri


def _s5_call(u3d, wb, wc, abr, abi, asr, asi, d, *, tl, w, lane_chunk=512):
    bsz, seq, _ = u3d.shape
    nblk = w // LANES
    n_state = nblk * STATES_PER_BLOCK
    est = (2 * tl * w * 4 + 2 * tl * w * 2 + 2 * tl * w * 4 + 2 * tl * n_state * 4
           + 2 * (wb.size + wc.size) * 2 + tl * 2 * STATES_PER_BLOCK * 4)
    c2 = lambda b, t: (0, 0)
    c3 = lambda b, t: (0, 0, 0)
    return pl.pallas_call(
        functools.partial(_s5_kernel, lane_chunk=lane_chunk),
        grid=(bsz, seq // tl),
        in_specs=[pl.BlockSpec((1, tl, w), lambda b, t: (b, t, 1)),
                  pl.BlockSpec(wb.shape, c3),
                  pl.BlockSpec(wc.shape, c3),
                  pl.BlockSpec((1, n_state), c2),
                  pl.BlockSpec((1, n_state), c2),
                  pl.BlockSpec((1, n_state), c2),
                  pl.BlockSpec((1, n_state), c2),
                  pl.BlockSpec((1, w), c2)],
        out_specs=[pl.BlockSpec((1, tl, w), lambda b, t: (b, t, 0)),
                   pl.BlockSpec((1, 1, n_state), lambda b, t: (b, 0, 0)),
                   pl.BlockSpec((1, 1, n_state), lambda b, t: (b, 0, 0))],
        out_shape=[jax.ShapeDtypeStruct((bsz, seq, w), BF16),
                   jax.ShapeDtypeStruct((bsz, 1, n_state), F32),
                   jax.ShapeDtypeStruct((bsz, 1, n_state), F32)],
        scratch_shapes=[pltpu.VMEM((nblk, tl, LANES), F32),
                        pltpu.VMEM((tl, w), F32),
                        pltpu.VMEM((tl, n_state), F32),
                        pltpu.VMEM((tl, n_state), F32),
                        pltpu.VMEM((SUBLANES, n_state), F32),
                        pltpu.VMEM((SUBLANES, n_state), F32),
                        pltpu.VMEM((SUBLANES, n_state), F32),
                        pltpu.VMEM((SUBLANES, n_state), F32),
                        pltpu.VMEM((1, n_state), F32),
                        pltpu.VMEM((1, n_state), F32)],
        compiler_params=pltpu.CompilerParams(dimension_semantics=("parallel", "arbitrary"),
                                             vmem_limit_bytes=_vmem_limit(est)),
        name="s5_scan",
    )(u3d, wb, wc, abr, abi, asr, asi, d)


def _s5_step_kernel(u_ref, h0r_ref, h0i_ref, wb_ref, wc_ref, are_ref, aim_ref, d_ref,
                    gy_ref, nre_ref, nim_ref):
    w = u_ref.shape[1]
    nblk = w // LANES
    spb = STATES_PER_BLOCK
    for c in range(nblk):
        cols = slice(c * LANES, (c + 1) * LANES)
        scols = slice(c * spb, (c + 1) * spb)
        u = u_ref[:, cols]
        r = _dot(u.astype(BF16), wb_ref[c])
        hr, hi = _cplx_step(are_ref[:, scols], aim_ref[:, scols], h0r_ref[:, scols], h0i_ref[:, scols],
                            r[:, :spb], r[:, spb:])
        nre_ref[:, scols] = hr
        nim_ref[:, scols] = hi
        y = _dot(hr.astype(BF16), wc_ref[c, :spb, :]) + _dot(hi.astype(BF16), wc_ref[c, spb:, :])
        gy_ref[:, cols] = jax.nn.gelu(y + d_ref[:, cols] * u).astype(BF16)


def _s5_step_call(u2d, h0r, h0i, wb, wc, abr, abi, d, *, w):
    n, n_state = h0r.shape
    c2 = lambda i: (0, 0)
    c3 = lambda i: (0, 0, 0)
    return pl.pallas_call(
        _s5_step_kernel,
        grid=(1,),
        in_specs=[pl.BlockSpec((n, w), lambda i: (0, 1)),
                  pl.BlockSpec((n, n_state), c2),
                  pl.BlockSpec((n, n_state), c2),
                  pl.BlockSpec(wb.shape, c3),
                  pl.BlockSpec(wc.shape, c3),
                  pl.BlockSpec((1, n_state), c2),
                  pl.BlockSpec((1, n_state), c2),
                  pl.BlockSpec((1, w), c2)],
        out_specs=[pl.BlockSpec((n, w), c2),
                   pl.BlockSpec((n, n_state), c2),
                   pl.BlockSpec((n, n_state), c2)],
        out_shape=[jax.ShapeDtypeStruct((n, w), BF16),
                   jax.ShapeDtypeStruct((n, n_state), F32),
                   jax.ShapeDtypeStruct((n, n_state), F32)],
        name="s5_step",
    )(u2d, h0r, h0i, wb, wc, abr, abi, d)


def _merge_kernel(hs_ref, gy_ref, sz_ref, x_ref, gt_ref, wp_ref, wg_ref, wo_ref, o_ref):
    d = x_ref.shape[1]
    ya = _dot(hs_ref[...], wp_ref[...])
    merged = sz_ref[:, :d].astype(F32) * ya
    glu = _dot(gy_ref[...], wg_ref[...])
    yb = glu[:, :d] * jax.nn.sigmoid(glu[:, d:])
    merged = merged + sz_ref[:, d:].astype(F32) * yb
    o = _dot(merged.astype(BF16), wo_ref[...])
    o_ref[...] = x_ref[...] + _rows(gt_ref) * o


def _merge_call(hs, gy, sz, x2d, mod, wp, wg, wo, *, tm, tiles_per_batch):
    m, d = x2d.shape
    w = hs.shape[1]
    est = (2 * (2 * tm * w * 2 + tm * 2 * d * 2 + 2 * tm * d * 4) + (wp.size + wg.size + wo.size) * 2
           + tm * d * 4 * 5)
    one = pl.Buffered(1)
    return pl.pallas_call(
        _merge_kernel,
        grid=(m // tm,),
        in_specs=[pl.BlockSpec((tm, w), lambda i: (i, 0)),
                  pl.BlockSpec((tm, w), lambda i: (i, 0)),
                  pl.BlockSpec((tm, 2 * d), lambda i: (i, 0)),
                  pl.BlockSpec((tm, d), lambda i: (i, 0)),
                  _mod_spec(mod, 2, d, tiles_per_batch, 1),
                  pl.BlockSpec(wp.shape, lambda i: (0, 0), pipeline_mode=one),
                  pl.BlockSpec(wg.shape, lambda i: (0, 0), pipeline_mode=one),
                  pl.BlockSpec(wo.shape, lambda i: (0, 0), pipeline_mode=one)],
        out_specs=pl.BlockSpec((tm, d), lambda i: (i, 0)),
        out_shape=jax.ShapeDtypeStruct((m, d), F32),
        compiler_params=pltpu.CompilerParams(dimension_semantics=("parallel",),
                                             vmem_limit_bytes=_vmem_limit(est)),
        name="merge",
    )(hs, gy, sz, x2d, mod, wp, wg, wo)


def _mlp_kernel(x_ref, sc_ref, sh_ref, gt_ref, g2_ref, gf_ref, wu_ref, wd_ref, o_ref, h_scr, *, final_norm):
    j = pl.program_id(1)
    nj = pl.num_programs(1)

    n_rows = x_ref.shape[0]

    @pl.when(j == 0)
    def _():
        def chunk(rows):
            h = _rmsnorm(x_ref[rows, :], g2_ref[...])
            h_scr[rows, :] = (h * (1.0 + _rows(sc_ref, rows)) + _rows(sh_ref, rows)).astype(BF16)
            o_ref[rows, :] = jnp.zeros((h.shape[0], h.shape[1]), F32)

        _for_row_chunks(n_rows, chunk)

    up = _dot(h_scr[...], wu_ref[...])
    act = jnp.square(jnp.maximum(up, 0.0)).astype(BF16)
    d = o_ref.shape[1]
    nc = min(d, MLP_OUT_CHUNK)
    for c in range(d // nc):
        cols = slice(c * nc, (c + 1) * nc)
        o_ref[:, cols] += _dot(act, wd_ref[:, cols])

    @pl.when(j == nj - 1)
    def _():
        def chunk(rows):
            x2 = x_ref[rows, :] + _rows(gt_ref, rows) * o_ref[rows, :]
            o_ref[rows, :] = _rmsnorm(x2, gf_ref[...]) if final_norm else x2

        _for_row_chunks(n_rows, chunk)


def _mlp_call(x2d, mod, g2, gf, w_up, w_down, *, tm, tiles_per_batch, final_norm, tf=512):
    m, d = x2d.shape
    dff = w_up.shape[1]
    est = (3 * tm * d * 4 + tm * d * 2 + 4 * d * tf * 2 + tm * tf * 6 + tm * MLP_OUT_CHUNK * 4) * 2 // 3 + (4 << 20)
    return pl.pallas_call(
        functools.partial(_mlp_kernel, final_norm=final_norm),
        grid=(m // tm, dff // tf),
        in_specs=[pl.BlockSpec((tm, d), lambda i, j: (i, 0), pipeline_mode=pl.Buffered(1)),
                  _mod_spec(mod, 4, d, tiles_per_batch, 2),
                  _mod_spec(mod, 3, d, tiles_per_batch, 2),
                  _mod_spec(mod, 5, d, tiles_per_batch, 2),
                  pl.BlockSpec((1, d), lambda i, j: (0, 0)),
                  pl.BlockSpec((1, d), lambda i, j: (0, 0)),
                  pl.BlockSpec((d, tf), lambda i, j: (0, j)),
                  pl.BlockSpec((tf, d), lambda i, j: (j, 0))],
        out_specs=pl.BlockSpec((tm, d), lambda i, j: (i, 0)),
        out_shape=jax.ShapeDtypeStruct((m, d), F32),
        scratch_shapes=[pltpu.VMEM((tm, d), BF16)],
        compiler_params=pltpu.CompilerParams(dimension_semantics=("parallel", "arbitrary"),
                                             vmem_limit_bytes=_vmem_limit(est)),
        name="mlp",
    )(x2d, mod, mod, mod, g2, gf, w_up, w_down)


def _pick_tile(n, pref):
    t = min(n, pref)
    assert n % t == 0, (n, t)
    return t


def kernel(x_prompt, x_sample, state_conv, state_lru, state_ssm_re, state_ssm_im, c_prompt, c_sample, w_ada, b_ada, g_norm1, g_norm2, w_in, conv_w, conv_b, w_rg_a, b_rg_a, w_rg_x, b_rg_x, lru_lambda, w_proj_a, ssm_a_re, ssm_a_im, ssm_log_dt, ssm_b_re, ssm_b_im, ssm_c_re, ssm_c_im, ssm_d, w_glu, w_out, w_up, w_down, g_final):
    depth = w_ada.shape[0]
    nb, seq, d = x_prompt.shape
    ns = x_sample.shape[0]
    assert x_sample.shape[1] == 1
    w = conv_w.shape[2]
    n_state = ssm_a_re.shape[1] * ssm_a_re.shape[2]
    assert ssm_b_re.shape[2:] == (SSM_STATE, SSM_GROUP) and w % LANES == 0
    assert w_in.shape[2] == 2 * w + 2 * d

    tl = _pick_tile(seq, 512)
    seg = tl // SUBLANES
    tm_in = _pick_tile(seq, 1024)
    tm_mg = _pick_tile(seq, 256)
    tm_mlp = _pick_tile(seq, 1024)

    xp = x_prompt.reshape(nb * seq, d)
    xs = x_sample.reshape(ns, d)
    pad = (-nb) % SUBLANES
    c_all = jnp.concatenate([c_prompt, jnp.zeros((pad, d), F32), c_sample], axis=0)

    outs_p = [[] for _ in range(4)]
    outs_s = [[] for _ in range(4)]
    for l in range(depth):
        last = l == depth - 1
        mod = _mod_call(c_all, w_ada[l], b_ada[l].reshape(1, -1))
        mod_p = mod[:nb].reshape(nb, 1, N_MOD * d)
        mod_s = mod[nb + pad:]

        g1 = g_norm1[l].reshape(1, d)
        g2 = g_norm2[l].reshape(1, d)
        gf = g_final.reshape(1, d)
        cw, cb = conv_w[l], conv_b[l].reshape(1, w)
        wg = jnp.concatenate([w_rg_a[l], w_rg_x[l]], axis=2).astype(BF16)
        ba, bx, lam = b_rg_a[l].reshape(1, w), b_rg_x[l].reshape(1, w), lru_lambda[l].reshape(1, w)
        bbr, bbi, abr, abi, asr, asi = _s5_prep_call(ssm_a_re[l], ssm_a_im[l], ssm_log_dt[l],
                                                     ssm_b_re[l], ssm_b_im[l], seg)
        wb, wc = _s5_block_weights(bbr, bbi, ssm_c_re[l], ssm_c_im[l])
        abr, abi, asr, asi = (v.reshape(1, n_state) for v in (abr, abi, asr, asi))
        dskip = ssm_d[l].reshape(1, w)
        wp, wgl, wo = w_proj_a[l].astype(BF16), w_glu[l].astype(BF16), w_out[l].astype(BF16)
        w_in_b, w_up_b, w_down_b = w_in[l].astype(BF16), w_up[l].astype(BF16), w_down[l].astype(BF16)

        u_p, sz_p = _inproj_call(xp, mod_p, g1, w_in_b, tm=tm_in, tiles_per_batch=seq // tm_in, w_mix=w)
        u3 = u_p.reshape(nb, seq, 2 * w)
        hs_p, nconv_p, nlru_p = _lru_call(u3, cw, cb, wg, ba, bx, lam, tl=tl)
        gy_p, nre_p, nim_p = _s5_call(u3, wb, wc, abr, abi, asr, asi, dskip, tl=tl, w=w)
        x1_p = _merge_call(hs_p.reshape(nb * seq, w), gy_p.reshape(nb * seq, w), sz_p, xp, mod_p, wp, wgl, wo,
                           tm=tm_mg, tiles_per_batch=seq // tm_mg)
        xp = _mlp_call(x1_p, mod_p, g2, gf, w_up_b, w_down_b, tm=tm_mlp, tiles_per_batch=seq // tm_mlp,
                       final_norm=last)

        u_s, sz_s = _inproj_call(xs, mod_s, g1, w_in_b, tm=ns, tiles_per_batch=1, w_mix=w)
        buf = jnp.transpose(state_conv[l], (1, 0, 2))
        hs_s, nconv_s, nlru_s = _lru_step_call(u_s, buf, state_lru[l], cw, cb, wg, ba, bx, lam)
        gy_s, nre_s, nim_s = _s5_step_call(u_s, state_ssm_re[l].reshape(ns, n_state),
                                           state_ssm_im[l].reshape(ns, n_state), wb, wc, abr, abi, dskip, w=w)
        x1_s = _merge_call(hs_s, gy_s, sz_s, xs, mod_s, wp, wgl, wo, tm=ns, tiles_per_batch=1)
        xs = _mlp_call(x1_s, mod_s, g2, gf, w_up_b, w_down_b, tm=ns, tiles_per_batch=1, final_norm=last)

        gshape = ssm_a_re.shape[1:]
        for acc, v in zip(outs_p, (nconv_p, nlru_p.reshape(nb, w),
                                   nre_p.reshape((nb,) + gshape), nim_p.reshape((nb,) + gshape))):
            acc.append(v)
        for acc, v in zip(outs_s, (jnp.transpose(nconv_s, (1, 0, 2)), nlru_s,
                                   nre_s.reshape((ns,) + gshape), nim_s.reshape((ns,) + gshape))):
            acc.append(v)

    y_prompt = xp.reshape(nb, seq, d)
    y_sample = xs.reshape(ns, 1, d)
    return (y_prompt, y_sample) + tuple(jnp.stack(a) for a in outs_p) + tuple(jnp.stack(a) for a in outs_s)
```

```python
import functools

import jax
import jax.numpy as jnp
from jax import lax
from jax.experimental import pallas as pl
from jax.experimental.pallas import tpu as pltpu

F32 = jnp.float32
BF16 = jnp.bfloat16

LANES = 128
SUBLANES = 8
VMEM_PHYSICAL_BYTES = 64 * 1024 * 1024
VMEM_LIMIT_CAP_BYTES = VMEM_PHYSICAL_BYTES - 6 * 1024 * 1024

LRU_C = 8.0
EPS = 1e-6
N_MOD = 6
SSM_GROUP = 16
SSM_STATE = 64
GROUPS_PER_BLOCK = LANES // SSM_GROUP
STATES_PER_BLOCK = GROUPS_PER_BLOCK * SSM_STATE


VMEM_SLACK_BYTES = 8 * 1024 * 1024


def _vmem_limit(nbytes):
    return int(min(VMEM_LIMIT_CAP_BYTES, max(32 * 1024 * 1024, nbytes + VMEM_SLACK_BYTES)))


ROW_CHUNK = 256
MLP_OUT_CHUNK = 512
INPROJ_CHUNK = 256
SCAN_UNROLL = 4


def _rows(ref, rows=None):
    if len(ref.shape) == 3:
        return ref[0]
    return ref[...] if rows is None else ref[rows, :]


def _for_row_chunks(n_rows, fn):
    rc = min(n_rows, ROW_CHUNK)
    assert n_rows % rc == 0
    if n_rows == rc:
        fn(slice(0, rc))
        return

    def body(i, carry):
        fn(pl.ds(pl.multiple_of(i * rc, rc), rc))
        return carry

    lax.fori_loop(0, n_rows // rc, body, 0)


def _dot(a, b):
    return jnp.dot(a, b, preferred_element_type=F32)


def _sigmoid(x):
    return 0.5 * jnp.tanh(0.5 * x) + 0.5


def _rmsnorm(x, g):
    ms = jnp.mean(x * x, axis=-1, keepdims=True)
    return (x * lax.rsqrt(ms + EPS)) * g


def _mod_kernel(c_ref, w_ref, b_ref, o_ref):
    c = c_ref[...]
    cs = (c * _sigmoid(c)).astype(BF16)
    o_ref[...] = _dot(cs, w_ref[...].astype(BF16)) + b_ref[...]


def _mod_call(c_all, w_ada, b_ada, tn=1024):
    n_rows, d = c_all.shape
    n = w_ada.shape[1]
    est = 2 * (d * tn * 4) + 2 * n_rows * tn * 4 + n_rows * d * 4 * 2 + d * tn * 2
    return pl.pallas_call(
        _mod_kernel,
        grid=(n // tn,),
        in_specs=[pl.BlockSpec((n_rows, d), lambda j: (0, 0)),
                  pl.BlockSpec((d, tn), lambda j: (0, j)),
                  pl.BlockSpec((1, tn), lambda j: (0, j))],
        out_specs=pl.BlockSpec((n_rows, tn), lambda j: (0, j)),
        out_shape=jax.ShapeDtypeStruct((n_rows, n), F32),
        compiler_params=pltpu.CompilerParams(dimension_semantics=("parallel",),
                                             vmem_limit_bytes=_vmem_limit(est)),
        name="mod",
    )(c_all, w_ada, b_ada)


def _mod_spec(mod, piece, d, tiles_per_batch, ngrid):
    if mod.ndim == 3:
        if ngrid == 1:
            return pl.BlockSpec((1, 1, d), lambda i: (i // tiles_per_batch, 0, piece))
        return pl.BlockSpec((1, 1, d), lambda i, j: (i // tiles_per_batch, 0, piece))
    rows = mod.shape[0]
    if ngrid == 1:
        return pl.BlockSpec((rows, d), lambda i: (0, piece))
    return pl.BlockSpec((rows, d), lambda i, j: (0, piece))


def _inproj_kernel(x_ref, sc_ref, sh_ref, g_ref, w_ref, u_ref, sz_ref, *rest, n_u_tiles):
    h_scr = rest[-1]
    j = pl.program_id(1)

    @pl.when(j == 0)
    def _():
        def chunk(rows):
            h = _rmsnorm(x_ref[rows, :], g_ref[...])
            h_scr[rows, :] = (h * (1.0 + _rows(sc_ref, rows)) + _rows(sh_ref, rows)).astype(BF16)

        _for_row_chunks(x_ref.shape[0], chunk)

    if len(rest) == 2:
        rest[0][...] = w_ref[...].astype(BF16)
        w_ref = rest[0]
    tn = w_ref.shape[1]
    nc = min(tn, INPROJ_CHUNK)

    @pl.when(j < n_u_tiles)
    def _():
        u_ref[...] = _dot(h_scr[...], w_ref[...])

    @pl.when(j >= n_u_tiles)
    def _():
        for c in range(tn // nc):
            cols = slice(c * nc, (c + 1) * nc)
            sz_ref[:, cols] = _sigmoid(_dot(h_scr[...], w_ref[:, cols])).astype(BF16)


def _inproj_call(x2d, mod, g1, w_in, *, tm, tiles_per_batch, w_mix, tn=1024):
    m, d = x2d.shape
    n = w_in.shape[1]
    n_u = 2 * w_mix
    n_u_tiles = n_u // tn
    emit_w = w_in.dtype != BF16
    assert not emit_w or m == tm
    wbytes = w_in.dtype.itemsize
    est = (2 * tm * d * 4 + 2 * d * tn * wbytes + 2 * tm * tn * 4 + 2 * tm * tn * 2 + tm * d * 2 + tm * tn * 4
           + (2 * d * tn * 2 if emit_w else 0))
    out_specs = [pl.BlockSpec((tm, tn), lambda i, j: (i, jnp.minimum(j, n_u_tiles - 1))),
                 pl.BlockSpec((tm, tn), lambda i, j: (i, jnp.maximum(j - n_u_tiles, 0)))]
    out_shape = [jax.ShapeDtypeStruct((m, n_u), F32),
                 jax.ShapeDtypeStruct((m, n - n_u), BF16)]
    if emit_w:
        out_specs.append(pl.BlockSpec((d, tn), lambda i, j: (0, j)))
        out_shape.append(jax.ShapeDtypeStruct((d, n), BF16))
    return pl.pallas_call(
        functools.partial(_inproj_kernel, n_u_tiles=n_u_tiles),
        grid=(m // tm, n // tn),
        in_specs=[pl.BlockSpec((tm, d), lambda i, j: (i, 0)),
                  _mod_spec(mod, 1, d, tiles_per_batch, 2),
                  _mod_spec(mod, 0, d, tiles_per_batch, 2),
                  pl.BlockSpec((1, d), lambda i, j: (0, 0)),
                  pl.BlockSpec((d, tn), lambda i, j: (0, j))],
        out_specs=out_specs,
        out_shape=out_shape,
        scratch_shapes=[pltpu.VMEM((tm, d), BF16)],
        compiler_params=pltpu.CompilerParams(dimension_semantics=("parallel", "arbitrary"),
                                             vmem_limit_bytes=_vmem_limit(est)),
        name="inproj",
    )(x2d, mod, mod, g1, w_in)


def _segments_to_sublanes(slab_ref, dst_ref, seg):
    nblk = slab_ref.shape[0]

    def body(tau, carry):
        r = pl.multiple_of(tau * SUBLANES, SUBLANES)
        for c in range(nblk):
            dst_ref[pl.ds(r, SUBLANES), c * LANES:(c + 1) * LANES] = (
                slab_ref[c, pl.ds(tau, SUBLANES, stride=seg), :])
        return carry

    lax.fori_loop(0, seg, body, 0, unroll=SCAN_UNROLL)


def _sublanes_to_segments(src_ref, slab_ref, seg):
    nblk = slab_ref.shape[0]

    def body(tau, carry):
        r = pl.multiple_of(tau * SUBLANES, SUBLANES)
        for c in range(nblk):
            slab_ref[c, pl.ds(tau, SUBLANES, stride=seg), :] = (
                src_ref[pl.ds(r, SUBLANES), c * LANES:(c + 1) * LANES])
        return carry

    lax.fori_loop(0, seg, body, 0, unroll=SCAN_UNROLL)


def _lru_gate_block(uc, g, ba, bx, sp):
    r = _sigmoid(g[:, :LANES] + ba)
    i = _sigmoid(g[:, LANES:] + bx)
    log_a = (-LRU_C * r) * sp
    a = jnp.exp(log_a)
    mult = jnp.sqrt(1.0 - jnp.exp(2.0 * log_a))
    return a, (mult * i) * uc


def _lru_kernel(u_ref, cw_ref, cb_ref, wg_ref, ba_ref, bx_ref, lam_ref,
                hs_ref, nconv_ref, nlru_ref,
                ext, slab, ucp, a_s, b_s, carry, e_s, p_s, i_s):
    t = pl.program_id(1)
    nt = pl.num_programs(1)
    tl, w = ucp.shape
    seg = tl // SUBLANES
    nblk = w // LANES
    hist = SUBLANES
    kw = cw_ref.shape[0]

    @pl.when(t == 0)
    def _():
        ext[0:hist, :] = jnp.zeros((hist, w), F32)
        carry[...] = jnp.zeros_like(carry)

    ext[hist:hist + tl, :] = u_ref[0]

    for c in range(nblk):
        cols = slice(c * LANES, (c + 1) * LANES)
        acc = cb_ref[:, cols] + ext[hist - kw + 1:hist - kw + 1 + tl, cols] * cw_ref[0:1, cols]
        for k in range(1, kw):
            acc = acc + ext[hist - kw + 1 + k:hist - kw + 1 + k + tl, cols] * cw_ref[k:k + 1, cols]
        slab[c] = acc

    tail = ext[hist + tl - (kw - 1):hist + tl, :]
    ext[hist - (kw - 1):hist, :] = tail

    @pl.when(t == nt - 1)
    def _():
        nconv_ref[0] = tail

    _segments_to_sublanes(slab, ucp, seg)

    sp = jax.nn.softplus(-lam_ref[...])
    for h in range(nblk):
        cols = slice(h * LANES, (h + 1) * LANES)
        uc = ucp[:, cols]
        g = _dot(uc.astype(BF16), wg_ref[h])
        a, b = _lru_gate_block(uc, g, ba_ref[:, cols], bx_ref[:, cols], sp[:, cols])
        a_s[:, cols] = a
        b_s[:, cols] = b

    def p1(tau, c):
        p, e = c
        r = pl.multiple_of(tau * SUBLANES, SUBLANES)
        at = a_s[pl.ds(r, SUBLANES), :]
        return at * p, at * e + b_s[pl.ds(r, SUBLANES), :]

    p_end, e_end = lax.fori_loop(0, seg, p1, (jnp.ones((SUBLANES, w), F32), jnp.zeros((SUBLANES, w), F32)),
                                 unroll=SCAN_UNROLL)
    p_s[...] = p_end
    e_s[...] = e_end

    row = carry[...]
    for j in range(SUBLANES):
        i_s[j:j + 1, :] = row
        row = p_s[j:j + 1, :] * row + e_s[j:j + 1, :]
    carry[...] = row

    def p2(tau, hcur):
        r = pl.multiple_of(tau * SUBLANES, SUBLANES)
        hcur = a_s[pl.ds(r, SUBLANES), :] * hcur + b_s[pl.ds(r, SUBLANES), :]
        b_s[pl.ds(r, SUBLANES), :] = hcur
        return hcur

    lax.fori_loop(0, seg, p2, i_s[...], unroll=SCAN_UNROLL)

    _sublanes_to_segments(b_s, slab, seg)
    for c in range(nblk):
        hs_ref[0, :, c * LANES:(c + 1) * LANES] = slab[c].astype(BF16)

    @pl.when(t == nt - 1)
    def _():
        nlru_ref[0] = row


def _lru_call(u3d, cw, cb, wg, ba, bx, lam, *, tl):
    bsz, seq, _ = u3d.shape
    w = cw.shape[1]
    kw = cw.shape[0]
    nblk = w // LANES
    est = (2 * tl * w * 4 + 2 * tl * w * 2 + (tl + SUBLANES) * w * 4 + 4 * tl * w * 4
           + nblk * LANES * 2 * LANES * 2 * 2)
    const2 = lambda b, t: (0, 0)
    return pl.pallas_call(
        _lru_kernel,
        grid=(bsz, seq // tl),
        in_specs=[pl.BlockSpec((1, tl, w), lambda b, t: (b, t, 0)),
                  pl.BlockSpec((kw, w), const2),
                  pl.BlockSpec((1, w), const2),
                  pl.BlockSpec((nblk, LANES, 2 * LANES), lambda b, t: (0, 0, 0)),
                  pl.BlockSpec((1, w), const2),
                  pl.BlockSpec((1, w), const2),
                  pl.BlockSpec((1, w), const2)],
        out_specs=[pl.BlockSpec((1, tl, w), lambda b, t: (b, t, 0)),
                   pl.BlockSpec((1, kw - 1, w), lambda b, t: (b, 0, 0)),
                   pl.BlockSpec((1, 1, w), lambda b, t: (b, 0, 0))],
        out_shape=[jax.ShapeDtypeStruct((bsz, seq, w), BF16),
                   jax.ShapeDtypeStruct((bsz, kw - 1, w), F32),
                   jax.ShapeDtypeStruct((bsz, 1, w), F32)],
        scratch_shapes=[pltpu.VMEM((tl + SUBLANES, w), F32),
                        pltpu.VMEM((nblk, tl, LANES), F32),
                        pltpu.VMEM((tl, w), F32),
                        pltpu.VMEM((tl, w), F32),
                        pltpu.VMEM((tl, w), F32),
                        pltpu.VMEM((1, w), F32),
                        pltpu.VMEM((SUBLANES, w), F32),
                        pltpu.VMEM((SUBLANES, w), F32),
                        pltpu.VMEM((SUBLANES, w), F32)],
        compiler_params=pltpu.CompilerParams(dimension_semantics=("parallel", "arbitrary"),
                                             vmem_limit_bytes=_vmem_limit(est)),
        name="lru_scan",
    )(u3d, cw, cb, wg, ba, bx, lam)


def _lru_step_kernel(u_ref, buf_ref, h0_ref, cw_ref, cb_ref, wg_ref, ba_ref, bx_ref, lam_ref,
                     hs_ref, nconv_ref, nlru_ref):
    w = cw_ref.shape[1]
    kw = cw_ref.shape[0]
    nblk = w // LANES
    u = u_ref[...]
    uc = cb_ref[...] + buf_ref[0] * cw_ref[0:1, :]
    for k in range(1, kw - 1):
        uc = uc + buf_ref[k] * cw_ref[k:k + 1, :]
    uc = uc + u * cw_ref[kw - 1:kw, :]
    for k in range(kw - 2):
        nconv_ref[k] = buf_ref[k + 1]
    nconv_ref[kw - 2] = u
    sp = jax.nn.softplus(-lam_ref[...])
    for h in range(nblk):
        cols = slice(h * LANES, (h + 1) * LANES)
        uch = uc[:, cols]
        g = _dot(uch.astype(BF16), wg_ref[h])
        a, b = _lru_gate_block(uch, g, ba_ref[:, cols], bx_ref[:, cols], sp[:, cols])
        hn = a * h0_ref[:, cols] + b
        nlru_ref[:, cols] = hn
        hs_ref[:, cols] = hn.astype(BF16)


def _lru_step_call(u2d, buf, h0, cw, cb, wg, ba, bx, lam):
    n = h0.shape[0]
    w = cw.shape[1]
    kw = cw.shape[0]
    nblk = w // LANES
    c2 = lambda i: (0, 0)
    c3 = lambda i: (0, 0, 0)
    return pl.pallas_call(
        _lru_step_kernel,
        grid=(1,),
        in_specs=[pl.BlockSpec((n, w), c2),
                  pl.BlockSpec((kw - 1, n, w), c3),
                  pl.BlockSpec((n, w), c2),
                  pl.BlockSpec((kw, w), c2),
                  pl.BlockSpec((1, w), c2),
                  pl.BlockSpec((nblk, LANES, 2 * LANES), c3),
                  pl.BlockSpec((1, w), c2),
                  pl.BlockSpec((1, w), c2),
                  pl.BlockSpec((1, w), c2)],
        out_specs=[pl.BlockSpec((n, w), c2),
                   pl.BlockSpec((kw - 1, n, w), c3),
                   pl.BlockSpec((n, w), c2)],
        out_shape=[jax.ShapeDtypeStruct((n, w), BF16),
                   jax.ShapeDtypeStruct((kw - 1, n, w), F32),
                   jax.ShapeDtypeStruct((n, w), F32)],
        name="lru_step",
    )(u2d, buf, h0, cw, cb, wg, ba, bx, lam)


def _s5_prep_kernel(arx_ref, aix_ref, dt_ref, br_ref, bi_ref, ar_ref, ai_ref,
                    bbr_ref, bbi_ref, abr_ref, abi_ref, asr_ref, asi_ref, *, log2_seg):
    dt = jnp.exp(dt_ref[...])

    def abar(ar, ai):
        mag = jnp.exp(dt * ar)
        ang = dt * ai
        return mag * jnp.cos(ang), mag * jnp.sin(ang)

    ar, ai = arx_ref[...], aix_ref[...]
    abr, abi = abar(ar, ai)
    den = ar * ar + ai * ai
    nr, ni = abr - 1.0, abi
    q_re = (nr * ar + ni * ai) / den
    q_im = (ni * ar - nr * ai) / den
    br, bi = br_ref[...], bi_ref[...]
    bbr_ref[...] = q_re * br - q_im * bi
    bbi_ref[...] = q_re * bi + q_im * br

    pr, pi = abar(ar_ref[...], ai_ref[...])
    abr_ref[...] = pr
    abi_ref[...] = pi
    for _ in range(log2_seg):
        pr, pi = pr * pr - pi * pi, 2.0 * (pr * pi)
    asr_ref[...] = pr
    asi_ref[...] = pi


def _s5_prep_call(a_re, a_im, log_dt, b_re, b_im, seg):
    g, p, k = b_re.shape
    log2_seg = seg.bit_length() - 1
    assert 1 << log2_seg == seg
    arx = jnp.repeat(a_re, k, axis=1)
    aix = jnp.repeat(a_im, k, axis=1)
    outs = pl.pallas_call(
        functools.partial(_s5_prep_kernel, log2_seg=log2_seg),
        out_shape=[jax.ShapeDtypeStruct((g, p * k), F32), jax.ShapeDtypeStruct((g, p * k), F32),
                   jax.ShapeDtypeStruct((g, p), F32), jax.ShapeDtypeStruct((g, p), F32),
                   jax.ShapeDtypeStruct((g, p), F32), jax.ShapeDtypeStruct((g, p), F32)],
        name="s5_prep",
    )(arx, aix, log_dt.reshape(g, 1), b_re.reshape(g, p * k), b_im.reshape(g, p * k), a_re, a_im)
    bbr, bbi, abr, abi, asr, asi = outs
    return bbr.reshape(g, p, k), bbi.reshape(g, p, k), abr, abi, asr, asi


def _s5_block_weights(bb_re, bb_im, c_re, c_im):
    g, p, k = bb_re.shape
    gb = GROUPS_PER_BLOCK
    nb = g // gb
    eye = jnp.eye(gb, dtype=F32)

    def in_blocks(bb):
        t = bb.reshape(nb, gb, p, k).transpose(0, 1, 3, 2)
        return jnp.einsum('cgkp,gh->cgkhp', t, eye).reshape(nb, gb * k, gb * p)

    def out_blocks(cc):
        t = cc.reshape(nb, gb, k, p)
        return jnp.einsum('cgkp,hg->chpgk', t, eye).reshape(nb, gb * p, gb * k)

    wb = jnp.concatenate([in_blocks(bb_re), in_blocks(bb_im)], axis=2).astype(BF16)
    wc = jnp.concatenate([out_blocks(c_re), -out_blocks(c_im)], axis=1).astype(BF16)
    return wb, wc


def _cplx_step(ar, ai, hr, hi, br, bi):
    return ar * hr - ai * hi + br, ar * hi + ai * hr + bi


def _s5_kernel(u_ref, wb_ref, wc_ref, are_ref, aim_ref, asr_ref, asi_ref, d_ref,
               gy_ref, nre_ref, nim_ref,
               slab, up, bre, bim, e_re, e_im, i_re, i_im, c_re, c_im, *, lane_chunk):
    t = pl.program_id(1)
    nt = pl.num_programs(1)
    tl, w = up.shape
    n_state = bre.shape[1]
    seg = tl // SUBLANES
    nblk = w // LANES
    spb = STATES_PER_BLOCK

    @pl.when(t == 0)
    def _():
        c_re[...] = jnp.zeros_like(c_re)
        c_im[...] = jnp.zeros_like(c_im)

    for c in range(nblk):
        slab[c] = u_ref[0, :, c * LANES:(c + 1) * LANES]
    _segments_to_sublanes(slab, up, seg)

    for c in range(nblk):
        r = _dot(up[:, c * LANES:(c + 1) * LANES].astype(BF16), wb_ref[c])
        bre[:, c * spb:(c + 1) * spb] = r[:, :spb]
        bim[:, c * spb:(c + 1) * spb] = r[:, spb:]

    n_chunks = n_state // lane_chunk

    for k in range(n_chunks):
        cols = slice(k * lane_chunk, (k + 1) * lane_chunk)
        ar = jnp.broadcast_to(are_ref[:, cols], (SUBLANES, lane_chunk))
        ai = jnp.broadcast_to(aim_ref[:, cols], (SUBLANES, lane_chunk))

        def p1(tau, hc, cols=cols, ar=ar, ai=ai):
            r = pl.multiple_of(tau * SUBLANES, SUBLANES)
            return _cplx_step(ar, ai, hc[0], hc[1], bre[pl.ds(r, SUBLANES), cols], bim[pl.ds(r, SUBLANES), cols])

        z = jnp.zeros((SUBLANES, lane_chunk), F32)
        er, ei = lax.fori_loop(0, seg, p1, (z, z), unroll=SCAN_UNROLL)
        e_re[:, cols] = er
        e_im[:, cols] = ei

    rr, ri = c_re[...], c_im[...]
    asr, asi = asr_ref[...], asi_ref[...]
    for j in range(SUBLANES):
        i_re[j:j + 1, :] = rr
        i_im[j:j + 1, :] = ri
        rr, ri = _cplx_step(asr, asi, rr, ri, e_re[j:j + 1, :], e_im[j:j + 1, :])
    c_re[...] = rr
    c_im[...] = ri

    for k in range(n_chunks):
        cols = slice(k * lane_chunk, (k + 1) * lane_chunk)
        ar = jnp.broadcast_to(are_ref[:, cols], (SUBLANES, lane_chunk))
        ai = jnp.broadcast_to(aim_ref[:, cols], (SUBLANES, lane_chunk))

        def p2(tau, hc, cols=cols, ar=ar, ai=ai):
            r = pl.multiple_of(tau * SUBLANES, SUBLANES)
            hr, hi = _cplx_step(ar, ai, hc[0], hc[1], bre[pl.ds(r, SUBLANES), cols], bim[pl.ds(r, SUBLANES), cols])
            bre[pl.ds(r, SUBLANES), cols] = hr
            bim[pl.ds(r, SUBLANES), cols] = hi
            return hr, hi

        lax.fori_loop(0, seg, p2, (i_re[:, cols], i_im[:, cols]), unroll=SCAN_UNROLL)

    for c in range(nblk):
        cols = slice(c * LANES, (c + 1) * LANES)
        scols = slice(c * spb, (c + 1) * spb)
        y = (_dot(bre[:, scols].astype(BF16), wc_ref[c, :spb, :])
             + _dot(bim[:, scols].astype(BF16), wc_ref[c, spb:, :]))
        ys = y + d_ref[:, cols] * up[:, cols]
        up[:, cols] = jax.nn.gelu(ys)

    _sublanes_to_segments(up, slab, seg)
    for c in range(nblk):
        gy_ref[0, :, c * LANES:(c + 1) * LANES] = slab[c].astype(BF16)

    @pl.when(t == nt - 1)
    def _():
        nre_ref[0] = rr
        nim_ref[0] = ri


def _s5_call(u3d, wb, wc, abr, abi, asr, asi, d, *, tl, w, lane_chunk=512):
    bsz, seq, _ = u3d.shape
    nblk = w // LANES
    n_state = nblk * STATES_PER_BLOCK
    est = (2 * tl * w * 4 + 2 * tl * w * 2 + 2 * tl * w * 4 + 2 * tl * n_state * 4
           + 2 * (wb.size + wc.size) * 2 + tl * 2 * STATES_PER_BLOCK * 4)
    c2 = lambda b, t: (0, 0)
    c3 = lambda b, t: (0, 0, 0)
    return pl.pallas_call(
        functools.partial(_s5_kernel, lane_chunk=lane_chunk),
        grid=(bsz, seq // tl),
        in_specs=[pl.BlockSpec((1, tl, w), lambda b, t: (b, t, 1)),
                  pl.BlockSpec(wb.shape, c3),
                  pl.BlockSpec(wc.shape, c3),
                  pl.BlockSpec((1, n_state), c2),
                  pl.BlockSpec((1, n_state), c2),
                  pl.BlockSpec((1, n_state), c2),
                  pl.BlockSpec((1, n_state), c2),
                  pl.BlockSpec((1, w), c2)],
        out_specs=[pl.BlockSpec((1, tl, w), lambda b, t: (b, t, 0)),
                   pl.BlockSpec((1, 1, n_state), lambda b, t: (b, 0, 0)),
                   pl.BlockSpec((1, 1, n_state), lambda b, t: (b, 0, 0))],
        out_shape=[jax.ShapeDtypeStruct((bsz, seq, w), BF16),
                   jax.ShapeDtypeStruct((bsz, 1, n_state), F32),
                   jax.ShapeDtypeStruct((bsz, 1, n_state), F32)],
        scratch_shapes=[pltpu.VMEM((nblk, tl, LANES), F32),
                        pltpu.VMEM((tl, w), F32),
                        pltpu.VMEM((tl, n_state), F32),
                        pltpu.VMEM((tl, n_state), F32),
                        pltpu.VMEM((SUBLANES, n_state), F32),
                        pltpu.VMEM((SUBLANES, n_state), F32),
                        pltpu.VMEM((SUBLANES, n_state), F32),
                        pltpu.VMEM((SUBLANES, n_state), F32),
                        pltpu.VMEM((1, n_state), F32),
                        pltpu.VMEM((1, n_state), F32)],
        compiler_params=pltpu.CompilerParams(dimension_semantics=("parallel", "arbitrary"),
                                             vmem_limit_bytes=_vmem_limit(est)),
        name="s5_scan",
    )(u3d, wb, wc, abr, abi, asr, asi, d)


def _s5_step_kernel(u_ref, h0r_ref, h0i_ref, wb_ref, wc_ref, are_ref, aim_ref, d_ref,
                    gy_ref, nre_ref, nim_ref):
    w = u_ref.shape[1]
    nblk = w // LANES
    spb = STATES_PER_BLOCK
    for c in range(nblk):
        cols = slice(c * LANES, (c + 1) * LANES)
        scols = slice(c * spb, (c + 1) * spb)
        u = u_ref[:, cols]
        r = _dot(u.astype(BF16), wb_ref[c])
        hr, hi = _cplx_step(are_ref[:, scols], aim_ref[:, scols], h0r_ref[:, scols], h0i_ref[:, scols],
                            r[:, :spb], r[:, spb:])
        nre_ref[:, scols] = hr
        nim_ref[:, scols] = hi
        y = _dot(hr.astype(BF16), wc_ref[c, :spb, :]) + _dot(hi.astype(BF16), wc_ref[c, spb:, :])
        gy_ref[:, cols] = jax.nn.gelu(y + d_ref[:, cols] * u).astype(BF16)


def _s5_step_call(u2d, h0r, h0i, wb, wc, abr, abi, d, *, w):
    n, n_state = h0r.shape
    c2 = lambda i: (0, 0)
    c3 = lambda i: (0, 0, 0)
    return pl.pallas_call(
        _s5_step_kernel,
        grid=(1,),
        in_specs=[pl.BlockSpec((n, w), lambda i: (0, 1)),
                  pl.BlockSpec((n, n_state), c2),
                  pl.BlockSpec((n, n_state), c2),
                  pl.BlockSpec(wb.shape, c3),
                  pl.BlockSpec(wc.shape, c3),
                  pl.BlockSpec((1, n_state), c2),
                  pl.BlockSpec((1, n_state), c2),
                  pl.BlockSpec((1, w), c2)],
        out_specs=[pl.BlockSpec((n, w), c2),
                   pl.BlockSpec((n, n_state), c2),
                   pl.BlockSpec((n, n_state), c2)],
        out_shape=[jax.ShapeDtypeStruct((n, w), BF16),
                   jax.ShapeDtypeStruct((n, n_state), F32),
                   jax.ShapeDtypeStruct((n, n_state), F32)],
        name="s5_step",
    )(u2d, h0r, h0i, wb, wc, abr, abi, d)


def _merge_kernel(hs_ref, gy_ref, sz_ref, x_ref, gt_ref, wp_ref, wg_ref, wo_ref, o_ref):
    d = x_ref.shape[1]
    ya = _dot(hs_ref[...], wp_ref[...])
    merged = sz_ref[:, :d].astype(F32) * ya
    glu = _dot(gy_ref[...], wg_ref[...])
    yb = glu[:, :d] * _sigmoid(glu[:, d:])
    merged = merged + sz_ref[:, d:].astype(F32) * yb
    o = _dot(merged.astype(BF16), wo_ref[...])
    o_ref[...] = x_ref[...] + _rows(gt_ref) * o


def _merge_call(hs, gy, sz, x2d, mod, wp, wg, wo, *, tm, tiles_per_batch):
    m, d = x2d.shape
    w = hs.shape[1]
    est = (2 * (2 * tm * w * 2 + tm * 2 * d * 2 + 2 * tm * d * 4) + (wp.size + wg.size + wo.size) * 2
           + tm * d * 4 * 5)
    one = pl.Buffered(1)
    return pl.pallas_call(
        _merge_kernel,
        grid=(m // tm,),
        in_specs=[pl.BlockSpec((tm, w), lambda i: (i, 0)),
                  pl.BlockSpec((tm, w), lambda i: (i, 0)),
                  pl.BlockSpec((tm, 2 * d), lambda i: (i, 0)),
                  pl.BlockSpec((tm, d), lambda i: (i, 0)),
                  _mod_spec(mod, 2, d, tiles_per_batch, 1),
                  pl.BlockSpec(wp.shape, lambda i: (0, 0), pipeline_mode=one),
                  pl.BlockSpec(wg.shape, lambda i: (0, 0), pipeline_mode=one),
                  pl.BlockSpec(wo.shape, lambda i: (0, 0), pipeline_mode=one)],
        out_specs=pl.BlockSpec((tm, d), lambda i: (i, 0)),
        out_shape=jax.ShapeDtypeStruct((m, d), F32),
        compiler_params=pltpu.CompilerParams(dimension_semantics=("parallel",),
                                             vmem_limit_bytes=_vmem_limit(est)),
        name="merge",
    )(hs, gy, sz, x2d, mod, wp, wg, wo)


def _mlp_kernel(x_ref, sc_ref, sh_ref, gt_ref, g2_ref, gf_ref, wu_ref, wd_ref, o_ref, *rest, final_norm):
    h_scr = rest[-1]
    j = pl.program_id(1)
    nj = pl.num_programs(1)

    n_rows = x_ref.shape[0]
    if len(rest) == 3:
        rest[0][...] = wu_ref[...].astype(BF16)
        rest[1][...] = wd_ref[...].astype(BF16)
        wu_ref, wd_ref = rest[0], rest[1]

    @pl.when(j == 0)
    def _():
        def chunk(rows):
            h = _rmsnorm(x_ref[rows, :], g2_ref[...])
            h_scr[rows, :] = (h * (1.0 + _rows(sc_ref, rows)) + _rows(sh_ref, rows)).astype(BF16)
            o_ref[rows, :] = jnp.zeros((h.shape[0], h.shape[1]), F32)

        _for_row_chunks(n_rows, chunk)

    up = _dot(h_scr[...], wu_ref[...])
    act = jnp.square(jnp.maximum(up, 0.0)).astype(BF16)
    d = o_ref.shape[1]
    nc = min(d, MLP_OUT_CHUNK)
    for c in range(d // nc):
        cols = slice(c * nc, (c + 1) * nc)
        o_ref[:, cols] += _dot(act, wd_ref[:, cols])

    @pl.when(j == nj - 1)
    def _():
        def chunk(rows):
            x2 = x_ref[rows, :] + _rows(gt_ref, rows) * o_ref[rows, :]
            o_ref[rows, :] = _rmsnorm(x2, gf_ref[...]) if final_norm else x2

        _for_row_chunks(n_rows, chunk)


def _mlp_call(x2d, mod, g2, gf, w_up, w_down, *, tm, tiles_per_batch, final_norm, tf=512):
    m, d = x2d.shape
    dff = w_up.shape[1]
    emit_w = w_up.dtype != BF16
    assert not emit_w or m == tm
    wbytes = w_up.dtype.itemsize
    est = (3 * tm * d * 4 + tm * d * 2 + 4 * d * tf * wbytes + tm * tf * 6 + tm * MLP_OUT_CHUNK * 4
           + (4 * d * tf * 2 if emit_w else 0))
    out_specs = [pl.BlockSpec((tm, d), lambda i, j: (i, 0))]
    out_shape = [jax.ShapeDtypeStruct((m, d), F32)]
    if emit_w:
        out_specs += [pl.BlockSpec((d, tf), lambda i, j: (0, j)), pl.BlockSpec((tf, d), lambda i, j: (j, 0))]
        out_shape += [jax.ShapeDtypeStruct((d, dff), BF16), jax.ShapeDtypeStruct((dff, d), BF16)]
    return pl.pallas_call(
        functools.partial(_mlp_kernel, final_norm=final_norm),
        grid=(m // tm, dff // tf),
        in_specs=[pl.BlockSpec((tm, d), lambda i, j: (i, 0), pipeline_mode=pl.Buffered(1)),
                  _mod_spec(mod, 4, d, tiles_per_batch, 2),
                  _mod_spec(mod, 3, d, tiles_per_batch, 2),
                  _mod_spec(mod, 5, d, tiles_per_batch, 2),
                  pl.BlockSpec((1, d), lambda i, j: (0, 0)),
                  pl.BlockSpec((1, d), lambda i, j: (0, 0)),
                  pl.BlockSpec((d, tf), lambda i, j: (0, j)),
                  pl.BlockSpec((tf, d), lambda i, j: (j, 0))],
        out_specs=out_specs,
        out_shape=out_shape,
        scratch_shapes=[pltpu.VMEM((tm, d), BF16)],
        compiler_params=pltpu.CompilerParams(dimension_semantics=("parallel", "arbitrary"),
                                             vmem_limit_bytes=_vmem_limit(est)),
        name="mlp",
    )(x2d, mod, mod, mod, g2, gf, w_up, w_down)


def _pick_tile(n, pref):
    t = min(n, pref)
    assert n % t == 0, (n, t)
    return t


def kernel(x_prompt, x_sample, state_conv, state_lru, state_ssm_re, state_ssm_im, c_prompt, c_sample, w_ada, b_ada, g_norm1, g_norm2, w_in, conv_w, conv_b, w_rg_a, b_rg_a, w_rg_x, b_rg_x, lru_lambda, w_proj_a, ssm_a_re, ssm_a_im, ssm_log_dt, ssm_b_re, ssm_b_im, ssm_c_re, ssm_c_im, ssm_d, w_glu, w_out, w_up, w_down, g_final):
    depth = w_ada.shape[0]
    nb, seq, d = x_prompt.shape
    ns = x_sample.shape[0]
    assert x_sample.shape[1] == 1
    w = conv_w.shape[2]
    n_state = ssm_a_re.shape[1] * ssm_a_re.shape[2]
    assert ssm_b_re.shape[2:] == (SSM_STATE, SSM_GROUP) and w % LANES == 0
    assert w_in.shape[2] == 2 * w + 2 * d

    tl = _pick_tile(seq, 512)
    seg = tl // SUBLANES
    tm_in = _pick_tile(seq, 1024)
    tm_mg = _pick_tile(seq, 256)
    tm_mlp = _pick_tile(seq, 1024)

    xp = x_prompt.reshape(nb * seq, d)
    xs = x_sample.reshape(ns, d)
    pad = (-nb) % SUBLANES
    c_all = jnp.concatenate([c_prompt, jnp.zeros((pad, d), F32), c_sample], axis=0)

    outs_p = [[] for _ in range(4)]
    outs_s = [[] for _ in range(4)]
    for l in range(depth):
        last = l == depth - 1
        mod = _mod_call(c_all, w_ada[l], b_ada[l].reshape(1, -1))
        mod_p = mod[:nb].reshape(nb, 1, N_MOD * d)
        mod_s = mod[nb + pad:]

        g1 = g_norm1[l].reshape(1, d)
        g2 = g_norm2[l].reshape(1, d)
        gf = g_final.reshape(1, d)
        cw, cb = conv_w[l], conv_b[l].reshape(1, w)
        wg = jnp.concatenate([w_rg_a[l], w_rg_x[l]], axis=2).astype(BF16)
        ba, bx, lam = b_rg_a[l].reshape(1, w), b_rg_x[l].reshape(1, w), lru_lambda[l].reshape(1, w)
        bbr, bbi, abr, abi, asr, asi = _s5_prep_call(ssm_a_re[l], ssm_a_im[l], ssm_log_dt[l],
                                                     ssm_b_re[l], ssm_b_im[l], seg)
        wb, wc = _s5_block_weights(bbr, bbi, ssm_c_re[l], ssm_c_im[l])
        abr, abi, asr, asi = (v.reshape(1, n_state) for v in (abr, abi, asr, asi))
        dskip = ssm_d[l].reshape(1, w)
        wp, wgl, wo = w_proj_a[l].astype(BF16), w_glu[l].astype(BF16), w_out[l].astype(BF16)

        u_s, sz_s, w_in_b = _inproj_call(xs, mod_s, g1, w_in[l], tm=ns, tiles_per_batch=1, w_mix=w)
        buf = jnp.transpose(state_conv[l], (1, 0, 2))
        hs_s, nconv_s, nlru_s = _lru_step_call(u_s, buf, state_lru[l], cw, cb, wg, ba, bx, lam)
        gy_s, nre_s, nim_s = _s5_step_call(u_s, state_ssm_re[l].reshape(ns, n_state),
                                           state_ssm_im[l].reshape(ns, n_state), wb, wc, abr, abi, dskip, w=w)
        x1_s = _merge_call(hs_s, gy_s, sz_s, xs, mod_s, wp, wgl, wo, tm=ns, tiles_per_batch=1)
        xs, w_up_b, w_down_b = _mlp_call(x1_s, mod_s, g2, gf, w_up[l], w_down[l], tm=ns, tiles_per_batch=1,
                                         final_norm=last)

        u_p, sz_p = _inproj_call(xp, mod_p, g1, w_in_b, tm=tm_in, tiles_per_batch=seq // tm_in, w_mix=w)
        u3 = u_p.reshape(nb, seq, 2 * w)
        hs_p, nconv_p, nlru_p = _lru_call(u3, cw, cb, wg, ba, bx, lam, tl=tl)
        gy_p, nre_p, nim_p = _s5_call(u3, wb, wc, abr, abi, asr, asi, dskip, tl=tl, w=w)
        x1_p = _merge_call(hs_p.reshape(nb * seq, w), gy_p.reshape(nb * seq, w), sz_p, xp, mod_p, wp, wgl, wo,
                           tm=tm_mg, tiles_per_batch=seq // tm_mg)
        (xp,) = _mlp_call(x1_p, mod_p, g2, gf, w_up_b, w_down_b, tm=tm_mlp, tiles_per_batch=seq // tm_mlp,
                          final_norm=last)

        gshape = ssm_a_re.shape[1:]
        for acc, v in zip(outs_p, (nconv_p, nlru_p.reshape(nb, w),
                                   nre_p.reshape((nb,) + gshape), nim_p.reshape((nb,) + gshape))):
            acc.append(v)
        for acc, v in zip(outs_s, (jnp.transpose(nconv_s, (1, 0, 2)), nlru_s,
                                   nre_s.reshape((ns,) + gshape), nim_s.reshape((ns,) + gshape))):
            acc.append(v)

    y_prompt = xp.reshape(nb, seq, d)
    y_sample = xs.reshape(ns, 1, d)
    return (y_prompt, y_sample) + tuple(jnp.stack(a) for a in outs_p) + tuple(jnp.stack(a) for a in outs_s)
```

```python
import functools

import jax
import jax.numpy as jnp
from jax import lax
from jax.experimental import pallas as pl
from jax.experimental.pallas import tpu as pltpu

F32 = jnp.float32
BF16 = jnp.bfloat16

LANES = 128
SUBLANES = 8
VMEM_PHYSICAL_BYTES = 64 * 1024 * 1024
VMEM_LIMIT_CAP_BYTES = VMEM_PHYSICAL_BYTES - 6 * 1024 * 1024

LRU_C = 8.0
EPS = 1e-6
N_MOD = 6
SSM_GROUP = 16
SSM_STATE = 64
GROUPS_PER_BLOCK = LANES // SSM_GROUP
STATES_PER_BLOCK = GROUPS_PER_BLOCK * SSM_STATE


VMEM_SLACK_BYTES = 8 * 1024 * 1024


def _vmem_limit(nbytes):
    return int(min(VMEM_LIMIT_CAP_BYTES, max(32 * 1024 * 1024, nbytes + VMEM_SLACK_BYTES)))


ROW_CHUNK = 256
MLP_OUT_CHUNK = 512
INPROJ_CHUNK = 256
SCAN_UNROLL = 4
S5_FOLD = 8


def _rows(ref, rows=None):
    if len(ref.shape) == 3:
        return ref[0]
    return ref[...] if rows is None else ref[rows, :]


def _for_row_chunks(n_rows, fn):
    rc = min(n_rows, ROW_CHUNK)
    assert n_rows % rc == 0
    if n_rows == rc:
        fn(slice(0, rc))
        return

    def body(i, carry):
        fn(pl.ds(pl.multiple_of(i * rc, rc), rc))
        return carry

    lax.fori_loop(0, n_rows // rc, body, 0)


def _dot(a, b):
    return jnp.dot(a, b, preferred_element_type=F32)


def _sigmoid(x):
    return 0.5 * jnp.tanh(0.5 * x) + 0.5


def _rmsnorm(x, g):
    ms = jnp.mean(x * x, axis=-1, keepdims=True)
    return (x * lax.rsqrt(ms + EPS)) * g


def _mod_kernel(c_ref, w_ref, b_ref, o_ref):
    c = c_ref[...]
    cs = (c * _sigmoid(c)).astype(BF16)
    o_ref[...] = _dot(cs, w_ref[...].astype(BF16)) + b_ref[...]


def _mod_call(c_all, w_ada, b_ada, tn=1024):
    n_rows, d = c_all.shape
    n = w_ada.shape[1]
    est = 2 * (d * tn * 4) + 2 * n_rows * tn * 4 + n_rows * d * 4 * 2 + d * tn * 2
    return pl.pallas_call(
        _mod_kernel,
        grid=(n // tn,),
        in_specs=[pl.BlockSpec((n_rows, d), lambda j: (0, 0)),
                  pl.BlockSpec((d, tn), lambda j: (0, j)),
                  pl.BlockSpec((1, tn), lambda j: (0, j))],
        out_specs=pl.BlockSpec((n_rows, tn), lambda j: (0, j)),
        out_shape=jax.ShapeDtypeStruct((n_rows, n), F32),
        compiler_params=pltpu.CompilerParams(dimension_semantics=("parallel",),
                                             vmem_limit_bytes=_vmem_limit(est)),
        name="mod",
    )(c_all, w_ada, b_ada)


def _mod_spec(mod, piece, d, tiles_per_batch, ngrid):
    if mod.ndim == 3:
        if ngrid == 1:
            return pl.BlockSpec((1, 1, d), lambda i: (i // tiles_per_batch, 0, piece))
        return pl.BlockSpec((1, 1, d), lambda i, j: (i // tiles_per_batch, 0, piece))
    rows = mod.shape[0]
    if ngrid == 1:
        return pl.BlockSpec((rows, d), lambda i: (0, piece))
    return pl.BlockSpec((rows, d), lambda i, j: (0, piece))


def _inproj_kernel(x_ref, sc_ref, sh_ref, g_ref, w_ref, u_ref, sz_ref, *rest, n_u_tiles):
    h_scr = rest[-1]
    j = pl.program_id(1)

    @pl.when(j == 0)
    def _():
        def chunk(rows):
            h = _rmsnorm(x_ref[rows, :], g_ref[...])
            h_scr[rows, :] = (h * (1.0 + _rows(sc_ref, rows)) + _rows(sh_ref, rows)).astype(BF16)

        _for_row_chunks(x_ref.shape[0], chunk)

    if len(rest) == 2:
        rest[0][...] = w_ref[...].astype(BF16)
        w_ref = rest[0]
    tn = w_ref.shape[1]
    nc = min(tn, INPROJ_CHUNK)

    @pl.when(j < n_u_tiles)
    def _():
        u_ref[...] = _dot(h_scr[...], w_ref[...])

    @pl.when(j >= n_u_tiles)
    def _():
        for c in range(tn // nc):
            cols = slice(c * nc, (c + 1) * nc)
            sz_ref[:, cols] = _sigmoid(_dot(h_scr[...], w_ref[:, cols])).astype(BF16)


def _inproj_call(x2d, mod, g1, w_in, *, tm, tiles_per_batch, w_mix, tn=1024):
    m, d = x2d.shape
    n = w_in.shape[1]
    n_u = 2 * w_mix
    n_u_tiles = n_u // tn
    emit_w = w_in.dtype != BF16
    assert not emit_w or m == tm
    wbytes = w_in.dtype.itemsize
    est = (2 * tm * d * 4 + 2 * d * tn * wbytes + 2 * tm * tn * 4 + 2 * tm * tn * 2 + tm * d * 2 + tm * tn * 4
           + (2 * d * tn * 2 if emit_w else 0))
    out_specs = [pl.BlockSpec((tm, tn), lambda i, j: (i, jnp.minimum(j, n_u_tiles - 1))),
                 pl.BlockSpec((tm, tn), lambda i, j: (i, jnp.maximum(j - n_u_tiles, 0)))]
    out_shape = [jax.ShapeDtypeStruct((m, n_u), F32),
                 jax.ShapeDtypeStruct((m, n - n_u), BF16)]
    if emit_w:
        out_specs.append(pl.BlockSpec((d, tn), lambda i, j: (0, j)))
        out_shape.append(jax.ShapeDtypeStruct((d, n), BF16))
    return pl.pallas_call(
        functools.partial(_inproj_kernel, n_u_tiles=n_u_tiles),
        grid=(m // tm, n // tn),
        in_specs=[pl.BlockSpec((tm, d), lambda i, j: (i, 0)),
                  _mod_spec(mod, 1, d, tiles_per_batch, 2),
                  _mod_spec(mod, 0, d, tiles_per_batch, 2),
                  pl.BlockSpec((1, d), lambda i, j: (0, 0)),
                  pl.BlockSpec((d, tn), lambda i, j: (0, j))],
        out_specs=out_specs,
        out_shape=out_shape,
        scratch_shapes=[pltpu.VMEM((tm, d), BF16)],
        compiler_params=pltpu.CompilerParams(dimension_semantics=("parallel", "arbitrary"),
                                             vmem_limit_bytes=_vmem_limit(est)),
        name="inproj",
    )(x2d, mod, mod, g1, w_in)


def _segments_to_sublanes(slab_ref, dst_ref, seg):
    nblk = slab_ref.shape[0]

    def body(tau, carry):
        r = pl.multiple_of(tau * SUBLANES, SUBLANES)
        for c in range(nblk):
            dst_ref[pl.ds(r, SUBLANES), c * LANES:(c + 1) * LANES] = (
                slab_ref[c, pl.ds(tau, SUBLANES, stride=seg), :])
        return carry

    lax.fori_loop(0, seg, body, 0, unroll=SCAN_UNROLL)


def _sublanes_to_segments(src_ref, slab_ref, seg):
    nblk = slab_ref.shape[0]

    def body(tau, carry):
        r = pl.multiple_of(tau * SUBLANES, SUBLANES)
        for c in range(nblk):
            slab_ref[c, pl.ds(tau, SUBLANES, stride=seg), :] = (
                src_ref[pl.ds(r, SUBLANES), c * LANES:(c + 1) * LANES])
        return carry

    lax.fori_loop(0, seg, body, 0, unroll=SCAN_UNROLL)


def _lru_gate_block(uc, g, ba, bx, sp):
    r = _sigmoid(g[:, :LANES] + ba)
    i = _sigmoid(g[:, LANES:] + bx)
    log_a = (-LRU_C * r) * sp
    a = jnp.exp(log_a)
    mult = jnp.sqrt(1.0 - jnp.exp(2.0 * log_a))
    return a, (mult * i) * uc


def _lru_kernel(u_ref, cw_ref, cb_ref, wg_ref, ba_ref, bx_ref, lam_ref,
                hs_ref, nconv_ref, nlru_ref,
                ext, slab, ucp, a_s, b_s, carry, e_s, p_s, i_s):
    t = pl.program_id(1)
    nt = pl.num_programs(1)
    tl, w = ucp.shape
    seg = tl // SUBLANES
    nblk = w // LANES
    hist = SUBLANES
    kw = cw_ref.shape[0]

    @pl.when(t == 0)
    def _():
        ext[0:hist, :] = jnp.zeros((hist, w), F32)
        carry[...] = jnp.zeros_like(carry)

    ext[hist:hist + tl, :] = u_ref[0]

    for c in range(nblk):
        cols = slice(c * LANES, (c + 1) * LANES)
        acc = cb_ref[:, cols] + ext[hist - kw + 1:hist - kw + 1 + tl, cols] * cw_ref[0:1, cols]
        for k in range(1, kw):
            acc = acc + ext[hist - kw + 1 + k:hist - kw + 1 + k + tl, cols] * cw_ref[k:k + 1, cols]
        slab[c] = acc

    tail = ext[hist + tl - (kw - 1):hist + tl, :]
    ext[hist - (kw - 1):hist, :] = tail

    @pl.when(t == nt - 1)
    def _():
        nconv_ref[0] = tail

    _segments_to_sublanes(slab, ucp, seg)

    sp = jax.nn.softplus(-lam_ref[...])
    for h in range(nblk):
        cols = slice(h * LANES, (h + 1) * LANES)
        uc = ucp[:, cols]
        g = _dot(uc.astype(BF16), wg_ref[h])
        a, b = _lru_gate_block(uc, g, ba_ref[:, cols], bx_ref[:, cols], sp[:, cols])
        a_s[:, cols] = a
        b_s[:, cols] = b

    def p1(tau, c):
        p, e = c
        r = pl.multiple_of(tau * SUBLANES, SUBLANES)
        at = a_s[pl.ds(r, SUBLANES), :]
        return at * p, at * e + b_s[pl.ds(r, SUBLANES), :]

    p_end, e_end = lax.fori_loop(0, seg, p1, (jnp.ones((SUBLANES, w), F32), jnp.zeros((SUBLANES, w), F32)),
                                 unroll=SCAN_UNROLL)
    p_s[...] = p_end
    e_s[...] = e_end

    row = carry[...]
    for j in range(SUBLANES):
        i_s[j:j + 1, :] = row
        row = p_s[j:j + 1, :] * row + e_s[j:j + 1, :]
    carry[...] = row

    def p2(tau, hcur):
        r = pl.multiple_of(tau * SUBLANES, SUBLANES)
        hcur = a_s[pl.ds(r, SUBLANES), :] * hcur + b_s[pl.ds(r, SUBLANES), :]
        b_s[pl.ds(r, SUBLANES), :] = hcur
        return hcur

    lax.fori_loop(0, seg, p2, i_s[...], unroll=SCAN_UNROLL)

    _sublanes_to_segments(b_s, slab, seg)
    for c in range(nblk):
        hs_ref[0, :, c * LANES:(c + 1) * LANES] = slab[c].astype(BF16)

    @pl.when(t == nt - 1)
    def _():
        nlru_ref[0] = row


def _lru_call(u3d, cw, cb, wg, ba, bx, lam, *, tl):
    bsz, seq, _ = u3d.shape
    w = cw.shape[1]
    kw = cw.shape[0]
    nblk = w // LANES
    est = (2 * tl * w * 4 + 2 * tl * w * 2 + (tl + SUBLANES) * w * 4 + 4 * tl * w * 4
           + nblk * LANES * 2 * LANES * 2 * 2)
    const2 = lambda b, t: (0, 0)
    return pl.pallas_call(
        _lru_kernel,
        grid=(bsz, seq // tl),
        in_specs=[pl.BlockSpec((1, tl, w), lambda b, t: (b, t, 0)),
                  pl.BlockSpec((kw, w), const2),
                  pl.BlockSpec((1, w), const2),
                  pl.BlockSpec((nblk, LANES, 2 * LANES), lambda b, t: (0, 0, 0)),
                  pl.BlockSpec((1, w), const2),
                  pl.BlockSpec((1, w), const2),
                  pl.BlockSpec((1, w), const2)],
        out_specs=[pl.BlockSpec((1, tl, w), lambda b, t: (b, t, 0)),
                   pl.BlockSpec((1, kw - 1, w), lambda b, t: (b, 0, 0)),
                   pl.BlockSpec((1, 1, w), lambda b, t: (b, 0, 0))],
        out_shape=[jax.ShapeDtypeStruct((bsz, seq, w), BF16),
                   jax.ShapeDtypeStruct((bsz, kw - 1, w), F32),
                   jax.ShapeDtypeStruct((bsz, 1, w), F32)],
        scratch_shapes=[pltpu.VMEM((tl + SUBLANES, w), F32),
                        pltpu.VMEM((nblk, tl, LANES), F32),
                        pltpu.VMEM((tl, w), F32),
                        pltpu.VMEM((tl, w), F32),
                        pltpu.VMEM((tl, w), F32),
                        pltpu.VMEM((1, w), F32),
                        pltpu.VMEM((SUBLANES, w), F32),
                        pltpu.VMEM((SUBLANES, w), F32),
                        pltpu.VMEM((SUBLANES, w), F32)],
        compiler_params=pltpu.CompilerParams(dimension_semantics=("parallel", "arbitrary"),
                                             vmem_limit_bytes=_vmem_limit(est)),
        name="lru_scan",
    )(u3d, cw, cb, wg, ba, bx, lam)


def _lru_step_kernel(u_ref, buf_ref, h0_ref, cw_ref, cb_ref, wg_ref, ba_ref, bx_ref, lam_ref,
                     hs_ref, nconv_ref, nlru_ref):
    w = cw_ref.shape[1]
    kw = cw_ref.shape[0]
    nblk = w // LANES
    u = u_ref[...]
    uc = cb_ref[...] + buf_ref[0] * cw_ref[0:1, :]
    for k in range(1, kw - 1):
        uc = uc + buf_ref[k] * cw_ref[k:k + 1, :]
    uc = uc + u * cw_ref[kw - 1:kw, :]
    for k in range(kw - 2):
        nconv_ref[k] = buf_ref[k + 1]
    nconv_ref[kw - 2] = u
    sp = jax.nn.softplus(-lam_ref[...])
    for h in range(nblk):
        cols = slice(h * LANES, (h + 1) * LANES)
        uch = uc[:, cols]
        g = _dot(uch.astype(BF16), wg_ref[h])
        a, b = _lru_gate_block(uch, g, ba_ref[:, cols], bx_ref[:, cols], sp[:, cols])
        hn = a * h0_ref[:, cols] + b
        nlru_ref[:, cols] = hn
        hs_ref[:, cols] = hn.astype(BF16)


def _lru_step_call(u2d, buf, h0, cw, cb, wg, ba, bx, lam):
    n = h0.shape[0]
    w = cw.shape[1]
    kw = cw.shape[0]
    nblk = w // LANES
    c2 = lambda i: (0, 0)
    c3 = lambda i: (0, 0, 0)
    return pl.pallas_call(
        _lru_step_kernel,
        grid=(1,),
        in_specs=[pl.BlockSpec((n, w), c2),
                  pl.BlockSpec((kw - 1, n, w), c3),
                  pl.BlockSpec((n, w), c2),
                  pl.BlockSpec((kw, w), c2),
                  pl.BlockSpec((1, w), c2),
                  pl.BlockSpec((nblk, LANES, 2 * LANES), c3),
                  pl.BlockSpec((1, w), c2),
                  pl.BlockSpec((1, w), c2),
                  pl.BlockSpec((1, w), c2)],
        out_specs=[pl.BlockSpec((n, w), c2),
                   pl.BlockSpec((kw - 1, n, w), c3),
                   pl.BlockSpec((n, w), c2)],
        out_shape=[jax.ShapeDtypeStruct((n, w), BF16),
                   jax.ShapeDtypeStruct((kw - 1, n, w), F32),
                   jax.ShapeDtypeStruct((n, w), F32)],
        name="lru_step",
    )(u2d, buf, h0, cw, cb, wg, ba, bx, lam)


def _s5_prep_kernel(arx_ref, aix_ref, dt_ref, br_ref, bi_ref, ar_ref, ai_ref,
                    bbr_ref, bbi_ref, pwr_ref, pwi_ref):
    dt = jnp.exp(dt_ref[...])

    def abar(ar, ai):
        mag = jnp.exp(dt * ar)
        ang = dt * ai
        return mag * jnp.cos(ang), mag * jnp.sin(ang)

    ar, ai = arx_ref[...], aix_ref[...]
    abr, abi = abar(ar, ai)
    den = ar * ar + ai * ai
    nr, ni = abr - 1.0, abi
    q_re = (nr * ar + ni * ai) / den
    q_im = (ni * ar - nr * ai) / den
    br, bi = br_ref[...], bi_ref[...]
    bbr_ref[...] = q_re * br - q_im * bi
    bbi_ref[...] = q_re * bi + q_im * br

    pr, pi = abar(ar_ref[...], ai_ref[...])
    qr, qi = jnp.ones_like(pr), jnp.zeros_like(pr)
    for m in range(pwr_ref.shape[0]):
        pwr_ref[m] = qr
        pwi_ref[m] = qi
        qr, qi = qr * pr - qi * pi, qr * pi + qi * pr


def _s5_prep_call(a_re, a_im, log_dt, b_re, b_im, n_pow):
    g, p, k = b_re.shape
    arx = jnp.repeat(a_re, k, axis=1)
    aix = jnp.repeat(a_im, k, axis=1)
    bbr, bbi, pwr, pwi = pl.pallas_call(
        _s5_prep_kernel,
        out_shape=[jax.ShapeDtypeStruct((g, p * k), F32), jax.ShapeDtypeStruct((g, p * k), F32),
                   jax.ShapeDtypeStruct((n_pow, g, p), F32), jax.ShapeDtypeStruct((n_pow, g, p), F32)],
        name="s5_prep",
    )(arx, aix, log_dt.reshape(g, 1), b_re.reshape(g, p * k), b_im.reshape(g, p * k), a_re, a_im)
    return bbr.reshape(g, p, k), bbi.reshape(g, p, k), pwr, pwi


def _s5_block_weights(bb_re, bb_im, c_re, c_im):
    g, p, k = bb_re.shape
    gb = GROUPS_PER_BLOCK
    nb = g // gb
    eye = jnp.eye(gb, dtype=F32)

    def in_blocks(bb):
        t = bb.reshape(nb, gb, p, k).transpose(0, 1, 3, 2)
        return jnp.einsum('cgkp,gh->cgkhp', t, eye).reshape(nb, gb * k, gb * p)

    def out_blocks(cc):
        t = cc.reshape(nb, gb, k, p)
        return jnp.einsum('cgkp,hg->chpgk', t, eye).reshape(nb, gb * p, gb * k)

    bblk = jnp.concatenate([in_blocks(bb_re), in_blocks(bb_im)], axis=2)
    return bblk, out_blocks(c_re), out_blocks(c_im)


def _cplx_mul(ar, ai, br, bi):
    return ar * br - ai * bi, ar * bi + ai * br


def _cplx_step(ar, ai, hr, hi, br, bi):
    return ar * hr - ai * hi + br, ar * hi + ai * hr + bi


def _s5_kernel(u_ref, bblk_ref, creb_ref, cimb_ref, pwr_ref, pwi_ref, pcr_ref, pci_ref, d_ref,
               gy_ref, nre_ref, nim_ref,
               ufb, buf, wbs, wqs, wts, yslab, out32):
    nbat, seq, _ = u_ref.shape
    spb = STATES_PER_BLOCK
    fold = wbs.shape[0] // LANES
    nq = seq // fold

    bre, bim = bblk_ref[0, :, :spb], bblk_ref[0, :, spb:]
    creb, cimb = creb_ref[0], cimb_ref[0]
    wcb = jnp.concatenate([creb, -cimb], axis=0).astype(BF16)
    lag = []
    for m in range(fold):
        wre, wim = _cplx_mul(bre, bim, pwr_ref[0, m:m + 1, :], pwi_ref[0, m:m + 1, :])
        rows = slice((fold - 1 - m) * LANES, (fold - m) * LANES)
        wbs[rows, :spb] = wre.astype(BF16)
        wbs[rows, spb:] = wim.astype(BF16)
        lag.append(_dot(wbs[rows, :], wcb))
    zero_blk = jnp.zeros((LANES, LANES), BF16)
    for r_in in range(fold):
        for r_out in range(fold):
            blk = lag[r_out - r_in].astype(BF16) if r_in <= r_out else zero_blk
            wts[r_in * LANES:(r_in + 1) * LANES, r_out * LANES:(r_out + 1) * LANES] = blk
    for r in range(fold):
        pr, pi = pcr_ref[0, r + 1], pci_ref[0, r + 1]
        cols = slice(r * LANES, (r + 1) * LANES)
        wqs[:spb, cols] = (creb * pr - cimb * pi).astype(BF16)
        wqs[spb:, cols] = (-(creb * pi + cimb * pr)).astype(BF16)

    for r in range(fold):
        for b in range(nbat):
            yslab[r, pl.ds(b, nq, stride=nbat), :] = u_ref[b, pl.ds(r, nq, stride=fold), :]
        ufb[:, r * LANES:(r + 1) * LANES] = yslab[r].astype(BF16)

    buf[...] = _dot(ufb[...], wbs[...])

    ar, ai = pwr_ref[0, fold:fold + 1, :], pwi_ref[0, fold:fold + 1, :]
    groups_per_tile = SUBLANES // nbat
    z = jnp.zeros((nbat, spb), F32)

    def tile_step(i, g):
        gr, gi = g
        r0 = pl.multiple_of(i * SUBLANES, SUBLANES)
        xr, xi = buf[pl.ds(r0, SUBLANES), :spb], buf[pl.ds(r0, SUBLANES), spb:]
        hr, hi = [], []
        for k in range(groups_per_tile):
            hr.append(gr)
            hi.append(gi)
            gr, gi = _cplx_step(ar, ai, gr, gi, xr[k * nbat:(k + 1) * nbat], xi[k * nbat:(k + 1) * nbat])
        buf[pl.ds(r0, SUBLANES), :spb] = jnp.concatenate(hr, axis=0)
        buf[pl.ds(r0, SUBLANES), spb:] = jnp.concatenate(hi, axis=0)
        return gr, gi

    g_re, g_im = lax.fori_loop(0, nq // groups_per_tile, tile_step, (z, z), unroll=2)
    for b in range(nbat):
        nre_ref[b] = g_re[b:b + 1, :]
        nim_ref[b] = g_im[b:b + 1, :]

    hsb = buf[...].astype(BF16)
    per_tile = 2
    for n in range(fold // per_tile):
        cols = slice(n * per_tile * LANES, (n + 1) * per_tile * LANES)
        kk = (n + 1) * per_tile * LANES
        yn = _dot(hsb, wqs[:, cols]) + _dot(ufb[:, :kk], wts[:kk, cols])
        for i in range(per_tile):
            yslab[n * per_tile + i] = yn[:, i * LANES:(i + 1) * LANES]

    for b in range(nbat):
        for r in range(fold):
            ys = (yslab[r, pl.ds(b, nq, stride=nbat), :]
                  + d_ref[...] * u_ref[b, pl.ds(r, nq, stride=fold), :])
            out32[b, pl.ds(r, nq, stride=fold), :] = jax.nn.gelu(ys)
        gy_ref[b] = out32[b].astype(BF16)


def _s5_call(u3d, bblk, creb, cimb, pw_row_re, pw_row_im, pw_col_re, pw_col_im, d, *, w):
    bsz, seq, _ = u3d.shape
    nblk = w // LANES
    spb = STATES_PER_BLOCK
    fold = S5_FOLD
    n_pow = fold + 1
    rows = bsz * (seq // fold)
    lane_padded_col = n_pow * spb * LANES * 4
    est = (2 * bsz * seq * LANES * 4 + 2 * bsz * seq * LANES * 2
           + rows * fold * LANES * 2 + 2 * rows * 2 * spb * 4
           + 3 * fold * LANES * 2 * spb * 2
           + rows * fold * LANES * 4 + bsz * seq * LANES * 4
           + rows * 2 * spb * 2
           + 4 * lane_padded_col + 2 * (LANES * 2 * spb + 2 * spb * LANES) * 4)
    return pl.pallas_call(
        _s5_kernel,
        grid=(nblk,),
        in_specs=[pl.BlockSpec((bsz, seq, LANES), lambda c: (0, 0, nblk + c)),
                  pl.BlockSpec((1, LANES, 2 * spb), lambda c: (c, 0, 0)),
                  pl.BlockSpec((1, spb, LANES), lambda c: (c, 0, 0)),
                  pl.BlockSpec((1, spb, LANES), lambda c: (c, 0, 0)),
                  pl.BlockSpec((1, n_pow, spb), lambda c: (c, 0, 0)),
                  pl.BlockSpec((1, n_pow, spb), lambda c: (c, 0, 0)),
                  pl.BlockSpec((1, n_pow, spb, 1), lambda c: (c, 0, 0, 0)),
                  pl.BlockSpec((1, n_pow, spb, 1), lambda c: (c, 0, 0, 0)),
                  pl.BlockSpec((1, LANES), lambda c: (0, c))],
        out_specs=[pl.BlockSpec((bsz, seq, LANES), lambda c: (0, 0, c)),
                   pl.BlockSpec((bsz, 1, spb), lambda c: (0, 0, c)),
                   pl.BlockSpec((bsz, 1, spb), lambda c: (0, 0, c))],
        out_shape=[jax.ShapeDtypeStruct((bsz, seq, w), BF16),
                   jax.ShapeDtypeStruct((bsz, 1, nblk * spb), F32),
                   jax.ShapeDtypeStruct((bsz, 1, nblk * spb), F32)],
        scratch_shapes=[pltpu.VMEM((rows, fold * LANES), BF16),
                        pltpu.VMEM((rows, 2 * spb), F32),
                        pltpu.VMEM((fold * LANES, 2 * spb), BF16),
                        pltpu.VMEM((2 * spb, fold * LANES), BF16),
                        pltpu.VMEM((fold * LANES, fold * LANES), BF16),
                        pltpu.VMEM((fold, rows, LANES), F32),
                        pltpu.VMEM((bsz, seq, LANES), F32)],
        compiler_params=pltpu.CompilerParams(dimension_semantics=("parallel",),
                                             vmem_limit_bytes=_vmem_limit(est)),
        name="s5_scan",
    )(u3d, bblk, creb, cimb, pw_row_re, pw_row_im, pw_col_re, pw_col_im, d)


def _s5_step_kernel(u_ref, h0r_ref, h0i_ref, wb_ref, wc_ref, are_ref, aim_ref, d_ref,
                    gy_ref, nre_ref, nim_ref):
    w = u_ref.shape[1]
    nblk = w // LANES
    spb = STATES_PER_BLOCK
    for c in range(nblk):
        cols = slice(c * LANES, (c + 1) * LANES)
        scols = slice(c * spb, (c + 1) * spb)
        u = u_ref[:, cols]
        r = _dot(u.astype(BF16), wb_ref[c])
        hr, hi = _cplx_step(are_ref[:, scols], aim_ref[:, scols], h0r_ref[:, scols], h0i_ref[:, scols],
                            r[:, :spb], r[:, spb:])
        nre_ref[:, scols] = hr
        nim_ref[:, scols] = hi
        y = _dot(hr.astype(BF16), wc_ref[c, :spb, :]) + _dot(hi.astype(BF16), wc_ref[c, spb:, :])
        gy_ref[:, cols] = jax.nn.gelu(y + d_ref[:, cols] * u).astype(BF16)


def _s5_step_call(u2d, h0r, h0i, wb, wc, abr, abi, d, *, w):
    n, n_state = h0r.shape
    c2 = lambda i: (0, 0)
    c3 = lambda i: (0, 0, 0)
    return pl.pallas_call(
        _s5_step_kernel,
        grid=(1,),
        in_specs=[pl.BlockSpec((n, w), lambda i: (0, 1)),
                  pl.BlockSpec((n, n_state), c2),
                  pl.BlockSpec((n, n_state), c2),
                  pl.BlockSpec(wb.shape, c3),
                  pl.BlockSpec(wc.shape, c3),
                  pl.BlockSpec((1, n_state), c2),
                  pl.BlockSpec((1, n_state), c2),
                  pl.BlockSpec((1, w), c2)],
        out_specs=[pl.BlockSpec((n, w), c2),
                   pl.BlockSpec((n, n_state), c2),
                   pl.BlockSpec((n, n_state), c2)],
        out_shape=[jax.ShapeDtypeStruct((n, w), BF16),
                   jax.ShapeDtypeStruct((n, n_state), F32),
                   jax.ShapeDtypeStruct((n, n_state), F32)],
        name="s5_step",
    )(u2d, h0r, h0i, wb, wc, abr, abi, d)


def _merge_kernel(hs_ref, gy_ref, sz_ref, x_ref, gt_ref, wp_ref, wg_ref, wo_ref, o_ref):
    d = x_ref.shape[1]
    ya = _dot(hs_ref[...], wp_ref[...])
    merged = sz_ref[:, :d].astype(F32) * ya
    glu = _dot(gy_ref[...], wg_ref[...])
    yb = glu[:, :d] * _sigmoid(glu[:, d:])
    merged = merged + sz_ref[:, d:].astype(F32) * yb
    o = _dot(merged.astype(BF16), wo_ref[...])
    o_ref[...] = x_ref[...] + _rows(gt_ref) * o


def _merge_call(hs, gy, sz, x2d, mod, wp, wg, wo, *, tm, tiles_per_batch):
    m, d = x2d.shape
    w = hs.shape[1]
    est = (2 * (2 * tm * w * 2 + tm * 2 * d * 2 + 2 * tm * d * 4) + (wp.size + wg.size + wo.size) * 2
           + tm * d * 4 * 5)
    one = pl.Buffered(1)
    return pl.pallas_call(
        _merge_kernel,
        grid=(m // tm,),
        in_specs=[pl.BlockSpec((tm, w), lambda i: (i, 0)),
                  pl.BlockSpec((tm, w), lambda i: (i, 0)),
                  pl.BlockSpec((tm, 2 * d), lambda i: (i, 0)),
                  pl.BlockSpec((tm, d), lambda i: (i, 0)),
                  _mod_spec(mod, 2, d, tiles_per_batch, 1),
                  pl.BlockSpec(wp.shape, lambda i: (0, 0), pipeline_mode=one),
                  pl.BlockSpec(wg.shape, lambda i: (0, 0), pipeline_mode=one),
                  pl.BlockSpec(wo.shape, lambda i: (0, 0), pipeline_mode=one)],
        out_specs=pl.BlockSpec((tm, d), lambda i: (i, 0)),
        out_shape=jax.ShapeDtypeStruct((m, d), F32),
        compiler_params=pltpu.CompilerParams(dimension_semantics=("parallel",),
                                             vmem_limit_bytes=_vmem_limit(est)),
        name="merge",
    )(hs, gy, sz, x2d, mod, wp, wg, wo)


def _mlp_kernel(x_ref, sc_ref, sh_ref, gt_ref, g2_ref, gf_ref, wu_ref, wd_ref, o_ref, *rest, final_norm):
    h_scr = rest[-1]
    j = pl.program_id(1)
    nj = pl.num_programs(1)

    n_rows = x_ref.shape[0]
    if len(rest) == 3:
        rest[0][...] = wu_ref[...].astype(BF16)
        rest[1][...] = wd_ref[...].astype(BF16)
        wu_ref, wd_ref = rest[0], rest[1]

    @pl.when(j == 0)
    def _():
        def chunk(rows):
            h = _rmsnorm(x_ref[rows, :], g2_ref[...])
            h_scr[rows, :] = (h * (1.0 + _rows(sc_ref, rows)) + _rows(sh_ref, rows)).astype(BF16)
            o_ref[rows, :] = jnp.zeros((h.shape[0], h.shape[1]), F32)

        _for_row_chunks(n_rows, chunk)

    up = _dot(h_scr[...], wu_ref[...])
    act = jnp.square(jnp.maximum(up, 0.0)).astype(BF16)
    d = o_ref.shape[1]
    nc = min(d, MLP_OUT_CHUNK)
    for c in range(d // nc):
        cols = slice(c * nc, (c + 1) * nc)
        o_ref[:, cols] += _dot(act, wd_ref[:, cols])

    @pl.when(j == nj - 1)
    def _():
        def chunk(rows):
            x2 = x_ref[rows, :] + _rows(gt_ref, rows) * o_ref[rows, :]
            o_ref[rows, :] = _rmsnorm(x2, gf_ref[...]) if final_norm else x2

        _for_row_chunks(n_rows, chunk)


def _mlp_call(x2d, mod, g2, gf, w_up, w_down, *, tm, tiles_per_batch, final_norm, tf=512):
    m, d = x2d.shape
    dff = w_up.shape[1]
    emit_w = w_up.dtype != BF16
    assert not emit_w or m == tm
    wbytes = w_up.dtype.itemsize
    est = (3 * tm * d * 4 + tm * d * 2 + 4 * d * tf * wbytes + tm * tf * 6 + tm * MLP_OUT_CHUNK * 4
           + (4 * d * tf * 2 if emit_w else 0))
    out_specs = [pl.BlockSpec((tm, d), lambda i, j: (i, 0))]
    out_shape = [jax.ShapeDtypeStruct((m, d), F32)]
    if emit_w:
        out_specs += [pl.BlockSpec((d, tf), lambda i, j: (0, j)), pl.BlockSpec((tf, d), lambda i, j: (j, 0))]
        out_shape += [jax.ShapeDtypeStruct((d, dff), BF16), jax.ShapeDtypeStruct((dff, d), BF16)]
    return pl.pallas_call(
        functools.partial(_mlp_kernel, final_norm=final_norm),
        grid=(m // tm, dff // tf),
        in_specs=[pl.BlockSpec((tm, d), lambda i, j: (i, 0), pipeline_mode=pl.Buffered(1)),
                  _mod_spec(mod, 4, d, tiles_per_batch, 2),
                  _mod_spec(mod, 3, d, tiles_per_batch, 2),
                  _mod_spec(mod, 5, d, tiles_per_batch, 2),
                  pl.BlockSpec((1, d), lambda i, j: (0, 0)),
                  pl.BlockSpec((1, d), lambda i, j: (0, 0)),
                  pl.BlockSpec((d, tf), lambda i, j: (0, j)),
                  pl.BlockSpec((tf, d), lambda i, j: (j, 0))],
        out_specs=out_specs,
        out_shape=out_shape,
        scratch_shapes=[pltpu.VMEM((tm, d), BF16)],
        compiler_params=pltpu.CompilerParams(dimension_semantics=("parallel", "arbitrary"),
                                             vmem_limit_bytes=_vmem_limit(est)),
        name="mlp",
    )(x2d, mod, mod, mod, g2, gf, w_up, w_down)


def _pick_tile(n, pref):
    t = min(n, pref)
    assert n % t == 0, (n, t)
    return t


def kernel(x_prompt, x_sample, state_conv, state_lru, state_ssm_re, state_ssm_im, c_prompt, c_sample, w_ada, b_ada, g_norm1, g_norm2, w_in, conv_w, conv_b, w_rg_a, b_rg_a, w_rg_x, b_rg_x, lru_lambda, w_proj_a, ssm_a_re, ssm_a_im, ssm_log_dt, ssm_b_re, ssm_b_im, ssm_c_re, ssm_c_im, ssm_d, w_glu, w_out, w_up, w_down, g_final):
    depth = w_ada.shape[0]
    nb, seq, d = x_prompt.shape
    ns = x_sample.shape[0]
    assert x_sample.shape[1] == 1
    w = conv_w.shape[2]
    n_state = ssm_a_re.shape[1] * ssm_a_re.shape[2]
    assert ssm_b_re.shape[2:] == (SSM_STATE, SSM_GROUP) and w % LANES == 0
    assert w_in.shape[2] == 2 * w + 2 * d

    tl = _pick_tile(seq, 512)
    seg = tl // SUBLANES
    tm_in = _pick_tile(seq, 1024)
    tm_mg = _pick_tile(seq, 256)
    tm_mlp = _pick_tile(seq, 1024)

    xp = x_prompt.reshape(nb * seq, d)
    xs = x_sample.reshape(ns, d)
    pad = (-nb) % SUBLANES
    c_all = jnp.concatenate([c_prompt, jnp.zeros((pad, d), F32), c_sample], axis=0)

    outs_p = [[] for _ in range(4)]
    outs_s = [[] for _ in range(4)]
    for l in range(depth):
        last = l == depth - 1
        mod = _mod_call(c_all, w_ada[l], b_ada[l].reshape(1, -1))
        mod_p = mod[:nb].reshape(nb, 1, N_MOD * d)
        mod_s = mod[nb + pad:]

        g1 = g_norm1[l].reshape(1, d)
        g2 = g_norm2[l].reshape(1, d)
        gf = g_final.reshape(1, d)
        cw, cb = conv_w[l], conv_b[l].reshape(1, w)
        wg = jnp.concatenate([w_rg_a[l], w_rg_x[l]], axis=2).astype(BF16)
        ba, bx, lam = b_rg_a[l].reshape(1, w), b_rg_x[l].reshape(1, w), lru_lambda[l].reshape(1, w)
        n_pow = S5_FOLD + 1
        bbr, bbi, pwr, pwi = _s5_prep_call(ssm_a_re[l], ssm_a_im[l], ssm_log_dt[l], ssm_b_re[l], ssm_b_im[l], n_pow)
        bblk, creb, cimb = _s5_block_weights(bbr, bbi, ssm_c_re[l], ssm_c_im[l])
        wb = bblk.astype(BF16)
        wc = jnp.concatenate([creb, -cimb], axis=1).astype(BF16)
        abr, abi = pwr[1].reshape(1, n_state), pwi[1].reshape(1, n_state)
        nblk = w // LANES
        pw_rows = [v.reshape(n_pow, nblk, STATES_PER_BLOCK).transpose(1, 0, 2) for v in (pwr, pwi)]
        pw_cols = [v[..., None] for v in pw_rows]
        dskip = ssm_d[l].reshape(1, w)
        wp, wgl, wo = w_proj_a[l].astype(BF16), w_glu[l].astype(BF16), w_out[l].astype(BF16)

        u_s, sz_s, w_in_b = _inproj_call(xs, mod_s, g1, w_in[l], tm=ns, tiles_per_batch=1, w_mix=w)
        buf = jnp.transpose(state_conv[l], (1, 0, 2))
        hs_s, nconv_s, nlru_s = _lru_step_call(u_s, buf, state_lru[l], cw, cb, wg, ba, bx, lam)
        gy_s, nre_s, nim_s = _s5_step_call(u_s, state_ssm_re[l].reshape(ns, n_state),
                                           state_ssm_im[l].reshape(ns, n_state), wb, wc, abr, abi, dskip, w=w)
        x1_s = _merge_call(hs_s, gy_s, sz_s, xs, mod_s, wp, wgl, wo, tm=ns, tiles_per_batch=1)
        xs, w_up_b, w_down_b = _mlp_call(x1_s, mod_s, g2, gf, w_up[l], w_down[l], tm=ns, tiles_per_batch=1,
                                         final_norm=last)

        u_p, sz_p = _inproj_call(xp, mod_p, g1, w_in_b, tm=tm_in, tiles_per_batch=seq // tm_in, w_mix=w)
        u3 = u_p.reshape(nb, seq, 2 * w)
        hs_p, nconv_p, nlru_p = _lru_call(u3, cw, cb, wg, ba, bx, lam, tl=tl)
        gy_p, nre_p, nim_p = _s5_call(u3, bblk, creb, cimb, *pw_rows, *pw_cols, dskip, w=w)
        x1_p = _merge_call(hs_p.reshape(nb * seq, w), gy_p.reshape(nb * seq, w), sz_p, xp, mod_p, wp, wgl, wo,
                           tm=tm_mg, tiles_per_batch=seq // tm_mg)
        (xp,) = _mlp_call(x1_p, mod_p, g2, gf, w_up_b, w_down_b, tm=tm_mlp, tiles_per_batch=seq // tm_mlp,
                          final_norm=last)

        gshape = ssm_a_re.shape[1:]
        for acc, v in zip(outs_p, (nconv_p, nlru_p.reshape(nb, w),
                                   nre_p.reshape((nb,) + gshape), nim_p.reshape((nb,) + gshape))):
            acc.append(v)
        for acc, v in zip(outs_s, (jnp.transpose(nconv_s, (1, 0, 2)), nlru_s,
                                   nre_s.reshape((ns,) + gshape), nim_s.reshape((ns,) + gshape))):
            acc.append(v)

    y_prompt = xp.reshape(nb, seq, d)
    y_sample = xs.reshape(ns, 1, d)
    return (y_prompt, y_sample) + tuple(jnp.stack(a) for a in outs_p) + tuple(jnp.stack(a) for a in outs_s)
```

```python
import functools

import jax
import jax.numpy as jnp
from jax import lax
from jax.experimental import pallas as pl
from jax.experimental.pallas import tpu as pltpu

F32 = jnp.float32
BF16 = jnp.bfloat16

LANES = 128
SUBLANES = 8
VMEM_PHYSICAL_BYTES = 64 * 1024 * 1024
VMEM_LIMIT_CAP_BYTES = VMEM_PHYSICAL_BYTES - 6 * 1024 * 1024

LRU_C = 8.0
EPS = 1e-6
N_MOD = 6
SSM_GROUP = 16
SSM_STATE = 64
GROUPS_PER_BLOCK = LANES // SSM_GROUP
STATES_PER_BLOCK = GROUPS_PER_BLOCK * SSM_STATE


VMEM_SLACK_BYTES = 8 * 1024 * 1024


def _vmem_limit(nbytes):
    return int(min(VMEM_LIMIT_CAP_BYTES, max(32 * 1024 * 1024, nbytes + VMEM_SLACK_BYTES)))


ROW_CHUNK = 256
MLP_OUT_CHUNK = 512
INPROJ_CHUNK = 256
S5_FOLD = 8


def _rows(ref, rows=None):
    if len(ref.shape) == 3:
        return ref[0]
    return ref[...] if rows is None else ref[rows, :]


def _for_row_chunks(n_rows, fn):
    rc = min(n_rows, ROW_CHUNK)
    assert n_rows % rc == 0
    if n_rows == rc:
        fn(slice(0, rc))
        return

    def body(i, carry):
        fn(pl.ds(pl.multiple_of(i * rc, rc), rc))
        return carry

    lax.fori_loop(0, n_rows // rc, body, 0)


def _dot(a, b):
    return jnp.dot(a, b, preferred_element_type=F32)


def _sigmoid(x):
    return 0.5 * jnp.tanh(0.5 * x) + 0.5


def _rmsnorm(x, g):
    ms = jnp.mean(x * x, axis=-1, keepdims=True)
    return (x * lax.rsqrt(ms + EPS)) * g


def _mod_kernel(c_ref, w_ref, b_ref, o_ref):
    c = c_ref[...]
    cs = (c * _sigmoid(c)).astype(BF16)
    o_ref[...] = _dot(cs, w_ref[...].astype(BF16)) + b_ref[...]


def _mod_call(c_all, w_ada, b_ada, tn=1024):
    n_rows, d = c_all.shape
    n = w_ada.shape[1]
    est = 2 * (d * tn * 4) + 2 * n_rows * tn * 4 + n_rows * d * 4 * 2 + d * tn * 2
    return pl.pallas_call(
        _mod_kernel,
        grid=(n // tn,),
        in_specs=[pl.BlockSpec((n_rows, d), lambda j: (0, 0)),
                  pl.BlockSpec((d, tn), lambda j: (0, j)),
                  pl.BlockSpec((1, tn), lambda j: (0, j))],
        out_specs=pl.BlockSpec((n_rows, tn), lambda j: (0, j)),
        out_shape=jax.ShapeDtypeStruct((n_rows, n), F32),
        compiler_params=pltpu.CompilerParams(dimension_semantics=("parallel",),
                                             vmem_limit_bytes=_vmem_limit(est)),
        name="mod",
    )(c_all, w_ada, b_ada)


def _mod_spec(mod, piece, d, tiles_per_batch, ngrid):
    if mod.ndim == 3:
        if ngrid == 1:
            return pl.BlockSpec((1, 1, d), lambda i: (i // tiles_per_batch, 0, piece))
        return pl.BlockSpec((1, 1, d), lambda i, j: (i // tiles_per_batch, 0, piece))
    rows = mod.shape[0]
    if ngrid == 1:
        return pl.BlockSpec((rows, d), lambda i: (0, piece))
    return pl.BlockSpec((rows, d), lambda i, j: (0, piece))


def _inproj_kernel(x_ref, sc_ref, sh_ref, g_ref, w_ref, u_ref, sz_ref, *rest, n_u_tiles):
    h_scr = rest[-1]
    j = pl.program_id(1)

    @pl.when(j == 0)
    def _():
        def chunk(rows):
            h = _rmsnorm(x_ref[rows, :], g_ref[...])
            h_scr[rows, :] = (h * (1.0 + _rows(sc_ref, rows)) + _rows(sh_ref, rows)).astype(BF16)

        _for_row_chunks(x_ref.shape[0], chunk)

    if len(rest) == 2:
        rest[0][...] = w_ref[...].astype(BF16)
        w_ref = rest[0]
    tn = w_ref.shape[1]
    nc = min(tn, INPROJ_CHUNK)

    @pl.when(j < n_u_tiles)
    def _():
        u_ref[...] = _dot(h_scr[...], w_ref[...])

    @pl.when(j >= n_u_tiles)
    def _():
        for c in range(tn // nc):
            cols = slice(c * nc, (c + 1) * nc)
            sz_ref[:, cols] = _sigmoid(_dot(h_scr[...], w_ref[:, cols])).astype(BF16)


def _inproj_call(x2d, mod, g1, w_in, *, tm, tiles_per_batch, w_mix, tn=1024):
    m, d = x2d.shape
    n = w_in.shape[1]
    n_u = 2 * w_mix
    n_u_tiles = n_u // tn
    emit_w = w_in.dtype != BF16
    assert not emit_w or m == tm
    wbytes = w_in.dtype.itemsize
    est = (2 * tm * d * 4 + 2 * d * tn * wbytes + 2 * tm * tn * 4 + 2 * tm * tn * 2 + tm * d * 2 + tm * tn * 4
           + (2 * d * tn * 2 if emit_w else 0))
    out_specs = [pl.BlockSpec((tm, tn), lambda i, j: (i, jnp.minimum(j, n_u_tiles - 1))),
                 pl.BlockSpec((tm, tn), lambda i, j: (i, jnp.maximum(j - n_u_tiles, 0)))]
    out_shape = [jax.ShapeDtypeStruct((m, n_u), F32),
                 jax.ShapeDtypeStruct((m, n - n_u), BF16)]
    if emit_w:
        out_specs.append(pl.BlockSpec((d, tn), lambda i, j: (0, j)))
        out_shape.append(jax.ShapeDtypeStruct((d, n), BF16))
    return pl.pallas_call(
        functools.partial(_inproj_kernel, n_u_tiles=n_u_tiles),
        grid=(m // tm, n // tn),
        in_specs=[pl.BlockSpec((tm, d), lambda i, j: (i, 0)),
                  _mod_spec(mod, 1, d, tiles_per_batch, 2),
                  _mod_spec(mod, 0, d, tiles_per_batch, 2),
                  pl.BlockSpec((1, d), lambda i, j: (0, 0)),
                  pl.BlockSpec((d, tn), lambda i, j: (0, j))],
        out_specs=out_specs,
        out_shape=out_shape,
        scratch_shapes=[pltpu.VMEM((tm, d), BF16)],
        compiler_params=pltpu.CompilerParams(dimension_semantics=("parallel", "arbitrary"),
                                             vmem_limit_bytes=_vmem_limit(est)),
        name="inproj",
    )(x2d, mod, mod, g1, w_in)


def _lru_gate_block(uc, g, ba, bx, sp):
    r = _sigmoid(g[:, :LANES] + ba)
    i = _sigmoid(g[:, LANES:] + bx)
    log_a = (-LRU_C * r) * sp
    a = jnp.exp(log_a)
    mult = jnp.sqrt(1.0 - jnp.exp(2.0 * log_a))
    return a, (mult * i) * uc


def _lru_kernel(u_ref, cw_ref, cb_ref, wg_ref, ba_ref, bx_ref, lam_ref,
                hs_ref, nconv_ref, nlru_ref,
                ext, slab, a_s, b_s, carry):
    t = pl.program_id(0)
    nt = pl.num_programs(0)
    nbat, tl, w = u_ref.shape
    nblk = w // LANES
    hist = SUBLANES
    kw = cw_ref.shape[0]
    steps_per_tile = SUBLANES // nbat

    @pl.when(t == 0)
    def _():
        ext[:, 0:hist, :] = jnp.zeros((nbat, hist, w), F32)
        carry[...] = jnp.zeros_like(carry)

    for b in range(nbat):
        ext[b, hist:hist + tl, :] = u_ref[b]
        for c in range(nblk):
            cols = slice(c * LANES, (c + 1) * LANES)
            acc = cb_ref[:, cols] + ext[b, hist - kw + 1:hist - kw + 1 + tl, cols] * cw_ref[0:1, cols]
            for k in range(1, kw):
                acc = acc + ext[b, hist - kw + 1 + k:hist - kw + 1 + k + tl, cols] * cw_ref[k:k + 1, cols]
            slab[c, pl.ds(b, tl, stride=nbat), :] = acc
        tail = ext[b, hist + tl - (kw - 1):hist + tl, :]
        ext[b, hist - (kw - 1):hist, :] = tail

        @pl.when(t == nt - 1)
        def _(b=b, tail=tail):
            nconv_ref[b] = tail

    sp = jax.nn.softplus(-lam_ref[...])
    for h in range(nblk):
        cols = slice(h * LANES, (h + 1) * LANES)
        uc = slab[h]
        g = _dot(uc.astype(BF16), wg_ref[h])
        a, b = _lru_gate_block(uc, g, ba_ref[:, cols], bx_ref[:, cols], sp[:, cols])
        a_s[:, cols] = a
        b_s[:, cols] = b

    def tile_step(i, hcur):
        r0 = pl.multiple_of(i * SUBLANES, SUBLANES)
        at, bt = a_s[pl.ds(r0, SUBLANES), :], b_s[pl.ds(r0, SUBLANES), :]
        hs = []
        for k in range(steps_per_tile):
            hcur = at[k * nbat:(k + 1) * nbat] * hcur + bt[k * nbat:(k + 1) * nbat]
            hs.append(hcur)
        htile = jnp.concatenate(hs, axis=0)
        for c in range(nblk):
            slab[c, pl.ds(r0, SUBLANES), :] = htile[:, c * LANES:(c + 1) * LANES]
        return hcur

    h_end = lax.fori_loop(0, tl // steps_per_tile, tile_step, carry[...], unroll=2)
    carry[...] = h_end

    for b in range(nbat):
        for c in range(nblk):
            hs_ref[b, :, c * LANES:(c + 1) * LANES] = slab[c, pl.ds(b, tl, stride=nbat), :].astype(BF16)

    @pl.when(t == nt - 1)
    def _():
        for b in range(nbat):
            nlru_ref[b] = h_end[b:b + 1, :]


def _lru_call(u3d, cw, cb, wg, ba, bx, lam, *, tl):
    bsz, seq, _ = u3d.shape
    w = cw.shape[1]
    kw = cw.shape[0]
    nblk = w // LANES
    assert SUBLANES % bsz == 0
    rows = bsz * tl
    est = (2 * rows * w * 4 + 2 * rows * w * 2 + bsz * (tl + SUBLANES) * w * 4 + 3 * rows * w * 4
           + nblk * LANES * 2 * LANES * 2 * 2 + rows * 2 * LANES * 4)
    const2 = lambda t: (0, 0)
    return pl.pallas_call(
        _lru_kernel,
        grid=(seq // tl,),
        in_specs=[pl.BlockSpec((bsz, tl, w), lambda t: (0, t, 0)),
                  pl.BlockSpec((kw, w), const2),
                  pl.BlockSpec((1, w), const2),
                  pl.BlockSpec((nblk, LANES, 2 * LANES), lambda t: (0, 0, 0)),
                  pl.BlockSpec((1, w), const2),
                  pl.BlockSpec((1, w), const2),
                  pl.BlockSpec((1, w), const2)],
        out_specs=[pl.BlockSpec((bsz, tl, w), lambda t: (0, t, 0)),
                   pl.BlockSpec((bsz, kw - 1, w), lambda t: (0, 0, 0)),
                   pl.BlockSpec((bsz, 1, w), lambda t: (0, 0, 0))],
        out_shape=[jax.ShapeDtypeStruct((bsz, seq, w), BF16),
                   jax.ShapeDtypeStruct((bsz, kw - 1, w), F32),
                   jax.ShapeDtypeStruct((bsz, 1, w), F32)],
        scratch_shapes=[pltpu.VMEM((bsz, tl + SUBLANES, w), F32),
                        pltpu.VMEM((nblk, rows, LANES), F32),
                        pltpu.VMEM((rows, w), F32),
                        pltpu.VMEM((rows, w), F32),
                        pltpu.VMEM((bsz, w), F32)],
        compiler_params=pltpu.CompilerParams(dimension_semantics=("arbitrary",),
                                             vmem_limit_bytes=_vmem_limit(est)),
        name="lru_scan",
    )(u3d, cw, cb, wg, ba, bx, lam)


def _lru_step_kernel(u_ref, buf_ref, h0_ref, cw_ref, cb_ref, wg_ref, ba_ref, bx_ref, lam_ref,
                     hs_ref, nconv_ref, nlru_ref):
    w = cw_ref.shape[1]
    kw = cw_ref.shape[0]
    nblk = w // LANES
    u = u_ref[...]
    uc = cb_ref[...] + buf_ref[0] * cw_ref[0:1, :]
    for k in range(1, kw - 1):
        uc = uc + buf_ref[k] * cw_ref[k:k + 1, :]
    uc = uc + u * cw_ref[kw - 1:kw, :]
    for k in range(kw - 2):
        nconv_ref[k] = buf_ref[k + 1]
    nconv_ref[kw - 2] = u
    sp = jax.nn.softplus(-lam_ref[...])
    for h in range(nblk):
        cols = slice(h * LANES, (h + 1) * LANES)
        uch = uc[:, cols]
        g = _dot(uch.astype(BF16), wg_ref[h])
        a, b = _lru_gate_block(uch, g, ba_ref[:, cols], bx_ref[:, cols], sp[:, cols])
        hn = a * h0_ref[:, cols] + b
        nlru_ref[:, cols] = hn
        hs_ref[:, cols] = hn.astype(BF16)


def _lru_step_call(u2d, buf, h0, cw, cb, wg, ba, bx, lam):
    n = h0.shape[0]
    w = cw.shape[1]
    kw = cw.shape[0]
    nblk = w // LANES
    c2 = lambda i: (0, 0)
    c3 = lambda i: (0, 0, 0)
    return pl.pallas_call(
        _lru_step_kernel,
        grid=(1,),
        in_specs=[pl.BlockSpec((n, w), c2),
                  pl.BlockSpec((kw - 1, n, w), c3),
                  pl.BlockSpec((n, w), c2),
                  pl.BlockSpec((kw, w), c2),
                  pl.BlockSpec((1, w), c2),
                  pl.BlockSpec((nblk, LANES, 2 * LANES), c3),
                  pl.BlockSpec((1, w), c2),
                  pl.BlockSpec((1, w), c2),
                  pl.BlockSpec((1, w), c2)],
        out_specs=[pl.BlockSpec((n, w), c2),
                   pl.BlockSpec((kw - 1, n, w), c3),
                   pl.BlockSpec((n, w), c2)],
        out_shape=[jax.ShapeDtypeStruct((n, w), BF16),
                   jax.ShapeDtypeStruct((kw - 1, n, w), F32),
                   jax.ShapeDtypeStruct((n, w), F32)],
        name="lru_step",
    )(u2d, buf, h0, cw, cb, wg, ba, bx, lam)


def _s5_prep_kernel(arx_ref, aix_ref, dt_ref, br_ref, bi_ref, ar_ref, ai_ref,
                    bbr_ref, bbi_ref, pwr_ref, pwi_ref):
    dt = jnp.exp(dt_ref[...])

    def abar(ar, ai):
        mag = jnp.exp(dt * ar)
        ang = dt * ai
        return mag * jnp.cos(ang), mag * jnp.sin(ang)

    ar, ai = arx_ref[...], aix_ref[...]
    abr, abi = abar(ar, ai)
    den = ar * ar + ai * ai
    nr, ni = abr - 1.0, abi
    q_re = (nr * ar + ni * ai) / den
    q_im = (ni * ar - nr * ai) / den
    br, bi = br_ref[...], bi_ref[...]
    bbr_ref[...] = q_re * br - q_im * bi
    bbi_ref[...] = q_re * bi + q_im * br

    pr, pi = abar(ar_ref[...], ai_ref[...])
    qr, qi = jnp.ones_like(pr), jnp.zeros_like(pr)
    for m in range(pwr_ref.shape[0]):
        pwr_ref[m] = qr
        pwi_ref[m] = qi
        qr, qi = qr * pr - qi * pi, qr * pi + qi * pr


def _s5_prep_call(a_re, a_im, log_dt, b_re, b_im, n_pow):
    g, p, k = b_re.shape
    arx = jnp.repeat(a_re, k, axis=1)
    aix = jnp.repeat(a_im, k, axis=1)
    bbr, bbi, pwr, pwi = pl.pallas_call(
        _s5_prep_kernel,
        out_shape=[jax.ShapeDtypeStruct((g, p * k), F32), jax.ShapeDtypeStruct((g, p * k), F32),
                   jax.ShapeDtypeStruct((n_pow, g, p), F32), jax.ShapeDtypeStruct((n_pow, g, p), F32)],
        name="s5_prep",
    )(arx, aix, log_dt.reshape(g, 1), b_re.reshape(g, p * k), b_im.reshape(g, p * k), a_re, a_im)
    return bbr.reshape(g, p, k), bbi.reshape(g, p, k), pwr, pwi


def _s5_block_weights(bb_re, bb_im, c_re, c_im):
    g, p, k = bb_re.shape
    gb = GROUPS_PER_BLOCK
    nb = g // gb
    eye = jnp.eye(gb, dtype=F32)

    def in_blocks(bb):
        t = bb.reshape(nb, gb, p, k).transpose(0, 1, 3, 2)
        return jnp.einsum('cgkp,gh->cgkhp', t, eye).reshape(nb, gb * k, gb * p)

    def out_blocks(cc):
        t = cc.reshape(nb, gb, k, p)
        return jnp.einsum('cgkp,hg->chpgk', t, eye).reshape(nb, gb * p, gb * k)

    bblk = jnp.concatenate([in_blocks(bb_re), in_blocks(bb_im)], axis=2)
    return bblk, out_blocks(c_re), out_blocks(c_im)


def _cplx_mul(ar, ai, br, bi):
    return ar * br - ai * bi, ar * bi + ai * br


def _cplx_step(ar, ai, hr, hi, br, bi):
    return ar * hr - ai * hi + br, ar * hi + ai * hr + bi


def _s5_kernel(u_ref, bblk_ref, creb_ref, cimb_ref, pwr_ref, pwi_ref, d_ref,
               gy_ref, nre_ref, nim_ref,
               ufb, buf, wbs, wqs, wts, yslab, out32):
    nbat, seq, _ = u_ref.shape
    spb = STATES_PER_BLOCK
    fold = wbs.shape[0] // LANES
    nq = seq // fold

    bre, bim = bblk_ref[0, :, :spb], bblk_ref[0, :, spb:]
    creb, cimb = creb_ref[0], cimb_ref[0]
    wcb = jnp.concatenate([creb, -cimb], axis=0).astype(BF16)
    lag = []
    for m in range(fold):
        wre, wim = _cplx_mul(bre, bim, pwr_ref[0, m:m + 1, :], pwi_ref[0, m:m + 1, :])
        rows = slice((fold - 1 - m) * LANES, (fold - m) * LANES)
        wbs[rows, :spb] = wre.astype(BF16)
        wbs[rows, spb:] = wim.astype(BF16)
        lag.append(_dot(wbs[rows, :], wcb))
    zero_blk = jnp.zeros((LANES, LANES), BF16)
    for r_in in range(fold):
        for r_out in range(fold):
            blk = lag[r_out - r_in].astype(BF16) if r_in <= r_out else zero_blk
            wts[r_in * LANES:(r_in + 1) * LANES, r_out * LANES:(r_out + 1) * LANES] = blk
    ctre, ctim = creb.T, cimb.T
    for r in range(fold):
        pr, pi = pwr_ref[0, r + 1:r + 2, :], pwi_ref[0, r + 1:r + 2, :]
        cols = slice(r * LANES, (r + 1) * LANES)
        wqs[:spb, cols] = (ctre * pr - ctim * pi).T.astype(BF16)
        wqs[spb:, cols] = (-(ctre * pi + ctim * pr)).T.astype(BF16)

    for r in range(fold):
        for b in range(nbat):
            yslab[r, pl.ds(b, nq, stride=nbat), :] = u_ref[b, pl.ds(r, nq, stride=fold), :]
        ufb[:, r * LANES:(r + 1) * LANES] = yslab[r].astype(BF16)

    buf[...] = _dot(ufb[...], wbs[...])

    ar, ai = pwr_ref[0, fold:fold + 1, :], pwi_ref[0, fold:fold + 1, :]
    groups_per_tile = SUBLANES // nbat
    z = jnp.zeros((nbat, spb), F32)

    def tile_step(i, g):
        gr, gi = g
        r0 = pl.multiple_of(i * SUBLANES, SUBLANES)
        xr, xi = buf[pl.ds(r0, SUBLANES), :spb], buf[pl.ds(r0, SUBLANES), spb:]
        hr, hi = [], []
        for k in range(groups_per_tile):
            hr.append(gr)
            hi.append(gi)
            gr, gi = _cplx_step(ar, ai, gr, gi, xr[k * nbat:(k + 1) * nbat], xi[k * nbat:(k + 1) * nbat])
        buf[pl.ds(r0, SUBLANES), :spb] = jnp.concatenate(hr, axis=0)
        buf[pl.ds(r0, SUBLANES), spb:] = jnp.concatenate(hi, axis=0)
        return gr, gi

    g_re, g_im = lax.fori_loop(0, nq // groups_per_tile, tile_step, (z, z), unroll=2)
    for b in range(nbat):
        nre_ref[b] = g_re[b:b + 1, :]
        nim_ref[b] = g_im[b:b + 1, :]

    hsb = buf[...].astype(BF16)
    per_tile = 2
    for n in range(fold // per_tile):
        cols = slice(n * per_tile * LANES, (n + 1) * per_tile * LANES)
        kk = (n + 1) * per_tile * LANES
        yn = _dot(hsb, wqs[:, cols]) + _dot(ufb[:, :kk], wts[:kk, cols])
        for i in range(per_tile):
            yslab[n * per_tile + i] = yn[:, i * LANES:(i + 1) * LANES]

    for b in range(nbat):
        for r in range(fold):
            ys = (yslab[r, pl.ds(b, nq, stride=nbat), :]
                  + d_ref[...] * u_ref[b, pl.ds(r, nq, stride=fold), :])
            out32[b, pl.ds(r, nq, stride=fold), :] = jax.nn.gelu(ys)
        gy_ref[b] = out32[b].astype(BF16)


def _s5_call(u3d, bblk, creb, cimb, pw_re, pw_im, d, *, w):
    bsz, seq, _ = u3d.shape
    nblk = w // LANES
    spb = STATES_PER_BLOCK
    fold = S5_FOLD
    n_pow = fold + 1
    assert SUBLANES % bsz == 0 and seq % fold == 0
    rows = bsz * (seq // fold)
    est = (2 * bsz * seq * LANES * 4 + 2 * bsz * seq * LANES * 2
           + rows * fold * LANES * 2 + 2 * rows * 2 * spb * 4
           + 3 * fold * LANES * 2 * spb * 2
           + rows * fold * LANES * 4 + bsz * seq * LANES * 4
           + rows * 2 * spb * 2
           + 2 * (LANES * 2 * spb + 2 * spb * LANES) * 4)
    return pl.pallas_call(
        _s5_kernel,
        grid=(nblk,),
        in_specs=[pl.BlockSpec((bsz, seq, LANES), lambda c: (0, 0, nblk + c)),
                  pl.BlockSpec((1, LANES, 2 * spb), lambda c: (c, 0, 0)),
                  pl.BlockSpec((1, spb, LANES), lambda c: (c, 0, 0)),
                  pl.BlockSpec((1, spb, LANES), lambda c: (c, 0, 0)),
                  pl.BlockSpec((1, n_pow, spb), lambda c: (c, 0, 0)),
                  pl.BlockSpec((1, n_pow, spb), lambda c: (c, 0, 0)),
                  pl.BlockSpec((1, LANES), lambda c: (0, c))],
        out_specs=[pl.BlockSpec((bsz, seq, LANES), lambda c: (0, 0, c)),
                   pl.BlockSpec((bsz, 1, spb), lambda c: (0, 0, c)),
                   pl.BlockSpec((bsz, 1, spb), lambda c: (0, 0, c))],
        out_shape=[jax.ShapeDtypeStruct((bsz, seq, w), BF16),
                   jax.ShapeDtypeStruct((bsz, 1, nblk * spb), F32),
                   jax.ShapeDtypeStruct((bsz, 1, nblk * spb), F32)],
        scratch_shapes=[pltpu.VMEM((rows, fold * LANES), BF16),
                        pltpu.VMEM((rows, 2 * spb), F32),
                        pltpu.VMEM((fold * LANES, 2 * spb), BF16),
                        pltpu.VMEM((2 * spb, fold * LANES), BF16),
                        pltpu.VMEM((fold * LANES, fold * LANES), BF16),
                        pltpu.VMEM((fold, rows, LANES), F32),
                        pltpu.VMEM((bsz, seq, LANES), F32)],
        compiler_params=pltpu.CompilerParams(dimension_semantics=("parallel",),
                                             vmem_limit_bytes=_vmem_limit(est)),
        name="s5_scan",
    )(u3d, bblk, creb, cimb, pw_re, pw_im, d)


def _s5_step_kernel(u_ref, h0r_ref, h0i_ref, wb_ref, wc_ref, are_ref, aim_ref, d_ref,
                    gy_ref, nre_ref, nim_ref):
    w = u_ref.shape[1]
    nblk = w // LANES
    spb = STATES_PER_BLOCK
    for c in range(nblk):
        cols = slice(c * LANES, (c + 1) * LANES)
        scols = slice(c * spb, (c + 1) * spb)
        u = u_ref[:, cols]
        r = _dot(u.astype(BF16), wb_ref[c])
        hr, hi = _cplx_step(are_ref[:, scols], aim_ref[:, scols], h0r_ref[:, scols], h0i_ref[:, scols],
                            r[:, :spb], r[:, spb:])
        nre_ref[:, scols] = hr
        nim_ref[:, scols] = hi
        y = _dot(hr.astype(BF16), wc_ref[c, :spb, :]) + _dot(hi.astype(BF16), wc_ref[c, spb:, :])
        gy_ref[:, cols] = jax.nn.gelu(y + d_ref[:, cols] * u).astype(BF16)


def _s5_step_call(u2d, h0r, h0i, wb, wc, abr, abi, d, *, w):
    n, n_state = h0r.shape
    c2 = lambda i: (0, 0)
    c3 = lambda i: (0, 0, 0)
    return pl.pallas_call(
        _s5_step_kernel,
        grid=(1,),
        in_specs=[pl.BlockSpec((n, w), lambda i: (0, 1)),
                  pl.BlockSpec((n, n_state), c2),
                  pl.BlockSpec((n, n_state), c2),
                  pl.BlockSpec(wb.shape, c3),
                  pl.BlockSpec(wc.shape, c3),
                  pl.BlockSpec((1, n_state), c2),
                  pl.BlockSpec((1, n_state), c2),
                  pl.BlockSpec((1, w), c2)],
        out_specs=[pl.BlockSpec((n, w), c2),
                   pl.BlockSpec((n, n_state), c2),
                   pl.BlockSpec((n, n_state), c2)],
        out_shape=[jax.ShapeDtypeStruct((n, w), BF16),
                   jax.ShapeDtypeStruct((n, n_state), F32),
                   jax.ShapeDtypeStruct((n, n_state), F32)],
        name="s5_step",
    )(u2d, h0r, h0i, wb, wc, abr, abi, d)


def _merge_kernel(hs_ref, gy_ref, sz_ref, x_ref, gt_ref, wp_ref, wg_ref, wo_ref, o_ref):
    d = x_ref.shape[1]
    ya = _dot(hs_ref[...], wp_ref[...])
    merged = sz_ref[:, :d].astype(F32) * ya
    glu = _dot(gy_ref[...], wg_ref[...])
    yb = glu[:, :d] * _sigmoid(glu[:, d:])
    merged = merged + sz_ref[:, d:].astype(F32) * yb
    o = _dot(merged.astype(BF16), wo_ref[...])
    o_ref[...] = x_ref[...] + _rows(gt_ref) * o


def _merge_call(hs, gy, sz, x2d, mod, wp, wg, wo, *, tm, tiles_per_batch):
    m, d = x2d.shape
    w = hs.shape[1]
    est = (2 * (2 * tm * w * 2 + tm * 2 * d * 2 + 2 * tm * d * 4) + (wp.size + wg.size + wo.size) * 2
           + tm * d * 4 * 5)
    one = pl.Buffered(1)
    return pl.pallas_call(
        _merge_kernel,
        grid=(m // tm,),
        in_specs=[pl.BlockSpec((tm, w), lambda i: (i, 0)),
                  pl.BlockSpec((tm, w), lambda i: (i, 0)),
                  pl.BlockSpec((tm, 2 * d), lambda i: (i, 0)),
                  pl.BlockSpec((tm, d), lambda i: (i, 0)),
                  _mod_spec(mod, 2, d, tiles_per_batch, 1),
                  pl.BlockSpec(wp.shape, lambda i: (0, 0), pipeline_mode=one),
                  pl.BlockSpec(wg.shape, lambda i: (0, 0), pipeline_mode=one),
                  pl.BlockSpec(wo.shape, lambda i: (0, 0), pipeline_mode=one)],
        out_specs=pl.BlockSpec((tm, d), lambda i: (i, 0)),
        out_shape=jax.ShapeDtypeStruct((m, d), F32),
        compiler_params=pltpu.CompilerParams(dimension_semantics=("parallel",),
                                             vmem_limit_bytes=_vmem_limit(est)),
        name="merge",
    )(hs, gy, sz, x2d, mod, wp, wg, wo)


def _mlp_kernel(x_ref, sc_ref, sh_ref, gt_ref, g2_ref, gf_ref, wu_ref, wd_ref, o_ref, *rest, final_norm):
    h_scr = rest[-1]
    j = pl.program_id(1)
    nj = pl.num_programs(1)

    n_rows = x_ref.shape[0]
    if len(rest) == 3:
        rest[0][...] = wu_ref[...].astype(BF16)
        rest[1][...] = wd_ref[...].astype(BF16)
        wu_ref, wd_ref = rest[0], rest[1]

    @pl.when(j == 0)
    def _():
        def chunk(rows):
            h = _rmsnorm(x_ref[rows, :], g2_ref[...])
            h_scr[rows, :] = (h * (1.0 + _rows(sc_ref, rows)) + _rows(sh_ref, rows)).astype(BF16)
            o_ref[rows, :] = jnp.zeros((h.shape[0], h.shape[1]), F32)

        _for_row_chunks(n_rows, chunk)

    up = _dot(h_scr[...], wu_ref[...])
    act = jnp.square(jnp.maximum(up, 0.0)).astype(BF16)
    d = o_ref.shape[1]
    nc = min(d, MLP_OUT_CHUNK)
    for c in range(d // nc):
        cols = slice(c * nc, (c + 1) * nc)
        o_ref[:, cols] += _dot(act, wd_ref[:, cols])

    @pl.when(j == nj - 1)
    def _():
        def chunk(rows):
            x2 = x_ref[rows, :] + _rows(gt_ref, rows) * o_ref[rows, :]
            o_ref[rows, :] = _rmsnorm(x2, gf_ref[...]) if final_norm else x2

        _for_row_chunks(n_rows, chunk)


def _mlp_call(x2d, mod, g2, gf, w_up, w_down, *, tm, tiles_per_batch, final_norm, tf=1024):
    m, d = x2d.shape
    dff = w_up.shape[1]
    emit_w = w_up.dtype != BF16
    assert not emit_w or m == tm
    wbytes = w_up.dtype.itemsize
    est = (3 * tm * d * 4 + tm * d * 2 + 4 * d * tf * wbytes + tm * tf * 6 + tm * MLP_OUT_CHUNK * 4
           + (4 * d * tf * 2 if emit_w else 0))
    out_specs = [pl.BlockSpec((tm, d), lambda i, j: (i, 0))]
    out_shape = [jax.ShapeDtypeStruct((m, d), F32)]
    if emit_w:
        out_specs += [pl.BlockSpec((d, tf), lambda i, j: (0, j)), pl.BlockSpec((tf, d), lambda i, j: (j, 0))]
        out_shape += [jax.ShapeDtypeStruct((d, dff), BF16), jax.ShapeDtypeStruct((dff, d), BF16)]
    return pl.pallas_call(
        functools.partial(_mlp_kernel, final_norm=final_norm),
        grid=(m // tm, dff // tf),
        in_specs=[pl.BlockSpec((tm, d), lambda i, j: (i, 0), pipeline_mode=pl.Buffered(1)),
                  _mod_spec(mod, 4, d, tiles_per_batch, 2),
                  _mod_spec(mod, 3, d, tiles_per_batch, 2),
                  _mod_spec(mod, 5, d, tiles_per_batch, 2),
                  pl.BlockSpec((1, d), lambda i, j: (0, 0)),
                  pl.BlockSpec((1, d), lambda i, j: (0, 0)),
                  pl.BlockSpec((d, tf), lambda i, j: (0, j)),
                  pl.BlockSpec((tf, d), lambda i, j: (j, 0))],
        out_specs=out_specs,
        out_shape=out_shape,
        scratch_shapes=[pltpu.VMEM((tm, d), BF16)],
        compiler_params=pltpu.CompilerParams(dimension_semantics=("parallel", "arbitrary"),
                                             vmem_limit_bytes=_vmem_limit(est)),
        name="mlp",
    )(x2d, mod, mod, mod, g2, gf, w_up, w_down)


def _pick_tile(n, pref):
    t = min(n, pref)
    assert n % t == 0, (n, t)
    return t


def kernel(x_prompt, x_sample, state_conv, state_lru, state_ssm_re, state_ssm_im, c_prompt, c_sample, w_ada, b_ada, g_norm1, g_norm2, w_in, conv_w, conv_b, w_rg_a, b_rg_a, w_rg_x, b_rg_x, lru_lambda, w_proj_a, ssm_a_re, ssm_a_im, ssm_log_dt, ssm_b_re, ssm_b_im, ssm_c_re, ssm_c_im, ssm_d, w_glu, w_out, w_up, w_down, g_final):
    depth = w_ada.shape[0]
    nb, seq, d = x_prompt.shape
    ns = x_sample.shape[0]
    assert x_sample.shape[1] == 1
    w = conv_w.shape[2]
    n_state = ssm_a_re.shape[1] * ssm_a_re.shape[2]
    assert ssm_b_re.shape[2:] == (SSM_STATE, SSM_GROUP) and w % LANES == 0
    assert w_in.shape[2] == 2 * w + 2 * d

    tl = _pick_tile(seq, 256)
    tm_in = _pick_tile(seq, 1024)
    tm_mg = _pick_tile(seq, 256)
    tm_mlp = _pick_tile(seq, 1024)

    xp = x_prompt.reshape(nb * seq, d)
    xs = x_sample.reshape(ns, d)
    pad = (-nb) % SUBLANES
    c_all = jnp.concatenate([c_prompt, jnp.zeros((pad, d), F32), c_sample], axis=0)

    outs_p = [[] for _ in range(4)]
    outs_s = [[] for _ in range(4)]
    for l in range(depth):
        last = l == depth - 1
        mod = _mod_call(c_all, w_ada[l], b_ada[l].reshape(1, -1))
        mod_p = mod[:nb].reshape(nb, 1, N_MOD * d)
        mod_s = mod[nb + pad:]

        g1 = g_norm1[l].reshape(1, d)
        g2 = g_norm2[l].reshape(1, d)
        gf = g_final.reshape(1, d)
        cw, cb = conv_w[l], conv_b[l].reshape(1, w)
        wg = jnp.concatenate([w_rg_a[l], w_rg_x[l]], axis=2).astype(BF16)
        ba, bx, lam = b_rg_a[l].reshape(1, w), b_rg_x[l].reshape(1, w), lru_lambda[l].reshape(1, w)
        n_pow = S5_FOLD + 1
        bbr, bbi, pwr, pwi = _s5_prep_call(ssm_a_re[l], ssm_a_im[l], ssm_log_dt[l], ssm_b_re[l], ssm_b_im[l], n_pow)
        bblk, creb, cimb = _s5_block_weights(bbr, bbi, ssm_c_re[l], ssm_c_im[l])
        wb = bblk.astype(BF16)
        wc = jnp.concatenate([creb, -cimb], axis=1).astype(BF16)
        abr, abi = pwr[1].reshape(1, n_state), pwi[1].reshape(1, n_state)
        nblk = w // LANES
        pw_rows = [v.reshape(n_pow, nblk, STATES_PER_BLOCK).transpose(1, 0, 2) for v in (pwr, pwi)]
        dskip = ssm_d[l].reshape(1, w)
        wp, wgl, wo = w_proj_a[l].astype(BF16), w_glu[l].astype(BF16), w_out[l].astype(BF16)

        u_s, sz_s, w_in_b = _inproj_call(xs, mod_s, g1, w_in[l], tm=ns, tiles_per_batch=1, w_mix=w)
        buf = jnp.transpose(state_conv[l], (1, 0, 2))
        hs_s, nconv_s, nlru_s = _lru_step_call(u_s, buf, state_lru[l], cw, cb, wg, ba, bx, lam)
        gy_s, nre_s, nim_s = _s5_step_call(u_s, state_ssm_re[l].reshape(ns, n_state),
                                           state_ssm_im[l].reshape(ns, n_state), wb, wc, abr, abi, dskip, w=w)
        x1_s = _merge_call(hs_s, gy_s, sz_s, xs, mod_s, wp, wgl, wo, tm=ns, tiles_per_batch=1)
        xs, w_up_b, w_down_b = _mlp_call(x1_s, mod_s, g2, gf, w_up[l], w_down[l], tm=ns, tiles_per_batch=1,
                                         final_norm=last)

        u_p, sz_p = _inproj_call(xp, mod_p, g1, w_in_b, tm=tm_in, tiles_per_batch=seq // tm_in, w_mix=w)
        u3 = u_p.reshape(nb, seq, 2 * w)
        hs_p, nconv_p, nlru_p = _lru_call(u3, cw, cb, wg, ba, bx, lam, tl=tl)
        gy_p, nre_p, nim_p = _s5_call(u3, bblk, creb, cimb, *pw_rows, dskip, w=w)
        x1_p = _merge_call(hs_p.reshape(nb * seq, w), gy_p.reshape(nb * seq, w), sz_p, xp, mod_p, wp, wgl, wo,
                           tm=tm_mg, tiles_per_batch=seq // tm_mg)
        (xp,) = _mlp_call(x1_p, mod_p, g2, gf, w_up_b, w_down_b, tm=tm_mlp, tiles_per_batch=seq // tm_mlp,
                          final_norm=last)

        gshape = ssm_a_re.shape[1:]
        for acc, v in zip(outs_p, (nconv_p, nlru_p.reshape(nb, w),
                                   nre_p.reshape((nb,) + gshape), nim_p.reshape((nb,) + gshape))):
            acc.append(v)
        for acc, v in zip(outs_s, (jnp.transpose(nconv_s, (1, 0, 2)), nlru_s,
                                   nre_s.reshape((ns,) + gshape), nim_s.reshape((ns,) + gshape))):
            acc.append(v)

    y_prompt = xp.reshape(nb, seq, d)
    y_sample = xs.reshape(ns, 1, d)
    return (y_prompt, y_sample) + tuple(jnp.stack(a) for a in outs_p) + tuple(jnp.stack(a) for a in outs_s)
```

```python
import functools

import jax
import jax.numpy as jnp
from jax import lax
from jax.experimental import pallas as pl
from jax.experimental.pallas import tpu as pltpu

F32 = jnp.float32
BF16 = jnp.bfloat16

LANES = 128
SUBLANES = 8
VMEM_PHYSICAL_BYTES = 64 * 1024 * 1024
VMEM_LIMIT_CAP_BYTES = VMEM_PHYSICAL_BYTES - 6 * 1024 * 1024

LRU_C = 8.0
EPS = 1e-6
N_MOD = 6
SSM_GROUP = 16
SSM_STATE = 64
GROUPS_PER_BLOCK = LANES // SSM_GROUP
STATES_PER_BLOCK = GROUPS_PER_BLOCK * SSM_STATE


VMEM_SLACK_BYTES = 8 * 1024 * 1024


def _vmem_limit(nbytes):
    return int(min(VMEM_LIMIT_CAP_BYTES, max(32 * 1024 * 1024, nbytes + VMEM_SLACK_BYTES)))


BF16_ROWS = 2 * SUBLANES
EPILOGUE_ROWS = 256
MLP_OUT_CHUNK = 512
INPROJ_CHUNK = 256
S5_FOLD = 8


def _rows(ref, rows=None):
    if len(ref.shape) == 3:
        return ref[0]
    return ref[...] if rows is None else ref[rows, :]


def _row_groups(n_rows, group, fn, unroll):
    assert n_rows % group == 0

    def body(i, carry):
        fn(pl.ds(pl.multiple_of(i * group, group), group))
        return carry

    lax.fori_loop(0, n_rows // group, body, 0, unroll=unroll)


def _rms_scale_pass(src, inv_scr, d):
    def fn(rows):
        x = src(rows)
        xx = x * x
        part = xx[:, 0:LANES]
        for c in range(1, d // LANES):
            part = part + xx[:, c * LANES:(c + 1) * LANES]
        inv_scr[rows, :] = part

    _row_groups(inv_scr.shape[0], SUBLANES, fn, unroll=8)
    ss = jnp.sum(inv_scr[...], axis=-1, keepdims=True)
    inv_scr[...] = jnp.broadcast_to(lax.rsqrt(ss * (1.0 / d) + EPS), inv_scr.shape)


def _lanes(inv, d):
    return jnp.concatenate([inv] * (d // LANES), axis=1)


def _sublane_rows(scr, slot, row=None):
    rows = slice(slot * SUBLANES, (slot + 1) * SUBLANES)
    if row is None:
        return scr[rows, :]
    scr[rows, :] = jnp.broadcast_to(row, (SUBLANES, row.shape[1]))
    return None


def _norm_mod_bf16(x_ref, g_ref, sc_ref, sh_ref, h_scr, inv_scr, row_scr):
    n_rows, d = x_ref.shape
    _rms_scale_pass(lambda rows: x_ref[rows, :], inv_scr, d)
    per_token = len(sc_ref.shape) == 2
    if not per_token:
        _sublane_rows(row_scr, 0, g_ref[...] * (1.0 + sc_ref[0]))
        _sublane_rows(row_scr, 1, sh_ref[0])

    def fn(rows):
        halves = []
        for k in range(BF16_ROWS // SUBLANES):
            r8 = pl.ds(pl.multiple_of(rows.start + k * SUBLANES, SUBLANES), SUBLANES)
            gm = g_ref[...] * (1.0 + sc_ref[r8, :]) if per_token else _sublane_rows(row_scr, 0)
            sh = sh_ref[r8, :] if per_token else _sublane_rows(row_scr, 1)
            halves.append((x_ref[r8, :] * _lanes(inv_scr[r8, :], d)) * gm + sh)
        h_scr[rows, :] = jnp.concatenate(halves, axis=0).astype(BF16)

    _row_groups(n_rows, BF16_ROWS, fn, unroll=2)


def _dot(a, b):
    return jnp.dot(a, b, preferred_element_type=F32)


def _sigmoid(x):
    return 0.5 * jnp.tanh(0.5 * x) + 0.5


def _mod_kernel(c_ref, w_ref, b_ref, os_ref, op_ref):
    c = c_ref[...]
    cs = (c * _sigmoid(c)).astype(BF16)
    mod = _dot(cs, w_ref[...].astype(BF16)) + b_ref[...]
    n_sample = os_ref.shape[0]
    os_ref[...] = mod[:n_sample]
    op_ref[...] = mod[n_sample:]


def _mod_call(c_all, n_sample, w_ada, b_ada, tn=1024):
    n_rows, d = c_all.shape
    n = w_ada.shape[1]
    est = 2 * (d * tn * 4) + 3 * n_rows * tn * 4 + n_rows * d * 4 * 2 + d * tn * 2
    return pl.pallas_call(
        _mod_kernel,
        grid=(n // tn,),
        in_specs=[pl.BlockSpec((n_rows, d), lambda j: (0, 0)),
                  pl.BlockSpec((d, tn), lambda j: (0, j)),
                  pl.BlockSpec((1, tn), lambda j: (0, j))],
        out_specs=[pl.BlockSpec((n_sample, tn), lambda j: (0, j)),
                   pl.BlockSpec((n_rows - n_sample, tn), lambda j: (0, j))],
        out_shape=[jax.ShapeDtypeStruct((n_sample, n), F32),
                   jax.ShapeDtypeStruct((n_rows - n_sample, n), F32)],
        compiler_params=pltpu.CompilerParams(dimension_semantics=("parallel",),
                                             vmem_limit_bytes=_vmem_limit(est)),
        name="mod",
    )(c_all, w_ada, b_ada)


def _mod_spec(mod, piece, d, tiles_per_batch, ngrid):
    if mod.ndim == 3:
        if ngrid == 1:
            return pl.BlockSpec((1, 1, d), lambda i: (i // tiles_per_batch, 0, piece))
        return pl.BlockSpec((1, 1, d), lambda i, j: (i // tiles_per_batch, 0, piece))
    rows = mod.shape[0]
    if ngrid == 1:
        return pl.BlockSpec((rows, d), lambda i: (0, piece))
    return pl.BlockSpec((rows, d), lambda i, j: (0, piece))


def _inproj_kernel(*refs, n_u_tiles, emit_w):
    x_ref, sc_ref, sh_ref, g_ref, w_ref, u_ref, sz_ref = refs[:7]
    h_scr, inv_scr, row_scr = refs[-3:]
    j = pl.program_id(1)

    @pl.when(j == 0)
    def _():
        _norm_mod_bf16(x_ref, g_ref, sc_ref, sh_ref, h_scr, inv_scr, row_scr)

    if emit_w:
        wb_ref = refs[7]
        wb_ref[...] = w_ref[...].astype(BF16)
        w_ref = wb_ref
    tn = w_ref.shape[1]
    nc = min(tn, INPROJ_CHUNK)

    @pl.when(j < n_u_tiles)
    def _():
        u_ref[...] = _dot(h_scr[...], w_ref[...])

    @pl.when(j >= n_u_tiles)
    def _():
        for c in range(tn // nc):
            cols = slice(c * nc, (c + 1) * nc)
            sz_ref[:, cols] = _sigmoid(_dot(h_scr[...], w_ref[:, cols])).astype(BF16)


def _inproj_call(x2d, mod, g1, w_in, *, tm, tiles_per_batch, w_mix, tn=1024):
    m, d = x2d.shape
    n = w_in.shape[1]
    n_u = 2 * w_mix
    n_u_tiles = n_u // tn
    emit_w = w_in.dtype != BF16
    assert not emit_w or m == tm
    wbytes = w_in.dtype.itemsize
    est = (2 * tm * d * 4 + 2 * d * tn * wbytes + 2 * tm * tn * 4 + 2 * tm * tn * 2 + tm * d * 2 + tm * tn * 4
           + (3 * d * tn * 2 if emit_w else 0))
    out_specs = [pl.BlockSpec((tm, tn), lambda i, j: (i, jnp.minimum(j, n_u_tiles - 1))),
                 pl.BlockSpec((tm, tn), lambda i, j: (i, jnp.maximum(j - n_u_tiles, 0)))]
    out_shape = [jax.ShapeDtypeStruct((m, n_u), F32),
                 jax.ShapeDtypeStruct((m, n - n_u), BF16)]
    if emit_w:
        out_specs.append(pl.BlockSpec((d, tn), lambda i, j: (0, j)))
        out_shape.append(jax.ShapeDtypeStruct((d, n), BF16))
    return pl.pallas_call(
        functools.partial(_inproj_kernel, n_u_tiles=n_u_tiles, emit_w=emit_w),
        grid=(m // tm, n // tn),
        in_specs=[pl.BlockSpec((tm, d), lambda i, j: (i, 0)),
                  _mod_spec(mod, 1, d, tiles_per_batch, 2),
                  _mod_spec(mod, 0, d, tiles_per_batch, 2),
                  pl.BlockSpec((1, d), lambda i, j: (0, 0)),
                  pl.BlockSpec((d, tn), lambda i, j: (0, j))],
        out_specs=out_specs,
        out_shape=out_shape,
        scratch_shapes=[pltpu.VMEM((tm, d), BF16), pltpu.VMEM((tm, LANES), F32),
                        pltpu.VMEM((2 * SUBLANES, d), F32)],
        compiler_params=pltpu.CompilerParams(dimension_semantics=("parallel", "arbitrary"),
                                             vmem_limit_bytes=_vmem_limit(est)),
        name="inproj",
    )(x2d, mod, mod, g1, w_in)


def _lru_gate_block(uc, g, ba, bx, sp):
    r = _sigmoid(g[:, :LANES] + ba)
    i = _sigmoid(g[:, LANES:] + bx)
    log_a = (-LRU_C * r) * sp
    a = jnp.exp(log_a)
    mult = jnp.sqrt(1.0 - jnp.exp(2.0 * log_a))
    return a, (mult * i) * uc


def _lru_kernel(u_ref, cw_ref, cb_ref, wg_ref, ba_ref, bx_ref, lam_ref,
                hs_ref, nconv_ref, nlru_ref,
                ext, slab, a_s, b_s, carry):
    t = pl.program_id(0)
    nt = pl.num_programs(0)
    nbat, tl, w = u_ref.shape
    nblk = w // LANES
    hist = SUBLANES
    kw = cw_ref.shape[0]
    steps_per_tile = SUBLANES // nbat

    @pl.when(t == 0)
    def _():
        ext[:, 0:hist, :] = jnp.zeros((nbat, hist, w), F32)
        carry[...] = jnp.zeros_like(carry)

    for b in range(nbat):
        ext[b, hist:hist + tl, :] = u_ref[b]
        for c in range(nblk):
            cols = slice(c * LANES, (c + 1) * LANES)
            acc = cb_ref[:, cols] + ext[b, hist - kw + 1:hist - kw + 1 + tl, cols] * cw_ref[0:1, cols]
            for k in range(1, kw):
                acc = acc + ext[b, hist - kw + 1 + k:hist - kw + 1 + k + tl, cols] * cw_ref[k:k + 1, cols]
            slab[c, pl.ds(b, tl, stride=nbat), :] = acc
        tail = ext[b, hist + tl - (kw - 1):hist + tl, :]
        ext[b, hist - (kw - 1):hist, :] = tail

        @pl.when(t == nt - 1)
        def _(b=b, tail=tail):
            nconv_ref[b] = tail

    sp = jax.nn.softplus(-lam_ref[...])
    for h in range(nblk):
        cols = slice(h * LANES, (h + 1) * LANES)
        uc = slab[h]
        g = _dot(uc.astype(BF16), wg_ref[h])
        a, b = _lru_gate_block(uc, g, ba_ref[:, cols], bx_ref[:, cols], sp[:, cols])
        a_s[:, cols] = a
        b_s[:, cols] = b

    def tile_step(i, hcur):
        r0 = pl.multiple_of(i * SUBLANES, SUBLANES)
        at, bt = a_s[pl.ds(r0, SUBLANES), :], b_s[pl.ds(r0, SUBLANES), :]
        hs = []
        for k in range(steps_per_tile):
            hcur = at[k * nbat:(k + 1) * nbat] * hcur + bt[k * nbat:(k + 1) * nbat]
            hs.append(hcur)
        htile = jnp.concatenate(hs, axis=0)
        for c in range(nblk):
            slab[c, pl.ds(r0, SUBLANES), :] = htile[:, c * LANES:(c + 1) * LANES]
        return hcur

    h_end = lax.fori_loop(0, tl // steps_per_tile, tile_step, carry[...], unroll=2)
    carry[...] = h_end

    for b in range(nbat):
        for c in range(nblk):
            hs_ref[b, :, c * LANES:(c + 1) * LANES] = slab[c, pl.ds(b, tl, stride=nbat), :].astype(BF16)

    @pl.when(t == nt - 1)
    def _():
        for b in range(nbat):
            nlru_ref[b] = h_end[b:b + 1, :]


def _lru_call(u3d, cw, cb, wg, ba, bx, lam, *, tl):
    bsz, seq, _ = u3d.shape
    w = cw.shape[1]
    kw = cw.shape[0]
    nblk = w // LANES
    assert SUBLANES % bsz == 0
    rows = bsz * tl
    est = (2 * rows * w * 4 + 2 * rows * w * 2 + bsz * (tl + SUBLANES) * w * 4 + 3 * rows * w * 4
           + nblk * LANES * 2 * LANES * 2 * 2 + rows * 2 * LANES * 4)
    const2 = lambda t: (0, 0)
    return pl.pallas_call(
        _lru_kernel,
        grid=(seq // tl,),
        in_specs=[pl.BlockSpec((bsz, tl, w), lambda t: (0, t, 0)),
                  pl.BlockSpec((kw, w), const2),
                  pl.BlockSpec((1, w), const2),
                  pl.BlockSpec((nblk, LANES, 2 * LANES), lambda t: (0, 0, 0)),
                  pl.BlockSpec((1, w), const2),
                  pl.BlockSpec((1, w), const2),
                  pl.BlockSpec((1, w), const2)],
        out_specs=[pl.BlockSpec((bsz, tl, w), lambda t: (0, t, 0)),
                   pl.BlockSpec((bsz, kw - 1, w), lambda t: (0, 0, 0)),
                   pl.BlockSpec((bsz, 1, w), lambda t: (0, 0, 0))],
        out_shape=[jax.ShapeDtypeStruct((bsz, seq, w), BF16),
                   jax.ShapeDtypeStruct((bsz, kw - 1, w), F32),
                   jax.ShapeDtypeStruct((bsz, 1, w), F32)],
        scratch_shapes=[pltpu.VMEM((bsz, tl + SUBLANES, w), F32),
                        pltpu.VMEM((nblk, rows, LANES), F32),
                        pltpu.VMEM((rows, w), F32),
                        pltpu.VMEM((rows, w), F32),
                        pltpu.VMEM((bsz, w), F32)],
        compiler_params=pltpu.CompilerParams(dimension_semantics=("arbitrary",),
                                             vmem_limit_bytes=_vmem_limit(est)),
        name="lru_scan",
    )(u3d, cw, cb, wg, ba, bx, lam)


def _lru_step_kernel(u_ref, buf_ref, h0_ref, cw_ref, cb_ref, wg_ref, ba_ref, bx_ref, lam_ref,
                     hs_ref, nconv_ref, nlru_ref):
    w = cw_ref.shape[1]
    kw = cw_ref.shape[0]
    nblk = w // LANES
    u = u_ref[...]
    uc = cb_ref[...] + buf_ref[0] * cw_ref[0:1, :]
    for k in range(1, kw - 1):
        uc = uc + buf_ref[k] * cw_ref[k:k + 1, :]
    uc = uc + u * cw_ref[kw - 1:kw, :]
    for k in range(kw - 2):
        nconv_ref[k] = buf_ref[k + 1]
    nconv_ref[kw - 2] = u
    sp = jax.nn.softplus(-lam_ref[...])
    for h in range(nblk):
        cols = slice(h * LANES, (h + 1) * LANES)
        uch = uc[:, cols]
        g = _dot(uch.astype(BF16), wg_ref[h])
        a, b = _lru_gate_block(uch, g, ba_ref[:, cols], bx_ref[:, cols], sp[:, cols])
        hn = a * h0_ref[:, cols] + b
        nlru_ref[:, cols] = hn
        hs_ref[:, cols] = hn.astype(BF16)


def _lru_step_call(u2d, buf, h0, cw, cb, wg, ba, bx, lam):
    n = h0.shape[0]
    w = cw.shape[1]
    kw = cw.shape[0]
    nblk = w // LANES
    c2 = lambda i: (0, 0)
    c3 = lambda i: (0, 0, 0)
    return pl.pallas_call(
        _lru_step_kernel,
        grid=(1,),
        in_specs=[pl.BlockSpec((n, w), c2),
                  pl.BlockSpec((kw - 1, n, w), c3),
                  pl.BlockSpec((n, w), c2),
                  pl.BlockSpec((kw, w), c2),
                  pl.BlockSpec((1, w), c2),
                  pl.BlockSpec((nblk, LANES, 2 * LANES), c3),
                  pl.BlockSpec((1, w), c2),
                  pl.BlockSpec((1, w), c2),
                  pl.BlockSpec((1, w), c2)],
        out_specs=[pl.BlockSpec((n, w), c2),
                   pl.BlockSpec((kw - 1, n, w), c3),
                   pl.BlockSpec((n, w), c2)],
        out_shape=[jax.ShapeDtypeStruct((n, w), BF16),
                   jax.ShapeDtypeStruct((kw - 1, n, w), F32),
                   jax.ShapeDtypeStruct((n, w), F32)],
        name="lru_step",
    )(u2d, buf, h0, cw, cb, wg, ba, bx, lam)


def _s5_prep_kernel(arx_ref, aix_ref, dt_ref, br_ref, bi_ref, ar_ref, ai_ref,
                    bbr_ref, bbi_ref, pwr_ref, pwi_ref):
    dt = jnp.exp(dt_ref[...])

    def abar(ar, ai):
        mag = jnp.exp(dt * ar)
        ang = dt * ai
        return mag * jnp.cos(ang), mag * jnp.sin(ang)

    ar, ai = arx_ref[...], aix_ref[...]
    abr, abi = abar(ar, ai)
    den = ar * ar + ai * ai
    nr, ni = abr - 1.0, abi
    q_re = (nr * ar + ni * ai) / den
    q_im = (ni * ar - nr * ai) / den
    br, bi = br_ref[...], bi_ref[...]
    bbr_ref[...] = q_re * br - q_im * bi
    bbi_ref[...] = q_re * bi + q_im * br

    pr, pi = abar(ar_ref[...], ai_ref[...])
    qr, qi = jnp.ones_like(pr), jnp.zeros_like(pr)
    for m in range(pwr_ref.shape[0]):
        pwr_ref[m] = qr
        pwi_ref[m] = qi
        qr, qi = qr * pr - qi * pi, qr * pi + qi * pr


def _s5_prep_call(a_re, a_im, log_dt, b_re, b_im, n_pow):
    g, p, k = b_re.shape
    arx = jnp.repeat(a_re, k, axis=1)
    aix = jnp.repeat(a_im, k, axis=1)
    bbr, bbi, pwr, pwi = pl.pallas_call(
        _s5_prep_kernel,
        out_shape=[jax.ShapeDtypeStruct((g, p * k), F32), jax.ShapeDtypeStruct((g, p * k), F32),
                   jax.ShapeDtypeStruct((n_pow, g, p), F32), jax.ShapeDtypeStruct((n_pow, g, p), F32)],
        name="s5_prep",
    )(arx, aix, log_dt.reshape(g, 1), b_re.reshape(g, p * k), b_im.reshape(g, p * k), a_re, a_im)
    return bbr.reshape(g, p, k), bbi.reshape(g, p, k), pwr, pwi


def _s5_block_weights(bb_re, bb_im, c_re, c_im):
    g, p, k = bb_re.shape
    gb = GROUPS_PER_BLOCK
    nb = g // gb
    eye = jnp.eye(gb, dtype=F32)

    def in_blocks(bb):
        t = bb.reshape(nb, gb, p, k).transpose(0, 1, 3, 2)
        return jnp.einsum('cgkp,gh->cgkhp', t, eye).reshape(nb, gb * k, gb * p)

    def out_blocks(cc):
        t = cc.reshape(nb, gb, k, p)
        return jnp.einsum('cgkp,hg->chpgk', t, eye).reshape(nb, gb * p, gb * k)

    bblk = jnp.concatenate([in_blocks(bb_re), in_blocks(bb_im)], axis=2)
    return bblk, out_blocks(c_re), out_blocks(c_im)


def _cplx_mul(ar, ai, br, bi):
    return ar * br - ai * bi, ar * bi + ai * br


def _cplx_step(ar, ai, hr, hi, br, bi):
    return ar * hr - ai * hi + br, ar * hi + ai * hr + bi


def _s5_kernel(u_ref, bblk_ref, creb_ref, cimb_ref, pwr_ref, pwi_ref, d_ref,
               gy_ref, nre_ref, nim_ref,
               ufb, buf, wbs, wqs, wts, yslab, out32):
    nbat, seq, _ = u_ref.shape
    spb = STATES_PER_BLOCK
    fold = wbs.shape[0] // LANES
    nq = seq // fold

    bre, bim = bblk_ref[0, :, :spb], bblk_ref[0, :, spb:]
    creb, cimb = creb_ref[0], cimb_ref[0]
    wcb = jnp.concatenate([creb, -cimb], axis=0).astype(BF16)
    lag = []
    for m in range(fold):
        wre, wim = _cplx_mul(bre, bim, pwr_ref[0, m:m + 1, :], pwi_ref[0, m:m + 1, :])
        rows = slice((fold - 1 - m) * LANES, (fold - m) * LANES)
        wbs[rows, :spb] = wre.astype(BF16)
        wbs[rows, spb:] = wim.astype(BF16)
        lag.append(_dot(wbs[rows, :], wcb))
    zero_blk = jnp.zeros((LANES, LANES), BF16)
    for r_in in range(fold):
        for r_out in range(fold):
            blk = lag[r_out - r_in].astype(BF16) if r_in <= r_out else zero_blk
            wts[r_in * LANES:(r_in + 1) * LANES, r_out * LANES:(r_out + 1) * LANES] = blk
    ctre, ctim = creb.T, cimb.T
    for r in range(fold):
        pr, pi = pwr_ref[0, r + 1:r + 2, :], pwi_ref[0, r + 1:r + 2, :]
        cols = slice(r * LANES, (r + 1) * LANES)
        wqs[:spb, cols] = (ctre * pr - ctim * pi).T.astype(BF16)
        wqs[spb:, cols] = (-(ctre * pi + ctim * pr)).T.astype(BF16)

    for r in range(fold):
        for b in range(nbat):
            yslab[r, pl.ds(b, nq, stride=nbat), :] = u_ref[b, pl.ds(r, nq, stride=fold), :]
        ufb[:, r * LANES:(r + 1) * LANES] = yslab[r].astype(BF16)

    buf[...] = _dot(ufb[...], wbs[...])

    ar, ai = pwr_ref[0, fold:fold + 1, :], pwi_ref[0, fold:fold + 1, :]
    groups_per_tile = SUBLANES // nbat
    z = jnp.zeros((nbat, spb), F32)

    def tile_step(i, g):
        gr, gi = g
        r0 = pl.multiple_of(i * SUBLANES, SUBLANES)
        xr, xi = buf[pl.ds(r0, SUBLANES), :spb], buf[pl.ds(r0, SUBLANES), spb:]
        hr, hi = [], []
        for k in range(groups_per_tile):
            hr.append(gr)
            hi.append(gi)
            gr, gi = _cplx_step(ar, ai, gr, gi, xr[k * nbat:(k + 1) * nbat], xi[k * nbat:(k + 1) * nbat])
        buf[pl.ds(r0, SUBLANES), :spb] = jnp.concatenate(hr, axis=0)
        buf[pl.ds(r0, SUBLANES), spb:] = jnp.concatenate(hi, axis=0)
        return gr, gi

    g_re, g_im = lax.fori_loop(0, nq // groups_per_tile, tile_step, (z, z), unroll=2)
    for b in range(nbat):
        nre_ref[b] = g_re[b:b + 1, :]
        nim_ref[b] = g_im[b:b + 1, :]

    hsb = buf[...].astype(BF16)
    per_tile = 2
    for n in range(fold // per_tile):
        cols = slice(n * per_tile * LANES, (n + 1) * per_tile * LANES)
        kk = (n + 1) * per_tile * LANES
        yn = _dot(hsb, wqs[:, cols]) + _dot(ufb[:, :kk], wts[:kk, cols])
        for i in range(per_tile):
            yslab[n * per_tile + i] = yn[:, i * LANES:(i + 1) * LANES]

    for b in range(nbat):
        for r in range(fold):
            ys = (yslab[r, pl.ds(b, nq, stride=nbat), :]
                  + d_ref[...] * u_ref[b, pl.ds(r, nq, stride=fold), :])
            out32[b, pl.ds(r, nq, stride=fold), :] = jax.nn.gelu(ys)
        gy_ref[b] = out32[b].astype(BF16)


def _s5_call(u3d, bblk, creb, cimb, pw_re, pw_im, d, *, w):
    bsz, seq, _ = u3d.shape
    nblk = w // LANES
    spb = STATES_PER_BLOCK
    fold = S5_FOLD
    n_pow = fold + 1
    assert SUBLANES % bsz == 0 and seq % fold == 0
    rows = bsz * (seq // fold)
    est = (2 * bsz * seq * LANES * 4 + 2 * bsz * seq * LANES * 2
           + rows * fold * LANES * 2 + 2 * rows * 2 * spb * 4
           + 3 * fold * LANES * 2 * spb * 2
           + rows * fold * LANES * 4 + bsz * seq * LANES * 4
           + rows * 2 * spb * 2
           + 2 * (LANES * 2 * spb + 2 * spb * LANES) * 4)
    return pl.pallas_call(
        _s5_kernel,
        grid=(nblk,),
        in_specs=[pl.BlockSpec((bsz, seq, LANES), lambda c: (0, 0, nblk + c)),
                  pl.BlockSpec((1, LANES, 2 * spb), lambda c: (c, 0, 0)),
                  pl.BlockSpec((1, spb, LANES), lambda c: (c, 0, 0)),
                  pl.BlockSpec((1, spb, LANES), lambda c: (c, 0, 0)),
                  pl.BlockSpec((1, n_pow, spb), lambda c: (c, 0, 0)),
                  pl.BlockSpec((1, n_pow, spb), lambda c: (c, 0, 0)),
                  pl.BlockSpec((1, LANES), lambda c: (0, c))],
        out_specs=[pl.BlockSpec((bsz, seq, LANES), lambda c: (0, 0, c)),
                   pl.BlockSpec((bsz, 1, spb), lambda c: (0, 0, c)),
                   pl.BlockSpec((bsz, 1, spb), lambda c: (0, 0, c))],
        out_shape=[jax.ShapeDtypeStruct((bsz, seq, w), BF16),
                   jax.ShapeDtypeStruct((bsz, 1, nblk * spb), F32),
                   jax.ShapeDtypeStruct((bsz, 1, nblk * spb), F32)],
        scratch_shapes=[pltpu.VMEM((rows, fold * LANES), BF16),
                        pltpu.VMEM((rows, 2 * spb), F32),
                        pltpu.VMEM((fold * LANES, 2 * spb), BF16),
                        pltpu.VMEM((2 * spb, fold * LANES), BF16),
                        pltpu.VMEM((fold * LANES, fold * LANES), BF16),
                        pltpu.VMEM((fold, rows, LANES), F32),
                        pltpu.VMEM((bsz, seq, LANES), F32)],
        compiler_params=pltpu.CompilerParams(dimension_semantics=("parallel",),
                                             vmem_limit_bytes=_vmem_limit(est)),
        name="s5_scan",
    )(u3d, bblk, creb, cimb, pw_re, pw_im, d)


def _s5_step_kernel(u_ref, h0r_ref, h0i_ref, wb_ref, wc_ref, are_ref, aim_ref, d_ref,
                    gy_ref, nre_ref, nim_ref):
    w = u_ref.shape[1]
    nblk = w // LANES
    spb = STATES_PER_BLOCK
    for c in range(nblk):
        cols = slice(c * LANES, (c + 1) * LANES)
        scols = slice(c * spb, (c + 1) * spb)
        u = u_ref[:, cols]
        r = _dot(u.astype(BF16), wb_ref[c])
        hr, hi = _cplx_step(are_ref[:, scols], aim_ref[:, scols], h0r_ref[:, scols], h0i_ref[:, scols],
                            r[:, :spb], r[:, spb:])
        nre_ref[:, scols] = hr
        nim_ref[:, scols] = hi
        y = _dot(hr.astype(BF16), wc_ref[c, :spb, :]) + _dot(hi.astype(BF16), wc_ref[c, spb:, :])
        gy_ref[:, cols] = jax.nn.gelu(y + d_ref[:, cols] * u).astype(BF16)


def _s5_step_call(u2d, h0r, h0i, wb, wc, abr, abi, d, *, w):
    n, n_state = h0r.shape
    c2 = lambda i: (0, 0)
    c3 = lambda i: (0, 0, 0)
    return pl.pallas_call(
        _s5_step_kernel,
        grid=(1,),
        in_specs=[pl.BlockSpec((n, w), lambda i: (0, 1)),
                  pl.BlockSpec((n, n_state), c2),
                  pl.BlockSpec((n, n_state), c2),
                  pl.BlockSpec(wb.shape, c3),
                  pl.BlockSpec(wc.shape, c3),
                  pl.BlockSpec((1, n_state), c2),
                  pl.BlockSpec((1, n_state), c2),
                  pl.BlockSpec((1, w), c2)],
        out_specs=[pl.BlockSpec((n, w), c2),
                   pl.BlockSpec((n, n_state), c2),
                   pl.BlockSpec((n, n_state), c2)],
        out_shape=[jax.ShapeDtypeStruct((n, w), BF16),
                   jax.ShapeDtypeStruct((n, n_state), F32),
                   jax.ShapeDtypeStruct((n, n_state), F32)],
        name="s5_step",
    )(u2d, h0r, h0i, wb, wc, abr, abi, d)


def _merge_kernel(hs_ref, gy_ref, sz_ref, x_ref, gt_ref, wp_ref, wg_ref, wo_ref, o_ref):
    d = x_ref.shape[1]
    ya = _dot(hs_ref[...], wp_ref[...])
    merged = sz_ref[:, :d].astype(F32) * ya
    glu = _dot(gy_ref[...], wg_ref[...])
    yb = glu[:, :d] * _sigmoid(glu[:, d:])
    merged = merged + sz_ref[:, d:].astype(F32) * yb
    o = _dot(merged.astype(BF16), wo_ref[...])
    o_ref[...] = x_ref[...] + _rows(gt_ref) * o


def _merge_call(hs, gy, sz, x2d, mod, wp, wg, wo, *, tm, tiles_per_batch):
    m, d = x2d.shape
    w = hs.shape[1]
    est = (2 * (2 * tm * w * 2 + tm * 2 * d * 2 + 2 * tm * d * 4) + (wp.size + wg.size + wo.size) * 2
           + tm * d * 4 * 5)
    one = pl.Buffered(1)
    return pl.pallas_call(
        _merge_kernel,
        grid=(m // tm,),
        in_specs=[pl.BlockSpec((tm, w), lambda i: (i, 0)),
                  pl.BlockSpec((tm, w), lambda i: (i, 0)),
                  pl.BlockSpec((tm, 2 * d), lambda i: (i, 0)),
                  pl.BlockSpec((tm, d), lambda i: (i, 0)),
                  _mod_spec(mod, 2, d, tiles_per_batch, 1),
                  pl.BlockSpec(wp.shape, lambda i: (0, 0), pipeline_mode=one),
                  pl.BlockSpec(wg.shape, lambda i: (0, 0), pipeline_mode=one),
                  pl.BlockSpec(wo.shape, lambda i: (0, 0), pipeline_mode=one)],
        out_specs=pl.BlockSpec((tm, d), lambda i: (i, 0)),
        out_shape=jax.ShapeDtypeStruct((m, d), F32),
        compiler_params=pltpu.CompilerParams(dimension_semantics=("parallel",),
                                             vmem_limit_bytes=_vmem_limit(est)),
        name="merge",
    )(hs, gy, sz, x2d, mod, wp, wg, wo)


def _mlp_kernel(*refs, final_norm, emit_w):
    x_ref, sc_ref, sh_ref, gt_ref, g2_ref, gf_ref, wu_ref, wd_ref, o_ref = refs[:9]
    h_scr, inv_scr, row_scr = refs[-3:]
    j = pl.program_id(1)
    nj = pl.num_programs(1)

    n_rows, d = x_ref.shape
    if emit_w:
        wub_ref, wdb_ref = refs[9:11]
        wub_ref[...] = wu_ref[...].astype(BF16)
        wdb_ref[...] = wd_ref[...].astype(BF16)
        wu_ref, wd_ref = wub_ref, wdb_ref

    @pl.when(j == 0)
    def _():
        _norm_mod_bf16(x_ref, g2_ref, sc_ref, sh_ref, h_scr, inv_scr, row_scr)
        o_ref[...] = jnp.zeros(o_ref.shape, F32)

    up = _dot(h_scr[...], wu_ref[...])
    act = jnp.square(jnp.maximum(up, 0.0)).astype(BF16)
    nc = min(d, MLP_OUT_CHUNK)
    for c in range(d // nc):
        cols = slice(c * nc, (c + 1) * nc)
        o_ref[:, cols] += _dot(act, wd_ref[:, cols])

    @pl.when(j == nj - 1)
    def _():
        def chunk(rows):
            x2 = x_ref[rows, :] + _rows(gt_ref, rows) * o_ref[rows, :]
            if final_norm:
                ms = jnp.mean(x2 * x2, axis=-1, keepdims=True)
                x2 = (x2 * lax.rsqrt(ms + EPS)) * gf_ref[...]
            o_ref[rows, :] = x2

        _row_groups(n_rows, min(n_rows, EPILOGUE_ROWS), chunk, unroll=1)


def _mlp_call(x2d, mod, g2, gf, w_up, w_down, *, tm, tiles_per_batch, final_norm, tf):
    m, d = x2d.shape
    dff = w_up.shape[1]
    emit_w = w_up.dtype != BF16
    assert not emit_w or m == tm
    wbytes = w_up.dtype.itemsize
    est = (3 * tm * d * 4 + tm * d * 2 + 4 * d * tf * wbytes + tm * tf * 6 + tm * MLP_OUT_CHUNK * 4
           + (6 * d * tf * 2 if emit_w else 0))
    out_specs = [pl.BlockSpec((tm, d), lambda i, j: (i, 0))]
    out_shape = [jax.ShapeDtypeStruct((m, d), F32)]
    if emit_w:
        out_specs += [pl.BlockSpec((d, tf), lambda i, j: (0, j)), pl.BlockSpec((tf, d), lambda i, j: (j, 0))]
        out_shape += [jax.ShapeDtypeStruct((d, dff), BF16), jax.ShapeDtypeStruct((dff, d), BF16)]
    return pl.pallas_call(
        functools.partial(_mlp_kernel, final_norm=final_norm, emit_w=emit_w),
        grid=(m // tm, dff // tf),
        in_specs=[pl.BlockSpec((tm, d), lambda i, j: (i, 0), pipeline_mode=pl.Buffered(1)),
                  _mod_spec(mod, 4, d, tiles_per_batch, 2),
                  _mod_spec(mod, 3, d, tiles_per_batch, 2),
                  _mod_spec(mod, 5, d, tiles_per_batch, 2),
                  pl.BlockSpec((1, d), lambda i, j: (0, 0)),
                  pl.BlockSpec((1, d), lambda i, j: (0, 0)),
                  pl.BlockSpec((d, tf), lambda i, j: (0, j)),
                  pl.BlockSpec((tf, d), lambda i, j: (j, 0))],
        out_specs=out_specs,
        out_shape=out_shape,
        scratch_shapes=[pltpu.VMEM((tm, d), BF16), pltpu.VMEM((tm, LANES), F32),
                        pltpu.VMEM((2 * SUBLANES, d), F32)],
        compiler_params=pltpu.CompilerParams(dimension_semantics=("parallel", "arbitrary"),
                                             vmem_limit_bytes=_vmem_limit(est)),
        name="mlp",
    )(x2d, mod, mod, mod, g2, gf, w_up, w_down)


def _pick_tile(n, pref):
    t = min(n, pref)
    assert n % t == 0, (n, t)
    return t


def kernel(x_prompt, x_sample, state_conv, state_lru, state_ssm_re, state_ssm_im, c_prompt, c_sample, w_ada, b_ada, g_norm1, g_norm2, w_in, conv_w, conv_b, w_rg_a, b_rg_a, w_rg_x, b_rg_x, lru_lambda, w_proj_a, ssm_a_re, ssm_a_im, ssm_log_dt, ssm_b_re, ssm_b_im, ssm_c_re, ssm_c_im, ssm_d, w_glu, w_out, w_up, w_down, g_final):
    depth = w_ada.shape[0]
    nb, seq, d = x_prompt.shape
    ns = x_sample.shape[0]
    assert x_sample.shape[1] == 1
    w = conv_w.shape[2]
    n_state = ssm_a_re.shape[1] * ssm_a_re.shape[2]
    assert ssm_b_re.shape[2:] == (SSM_STATE, SSM_GROUP) and w % LANES == 0
    assert w_in.shape[2] == 2 * w + 2 * d

    tl = _pick_tile(seq, 256)
    tm_in = _pick_tile(seq, 1024)
    tm_mg = _pick_tile(seq, 256)
    tm_mlp = _pick_tile(seq, 1024)

    xp = x_prompt.reshape(nb * seq, d)
    xs = x_sample.reshape(ns, d)
    assert ns % SUBLANES == 0
    pad = (-nb) % SUBLANES
    c_all = jnp.concatenate([c_sample, c_prompt, jnp.zeros((pad, d), F32)], axis=0)

    outs_p = [[] for _ in range(4)]
    outs_s = [[] for _ in range(4)]
    for l in range(depth):
        last = l == depth - 1
        mod_s, mod_p = _mod_call(c_all, ns, w_ada[l], b_ada[l].reshape(1, -1))
        mod_p = mod_p.reshape(nb + pad, 1, N_MOD * d)

        g1 = g_norm1[l].reshape(1, d)
        g2 = g_norm2[l].reshape(1, d)
        gf = g_final.reshape(1, d)
        cw, cb = conv_w[l], conv_b[l].reshape(1, w)
        wg = jnp.concatenate([w_rg_a[l], w_rg_x[l]], axis=2).astype(BF16)
        ba, bx, lam = b_rg_a[l].reshape(1, w), b_rg_x[l].reshape(1, w), lru_lambda[l].reshape(1, w)
        n_pow = S5_FOLD + 1
        bbr, bbi, pwr, pwi = _s5_prep_call(ssm_a_re[l], ssm_a_im[l], ssm_log_dt[l], ssm_b_re[l], ssm_b_im[l], n_pow)
        bblk, creb, cimb = _s5_block_weights(bbr, bbi, ssm_c_re[l], ssm_c_im[l])
        wb = bblk.astype(BF16)
        wc = jnp.concatenate([creb, -cimb], axis=1).astype(BF16)
        abr, abi = pwr[1].reshape(1, n_state), pwi[1].reshape(1, n_state)
        nblk = w // LANES
        pw_rows = [v.reshape(n_pow, nblk, STATES_PER_BLOCK).transpose(1, 0, 2) for v in (pwr, pwi)]
        dskip = ssm_d[l].reshape(1, w)
        wp, wgl, wo = w_proj_a[l].astype(BF16), w_glu[l].astype(BF16), w_out[l].astype(BF16)

        u_s, sz_s, w_in_b = _inproj_call(xs, mod_s, g1, w_in[l], tm=ns, tiles_per_batch=1, w_mix=w)
        buf = jnp.transpose(state_conv[l], (1, 0, 2))
        hs_s, nconv_s, nlru_s = _lru_step_call(u_s, buf, state_lru[l], cw, cb, wg, ba, bx, lam)
        gy_s, nre_s, nim_s = _s5_step_call(u_s, state_ssm_re[l].reshape(ns, n_state),
                                           state_ssm_im[l].reshape(ns, n_state), wb, wc, abr, abi, dskip, w=w)
        x1_s = _merge_call(hs_s, gy_s, sz_s, xs, mod_s, wp, wgl, wo, tm=ns, tiles_per_batch=1)
        xs, w_up_b, w_down_b = _mlp_call(x1_s, mod_s, g2, gf, w_up[l], w_down[l], tm=ns, tiles_per_batch=1,
                                         final_norm=last, tf=1024)

        u_p, sz_p = _inproj_call(xp, mod_p, g1, w_in_b, tm=tm_in, tiles_per_batch=seq // tm_in, w_mix=w)
        u3 = u_p.reshape(nb, seq, 2 * w)
        hs_p, nconv_p, nlru_p = _lru_call(u3, cw, cb, wg, ba, bx, lam, tl=tl)
        gy_p, nre_p, nim_p = _s5_call(u3, bblk, creb, cimb, *pw_rows, dskip, w=w)
        x1_p = _merge_call(hs_p.reshape(nb * seq, w), gy_p.reshape(nb * seq, w), sz_p, xp, mod_p, wp, wgl, wo,
                           tm=tm_mg, tiles_per_batch=seq // tm_mg)
        (xp,) = _mlp_call(x1_p, mod_p, g2, gf, w_up_b, w_down_b, tm=tm_mlp, tiles_per_batch=seq // tm_mlp,
                          final_norm=last, tf=1024)

        gshape = ssm_a_re.shape[1:]
        for acc, v in zip(outs_p, (nconv_p, nlru_p.reshape(nb, w),
                                   nre_p.reshape((nb,) + gshape), nim_p.reshape((nb,) + gshape))):
            acc.append(v)
        for acc, v in zip(outs_s, (jnp.transpose(nconv_s, (1, 0, 2)), nlru_s,
                                   nre_s.reshape((ns,) + gshape), nim_s.reshape((ns,) + gshape))):
            acc.append(v)

    y_prompt = xp.reshape(nb, seq, d)
    y_sample = xs.reshape(ns, 1, d)
    stack = lambda a: a[0][None] if len(a) == 1 else jnp.stack(a)
    return (y_prompt, y_sample) + tuple(stack(a) for a in outs_p) + tuple(stack(a) for a in outs_s)
```

```python
import functools

import jax
import jax.numpy as jnp
from jax import lax
from jax.experimental import pallas as pl
from jax.experimental.pallas import tpu as pltpu

F32 = jnp.float32
BF16 = jnp.bfloat16

LANES = 128
SUBLANES = 8
VMEM_PHYSICAL_BYTES = 64 * 1024 * 1024
VMEM_LIMIT_CAP_BYTES = VMEM_PHYSICAL_BYTES - 6 * 1024 * 1024

LRU_C = 8.0
EPS = 1e-6
N_MOD = 6
SSM_GROUP = 16
SSM_STATE = 64
GROUPS_PER_BLOCK = LANES // SSM_GROUP
STATES_PER_BLOCK = GROUPS_PER_BLOCK * SSM_STATE


VMEM_SLACK_BYTES = 8 * 1024 * 1024


def _vmem_limit(nbytes):
    return int(min(VMEM_LIMIT_CAP_BYTES, max(32 * 1024 * 1024, nbytes + VMEM_SLACK_BYTES)))


BF16_ROWS = 2 * SUBLANES
EPILOGUE_ROWS = 256
MLP_OUT_CHUNK = 512
INPROJ_CHUNK = 256
S5_FOLD = 8


def _rows(ref, rows=None):
    if len(ref.shape) == 3:
        return ref[0]
    return ref[...] if rows is None else ref[rows, :]


def _row_groups(n_rows, group, fn, unroll):
    assert n_rows % group == 0

    def body(i, carry):
        fn(pl.ds(pl.multiple_of(i * group, group), group))
        return carry

    lax.fori_loop(0, n_rows // group, body, 0, unroll=unroll)


def _rms_scale_pass(src, inv_scr, d):
    def fn(rows):
        x = src(rows)
        xx = x * x
        part = xx[:, 0:LANES]
        for c in range(1, d // LANES):
            part = part + xx[:, c * LANES:(c + 1) * LANES]
        inv_scr[rows, :] = part

    _row_groups(inv_scr.shape[0], SUBLANES, fn, unroll=8)
    ss = jnp.sum(inv_scr[...], axis=-1, keepdims=True)
    inv_scr[...] = jnp.broadcast_to(lax.rsqrt(ss * (1.0 / d) + EPS), inv_scr.shape)


def _lanes(inv, d):
    return jnp.concatenate([inv] * (d // LANES), axis=1)


def _sublane_rows(scr, slot, row=None):
    rows = slice(slot * SUBLANES, (slot + 1) * SUBLANES)
    if row is None:
        return scr[rows, :]
    scr[rows, :] = jnp.broadcast_to(row, (SUBLANES, row.shape[1]))
    return None


def _norm_mod_bf16(x_ref, g_ref, sc_ref, sh_ref, h_scr, inv_scr, row_scr):
    n_rows, d = x_ref.shape
    _rms_scale_pass(lambda rows: x_ref[rows, :], inv_scr, d)
    per_token = len(sc_ref.shape) == 2
    if not per_token:
        _sublane_rows(row_scr, 0, g_ref[...] * (1.0 + sc_ref[0]))
        _sublane_rows(row_scr, 1, sh_ref[0])

    def fn(rows):
        halves = []
        for k in range(BF16_ROWS // SUBLANES):
            r8 = pl.ds(pl.multiple_of(rows.start + k * SUBLANES, SUBLANES), SUBLANES)
            gm = g_ref[...] * (1.0 + sc_ref[r8, :]) if per_token else _sublane_rows(row_scr, 0)
            sh = sh_ref[r8, :] if per_token else _sublane_rows(row_scr, 1)
            halves.append((x_ref[r8, :] * _lanes(inv_scr[r8, :], d)) * gm + sh)
        h_scr[rows, :] = jnp.concatenate(halves, axis=0).astype(BF16)

    _row_groups(n_rows, BF16_ROWS, fn, unroll=2)


def _dot(a, b):
    return jnp.dot(a, b, preferred_element_type=F32)


def _sigmoid(x):
    return 0.5 * jnp.tanh(0.5 * x) + 0.5


def _mod_kernel(c_ref, w_ref, b_ref, os_ref, op_ref):
    c = c_ref[...]
    cs = (c * _sigmoid(c)).astype(BF16)
    mod = _dot(cs, w_ref[...].astype(BF16)) + b_ref[...]
    n_sample = os_ref.shape[0]
    os_ref[...] = mod[:n_sample]
    op_ref[...] = mod[n_sample:]


def _mod_call(c_all, n_sample, w_ada, b_ada, tn=1024):
    n_rows, d = c_all.shape
    n = w_ada.shape[1]
    est = 2 * (d * tn * 4) + 3 * n_rows * tn * 4 + n_rows * d * 4 * 2 + d * tn * 2
    return pl.pallas_call(
        _mod_kernel,
        grid=(n // tn,),
        in_specs=[pl.BlockSpec((n_rows, d), lambda j: (0, 0)),
                  pl.BlockSpec((d, tn), lambda j: (0, j)),
                  pl.BlockSpec((1, tn), lambda j: (0, j))],
        out_specs=[pl.BlockSpec((n_sample, tn), lambda j: (0, j)),
                   pl.BlockSpec((n_rows - n_sample, tn), lambda j: (0, j))],
        out_shape=[jax.ShapeDtypeStruct((n_sample, n), F32),
                   jax.ShapeDtypeStruct((n_rows - n_sample, n), F32)],
        compiler_params=pltpu.CompilerParams(dimension_semantics=("parallel",),
                                             vmem_limit_bytes=_vmem_limit(est)),
        name="mod",
    )(c_all, w_ada, b_ada)


def _mod_spec(mod, piece, d, tiles_per_batch, ngrid):
    if mod.ndim == 3:
        if ngrid == 1:
            return pl.BlockSpec((1, 1, d), lambda i: (i // tiles_per_batch, 0, piece))
        return pl.BlockSpec((1, 1, d), lambda i, j: (i // tiles_per_batch, 0, piece))
    rows = mod.shape[0]
    if ngrid == 1:
        return pl.BlockSpec((rows, d), lambda i: (0, piece))
    return pl.BlockSpec((rows, d), lambda i, j: (0, piece))


def _inproj_kernel(*refs, n_u_tiles, emit_w):
    x_ref, sc_ref, sh_ref, g_ref, w_ref, u_ref, sz_ref = refs[:7]
    h_scr, inv_scr, row_scr = refs[-3:]
    j = pl.program_id(1)

    @pl.when(j == 0)
    def _():
        _norm_mod_bf16(x_ref, g_ref, sc_ref, sh_ref, h_scr, inv_scr, row_scr)

    if emit_w:
        wb_ref = refs[7]
        wb_ref[...] = w_ref[...].astype(BF16)
        w_ref = wb_ref
    tn = w_ref.shape[1]
    nc = min(tn, INPROJ_CHUNK)

    @pl.when(j < n_u_tiles)
    def _():
        u_ref[...] = _dot(h_scr[...], w_ref[...])

    @pl.when(j >= n_u_tiles)
    def _():
        for c in range(tn // nc):
            cols = slice(c * nc, (c + 1) * nc)
            sz_ref[:, cols] = _sigmoid(_dot(h_scr[...], w_ref[:, cols])).astype(BF16)


def _inproj_call(x2d, mod, g1, w_in, *, tm, tiles_per_batch, w_mix, tn=1024):
    m, d = x2d.shape
    n = w_in.shape[1]
    n_u = 2 * w_mix
    n_u_tiles = n_u // tn
    emit_w = w_in.dtype != BF16
    assert not emit_w or m == tm
    wbytes = w_in.dtype.itemsize
    est = (2 * tm * d * 4 + 2 * d * tn * wbytes + 2 * tm * tn * 4 + 2 * tm * tn * 2 + tm * d * 2 + tm * tn * 4
           + (3 * d * tn * 2 if emit_w else 0))
    out_specs = [pl.BlockSpec((tm, tn), lambda i, j: (i, jnp.minimum(j, n_u_tiles - 1))),
                 pl.BlockSpec((tm, tn), lambda i, j: (i, jnp.maximum(j - n_u_tiles, 0)))]
    out_shape = [jax.ShapeDtypeStruct((m, n_u), F32),
                 jax.ShapeDtypeStruct((m, n - n_u), BF16)]
    if emit_w:
        out_specs.append(pl.BlockSpec((d, tn), lambda i, j: (0, j)))
        out_shape.append(jax.ShapeDtypeStruct((d, n), BF16))
    return pl.pallas_call(
        functools.partial(_inproj_kernel, n_u_tiles=n_u_tiles, emit_w=emit_w),
        grid=(m // tm, n // tn),
        in_specs=[pl.BlockSpec((tm, d), lambda i, j: (i, 0)),
                  _mod_spec(mod, 1, d, tiles_per_batch, 2),
                  _mod_spec(mod, 0, d, tiles_per_batch, 2),
                  pl.BlockSpec((1, d), lambda i, j: (0, 0)),
                  pl.BlockSpec((d, tn), lambda i, j: (0, j))],
        out_specs=out_specs,
        out_shape=out_shape,
        scratch_shapes=[pltpu.VMEM((tm, d), BF16), pltpu.VMEM((tm, LANES), F32),
                        pltpu.VMEM((2 * SUBLANES, d), F32)],
        compiler_params=pltpu.CompilerParams(dimension_semantics=("parallel", "arbitrary"),
                                             vmem_limit_bytes=_vmem_limit(est)),
        name="inproj",
    )(x2d, mod, mod, g1, w_in)


def _lru_gate_block(uc, g, ba, bx, sp):
    r = _sigmoid(g[:, :LANES] + ba)
    i = _sigmoid(g[:, LANES:] + bx)
    log_a = (-LRU_C * r) * sp
    a = jnp.exp(log_a)
    mult = jnp.sqrt(1.0 - jnp.exp(2.0 * log_a))
    return a, (mult * i) * uc


def _lru_kernel(*refs, n_side):
    u_ref, cw_ref, cb_ref, wg_ref, ba_ref, bx_ref, lam_ref = refs[:7]
    hs_ref, nconv_ref, nlru_ref = refs[7 + n_side:10 + n_side]
    ext, slab, a_s, b_s, carry = refs[-5:]
    _side_cast(refs[7:7 + n_side], refs[10 + n_side:10 + 2 * n_side])
    t = pl.program_id(0)
    nt = pl.num_programs(0)
    nbat, tl, w = u_ref.shape
    nblk = w // LANES
    hist = SUBLANES
    kw = cw_ref.shape[0]
    steps_per_tile = SUBLANES // nbat

    @pl.when(t == 0)
    def _():
        ext[:, 0:hist, :] = jnp.zeros((nbat, hist, w), F32)
        carry[...] = jnp.zeros_like(carry)

    for b in range(nbat):
        ext[b, hist:hist + tl, :] = u_ref[b]
        for c in range(nblk):
            cols = slice(c * LANES, (c + 1) * LANES)
            acc = cb_ref[:, cols] + ext[b, hist - kw + 1:hist - kw + 1 + tl, cols] * cw_ref[0:1, cols]
            for k in range(1, kw):
                acc = acc + ext[b, hist - kw + 1 + k:hist - kw + 1 + k + tl, cols] * cw_ref[k:k + 1, cols]
            slab[c, pl.ds(b, tl, stride=nbat), :] = acc
        tail = ext[b, hist + tl - (kw - 1):hist + tl, :]
        ext[b, hist - (kw - 1):hist, :] = tail

        @pl.when(t == nt - 1)
        def _(b=b, tail=tail):
            nconv_ref[b] = tail

    sp = jax.nn.softplus(-lam_ref[...])
    for h in range(nblk):
        cols = slice(h * LANES, (h + 1) * LANES)
        uc = slab[h]
        g = _dot(uc.astype(BF16), wg_ref[h])
        a, b = _lru_gate_block(uc, g, ba_ref[:, cols], bx_ref[:, cols], sp[:, cols])
        a_s[:, cols] = a
        b_s[:, cols] = b

    def tile_step(i, hcur):
        r0 = pl.multiple_of(i * SUBLANES, SUBLANES)
        at, bt = a_s[pl.ds(r0, SUBLANES), :], b_s[pl.ds(r0, SUBLANES), :]
        hs = []
        for k in range(steps_per_tile):
            hcur = at[k * nbat:(k + 1) * nbat] * hcur + bt[k * nbat:(k + 1) * nbat]
            hs.append(hcur)
        htile = jnp.concatenate(hs, axis=0)
        for c in range(nblk):
            slab[c, pl.ds(r0, SUBLANES), :] = htile[:, c * LANES:(c + 1) * LANES]
        return hcur

    h_end = lax.fori_loop(0, tl // steps_per_tile, tile_step, carry[...], unroll=2)
    carry[...] = h_end

    for b in range(nbat):
        for c in range(nblk):
            hs_ref[b, :, c * LANES:(c + 1) * LANES] = slab[c, pl.ds(b, tl, stride=nbat), :].astype(BF16)

    @pl.when(t == nt - 1)
    def _():
        for b in range(nbat):
            nlru_ref[b] = h_end[b:b + 1, :]


def _lru_call(u3d, cw, cb, wg, ba, bx, lam, *, tl, side=()):
    bsz, seq, _ = u3d.shape
    w = cw.shape[1]
    kw = cw.shape[0]
    nblk = w // LANES
    assert SUBLANES % bsz == 0
    rows = bsz * tl
    side_in, side_out, side_shapes, side_bytes = _side_cast_specs(side, seq // tl)
    est = (2 * rows * w * 4 + 2 * rows * w * 2 + bsz * (tl + SUBLANES) * w * 4 + 3 * rows * w * 4
           + nblk * LANES * 2 * LANES * 2 * 2 + rows * 2 * LANES * 4 + side_bytes)
    const2 = lambda t: (0, 0)
    return pl.pallas_call(
        functools.partial(_lru_kernel, n_side=len(side)),
        grid=(seq // tl,),
        in_specs=[pl.BlockSpec((bsz, tl, w), lambda t: (0, t, 0)),
                  pl.BlockSpec((kw, w), const2),
                  pl.BlockSpec((1, w), const2),
                  pl.BlockSpec((nblk, LANES, 2 * LANES), lambda t: (0, 0, 0)),
                  pl.BlockSpec((1, w), const2),
                  pl.BlockSpec((1, w), const2),
                  pl.BlockSpec((1, w), const2)] + side_in,
        out_specs=[pl.BlockSpec((bsz, tl, w), lambda t: (0, t, 0)),
                   pl.BlockSpec((bsz, kw - 1, w), lambda t: (0, 0, 0)),
                   pl.BlockSpec((bsz, 1, w), lambda t: (0, 0, 0))] + side_out,
        out_shape=[jax.ShapeDtypeStruct((bsz, seq, w), BF16),
                   jax.ShapeDtypeStruct((bsz, kw - 1, w), F32),
                   jax.ShapeDtypeStruct((bsz, 1, w), F32)] + side_shapes,
        scratch_shapes=[pltpu.VMEM((bsz, tl + SUBLANES, w), F32),
                        pltpu.VMEM((nblk, rows, LANES), F32),
                        pltpu.VMEM((rows, w), F32),
                        pltpu.VMEM((rows, w), F32),
                        pltpu.VMEM((bsz, w), F32)],
        compiler_params=pltpu.CompilerParams(dimension_semantics=("arbitrary",),
                                             vmem_limit_bytes=_vmem_limit(est)),
        name="lru_scan",
    )(u3d, cw, cb, wg, ba, bx, lam, *(a for a, _ in side))


def _lru_step_kernel(u_ref, buf_ref, h0_ref, cw_ref, cb_ref, wg_ref, ba_ref, bx_ref, lam_ref,
                     hs_ref, nconv_ref, nlru_ref):
    w = cw_ref.shape[1]
    kw = cw_ref.shape[0]
    nblk = w // LANES
    u = u_ref[...]
    uc = cb_ref[...] + buf_ref[0] * cw_ref[0:1, :]
    for k in range(1, kw - 1):
        uc = uc + buf_ref[k] * cw_ref[k:k + 1, :]
    uc = uc + u * cw_ref[kw - 1:kw, :]
    for k in range(kw - 2):
        nconv_ref[k] = buf_ref[k + 1]
    nconv_ref[kw - 2] = u
    sp = jax.nn.softplus(-lam_ref[...])
    for h in range(nblk):
        cols = slice(h * LANES, (h + 1) * LANES)
        uch = uc[:, cols]
        g = _dot(uch.astype(BF16), wg_ref[h])
        a, b = _lru_gate_block(uch, g, ba_ref[:, cols], bx_ref[:, cols], sp[:, cols])
        hn = a * h0_ref[:, cols] + b
        nlru_ref[:, cols] = hn
        hs_ref[:, cols] = hn.astype(BF16)


def _lru_step_call(u2d, buf, h0, cw, cb, wg, ba, bx, lam):
    n = h0.shape[0]
    w = cw.shape[1]
    kw = cw.shape[0]
    nblk = w // LANES
    c2 = lambda i: (0, 0)
    c3 = lambda i: (0, 0, 0)
    return pl.pallas_call(
        _lru_step_kernel,
        grid=(1,),
        in_specs=[pl.BlockSpec((n, w), c2),
                  pl.BlockSpec((kw - 1, n, w), c3),
                  pl.BlockSpec((n, w), c2),
                  pl.BlockSpec((kw, w), c2),
                  pl.BlockSpec((1, w), c2),
                  pl.BlockSpec((nblk, LANES, 2 * LANES), c3),
                  pl.BlockSpec((1, w), c2),
                  pl.BlockSpec((1, w), c2),
                  pl.BlockSpec((1, w), c2)],
        out_specs=[pl.BlockSpec((n, w), c2),
                   pl.BlockSpec((kw - 1, n, w), c3),
                   pl.BlockSpec((n, w), c2)],
        out_shape=[jax.ShapeDtypeStruct((n, w), BF16),
                   jax.ShapeDtypeStruct((kw - 1, n, w), F32),
                   jax.ShapeDtypeStruct((n, w), F32)],
        name="lru_step",
    )(u2d, buf, h0, cw, cb, wg, ba, bx, lam)


def _s5_prep_kernel(arx_ref, aix_ref, dt_ref, br_ref, bi_ref, ar_ref, ai_ref,
                    bbr_ref, bbi_ref, pwr_ref, pwi_ref):
    dt = jnp.exp(dt_ref[...])

    def abar(ar, ai):
        mag = jnp.exp(dt * ar)
        ang = dt * ai
        return mag * jnp.cos(ang), mag * jnp.sin(ang)

    ar, ai = arx_ref[...], aix_ref[...]
    abr, abi = abar(ar, ai)
    den = ar * ar + ai * ai
    nr, ni = abr - 1.0, abi
    q_re = (nr * ar + ni * ai) / den
    q_im = (ni * ar - nr * ai) / den
    br, bi = br_ref[...], bi_ref[...]
    bbr_ref[...] = q_re * br - q_im * bi
    bbi_ref[...] = q_re * bi + q_im * br

    pr, pi = abar(ar_ref[...], ai_ref[...])
    qr, qi = jnp.ones_like(pr), jnp.zeros_like(pr)
    for m in range(pwr_ref.shape[0]):
        pwr_ref[m] = qr
        pwi_ref[m] = qi
        qr, qi = qr * pr - qi * pi, qr * pi + qi * pr


def _s5_prep_call(a_re, a_im, log_dt, b_re, b_im, n_pow):
    g, p, k = b_re.shape
    arx = jnp.repeat(a_re, k, axis=1)
    aix = jnp.repeat(a_im, k, axis=1)
    bbr, bbi, pwr, pwi = pl.pallas_call(
        _s5_prep_kernel,
        out_shape=[jax.ShapeDtypeStruct((g, p * k), F32), jax.ShapeDtypeStruct((g, p * k), F32),
                   jax.ShapeDtypeStruct((n_pow, g, p), F32), jax.ShapeDtypeStruct((n_pow, g, p), F32)],
        name="s5_prep",
    )(arx, aix, log_dt.reshape(g, 1), b_re.reshape(g, p * k), b_im.reshape(g, p * k), a_re, a_im)
    return bbr.reshape(g, p, k), bbi.reshape(g, p, k), pwr, pwi


def _s5_block_weights(bb_re, bb_im, c_re, c_im):
    g, p, k = bb_re.shape
    gb = GROUPS_PER_BLOCK
    nb = g // gb
    eye = jnp.eye(gb, dtype=F32)

    def in_blocks(bb):
        t = bb.reshape(nb, gb, p, k).transpose(0, 1, 3, 2)
        return jnp.einsum('cgkp,gh->cgkhp', t, eye).reshape(nb, gb * k, gb * p)

    def out_blocks(cc):
        t = cc.reshape(nb, gb, k, p)
        return jnp.einsum('cgkp,hg->chpgk', t, eye).reshape(nb, gb * p, gb * k)

    bblk = jnp.concatenate([in_blocks(bb_re), in_blocks(bb_im)], axis=2)
    return bblk, out_blocks(c_re), out_blocks(c_im)


def _cplx_mul(ar, ai, br, bi):
    return ar * br - ai * bi, ar * bi + ai * br


def _cplx_step(ar, ai, hr, hi, br, bi):
    return ar * hr - ai * hi + br, ar * hi + ai * hr + bi


def _s5_kernel(u_ref, bblk_ref, creb_ref, cimb_ref, pwr_ref, pwi_ref, d_ref,
               gy_ref, nre_ref, nim_ref,
               ufb, buf, wbs, wqs, wts, yslab, out32):
    nbat, seq, _ = u_ref.shape
    spb = STATES_PER_BLOCK
    fold = wbs.shape[0] // LANES
    nq = seq // fold

    bre, bim = bblk_ref[0, :, :spb], bblk_ref[0, :, spb:]
    creb, cimb = creb_ref[0], cimb_ref[0]
    wcb = jnp.concatenate([creb, -cimb], axis=0).astype(BF16)
    lag = []
    for m in range(fold):
        wre, wim = _cplx_mul(bre, bim, pwr_ref[0, m:m + 1, :], pwi_ref[0, m:m + 1, :])
        rows = slice((fold - 1 - m) * LANES, (fold - m) * LANES)
        wbs[rows, :spb] = wre.astype(BF16)
        wbs[rows, spb:] = wim.astype(BF16)
        lag.append(_dot(wbs[rows, :], wcb))
    zero_blk = jnp.zeros((LANES, LANES), BF16)
    for r_in in range(fold):
        for r_out in range(fold):
            blk = lag[r_out - r_in].astype(BF16) if r_in <= r_out else zero_blk
            wts[r_in * LANES:(r_in + 1) * LANES, r_out * LANES:(r_out + 1) * LANES] = blk
    ctre, ctim = creb.T, cimb.T
    for r in range(fold):
        pr, pi = pwr_ref[0, r + 1:r + 2, :], pwi_ref[0, r + 1:r + 2, :]
        cols = slice(r * LANES, (r + 1) * LANES)
        wqs[:spb, cols] = (ctre * pr - ctim * pi).T.astype(BF16)
        wqs[spb:, cols] = (-(ctre * pi + ctim * pr)).T.astype(BF16)

    for r in range(fold):
        for b in range(nbat):
            yslab[r, pl.ds(b, nq, stride=nbat), :] = u_ref[b, pl.ds(r, nq, stride=fold), :]
        ufb[:, r * LANES:(r + 1) * LANES] = yslab[r].astype(BF16)

    buf[...] = _dot(ufb[...], wbs[...])

    ar, ai = pwr_ref[0, fold:fold + 1, :], pwi_ref[0, fold:fold + 1, :]
    groups_per_tile = SUBLANES // nbat
    z = jnp.zeros((nbat, spb), F32)

    def tile_step(i, g):
        gr, gi = g
        r0 = pl.multiple_of(i * SUBLANES, SUBLANES)
        xr, xi = buf[pl.ds(r0, SUBLANES), :spb], buf[pl.ds(r0, SUBLANES), spb:]
        hr, hi = [], []
        for k in range(groups_per_tile):
            hr.append(gr)
            hi.append(gi)
            gr, gi = _cplx_step(ar, ai, gr, gi, xr[k * nbat:(k + 1) * nbat], xi[k * nbat:(k + 1) * nbat])
        buf[pl.ds(r0, SUBLANES), :spb] = jnp.concatenate(hr, axis=0)
        buf[pl.ds(r0, SUBLANES), spb:] = jnp.concatenate(hi, axis=0)
        return gr, gi

    g_re, g_im = lax.fori_loop(0, nq // groups_per_tile, tile_step, (z, z), unroll=2)
    for b in range(nbat):
        nre_ref[b] = g_re[b:b + 1, :]
        nim_ref[b] = g_im[b:b + 1, :]

    hsb = buf[...].astype(BF16)
    per_tile = 2
    for n in range(fold // per_tile):
        cols = slice(n * per_tile * LANES, (n + 1) * per_tile * LANES)
        kk = (n + 1) * per_tile * LANES
        yn = _dot(hsb, wqs[:, cols]) + _dot(ufb[:, :kk], wts[:kk, cols])
        for i in range(per_tile):
            yslab[n * per_tile + i] = yn[:, i * LANES:(i + 1) * LANES]

    for b in range(nbat):
        for r in range(fold):
            ys = (yslab[r, pl.ds(b, nq, stride=nbat), :]
                  + d_ref[...] * u_ref[b, pl.ds(r, nq, stride=fold), :])
            out32[b, pl.ds(r, nq, stride=fold), :] = jax.nn.gelu(ys)
        gy_ref[b] = out32[b].astype(BF16)


def _s5_call(u3d, bblk, creb, cimb, pw_re, pw_im, d, *, w):
    bsz, seq, _ = u3d.shape
    nblk = w // LANES
    spb = STATES_PER_BLOCK
    fold = S5_FOLD
    n_pow = fold + 1
    assert SUBLANES % bsz == 0 and seq % fold == 0
    rows = bsz * (seq // fold)
    est = (2 * bsz * seq * LANES * 4 + 2 * bsz * seq * LANES * 2
           + rows * fold * LANES * 2 + 2 * rows * 2 * spb * 4
           + 3 * fold * LANES * 2 * spb * 2
           + rows * fold * LANES * 4 + bsz * seq * LANES * 4
           + rows * 2 * spb * 2
           + 2 * (LANES * 2 * spb + 2 * spb * LANES) * 4)
    return pl.pallas_call(
        _s5_kernel,
        grid=(nblk,),
        in_specs=[pl.BlockSpec((bsz, seq, LANES), lambda c: (0, 0, nblk + c)),
                  pl.BlockSpec((1, LANES, 2 * spb), lambda c: (c, 0, 0)),
                  pl.BlockSpec((1, spb, LANES), lambda c: (c, 0, 0)),
                  pl.BlockSpec((1, spb, LANES), lambda c: (c, 0, 0)),
                  pl.BlockSpec((1, n_pow, spb), lambda c: (c, 0, 0)),
                  pl.BlockSpec((1, n_pow, spb), lambda c: (c, 0, 0)),
                  pl.BlockSpec((1, LANES), lambda c: (0, c))],
        out_specs=[pl.BlockSpec((bsz, seq, LANES), lambda c: (0, 0, c)),
                   pl.BlockSpec((bsz, 1, spb), lambda c: (0, 0, c)),
                   pl.BlockSpec((bsz, 1, spb), lambda c: (0, 0, c))],
        out_shape=[jax.ShapeDtypeStruct((bsz, seq, w), BF16),
                   jax.ShapeDtypeStruct((bsz, 1, nblk * spb), F32),
                   jax.ShapeDtypeStruct((bsz, 1, nblk * spb), F32)],
        scratch_shapes=[pltpu.VMEM((rows, fold * LANES), BF16),
                        pltpu.VMEM((rows, 2 * spb), F32),
                        pltpu.VMEM((fold * LANES, 2 * spb), BF16),
                        pltpu.VMEM((2 * spb, fold * LANES), BF16),
                        pltpu.VMEM((fold * LANES, fold * LANES), BF16),
                        pltpu.VMEM((fold, rows, LANES), F32),
                        pltpu.VMEM((bsz, seq, LANES), F32)],
        compiler_params=pltpu.CompilerParams(dimension_semantics=("parallel",),
                                             vmem_limit_bytes=_vmem_limit(est)),
        name="s5_scan",
    )(u3d, bblk, creb, cimb, pw_re, pw_im, d)


def _s5_step_kernel(u_ref, h0r_ref, h0i_ref, wb_ref, wc_ref, are_ref, aim_ref, d_ref,
                    gy_ref, nre_ref, nim_ref):
    w = u_ref.shape[1]
    nblk = w // LANES
    spb = STATES_PER_BLOCK
    for c in range(nblk):
        cols = slice(c * LANES, (c + 1) * LANES)
        scols = slice(c * spb, (c + 1) * spb)
        u = u_ref[:, cols]
        r = _dot(u.astype(BF16), wb_ref[c])
        hr, hi = _cplx_step(are_ref[:, scols], aim_ref[:, scols], h0r_ref[:, scols], h0i_ref[:, scols],
                            r[:, :spb], r[:, spb:])
        nre_ref[:, scols] = hr
        nim_ref[:, scols] = hi
        y = _dot(hr.astype(BF16), wc_ref[c, :spb, :]) + _dot(hi.astype(BF16), wc_ref[c, spb:, :])
        gy_ref[:, cols] = jax.nn.gelu(y + d_ref[:, cols] * u).astype(BF16)


def _s5_step_call(u2d, h0r, h0i, wb, wc, abr, abi, d, *, w):
    n, n_state = h0r.shape
    c2 = lambda i: (0, 0)
    c3 = lambda i: (0, 0, 0)
    return pl.pallas_call(
        _s5_step_kernel,
        grid=(1,),
        in_specs=[pl.BlockSpec((n, w), lambda i: (0, 1)),
                  pl.BlockSpec((n, n_state), c2),
                  pl.BlockSpec((n, n_state), c2),
                  pl.BlockSpec(wb.shape, c3),
                  pl.BlockSpec(wc.shape, c3),
                  pl.BlockSpec((1, n_state), c2),
                  pl.BlockSpec((1, n_state), c2),
                  pl.BlockSpec((1, w), c2)],
        out_specs=[pl.BlockSpec((n, w), c2),
                   pl.BlockSpec((n, n_state), c2),
                   pl.BlockSpec((n, n_state), c2)],
        out_shape=[jax.ShapeDtypeStruct((n, w), BF16),
                   jax.ShapeDtypeStruct((n, n_state), F32),
                   jax.ShapeDtypeStruct((n, n_state), F32)],
        name="s5_step",
    )(u2d, h0r, h0i, wb, wc, abr, abi, d)


def _side_cast_specs(side, n_steps):
    in_specs, out_specs, out_shapes, nbytes = [], [], [], 0
    for arr, axis in side:
        blk = tuple(s // n_steps if a == axis else s for a, s in enumerate(arr.shape))
        assert arr.ndim == 2 and arr.shape[axis] % n_steps == 0 and blk[1] % LANES == 0 and blk[0] % BF16_ROWS == 0
        idx = (lambda i: (i, 0)) if axis == 0 else (lambda i: (0, i))
        in_specs.append(pl.BlockSpec(blk, idx))
        out_specs.append(pl.BlockSpec(blk, idx))
        out_shapes.append(jax.ShapeDtypeStruct(arr.shape, BF16))
        nbytes += 2 * blk[0] * blk[1] * (4 + 2)
    return in_specs, out_specs, out_shapes, nbytes


def _side_cast(src_refs, dst_refs):
    for src, dst in zip(src_refs, dst_refs, strict=True):
        dst[...] = src[...].astype(BF16)


def _merge_kernel(*refs, n_side):
    hs_ref, gy_ref, sz_ref, x_ref, gt_ref, wp_ref, wg_ref, wo_ref = refs[:8]
    o_ref = refs[8 + n_side]
    _side_cast(refs[8:8 + n_side], refs[9 + n_side:])
    d = x_ref.shape[1]
    ya = _dot(hs_ref[...], wp_ref[...])
    merged = sz_ref[:, :d].astype(F32) * ya
    glu = _dot(gy_ref[...], wg_ref[...])
    yb = glu[:, :d] * _sigmoid(glu[:, d:])
    merged = merged + sz_ref[:, d:].astype(F32) * yb
    o = _dot(merged.astype(BF16), wo_ref[...])
    o_ref[...] = x_ref[...] + _rows(gt_ref) * o


def _merge_call(hs, gy, sz, x2d, mod, wp, wg, wo, *, tm, tiles_per_batch, side=()):
    m, d = x2d.shape
    w = hs.shape[1]
    side_in, side_out, side_shapes, side_bytes = _side_cast_specs(side, m // tm)
    est = (2 * (2 * tm * w * 2 + tm * 2 * d * 2 + 2 * tm * d * 4) + (wp.size + wg.size + wo.size) * 2
           + tm * d * 4 * 5 + side_bytes)
    one = pl.Buffered(1)
    return pl.pallas_call(
        functools.partial(_merge_kernel, n_side=len(side)),
        grid=(m // tm,),
        in_specs=[pl.BlockSpec((tm, w), lambda i: (i, 0)),
                  pl.BlockSpec((tm, w), lambda i: (i, 0)),
                  pl.BlockSpec((tm, 2 * d), lambda i: (i, 0)),
                  pl.BlockSpec((tm, d), lambda i: (i, 0)),
                  _mod_spec(mod, 2, d, tiles_per_batch, 1),
                  pl.BlockSpec(wp.shape, lambda i: (0, 0), pipeline_mode=one),
                  pl.BlockSpec(wg.shape, lambda i: (0, 0), pipeline_mode=one),
                  pl.BlockSpec(wo.shape, lambda i: (0, 0), pipeline_mode=one)] + side_in,
        out_specs=[pl.BlockSpec((tm, d), lambda i: (i, 0))] + side_out,
        out_shape=[jax.ShapeDtypeStruct((m, d), F32)] + side_shapes,
        compiler_params=pltpu.CompilerParams(dimension_semantics=("parallel",),
                                             vmem_limit_bytes=_vmem_limit(est)),
        name="merge",
    )(hs, gy, sz, x2d, mod, wp, wg, wo, *(a for a, _ in side))


def _mlp_kernel(x_ref, sc_ref, sh_ref, gt_ref, g2_ref, gf_ref, wu_ref, wd_ref, o_ref, h_scr, inv_scr, row_scr, *,
                final_norm):
    j = pl.program_id(1)
    nj = pl.num_programs(1)
    n_rows, d = x_ref.shape

    @pl.when(j == 0)
    def _():
        _norm_mod_bf16(x_ref, g2_ref, sc_ref, sh_ref, h_scr, inv_scr, row_scr)
        o_ref[...] = jnp.zeros(o_ref.shape, F32)

    up = _dot(h_scr[...], wu_ref[...])
    act = jnp.square(jnp.maximum(up, 0.0)).astype(BF16)
    nc = min(d, MLP_OUT_CHUNK)
    for c in range(d // nc):
        cols = slice(c * nc, (c + 1) * nc)
        o_ref[:, cols] += _dot(act, wd_ref[:, cols])

    @pl.when(j == nj - 1)
    def _():
        def chunk(rows):
            x2 = x_ref[rows, :] + _rows(gt_ref, rows) * o_ref[rows, :]
            if final_norm:
                ms = jnp.mean(x2 * x2, axis=-1, keepdims=True)
                x2 = (x2 * lax.rsqrt(ms + EPS)) * gf_ref[...]
            o_ref[rows, :] = x2

        _row_groups(n_rows, min(n_rows, EPILOGUE_ROWS), chunk, unroll=1)


def _mlp_call(x2d, mod, g2, gf, w_up, w_down, *, tm, tiles_per_batch, final_norm, tf):
    m, d = x2d.shape
    dff = w_up.shape[1]
    assert w_up.dtype == BF16 and w_down.dtype == BF16
    est = 3 * tm * d * 4 + tm * d * 2 + 4 * d * tf * 2 + tm * tf * 6 + tm * MLP_OUT_CHUNK * 4
    return pl.pallas_call(
        functools.partial(_mlp_kernel, final_norm=final_norm),
        grid=(m // tm, dff // tf),
        in_specs=[pl.BlockSpec((tm, d), lambda i, j: (i, 0), pipeline_mode=pl.Buffered(1)),
                  _mod_spec(mod, 4, d, tiles_per_batch, 2),
                  _mod_spec(mod, 3, d, tiles_per_batch, 2),
                  _mod_spec(mod, 5, d, tiles_per_batch, 2),
                  pl.BlockSpec((1, d), lambda i, j: (0, 0)),
                  pl.BlockSpec((1, d), lambda i, j: (0, 0)),
                  pl.BlockSpec((d, tf), lambda i, j: (0, j)),
                  pl.BlockSpec((tf, d), lambda i, j: (j, 0))],
        out_specs=pl.BlockSpec((tm, d), lambda i, j: (i, 0)),
        out_shape=jax.ShapeDtypeStruct((m, d), F32),
        scratch_shapes=[pltpu.VMEM((tm, d), BF16), pltpu.VMEM((tm, LANES), F32),
                        pltpu.VMEM((2 * SUBLANES, d), F32)],
        compiler_params=pltpu.CompilerParams(dimension_semantics=("parallel", "arbitrary"),
                                             vmem_limit_bytes=_vmem_limit(est)),
        name="mlp",
    )(x2d, mod, mod, mod, g2, gf, w_up, w_down)


def _pick_tile(n, pref):
    t = min(n, pref)
    assert n % t == 0, (n, t)
    return t


def kernel(x_prompt, x_sample, state_conv, state_lru, state_ssm_re, state_ssm_im, c_prompt, c_sample, w_ada, b_ada, g_norm1, g_norm2, w_in, conv_w, conv_b, w_rg_a, b_rg_a, w_rg_x, b_rg_x, lru_lambda, w_proj_a, ssm_a_re, ssm_a_im, ssm_log_dt, ssm_b_re, ssm_b_im, ssm_c_re, ssm_c_im, ssm_d, w_glu, w_out, w_up, w_down, g_final):
    depth = w_ada.shape[0]
    nb, seq, d = x_prompt.shape
    ns = x_sample.shape[0]
    assert x_sample.shape[1] == 1
    w = conv_w.shape[2]
    n_state = ssm_a_re.shape[1] * ssm_a_re.shape[2]
    assert ssm_b_re.shape[2:] == (SSM_STATE, SSM_GROUP) and w % LANES == 0
    assert w_in.shape[2] == 2 * w + 2 * d

    tl = _pick_tile(seq, 256)
    tm_in = _pick_tile(seq, 1024)
    tm_mg = _pick_tile(seq, 256)
    tm_mlp = _pick_tile(seq, 1024)

    xp = x_prompt.reshape(nb * seq, d)
    xs = x_sample.reshape(ns, d)
    assert ns % SUBLANES == 0
    pad = (-nb) % SUBLANES
    c_all = jnp.concatenate([c_sample, c_prompt, jnp.zeros((pad, d), F32)], axis=0)

    outs_p = [[] for _ in range(4)]
    outs_s = [[] for _ in range(4)]
    for l in range(depth):
        last = l == depth - 1
        mod_s, mod_p = _mod_call(c_all, ns, w_ada[l], b_ada[l].reshape(1, -1))
        mod_p = mod_p.reshape(nb + pad, 1, N_MOD * d)

        g1 = g_norm1[l].reshape(1, d)
        g2 = g_norm2[l].reshape(1, d)
        gf = g_final.reshape(1, d)
        cw, cb = conv_w[l], conv_b[l].reshape(1, w)
        wg = jnp.concatenate([w_rg_a[l], w_rg_x[l]], axis=2).astype(BF16)
        ba, bx, lam = b_rg_a[l].reshape(1, w), b_rg_x[l].reshape(1, w), lru_lambda[l].reshape(1, w)
        n_pow = S5_FOLD + 1
        bbr, bbi, pwr, pwi = _s5_prep_call(ssm_a_re[l], ssm_a_im[l], ssm_log_dt[l], ssm_b_re[l], ssm_b_im[l], n_pow)
        bblk, creb, cimb = _s5_block_weights(bbr, bbi, ssm_c_re[l], ssm_c_im[l])
        wb = bblk.astype(BF16)
        wc = jnp.concatenate([creb, -cimb], axis=1).astype(BF16)
        abr, abi = pwr[1].reshape(1, n_state), pwi[1].reshape(1, n_state)
        nblk = w // LANES
        pw_rows = [v.reshape(n_pow, nblk, STATES_PER_BLOCK).transpose(1, 0, 2) for v in (pwr, pwi)]
        dskip = ssm_d[l].reshape(1, w)

        u_s, sz_s, w_in_b = _inproj_call(xs, mod_s, g1, w_in[l], tm=ns, tiles_per_batch=1, w_mix=w)

        u_p, sz_p = _inproj_call(xp, mod_p, g1, w_in_b, tm=tm_in, tiles_per_batch=seq // tm_in, w_mix=w)
        u3 = u_p.reshape(nb, seq, 2 * w)
        hs_p, nconv_p, nlru_p, wp, wgl, wo = _lru_call(u3, cw, cb, wg, ba, bx, lam, tl=tl,
                                                       side=((w_proj_a[l], 1), (w_glu[l], 1), (w_out[l], 1)))
        gy_p, nre_p, nim_p = _s5_call(u3, bblk, creb, cimb, *pw_rows, dskip, w=w)
        x1_p, w_up_b, w_down_b = _merge_call(hs_p.reshape(nb * seq, w), gy_p.reshape(nb * seq, w), sz_p, xp, mod_p,
                                             wp, wgl, wo, tm=tm_mg, tiles_per_batch=seq // tm_mg,
                                             side=((w_up[l], 1), (w_down[l], 0)))
        xp = _mlp_call(x1_p, mod_p, g2, gf, w_up_b, w_down_b, tm=tm_mlp, tiles_per_batch=seq // tm_mlp,
                       final_norm=last, tf=1024)

        buf = jnp.transpose(state_conv[l], (1, 0, 2))
        hs_s, nconv_s, nlru_s = _lru_step_call(u_s, buf, state_lru[l], cw, cb, wg, ba, bx, lam)
        gy_s, nre_s, nim_s = _s5_step_call(u_s, state_ssm_re[l].reshape(ns, n_state),
                                           state_ssm_im[l].reshape(ns, n_state), wb, wc, abr, abi, dskip, w=w)
        (x1_s,) = _merge_call(hs_s, gy_s, sz_s, xs, mod_s, wp, wgl, wo, tm=ns, tiles_per_batch=1)
        xs = _mlp_call(x1_s, mod_s, g2, gf, w_up_b, w_down_b, tm=ns, tiles_per_batch=1, final_norm=last, tf=1024)

        gshape = ssm_a_re.shape[1:]
        for acc, v in zip(outs_p, (nconv_p, nlru_p.reshape(nb, w),
                                   nre_p.reshape((nb,) + gshape), nim_p.reshape((nb,) + gshape))):
            acc.append(v)
        for acc, v in zip(outs_s, (jnp.transpose(nconv_s, (1, 0, 2)), nlru_s,
                                   nre_s.reshape((ns,) + gshape), nim_s.reshape((ns,) + gshape))):
            acc.append(v)

    y_prompt = xp.reshape(nb, seq, d)
    y_sample = xs.reshape(ns, 1, d)
    stack = lambda a: a[0][None] if len(a) == 1 else jnp.stack(a)
    return (y_prompt, y_sample) + tuple(stack(a) for a in outs_p) + tuple(stack(a) for a in outs_s)
```

```python
import functools

import jax
import jax.numpy as jnp
from jax import lax
from jax.experimental import pallas as pl
from jax.experimental.pallas import tpu as pltpu

F32 = jnp.float32
BF16 = jnp.bfloat16

LANES = 128
SUBLANES = 8
VMEM_PHYSICAL_BYTES = 64 * 1024 * 1024
VMEM_LIMIT_CAP_BYTES = VMEM_PHYSICAL_BYTES - 6 * 1024 * 1024

LRU_C = 8.0
EPS = 1e-6
N_MOD = 6
MLP_MOD_FIRST = 3
SSM_GROUP = 16
SSM_STATE = 64
GROUPS_PER_BLOCK = LANES // SSM_GROUP
STATES_PER_BLOCK = GROUPS_PER_BLOCK * SSM_STATE


VMEM_SLACK_BYTES = 8 * 1024 * 1024


def _vmem_limit(nbytes):
    return int(min(VMEM_LIMIT_CAP_BYTES, max(32 * 1024 * 1024, nbytes + VMEM_SLACK_BYTES)))


BF16_ROWS = 2 * SUBLANES
EPILOGUE_ROWS = 256
MLP_OUT_CHUNK = 512
INPROJ_CHUNK = 256
S5_FOLD = 8


def _rows(ref, rows=None):
    if len(ref.shape) == 3:
        return ref[0]
    return ref[...] if rows is None else ref[rows, :]


def _row_groups(n_rows, group, fn, unroll):
    assert n_rows % group == 0

    def body(i, carry):
        fn(pl.ds(pl.multiple_of(i * group, group), group))
        return carry

    lax.fori_loop(0, n_rows // group, body, 0, unroll=unroll)


def _rms_scale_pass(src, inv_scr, d):
    def fn(rows):
        x = src(rows)
        xx = x * x
        part = xx[:, 0:LANES]
        for c in range(1, d // LANES):
            part = part + xx[:, c * LANES:(c + 1) * LANES]
        inv_scr[rows, :] = part

    _row_groups(inv_scr.shape[0], SUBLANES, fn, unroll=8)
    ss = jnp.sum(inv_scr[...], axis=-1, keepdims=True)
    inv_scr[...] = jnp.broadcast_to(lax.rsqrt(ss * (1.0 / d) + EPS), inv_scr.shape)


def _lanes(inv, d):
    return jnp.concatenate([inv] * (d // LANES), axis=1)


def _sublane_rows(scr, slot, row=None):
    rows = slice(slot * SUBLANES, (slot + 1) * SUBLANES)
    if row is None:
        return scr[rows, :]
    scr[rows, :] = jnp.broadcast_to(row, (SUBLANES, row.shape[1]))
    return None


def _norm_mod_bf16(x_ref, g_ref, sc_ref, sh_ref, h_scr, inv_scr, row_scr):
    n_rows, d = x_ref.shape
    _rms_scale_pass(lambda rows: x_ref[rows, :], inv_scr, d)
    per_token = len(sc_ref.shape) == 2
    if not per_token:
        _sublane_rows(row_scr, 0, g_ref[...] * (1.0 + sc_ref[0]))
        _sublane_rows(row_scr, 1, sh_ref[0])

    def fn(rows):
        halves = []
        for k in range(BF16_ROWS // SUBLANES):
            r8 = pl.ds(pl.multiple_of(rows.start + k * SUBLANES, SUBLANES), SUBLANES)
            gm = g_ref[...] * (1.0 + sc_ref[r8, :]) if per_token else _sublane_rows(row_scr, 0)
            sh = sh_ref[r8, :] if per_token else _sublane_rows(row_scr, 1)
            halves.append((x_ref[r8, :] * _lanes(inv_scr[r8, :], d)) * gm + sh)
        h_scr[rows, :] = jnp.concatenate(halves, axis=0).astype(BF16)

    _row_groups(n_rows, BF16_ROWS, fn, unroll=2)


def _dot(a, b):
    return jnp.dot(a, b, preferred_element_type=F32)


def _sigmoid(x):
    return 0.5 * jnp.tanh(0.5 * x) + 0.5


def _mod_kernel(c_ref, w_ref, b_ref, os_ref, op_ref):
    c = c_ref[...]
    cs = (c * _sigmoid(c)).astype(BF16)
    mod = _dot(cs, w_ref[...].astype(BF16)) + b_ref[...]
    n_sample = os_ref.shape[0]
    os_ref[...] = mod[:n_sample]
    op_ref[...] = mod[n_sample:]


def _mod_call(c_all, n_sample, w_ada, b_ada, n, tn=1024):
    n_rows, d = c_all.shape
    assert n % tn == 0
    est = 2 * (d * tn * 4) + 3 * n_rows * tn * 4 + n_rows * d * 4 * 2 + d * tn * 2
    return pl.pallas_call(
        _mod_kernel,
        grid=(n // tn,),
        in_specs=[pl.BlockSpec((n_rows, d), lambda j: (0, 0)),
                  pl.BlockSpec((d, tn), lambda j: (0, j)),
                  pl.BlockSpec((1, tn), lambda j: (0, j))],
        out_specs=[pl.BlockSpec((n_sample, tn), lambda j: (0, j)),
                   pl.BlockSpec((n_rows - n_sample, tn), lambda j: (0, j))],
        out_shape=[jax.ShapeDtypeStruct((n_sample, n), F32),
                   jax.ShapeDtypeStruct((n_rows - n_sample, n), F32)],
        compiler_params=pltpu.CompilerParams(dimension_semantics=("parallel",),
                                             vmem_limit_bytes=_vmem_limit(est)),
        name="mod",
    )(c_all, w_ada, b_ada)


def _mod_spec(mod, piece, d, tiles_per_batch, ngrid):
    if mod.ndim == 3:
        if ngrid == 1:
            return pl.BlockSpec((1, 1, d), lambda i: (i // tiles_per_batch, 0, piece))
        return pl.BlockSpec((1, 1, d), lambda i, j: (i // tiles_per_batch, 0, piece))
    rows = mod.shape[0]
    if ngrid == 1:
        return pl.BlockSpec((rows, d), lambda i: (0, piece))
    return pl.BlockSpec((rows, d), lambda i, j: (0, piece))


def _inproj_kernel(*refs, n_u_tiles, emit_w):
    x_ref, sc_ref, sh_ref, g_ref, w_ref, u_ref, z_ref = refs[:7]
    h_scr, inv_scr, row_scr = refs[-3:]
    j = pl.program_id(1)

    @pl.when(j == 0)
    def _():
        _norm_mod_bf16(x_ref, g_ref, sc_ref, sh_ref, h_scr, inv_scr, row_scr)

    if emit_w:
        wb_ref = refs[7]
        wb_ref[...] = w_ref[...].astype(BF16)
        w_ref = wb_ref
    tn = w_ref.shape[1]
    nc = min(tn, INPROJ_CHUNK)

    @pl.when(j < n_u_tiles)
    def _():
        u_ref[...] = _dot(h_scr[...], w_ref[...])

    @pl.when(j >= n_u_tiles)
    def _():
        for c in range(tn // nc):
            cols = slice(c * nc, (c + 1) * nc)
            z_ref[:, cols] = _dot(h_scr[...], w_ref[:, cols]).astype(BF16)


def _inproj_call(x2d, mod, g1, w_in, *, tm, tiles_per_batch, w_mix, tn=1024):
    m, d = x2d.shape
    n = w_in.shape[1]
    n_u = 2 * w_mix
    n_u_tiles = n_u // tn
    emit_w = w_in.dtype != BF16
    assert not emit_w or m == tm
    wbytes = w_in.dtype.itemsize
    est = (2 * tm * d * 4 + 2 * d * tn * wbytes + 2 * tm * tn * 4 + 2 * tm * tn * 2 + tm * d * 2 + tm * tn * 4
           + (3 * d * tn * 2 if emit_w else 0))
    out_specs = [pl.BlockSpec((tm, tn), lambda i, j: (i, jnp.minimum(j, n_u_tiles - 1))),
                 pl.BlockSpec((tm, tn), lambda i, j: (i, jnp.maximum(j - n_u_tiles, 0)))]
    out_shape = [jax.ShapeDtypeStruct((m, n_u), F32),
                 jax.ShapeDtypeStruct((m, n - n_u), BF16)]
    if emit_w:
        out_specs.append(pl.BlockSpec((d, tn), lambda i, j: (0, j)))
        out_shape.append(jax.ShapeDtypeStruct((d, n), BF16))
    return pl.pallas_call(
        functools.partial(_inproj_kernel, n_u_tiles=n_u_tiles, emit_w=emit_w),
        grid=(m // tm, n // tn),
        in_specs=[pl.BlockSpec((tm, d), lambda i, j: (i, 0)),
                  _mod_spec(mod, 1, d, tiles_per_batch, 2),
                  _mod_spec(mod, 0, d, tiles_per_batch, 2),
                  pl.BlockSpec((1, d), lambda i, j: (0, 0)),
                  pl.BlockSpec((d, tn), lambda i, j: (0, j))],
        out_specs=out_specs,
        out_shape=out_shape,
        scratch_shapes=[pltpu.VMEM((tm, d), BF16), pltpu.VMEM((tm, LANES), F32),
                        pltpu.VMEM((2 * SUBLANES, d), F32)],
        compiler_params=pltpu.CompilerParams(dimension_semantics=("parallel", "arbitrary"),
                                             vmem_limit_bytes=_vmem_limit(est)),
        name="inproj",
    )(x2d, mod, mod, g1, w_in)


def _lru_gate_block(uc, g, ba, bx, sp):
    r = _sigmoid(g[:, :LANES] + ba)
    i = _sigmoid(g[:, LANES:] + bx)
    log_a = (-LRU_C * r) * sp
    a = jnp.exp(log_a)
    mult = jnp.sqrt(1.0 - jnp.exp(2.0 * log_a))
    return a, (mult * i) * uc


def _lru_kernel(*refs, n_side):
    u_ref, cw_ref, cb_ref, wg_ref, ba_ref, bx_ref, lam_ref = refs[:7]
    hs_ref, nconv_ref, nlru_ref = refs[7 + n_side:10 + n_side]
    ext, slab, a_s, b_s, carry = refs[-5:]
    _side_cast(refs[7:7 + n_side], refs[10 + n_side:10 + 2 * n_side])
    t = pl.program_id(0)
    nt = pl.num_programs(0)
    nbat, tl, w = u_ref.shape
    nblk = w // LANES
    hist = SUBLANES
    kw = cw_ref.shape[0]
    steps_per_tile = SUBLANES // nbat

    @pl.when(t == 0)
    def _():
        ext[:, 0:hist, :] = jnp.zeros((nbat, hist, w), F32)
        carry[...] = jnp.zeros_like(carry)

    for b in range(nbat):
        ext[b, hist:hist + tl, :] = u_ref[b]
        for c in range(nblk):
            cols = slice(c * LANES, (c + 1) * LANES)
            acc = cb_ref[:, cols] + ext[b, hist - kw + 1:hist - kw + 1 + tl, cols] * cw_ref[0:1, cols]
            for k in range(1, kw):
                acc = acc + ext[b, hist - kw + 1 + k:hist - kw + 1 + k + tl, cols] * cw_ref[k:k + 1, cols]
            slab[c, pl.ds(b, tl, stride=nbat), :] = acc
        tail = ext[b, hist + tl - (kw - 1):hist + tl, :]
        ext[b, hist - (kw - 1):hist, :] = tail

        @pl.when(t == nt - 1)
        def _(b=b, tail=tail):
            nconv_ref[b] = tail

    sp = jax.nn.softplus(-lam_ref[...])
    for h in range(nblk):
        cols = slice(h * LANES, (h + 1) * LANES)
        uc = slab[h]
        g = _dot(uc.astype(BF16), wg_ref[h])
        a, b = _lru_gate_block(uc, g, ba_ref[:, cols], bx_ref[:, cols], sp[:, cols])
        a_s[:, cols] = a
        b_s[:, cols] = b

    def tile_step(i, hcur):
        r0 = pl.multiple_of(i * SUBLANES, SUBLANES)
        at, bt = a_s[pl.ds(r0, SUBLANES), :], b_s[pl.ds(r0, SUBLANES), :]
        hs = []
        for k in range(steps_per_tile):
            hcur = at[k * nbat:(k + 1) * nbat] * hcur + bt[k * nbat:(k + 1) * nbat]
            hs.append(hcur)
        htile = jnp.concatenate(hs, axis=0)
        for c in range(nblk):
            slab[c, pl.ds(r0, SUBLANES), :] = htile[:, c * LANES:(c + 1) * LANES]
        return hcur

    h_end = lax.fori_loop(0, tl // steps_per_tile, tile_step, carry[...], unroll=2)
    carry[...] = h_end

    for b in range(nbat):
        for c in range(nblk):
            hs_ref[b, :, c * LANES:(c + 1) * LANES] = slab[c, pl.ds(b, tl, stride=nbat), :].astype(BF16)

    @pl.when(t == nt - 1)
    def _():
        for b in range(nbat):
            nlru_ref[b] = h_end[b:b + 1, :]


def _lru_call(u3d, cw, cb, wg, ba, bx, lam, *, tl, side=()):
    bsz, seq, _ = u3d.shape
    w = cw.shape[1]
    kw = cw.shape[0]
    nblk = w // LANES
    assert SUBLANES % bsz == 0
    rows = bsz * tl
    side_in, side_out, side_shapes, side_bytes = _side_cast_specs(side, seq // tl)
    est = (2 * rows * w * 4 + 2 * rows * w * 2 + bsz * (tl + SUBLANES) * w * 4 + 3 * rows * w * 4
           + nblk * LANES * 2 * LANES * 2 * 2 + rows * 2 * LANES * 4 + side_bytes)
    const2 = lambda t: (0, 0)
    return pl.pallas_call(
        functools.partial(_lru_kernel, n_side=len(side)),
        grid=(seq // tl,),
        in_specs=[pl.BlockSpec((bsz, tl, w), lambda t: (0, t, 0)),
                  pl.BlockSpec((kw, w), const2),
                  pl.BlockSpec((1, w), const2),
                  pl.BlockSpec((nblk, LANES, 2 * LANES), lambda t: (0, 0, 0)),
                  pl.BlockSpec((1, w), const2),
                  pl.BlockSpec((1, w), const2),
                  pl.BlockSpec((1, w), const2)] + side_in,
        out_specs=[pl.BlockSpec((bsz, tl, w), lambda t: (0, t, 0)),
                   pl.BlockSpec((bsz, kw - 1, w), lambda t: (0, 0, 0)),
                   pl.BlockSpec((bsz, 1, w), lambda t: (0, 0, 0))] + side_out,
        out_shape=[jax.ShapeDtypeStruct((bsz, seq, w), BF16),
                   jax.ShapeDtypeStruct((bsz, kw - 1, w), F32),
                   jax.ShapeDtypeStruct((bsz, 1, w), F32)] + side_shapes,
        scratch_shapes=[pltpu.VMEM((bsz, tl + SUBLANES, w), F32),
                        pltpu.VMEM((nblk, rows, LANES), F32),
                        pltpu.VMEM((rows, w), F32),
                        pltpu.VMEM((rows, w), F32),
                        pltpu.VMEM((bsz, w), F32)],
        compiler_params=pltpu.CompilerParams(dimension_semantics=("arbitrary",),
                                             vmem_limit_bytes=_vmem_limit(est)),
        name="lru_scan",
    )(u3d, cw, cb, wg, ba, bx, lam, *(a for a, _ in side))


def _lru_step_kernel(u_ref, buf_ref, h0_ref, cw_ref, cb_ref, wg_ref, ba_ref, bx_ref, lam_ref,
                     hs_ref, nconv_ref, nlru_ref):
    w = cw_ref.shape[1]
    kw = cw_ref.shape[0]
    nblk = w // LANES
    u = u_ref[...]
    uc = cb_ref[...] + buf_ref[0] * cw_ref[0:1, :]
    for k in range(1, kw - 1):
        uc = uc + buf_ref[k] * cw_ref[k:k + 1, :]
    uc = uc + u * cw_ref[kw - 1:kw, :]
    for k in range(kw - 2):
        nconv_ref[k] = buf_ref[k + 1]
    nconv_ref[kw - 2] = u
    sp = jax.nn.softplus(-lam_ref[...])
    for h in range(nblk):
        cols = slice(h * LANES, (h + 1) * LANES)
        uch = uc[:, cols]
        g = _dot(uch.astype(BF16), wg_ref[h])
        a, b = _lru_gate_block(uch, g, ba_ref[:, cols], bx_ref[:, cols], sp[:, cols])
        hn = a * h0_ref[:, cols] + b
        nlru_ref[:, cols] = hn
        hs_ref[:, cols] = hn.astype(BF16)


def _lru_step_call(u2d, buf, h0, cw, cb, wg, ba, bx, lam):
    n = h0.shape[0]
    w = cw.shape[1]
    kw = cw.shape[0]
    nblk = w // LANES
    c2 = lambda i: (0, 0)
    c3 = lambda i: (0, 0, 0)
    return pl.pallas_call(
        _lru_step_kernel,
        grid=(1,),
        in_specs=[pl.BlockSpec((n, w), c2),
                  pl.BlockSpec((kw - 1, n, w), c3),
                  pl.BlockSpec((n, w), c2),
                  pl.BlockSpec((kw, w), c2),
                  pl.BlockSpec((1, w), c2),
                  pl.BlockSpec((nblk, LANES, 2 * LANES), c3),
                  pl.BlockSpec((1, w), c2),
                  pl.BlockSpec((1, w), c2),
                  pl.BlockSpec((1, w), c2)],
        out_specs=[pl.BlockSpec((n, w), c2),
                   pl.BlockSpec((kw - 1, n, w), c3),
                   pl.BlockSpec((n, w), c2)],
        out_shape=[jax.ShapeDtypeStruct((n, w), BF16),
                   jax.ShapeDtypeStruct((kw - 1, n, w), F32),
                   jax.ShapeDtypeStruct((n, w), F32)],
        name="lru_step",
    )(u2d, buf, h0, cw, cb, wg, ba, bx, lam)


def _s5_prep_kernel(arx_ref, aix_ref, dt_ref, br_ref, bi_ref, ar_ref, ai_ref,
                    bbr_ref, bbi_ref, pwr_ref, pwi_ref):
    dt = jnp.exp(dt_ref[...])

    def abar(ar, ai):
        mag = jnp.exp(dt * ar)
        ang = dt * ai
        return mag * jnp.cos(ang), mag * jnp.sin(ang)

    ar, ai = arx_ref[...], aix_ref[...]
    abr, abi = abar(ar, ai)
    den = ar * ar + ai * ai
    nr, ni = abr - 1.0, abi
    q_re = (nr * ar + ni * ai) / den
    q_im = (ni * ar - nr * ai) / den
    br, bi = br_ref[...], bi_ref[...]
    bbr_ref[...] = q_re * br - q_im * bi
    bbi_ref[...] = q_re * bi + q_im * br

    pr, pi = abar(ar_ref[...], ai_ref[...])
    qr, qi = jnp.ones_like(pr), jnp.zeros_like(pr)
    for m in range(pwr_ref.shape[0]):
        pwr_ref[m] = qr
        pwi_ref[m] = qi
        qr, qi = qr * pr - qi * pi, qr * pi + qi * pr


def _s5_prep_call(a_re, a_im, log_dt, b_re, b_im, n_pow):
    g, p, k = b_re.shape
    arx = jnp.repeat(a_re, k, axis=1)
    aix = jnp.repeat(a_im, k, axis=1)
    bbr, bbi, pwr, pwi = pl.pallas_call(
        _s5_prep_kernel,
        out_shape=[jax.ShapeDtypeStruct((g, p * k), F32), jax.ShapeDtypeStruct((g, p * k), F32),
                   jax.ShapeDtypeStruct((n_pow, g, p), F32), jax.ShapeDtypeStruct((n_pow, g, p), F32)],
        name="s5_prep",
    )(arx, aix, log_dt.reshape(g, 1), b_re.reshape(g, p * k), b_im.reshape(g, p * k), a_re, a_im)
    return bbr.reshape(g, p, k), bbi.reshape(g, p, k), pwr, pwi


def _s5_block_weights(bb_re, bb_im, c_re, c_im):
    g, p, k = bb_re.shape
    gb = GROUPS_PER_BLOCK
    nb = g // gb
    eye = jnp.eye(gb, dtype=F32)

    def in_blocks(bb):
        t = bb.reshape(nb, gb, p, k).transpose(0, 1, 3, 2)
        return jnp.einsum('cgkp,gh->cgkhp', t, eye).reshape(nb, gb * k, gb * p)

    def out_blocks(cc):
        t = cc.reshape(nb, gb, k, p)
        return jnp.einsum('cgkp,hg->chpgk', t, eye).reshape(nb, gb * p, gb * k)

    bblk = jnp.concatenate([in_blocks(bb_re), in_blocks(bb_im)], axis=2)
    return bblk, out_blocks(c_re), out_blocks(c_im)


def _cplx_mul(ar, ai, br, bi):
    return ar * br - ai * bi, ar * bi + ai * br


def _cplx_step(ar, ai, hr, hi, br, bi):
    return ar * hr - ai * hi + br, ar * hi + ai * hr + bi


def _s5_kernel(u_ref, bblk_ref, creb_ref, cimb_ref, pwr_ref, pwi_ref, d_ref, c_ref, wa_ref, ba_ref,
               gy_ref, nre_ref, nim_ref, ms_ref, mp_ref,
               ufb, buf, wbs, wqs, wts, yslab, out32):
    _mod_kernel(c_ref, wa_ref, ba_ref, ms_ref, mp_ref)
    nbat, seq, _ = u_ref.shape
    spb = STATES_PER_BLOCK
    fold = wbs.shape[0] // LANES
    nq = seq // fold

    bre, bim = bblk_ref[0, :, :spb], bblk_ref[0, :, spb:]
    creb, cimb = creb_ref[0], cimb_ref[0]
    wcb = jnp.concatenate([creb, -cimb], axis=0).astype(BF16)
    lag = []
    for m in range(fold):
        wre, wim = _cplx_mul(bre, bim, pwr_ref[0, m:m + 1, :], pwi_ref[0, m:m + 1, :])
        rows = slice((fold - 1 - m) * LANES, (fold - m) * LANES)
        wbs[rows, :spb] = wre.astype(BF16)
        wbs[rows, spb:] = wim.astype(BF16)
        lag.append(_dot(wbs[rows, :], wcb))
    zero_blk = jnp.zeros((LANES, LANES), BF16)
    for r_in in range(fold):
        for r_out in range(fold):
            blk = lag[r_out - r_in].astype(BF16) if r_in <= r_out else zero_blk
            wts[r_in * LANES:(r_in + 1) * LANES, r_out * LANES:(r_out + 1) * LANES] = blk
    ctre, ctim = creb.T, cimb.T
    for r in range(fold):
        pr, pi = pwr_ref[0, r + 1:r + 2, :], pwi_ref[0, r + 1:r + 2, :]
        cols = slice(r * LANES, (r + 1) * LANES)
        wqs[:spb, cols] = (ctre * pr - ctim * pi).T.astype(BF16)
        wqs[spb:, cols] = (-(ctre * pi + ctim * pr)).T.astype(BF16)

    for r in range(fold):
        for b in range(nbat):
            yslab[r, pl.ds(b, nq, stride=nbat), :] = u_ref[b, pl.ds(r, nq, stride=fold), :]
        ufb[:, r * LANES:(r + 1) * LANES] = yslab[r].astype(BF16)

    buf[...] = _dot(ufb[...], wbs[...])

    ar, ai = pwr_ref[0, fold:fold + 1, :], pwi_ref[0, fold:fold + 1, :]
    groups_per_tile = SUBLANES // nbat
    z = jnp.zeros((nbat, spb), F32)

    def tile_step(i, g):
        gr, gi = g
        r0 = pl.multiple_of(i * SUBLANES, SUBLANES)
        xr, xi = buf[pl.ds(r0, SUBLANES), :spb], buf[pl.ds(r0, SUBLANES), spb:]
        hr, hi = [], []
        for k in range(groups_per_tile):
            hr.append(gr)
            hi.append(gi)
            gr, gi = _cplx_step(ar, ai, gr, gi, xr[k * nbat:(k + 1) * nbat], xi[k * nbat:(k + 1) * nbat])
        buf[pl.ds(r0, SUBLANES), :spb] = jnp.concatenate(hr, axis=0)
        buf[pl.ds(r0, SUBLANES), spb:] = jnp.concatenate(hi, axis=0)
        return gr, gi

    g_re, g_im = lax.fori_loop(0, nq // groups_per_tile, tile_step, (z, z), unroll=2)
    for b in range(nbat):
        nre_ref[b] = g_re[b:b + 1, :]
        nim_ref[b] = g_im[b:b + 1, :]

    hsb = buf[...].astype(BF16)
    per_tile = 2
    for n in range(fold // per_tile):
        cols = slice(n * per_tile * LANES, (n + 1) * per_tile * LANES)
        kk = (n + 1) * per_tile * LANES
        yn = _dot(hsb, wqs[:, cols]) + _dot(ufb[:, :kk], wts[:kk, cols])
        for i in range(per_tile):
            yslab[n * per_tile + i] = yn[:, i * LANES:(i + 1) * LANES]

    for b in range(nbat):
        for r in range(fold):
            ys = (yslab[r, pl.ds(b, nq, stride=nbat), :]
                  + d_ref[...] * u_ref[b, pl.ds(r, nq, stride=fold), :])
            out32[b, pl.ds(r, nq, stride=fold), :] = jax.nn.gelu(ys)
        gy_ref[b] = out32[b].astype(BF16)


def _s5_call(u3d, bblk, creb, cimb, pw_re, pw_im, d, c_all, n_sample, w_ada, b_ada, first_col, *, w):
    bsz, seq, _ = u3d.shape
    nblk = w // LANES
    spb = STATES_PER_BLOCK
    fold = S5_FOLD
    n_pow = fold + 1
    assert SUBLANES % bsz == 0 and seq % fold == 0
    rows = bsz * (seq // fold)
    n_c, dm = c_all.shape
    mod_cols = w_ada.shape[1] - first_col
    tn = mod_cols // nblk
    assert mod_cols % nblk == 0 and tn % LANES == 0 and first_col % tn == 0
    est = (2 * bsz * seq * LANES * 4 + 2 * bsz * seq * LANES * 2
           + 2 * dm * tn * 4 + dm * tn * 2 + 2 * n_c * dm * 4 + 3 * n_c * tn * 4
           + rows * fold * LANES * 2 + 2 * rows * 2 * spb * 4
           + 3 * fold * LANES * 2 * spb * 2
           + rows * fold * LANES * 4 + bsz * seq * LANES * 4
           + rows * 2 * spb * 2
           + 2 * (LANES * 2 * spb + 2 * spb * LANES) * 4)
    return pl.pallas_call(
        _s5_kernel,
        grid=(nblk,),
        in_specs=[pl.BlockSpec((bsz, seq, LANES), lambda c: (0, 0, nblk + c)),
                  pl.BlockSpec((1, LANES, 2 * spb), lambda c: (c, 0, 0)),
                  pl.BlockSpec((1, spb, LANES), lambda c: (c, 0, 0)),
                  pl.BlockSpec((1, spb, LANES), lambda c: (c, 0, 0)),
                  pl.BlockSpec((1, n_pow, spb), lambda c: (c, 0, 0)),
                  pl.BlockSpec((1, n_pow, spb), lambda c: (c, 0, 0)),
                  pl.BlockSpec((1, LANES), lambda c: (0, c)),
                  pl.BlockSpec((n_c, dm), lambda c: (0, 0)),
                  pl.BlockSpec((dm, tn), lambda c: (0, first_col // tn + c)),
                  pl.BlockSpec((1, tn), lambda c: (0, first_col // tn + c))],
        out_specs=[pl.BlockSpec((bsz, seq, LANES), lambda c: (0, 0, c)),
                   pl.BlockSpec((bsz, 1, spb), lambda c: (0, 0, c)),
                   pl.BlockSpec((bsz, 1, spb), lambda c: (0, 0, c)),
                   pl.BlockSpec((n_sample, tn), lambda c: (0, c)),
                   pl.BlockSpec((n_c - n_sample, tn), lambda c: (0, c))],
        out_shape=[jax.ShapeDtypeStruct((bsz, seq, w), BF16),
                   jax.ShapeDtypeStruct((bsz, 1, nblk * spb), F32),
                   jax.ShapeDtypeStruct((bsz, 1, nblk * spb), F32),
                   jax.ShapeDtypeStruct((n_sample, mod_cols), F32),
                   jax.ShapeDtypeStruct((n_c - n_sample, mod_cols), F32)],
        scratch_shapes=[pltpu.VMEM((rows, fold * LANES), BF16),
                        pltpu.VMEM((rows, 2 * spb), F32),
                        pltpu.VMEM((fold * LANES, 2 * spb), BF16),
                        pltpu.VMEM((2 * spb, fold * LANES), BF16),
                        pltpu.VMEM((fold * LANES, fold * LANES), BF16),
                        pltpu.VMEM((fold, rows, LANES), F32),
                        pltpu.VMEM((bsz, seq, LANES), F32)],
        compiler_params=pltpu.CompilerParams(dimension_semantics=("parallel",),
                                             vmem_limit_bytes=_vmem_limit(est)),
        name="s5_scan",
    )(u3d, bblk, creb, cimb, pw_re, pw_im, d, c_all, w_ada, b_ada)


def _s5_step_kernel(u_ref, h0r_ref, h0i_ref, wb_ref, wc_ref, are_ref, aim_ref, d_ref,
                    gy_ref, nre_ref, nim_ref):
    w = u_ref.shape[1]
    nblk = w // LANES
    spb = STATES_PER_BLOCK
    for c in range(nblk):
        cols = slice(c * LANES, (c + 1) * LANES)
        scols = slice(c * spb, (c + 1) * spb)
        u = u_ref[:, cols]
        r = _dot(u.astype(BF16), wb_ref[c])
        hr, hi = _cplx_step(are_ref[:, scols], aim_ref[:, scols], h0r_ref[:, scols], h0i_ref[:, scols],
                            r[:, :spb], r[:, spb:])
        nre_ref[:, scols] = hr
        nim_ref[:, scols] = hi
        y = _dot(hr.astype(BF16), wc_ref[c, :spb, :]) + _dot(hi.astype(BF16), wc_ref[c, spb:, :])
        gy_ref[:, cols] = jax.nn.gelu(y + d_ref[:, cols] * u).astype(BF16)


def _s5_step_call(u2d, h0r, h0i, wb, wc, abr, abi, d, *, w):
    n, n_state = h0r.shape
    c2 = lambda i: (0, 0)
    c3 = lambda i: (0, 0, 0)
    return pl.pallas_call(
        _s5_step_kernel,
        grid=(1,),
        in_specs=[pl.BlockSpec((n, w), lambda i: (0, 1)),
                  pl.BlockSpec((n, n_state), c2),
                  pl.BlockSpec((n, n_state), c2),
                  pl.BlockSpec(wb.shape, c3),
                  pl.BlockSpec(wc.shape, c3),
                  pl.BlockSpec((1, n_state), c2),
                  pl.BlockSpec((1, n_state), c2),
                  pl.BlockSpec((1, w), c2)],
        out_specs=[pl.BlockSpec((n, w), c2),
                   pl.BlockSpec((n, n_state), c2),
                   pl.BlockSpec((n, n_state), c2)],
        out_shape=[jax.ShapeDtypeStruct((n, w), BF16),
                   jax.ShapeDtypeStruct((n, n_state), F32),
                   jax.ShapeDtypeStruct((n, n_state), F32)],
        name="s5_step",
    )(u2d, h0r, h0i, wb, wc, abr, abi, d)


def _side_cast_specs(side, n_steps):
    in_specs, out_specs, out_shapes, nbytes = [], [], [], 0
    for arr, axis in side:
        blk = tuple(s // n_steps if a == axis else s for a, s in enumerate(arr.shape))
        assert arr.ndim == 2 and arr.shape[axis] % n_steps == 0 and blk[1] % LANES == 0 and blk[0] % BF16_ROWS == 0
        idx = (lambda i: (i, 0)) if axis == 0 else (lambda i: (0, i))
        in_specs.append(pl.BlockSpec(blk, idx))
        out_specs.append(pl.BlockSpec(blk, idx))
        out_shapes.append(jax.ShapeDtypeStruct(arr.shape, BF16))
        nbytes += 2 * blk[0] * blk[1] * (4 + 2)
    return in_specs, out_specs, out_shapes, nbytes


def _side_cast(src_refs, dst_refs):
    for src, dst in zip(src_refs, dst_refs, strict=True):
        dst[...] = src[...].astype(BF16)


def _merge_kernel(*refs, n_side):
    hs_ref, gy_ref, z_ref, x_ref, gt_ref, wp_ref, wg_ref, wo_ref = refs[:8]
    o_ref = refs[8 + n_side]
    _side_cast(refs[8:8 + n_side], refs[9 + n_side:])
    d = x_ref.shape[1]
    ya = _dot(hs_ref[...], wp_ref[...])
    merged = _sigmoid(z_ref[:, :d].astype(F32)) * ya
    glu = _dot(gy_ref[...], wg_ref[...])
    yb = glu[:, :d] * _sigmoid(glu[:, d:])
    merged = merged + _sigmoid(z_ref[:, d:].astype(F32)) * yb
    o = _dot(merged.astype(BF16), wo_ref[...])
    o_ref[...] = x_ref[...] + _rows(gt_ref) * o


def _merge_call(hs, gy, sz, x2d, mod, wp, wg, wo, *, tm, tiles_per_batch, side=()):
    m, d = x2d.shape
    w = hs.shape[1]
    side_in, side_out, side_shapes, side_bytes = _side_cast_specs(side, m // tm)
    est = (2 * (2 * tm * w * 2 + tm * 2 * d * 2 + 2 * tm * d * 4) + (wp.size + wg.size + wo.size) * 2
           + tm * d * 4 * 5 + side_bytes)
    one = pl.Buffered(1)
    return pl.pallas_call(
        functools.partial(_merge_kernel, n_side=len(side)),
        grid=(m // tm,),
        in_specs=[pl.BlockSpec((tm, w), lambda i: (i, 0)),
                  pl.BlockSpec((tm, w), lambda i: (i, 0)),
                  pl.BlockSpec((tm, 2 * d), lambda i: (i, 0)),
                  pl.BlockSpec((tm, d), lambda i: (i, 0)),
                  _mod_spec(mod, 2, d, tiles_per_batch, 1),
                  pl.BlockSpec(wp.shape, lambda i: (0, 0), pipeline_mode=one),
                  pl.BlockSpec(wg.shape, lambda i: (0, 0), pipeline_mode=one),
                  pl.BlockSpec(wo.shape, lambda i: (0, 0), pipeline_mode=one)] + side_in,
        out_specs=[pl.BlockSpec((tm, d), lambda i: (i, 0))] + side_out,
        out_shape=[jax.ShapeDtypeStruct((m, d), F32)] + side_shapes,
        compiler_params=pltpu.CompilerParams(dimension_semantics=("parallel",),
                                             vmem_limit_bytes=_vmem_limit(est)),
        name="merge",
    )(hs, gy, sz, x2d, mod, wp, wg, wo, *(a for a, _ in side))


def _mlp_kernel(x_ref, sc_ref, sh_ref, gt_ref, g2_ref, gf_ref, wu_ref, wd_ref, o_ref, h_scr, inv_scr, row_scr, *,
                final_norm):
    j = pl.program_id(1)
    nj = pl.num_programs(1)
    n_rows, d = x_ref.shape

    @pl.when(j == 0)
    def _():
        _norm_mod_bf16(x_ref, g2_ref, sc_ref, sh_ref, h_scr, inv_scr, row_scr)
        o_ref[...] = jnp.zeros(o_ref.shape, F32)

    up = _dot(h_scr[...], wu_ref[...])
    act = jnp.square(jnp.maximum(up, 0.0)).astype(BF16)
    nc = min(d, MLP_OUT_CHUNK)
    for c in range(d // nc):
        cols = slice(c * nc, (c + 1) * nc)
        o_ref[:, cols] += _dot(act, wd_ref[:, cols])

    @pl.when(j == nj - 1)
    def _():
        def chunk(rows):
            x2 = x_ref[rows, :] + _rows(gt_ref, rows) * o_ref[rows, :]
            if final_norm:
                ms = jnp.mean(x2 * x2, axis=-1, keepdims=True)
                x2 = (x2 * lax.rsqrt(ms + EPS)) * gf_ref[...]
            o_ref[rows, :] = x2

        _row_groups(n_rows, min(n_rows, EPILOGUE_ROWS), chunk, unroll=1)


def _mlp_call(x2d, mod, g2, gf, w_up, w_down, *, tm, tiles_per_batch, final_norm, tf):
    m, d = x2d.shape
    dff = w_up.shape[1]
    assert w_up.dtype == BF16 and w_down.dtype == BF16
    est = 3 * tm * d * 4 + tm * d * 2 + 4 * d * tf * 2 + tm * tf * 6 + tm * MLP_OUT_CHUNK * 4
    return pl.pallas_call(
        functools.partial(_mlp_kernel, final_norm=final_norm),
        grid=(m // tm, dff // tf),
        in_specs=[pl.BlockSpec((tm, d), lambda i, j: (i, 0), pipeline_mode=pl.Buffered(1)),
                  _mod_spec(mod, 4 - MLP_MOD_FIRST, d, tiles_per_batch, 2),
                  _mod_spec(mod, 3 - MLP_MOD_FIRST, d, tiles_per_batch, 2),
                  _mod_spec(mod, 5 - MLP_MOD_FIRST, d, tiles_per_batch, 2),
                  pl.BlockSpec((1, d), lambda i, j: (0, 0)),
                  pl.BlockSpec((1, d), lambda i, j: (0, 0)),
                  pl.BlockSpec((d, tf), lambda i, j: (0, j)),
                  pl.BlockSpec((tf, d), lambda i, j: (j, 0))],
        out_specs=pl.BlockSpec((tm, d), lambda i, j: (i, 0)),
        out_shape=jax.ShapeDtypeStruct((m, d), F32),
        scratch_shapes=[pltpu.VMEM((tm, d), BF16), pltpu.VMEM((tm, LANES), F32),
                        pltpu.VMEM((2 * SUBLANES, d), F32)],
        compiler_params=pltpu.CompilerParams(dimension_semantics=("parallel", "arbitrary"),
                                             vmem_limit_bytes=_vmem_limit(est)),
        name="mlp",
    )(x2d, mod, mod, mod, g2, gf, w_up, w_down)


def _pick_tile(n, pref):
    t = min(n, pref)
    assert n % t == 0, (n, t)
    return t


def kernel(x_prompt, x_sample, state_conv, state_lru, state_ssm_re, state_ssm_im, c_prompt, c_sample, w_ada, b_ada, g_norm1, g_norm2, w_in, conv_w, conv_b, w_rg_a, b_rg_a, w_rg_x, b_rg_x, lru_lambda, w_proj_a, ssm_a_re, ssm_a_im, ssm_log_dt, ssm_b_re, ssm_b_im, ssm_c_re, ssm_c_im, ssm_d, w_glu, w_out, w_up, w_down, g_final):
    depth = w_ada.shape[0]
    nb, seq, d = x_prompt.shape
    ns = x_sample.shape[0]
    assert x_sample.shape[1] == 1
    w = conv_w.shape[2]
    n_state = ssm_a_re.shape[1] * ssm_a_re.shape[2]
    assert ssm_b_re.shape[2:] == (SSM_STATE, SSM_GROUP) and w % LANES == 0
    assert w_in.shape[2] == 2 * w + 2 * d

    tl = _pick_tile(seq, 256)
    tm_in = _pick_tile(seq, 1024)
    tm_mg = _pick_tile(seq, 256)
    tm_mlp = _pick_tile(seq, 1024)

    xp = x_prompt.reshape(nb * seq, d)
    xs = x_sample.reshape(ns, d)
    assert ns % SUBLANES == 0
    pad = (-nb) % SUBLANES
    c_all = jnp.concatenate([c_sample, c_prompt, jnp.zeros((pad, d), F32)], axis=0)

    outs_p = [[] for _ in range(4)]
    outs_s = [[] for _ in range(4)]
    for l in range(depth):
        last = l == depth - 1
        b_ada_row = b_ada[l].reshape(1, -1)
        mod_s, mod_p = _mod_call(c_all, ns, w_ada[l], b_ada_row, MLP_MOD_FIRST * d)
        mod_p = mod_p.reshape(nb + pad, 1, MLP_MOD_FIRST * d)

        g1 = g_norm1[l].reshape(1, d)
        g2 = g_norm2[l].reshape(1, d)
        gf = g_final.reshape(1, d)
        cw, cb = conv_w[l], conv_b[l].reshape(1, w)
        wg = jnp.concatenate([w_rg_a[l], w_rg_x[l]], axis=2).astype(BF16)
        ba, bx, lam = b_rg_a[l].reshape(1, w), b_rg_x[l].reshape(1, w), lru_lambda[l].reshape(1, w)
        n_pow = S5_FOLD + 1
        bbr, bbi, pwr, pwi = _s5_prep_call(ssm_a_re[l], ssm_a_im[l], ssm_log_dt[l], ssm_b_re[l], ssm_b_im[l], n_pow)
        bblk, creb, cimb = _s5_block_weights(bbr, bbi, ssm_c_re[l], ssm_c_im[l])
        wb = bblk.astype(BF16)
        wc = jnp.concatenate([creb, -cimb], axis=1).astype(BF16)
        abr, abi = pwr[1].reshape(1, n_state), pwi[1].reshape(1, n_state)
        nblk = w // LANES
        pw_rows = [v.reshape(n_pow, nblk, STATES_PER_BLOCK).transpose(1, 0, 2) for v in (pwr, pwi)]
        dskip = ssm_d[l].reshape(1, w)

        u_s, sz_s, w_in_b = _inproj_call(xs, mod_s, g1, w_in[l], tm=ns, tiles_per_batch=1, w_mix=w)

        u_p, sz_p = _inproj_call(xp, mod_p, g1, w_in_b, tm=tm_in, tiles_per_batch=seq // tm_in, w_mix=w)
        u3 = u_p.reshape(nb, seq, 2 * w)
        hs_p, nconv_p, nlru_p, wp, wgl, wo = _lru_call(u3, cw, cb, wg, ba, bx, lam, tl=tl,
                                                       side=((w_proj_a[l], 1), (w_glu[l], 1), (w_out[l], 1)))
        gy_p, nre_p, nim_p, mlp_mod_s, mlp_mod_p = _s5_call(u3, bblk, creb, cimb, *pw_rows, dskip, c_all, ns,
                                                            w_ada[l], b_ada_row, MLP_MOD_FIRST * d, w=w)
        mlp_mod_p = mlp_mod_p.reshape(nb + pad, 1, (N_MOD - MLP_MOD_FIRST) * d)
        x1_p, w_up_b, w_down_b = _merge_call(hs_p.reshape(nb * seq, w), gy_p.reshape(nb * seq, w), sz_p, xp, mod_p,
                                             wp, wgl, wo, tm=tm_mg, tiles_per_batch=seq // tm_mg,
                                             side=((w_up[l], 1), (w_down[l], 0)))
        xp = _mlp_call(x1_p, mlp_mod_p, g2, gf, w_up_b, w_down_b, tm=tm_mlp, tiles_per_batch=seq // tm_mlp,
                       final_norm=last, tf=1024)

        buf = jnp.transpose(state_conv[l], (1, 0, 2))
        hs_s, nconv_s, nlru_s = _lru_step_call(u_s, buf, state_lru[l], cw, cb, wg, ba, bx, lam)
        gy_s, nre_s, nim_s = _s5_step_call(u_s, state_ssm_re[l].reshape(ns, n_state),
                                           state_ssm_im[l].reshape(ns, n_state), wb, wc, abr, abi, dskip, w=w)
        (x1_s,) = _merge_call(hs_s, gy_s, sz_s, xs, mod_s, wp, wgl, wo, tm=ns, tiles_per_batch=1)
        xs = _mlp_call(x1_s, mlp_mod_s, g2, gf, w_up_b, w_down_b, tm=ns, tiles_per_batch=1, final_norm=last,
                       tf=1024)

        gshape = ssm_a_re.shape[1:]
        for acc, v in zip(outs_p, (nconv_p, nlru_p.reshape(nb, w),
                                   nre_p.reshape((nb,) + gshape), nim_p.reshape((nb,) + gshape))):
            acc.append(v)
        for acc, v in zip(outs_s, (jnp.transpose(nconv_s, (1, 0, 2)), nlru_s,
                                   nre_s.reshape((ns,) + gshape), nim_s.reshape((ns,) + gshape))):
            acc.append(v)

    y_prompt = xp.reshape(nb, seq, d)
    y_sample = xs.reshape(ns, 1, d)
    stack = lambda a: a[0][None] if len(a) == 1 else jnp.stack(a)
    return (y_prompt, y_sample) + tuple(stack(a) for a in outs_p) + tuple(stack(a) for a in outs_s)
```

```python
import functools

import jax
import jax.numpy as jnp
from jax import lax
from jax.experimental import pallas as pl
from jax.experimental.pallas import tpu as pltpu

F32 = jnp.float32
BF16 = jnp.bfloat16

LANES = 128
SUBLANES = 8
VMEM_PHYSICAL_BYTES = 64 * 1024 * 1024
VMEM_LIMIT_CAP_BYTES = VMEM_PHYSICAL_BYTES - 6 * 1024 * 1024

LRU_C = 8.0
EPS = 1e-6
N_MOD = 6
MLP_MOD_FIRST = 3
SSM_GROUP = 16
SSM_STATE = 64
GROUPS_PER_BLOCK = LANES // SSM_GROUP
STATES_PER_BLOCK = GROUPS_PER_BLOCK * SSM_STATE


VMEM_SLACK_BYTES = 8 * 1024 * 1024


def _vmem_limit(nbytes):
    return int(min(VMEM_LIMIT_CAP_BYTES, max(32 * 1024 * 1024, nbytes + VMEM_SLACK_BYTES)))


BF16_ROWS = 2 * SUBLANES
EPILOGUE_ROWS = 256
MLP_OUT_CHUNK = 512
INPROJ_CHUNK = 256
S5_FOLD = 8


def _rows(ref, rows=None):
    if len(ref.shape) == 3:
        return ref[0]
    return ref[...] if rows is None else ref[rows, :]


def _row_groups(n_rows, group, fn, unroll):
    assert n_rows % group == 0

    def body(i, carry):
        fn(pl.ds(pl.multiple_of(i * group, group), group))
        return carry

    lax.fori_loop(0, n_rows // group, body, 0, unroll=unroll)


def _rms_scale_pass(src, inv_scr, d):
    def fn(rows):
        x = src(rows)
        xx = x * x
        part = xx[:, 0:LANES]
        for c in range(1, d // LANES):
            part = part + xx[:, c * LANES:(c + 1) * LANES]
        inv_scr[rows, :] = part

    _row_groups(inv_scr.shape[0], SUBLANES, fn, unroll=8)
    ss = jnp.sum(inv_scr[...], axis=-1, keepdims=True)
    inv_scr[...] = jnp.broadcast_to(lax.rsqrt(ss * (1.0 / d) + EPS), inv_scr.shape)


def _lanes(inv, d):
    return jnp.concatenate([inv] * (d // LANES), axis=1)


def _sublane_rows(scr, slot, row=None):
    rows = slice(slot * SUBLANES, (slot + 1) * SUBLANES)
    if row is None:
        return scr[rows, :]
    scr[rows, :] = jnp.broadcast_to(row, (SUBLANES, row.shape[1]))
    return None


def _norm_mod_bf16(x_ref, g_ref, sc_ref, sh_ref, h_scr, inv_scr, row_scr):
    n_rows, d = x_ref.shape
    _rms_scale_pass(lambda rows: x_ref[rows, :], inv_scr, d)
    per_token = len(sc_ref.shape) == 2
    if not per_token:
        _sublane_rows(row_scr, 0, g_ref[...] * (1.0 + sc_ref[0]))
        _sublane_rows(row_scr, 1, sh_ref[0])

    def fn(rows):
        halves = []
        for k in range(BF16_ROWS // SUBLANES):
            r8 = pl.ds(pl.multiple_of(rows.start + k * SUBLANES, SUBLANES), SUBLANES)
            gm = g_ref[...] * (1.0 + sc_ref[r8, :]) if per_token else _sublane_rows(row_scr, 0)
            sh = sh_ref[r8, :] if per_token else _sublane_rows(row_scr, 1)
            halves.append((x_ref[r8, :] * _lanes(inv_scr[r8, :], d)) * gm + sh)
        h_scr[rows, :] = jnp.concatenate(halves, axis=0).astype(BF16)

    _row_groups(n_rows, BF16_ROWS, fn, unroll=2)


def _dot(a, b):
    return jnp.dot(a, b, preferred_element_type=F32)


def _sigmoid(x):
    return 0.5 * jnp.tanh(0.5 * x) + 0.5


def _mod_kernel(c_ref, w_ref, b_ref, os_ref, op_ref):
    c = c_ref[...]
    cs = (c * _sigmoid(c)).astype(BF16)
    mod = _dot(cs, w_ref[...].astype(BF16)) + b_ref[...]
    n_sample = os_ref.shape[0]
    os_ref[...] = mod[:n_sample]
    op_ref[...] = mod[n_sample:]


def _mod_call(c_all, n_sample, w_ada, b_ada, n, tn=1024):
    n_rows, d = c_all.shape
    assert n % tn == 0
    est = 2 * (d * tn * 4) + 3 * n_rows * tn * 4 + n_rows * d * 4 * 2 + d * tn * 2
    return pl.pallas_call(
        _mod_kernel,
        grid=(n // tn,),
        in_specs=[pl.BlockSpec((n_rows, d), lambda j: (0, 0)),
                  pl.BlockSpec((d, tn), lambda j: (0, j)),
                  pl.BlockSpec((1, tn), lambda j: (0, j))],
        out_specs=[pl.BlockSpec((n_sample, tn), lambda j: (0, j)),
                   pl.BlockSpec((n_rows - n_sample, tn), lambda j: (0, j))],
        out_shape=[jax.ShapeDtypeStruct((n_sample, n), F32),
                   jax.ShapeDtypeStruct((n_rows - n_sample, n), F32)],
        compiler_params=pltpu.CompilerParams(dimension_semantics=("parallel",),
                                             vmem_limit_bytes=_vmem_limit(est)),
        name="mod",
    )(c_all, w_ada, b_ada)


def _mod_spec(mod, piece, d, tiles_per_batch, ngrid):
    if mod.ndim == 3:
        if ngrid == 1:
            return pl.BlockSpec((1, 1, d), lambda i: (i // tiles_per_batch, 0, piece))
        return pl.BlockSpec((1, 1, d), lambda i, j: (i // tiles_per_batch, 0, piece))
    rows = mod.shape[0]
    if ngrid == 1:
        return pl.BlockSpec((rows, d), lambda i: (0, piece))
    return pl.BlockSpec((rows, d), lambda i, j: (0, piece))


def _inproj_kernel(*refs, n_u_tiles, emit_w):
    x_ref, sc_ref, sh_ref, g_ref, w_ref, u_ref, z_ref = refs[:7]
    h_scr, inv_scr, row_scr = refs[-3:]
    j = pl.program_id(1)

    @pl.when(j == 0)
    def _():
        _norm_mod_bf16(x_ref, g_ref, sc_ref, sh_ref, h_scr, inv_scr, row_scr)

    if emit_w:
        wb_ref = refs[7]
        wb_ref[...] = w_ref[...].astype(BF16)
        w_ref = wb_ref
    tn = w_ref.shape[1]
    nc = min(tn, INPROJ_CHUNK)

    @pl.when(j < n_u_tiles)
    def _():
        u_ref[...] = _dot(h_scr[...], w_ref[...])

    @pl.when(j >= n_u_tiles)
    def _():
        for c in range(tn // nc):
            cols = slice(c * nc, (c + 1) * nc)
            z_ref[:, cols] = _dot(h_scr[...], w_ref[:, cols]).astype(BF16)


def _inproj_call(x2d, mod, g1, w_in, *, tm, tiles_per_batch, w_mix, tn=1024):
    m, d = x2d.shape
    n = w_in.shape[1]
    n_u = 2 * w_mix
    n_u_tiles = n_u // tn
    emit_w = w_in.dtype != BF16
    assert not emit_w or m == tm
    wbytes = w_in.dtype.itemsize
    est = (2 * tm * d * 4 + 2 * d * tn * wbytes + 2 * tm * tn * 4 + 2 * tm * tn * 2 + tm * d * 2 + tm * tn * 4
           + (3 * d * tn * 2 if emit_w else 0))
    out_specs = [pl.BlockSpec((tm, tn), lambda i, j: (i, jnp.minimum(j, n_u_tiles - 1))),
                 pl.BlockSpec((tm, tn), lambda i, j: (i, jnp.maximum(j - n_u_tiles, 0)))]
    out_shape = [jax.ShapeDtypeStruct((m, n_u), F32),
                 jax.ShapeDtypeStruct((m, n - n_u), BF16)]
    if emit_w:
        out_specs.append(pl.BlockSpec((d, tn), lambda i, j: (0, j)))
        out_shape.append(jax.ShapeDtypeStruct((d, n), BF16))
    return pl.pallas_call(
        functools.partial(_inproj_kernel, n_u_tiles=n_u_tiles, emit_w=emit_w),
        grid=(m // tm, n // tn),
        in_specs=[pl.BlockSpec((tm, d), lambda i, j: (i, 0)),
                  _mod_spec(mod, 1, d, tiles_per_batch, 2),
                  _mod_spec(mod, 0, d, tiles_per_batch, 2),
                  pl.BlockSpec((1, d), lambda i, j: (0, 0)),
                  pl.BlockSpec((d, tn), lambda i, j: (0, j))],
        out_specs=out_specs,
        out_shape=out_shape,
        scratch_shapes=[pltpu.VMEM((tm, d), BF16), pltpu.VMEM((tm, LANES), F32),
                        pltpu.VMEM((2 * SUBLANES, d), F32)],
        compiler_params=pltpu.CompilerParams(dimension_semantics=("parallel", "arbitrary"),
                                             vmem_limit_bytes=_vmem_limit(est)),
        name="inproj",
    )(x2d, mod, mod, g1, w_in)


def _lru_gate_block(uc, g, ba, bx, sp):
    r = _sigmoid(g[:, :LANES] + ba)
    i = _sigmoid(g[:, LANES:] + bx)
    log_a = (-LRU_C * r) * sp
    a = jnp.exp(log_a)
    m2 = 1.0 - jnp.exp(2.0 * log_a)
    mult = jnp.where(m2 > 0.0, m2 * lax.rsqrt(m2), 0.0)
    return a, (mult * i) * uc


def _lru_kernel(*refs, n_side):
    u_ref, cw_ref, cb_ref, wg_ref, ba_ref, bx_ref, lam_ref = refs[:7]
    hs_ref, nconv_ref, nlru_ref = refs[7 + n_side:10 + n_side]
    ext, slab, a_s, b_s, carry = refs[-5:]
    _side_cast(refs[7:7 + n_side], refs[10 + n_side:10 + 2 * n_side])
    t = pl.program_id(0)
    nt = pl.num_programs(0)
    nbat, tl, w = u_ref.shape
    nblk = w // LANES
    hist = SUBLANES
    kw = cw_ref.shape[0]
    steps_per_tile = SUBLANES // nbat

    @pl.when(t == 0)
    def _():
        ext[:, 0:hist, :] = jnp.zeros((nbat, hist, w), F32)
        carry[...] = jnp.zeros_like(carry)

    for b in range(nbat):
        ext[b, hist:hist + tl, :] = u_ref[b]
        for c in range(nblk):
            cols = slice(c * LANES, (c + 1) * LANES)
            acc = cb_ref[:, cols] + ext[b, hist - kw + 1:hist - kw + 1 + tl, cols] * cw_ref[0:1, cols]
            for k in range(1, kw):
                acc = acc + ext[b, hist - kw + 1 + k:hist - kw + 1 + k + tl, cols] * cw_ref[k:k + 1, cols]
            slab[c, pl.ds(b, tl, stride=nbat), :] = acc
        tail = ext[b, hist + tl - (kw - 1):hist + tl, :]
        ext[b, hist - (kw - 1):hist, :] = tail

        @pl.when(t == nt - 1)
        def _(b=b, tail=tail):
            nconv_ref[b] = tail

    sp = jax.nn.softplus(-lam_ref[...])
    for h in range(nblk):
        cols = slice(h * LANES, (h + 1) * LANES)
        uc = slab[h]
        g = _dot(uc.astype(BF16), wg_ref[h])
        a, b = _lru_gate_block(uc, g, ba_ref[:, cols], bx_ref[:, cols], sp[:, cols])
        a_s[:, cols] = a
        b_s[:, cols] = b

    def tile_step(i, hcur):
        r0 = pl.multiple_of(i * SUBLANES, SUBLANES)
        at, bt = a_s[pl.ds(r0, SUBLANES), :], b_s[pl.ds(r0, SUBLANES), :]
        hs = []
        for k in range(steps_per_tile):
            hcur = at[k * nbat:(k + 1) * nbat] * hcur + bt[k * nbat:(k + 1) * nbat]
            hs.append(hcur)
        htile = jnp.concatenate(hs, axis=0)
        for c in range(nblk):
            slab[c, pl.ds(r0, SUBLANES), :] = htile[:, c * LANES:(c + 1) * LANES]
        return hcur

    h_end = lax.fori_loop(0, tl // steps_per_tile, tile_step, carry[...], unroll=2)
    carry[...] = h_end

    for b in range(nbat):
        for c in range(nblk):
            hs_ref[b, :, c * LANES:(c + 1) * LANES] = slab[c, pl.ds(b, tl, stride=nbat), :].astype(BF16)

    @pl.when(t == nt - 1)
    def _():
        for b in range(nbat):
            nlru_ref[b] = h_end[b:b + 1, :]


def _lru_call(u3d, cw, cb, wg, ba, bx, lam, *, tl, side=()):
    bsz, seq, _ = u3d.shape
    w = cw.shape[1]
    kw = cw.shape[0]
    nblk = w // LANES
    assert SUBLANES % bsz == 0
    rows = bsz * tl
    side_in, side_out, side_shapes, side_bytes = _side_cast_specs(side, seq // tl)
    est = (2 * rows * w * 4 + 2 * rows * w * 2 + bsz * (tl + SUBLANES) * w * 4 + 3 * rows * w * 4
           + nblk * LANES * 2 * LANES * 2 * 2 + rows * 2 * LANES * 4 + side_bytes)
    const2 = lambda t: (0, 0)
    return pl.pallas_call(
        functools.partial(_lru_kernel, n_side=len(side)),
        grid=(seq // tl,),
        in_specs=[pl.BlockSpec((bsz, tl, w), lambda t: (0, t, 0)),
                  pl.BlockSpec((kw, w), const2),
                  pl.BlockSpec((1, w), const2),
                  pl.BlockSpec((nblk, LANES, 2 * LANES), lambda t: (0, 0, 0)),
                  pl.BlockSpec((1, w), const2),
                  pl.BlockSpec((1, w), const2),
                  pl.BlockSpec((1, w), const2)] + side_in,
        out_specs=[pl.BlockSpec((bsz, tl, w), lambda t: (0, t, 0)),
                   pl.BlockSpec((bsz, kw - 1, w), lambda t: (0, 0, 0)),
                   pl.BlockSpec((bsz, 1, w), lambda t: (0, 0, 0))] + side_out,
        out_shape=[jax.ShapeDtypeStruct((bsz, seq, w), BF16),
                   jax.ShapeDtypeStruct((bsz, kw - 1, w), F32),
                   jax.ShapeDtypeStruct((bsz, 1, w), F32)] + side_shapes,
        scratch_shapes=[pltpu.VMEM((bsz, tl + SUBLANES, w), F32),
                        pltpu.VMEM((nblk, rows, LANES), F32),
                        pltpu.VMEM((rows, w), F32),
                        pltpu.VMEM((rows, w), F32),
                        pltpu.VMEM((bsz, w), F32)],
        compiler_params=pltpu.CompilerParams(dimension_semantics=("arbitrary",),
                                             vmem_limit_bytes=_vmem_limit(est)),
        name="lru_scan",
    )(u3d, cw, cb, wg, ba, bx, lam, *(a for a, _ in side))


def _lru_step_kernel(u_ref, buf_ref, h0_ref, cw_ref, cb_ref, wg_ref, ba_ref, bx_ref, lam_ref,
                     hs_ref, nconv_ref, nlru_ref):
    w = cw_ref.shape[1]
    kw = cw_ref.shape[0]
    nblk = w // LANES
    u = u_ref[...]
    uc = cb_ref[...] + buf_ref[0] * cw_ref[0:1, :]
    for k in range(1, kw - 1):
        uc = uc + buf_ref[k] * cw_ref[k:k + 1, :]
    uc = uc + u * cw_ref[kw - 1:kw, :]
    for k in range(kw - 2):
        nconv_ref[k] = buf_ref[k + 1]
    nconv_ref[kw - 2] = u
    sp = jax.nn.softplus(-lam_ref[...])
    for h in range(nblk):
        cols = slice(h * LANES, (h + 1) * LANES)
        uch = uc[:, cols]
        g = _dot(uch.astype(BF16), wg_ref[h])
        a, b = _lru_gate_block(uch, g, ba_ref[:, cols], bx_ref[:, cols], sp[:, cols])
        hn = a * h0_ref[:, cols] + b
        nlru_ref[:, cols] = hn
        hs_ref[:, cols] = hn.astype(BF16)


def _lru_step_call(u2d, buf, h0, cw, cb, wg, ba, bx, lam):
    n = h0.shape[0]
    w = cw.shape[1]
    kw = cw.shape[0]
    nblk = w // LANES
    c2 = lambda i: (0, 0)
    c3 = lambda i: (0, 0, 0)
    return pl.pallas_call(
        _lru_step_kernel,
        grid=(1,),
        in_specs=[pl.BlockSpec((n, w), c2),
                  pl.BlockSpec((kw - 1, n, w), c3),
                  pl.BlockSpec((n, w), c2),
                  pl.BlockSpec((kw, w), c2),
                  pl.BlockSpec((1, w), c2),
                  pl.BlockSpec((nblk, LANES, 2 * LANES), c3),
                  pl.BlockSpec((1, w), c2),
                  pl.BlockSpec((1, w), c2),
                  pl.BlockSpec((1, w), c2)],
        out_specs=[pl.BlockSpec((n, w), c2),
                   pl.BlockSpec((kw - 1, n, w), c3),
                   pl.BlockSpec((n, w), c2)],
        out_shape=[jax.ShapeDtypeStruct((n, w), BF16),
                   jax.ShapeDtypeStruct((kw - 1, n, w), F32),
                   jax.ShapeDtypeStruct((n, w), F32)],
        name="lru_step",
    )(u2d, buf, h0, cw, cb, wg, ba, bx, lam)


def _s5_prep_kernel(arx_ref, aix_ref, dt_ref, br_ref, bi_ref, ar_ref, ai_ref,
                    bb_ref, pwr_ref, pwi_ref):
    dt = jnp.exp(dt_ref[...])

    def abar(ar, ai):
        mag = jnp.exp(dt * ar)
        ang = dt * ai
        return mag * jnp.cos(ang), mag * jnp.sin(ang)

    ar, ai = arx_ref[...], aix_ref[...]
    abr, abi = abar(ar, ai)
    den = ar * ar + ai * ai
    nr, ni = abr - 1.0, abi
    q_re = (nr * ar + ni * ai) / den
    q_im = (ni * ar - nr * ai) / den
    br, bi = br_ref[...], bi_ref[...]
    bb_ref[0] = q_re * br - q_im * bi
    bb_ref[1] = q_re * bi + q_im * br

    pr, pi = abar(ar_ref[...], ai_ref[...])
    qr, qi = jnp.ones_like(pr), jnp.zeros_like(pr)
    for m in range(pwr_ref.shape[0]):
        pwr_ref[m] = qr
        pwi_ref[m] = qi
        qr, qi = qr * pr - qi * pi, qr * pi + qi * pr


def _s5_prep_call(a_re, a_im, log_dt, b_re, b_im, n_pow):
    g, p, k = b_re.shape
    arx = jnp.repeat(a_re, k, axis=1)
    aix = jnp.repeat(a_im, k, axis=1)
    bb, pwr, pwi = pl.pallas_call(
        _s5_prep_kernel,
        out_shape=[jax.ShapeDtypeStruct((2, g, p * k), F32),
                   jax.ShapeDtypeStruct((n_pow, g, p), F32), jax.ShapeDtypeStruct((n_pow, g, p), F32)],
        name="s5_prep",
    )(arx, aix, log_dt.reshape(g, 1), b_re.reshape(g, p * k), b_im.reshape(g, p * k), a_re, a_im)
    return bb.reshape(2, g, p, k), pwr, pwi


def _s5_block_weights(bb, c_re, c_im):
    _, g, p, k = bb.shape
    gb = GROUPS_PER_BLOCK
    nb = g // gb
    eye = jnp.eye(gb, dtype=F32)
    t = bb.reshape(2, nb, gb, p, k).transpose(1, 2, 4, 0, 3)
    bblk = jnp.einsum('cgkrp,gh->cgkrhp', t, eye).reshape(nb, gb * k, 2 * gb * p)
    cc = jnp.stack([c_re, c_im]).reshape(2, nb, gb, k, p)
    cblk = jnp.einsum('rcgkp,hg->rchpgk', cc, eye).reshape(2, nb, gb * p, gb * k)
    return bblk, cblk[0], cblk[1]


def _cplx_mul(ar, ai, br, bi):
    return ar * br - ai * bi, ar * bi + ai * br


def _cplx_step(ar, ai, hr, hi, br, bi):
    return ar * hr - ai * hi + br, ar * hi + ai * hr + bi


def _s5_kernel(u_ref, bblk_ref, creb_ref, cimb_ref, pwr_ref, pwi_ref, d_ref, c_ref, wa_ref, ba_ref,
               gy_ref, nre_ref, nim_ref, ms_ref, mp_ref,
               ufb, buf, wbs, wqs, wts, yslab, out32):
    _mod_kernel(c_ref, wa_ref, ba_ref, ms_ref, mp_ref)
    nbat, seq, _ = u_ref.shape
    spb = STATES_PER_BLOCK
    fold = wbs.shape[0] // LANES
    nq = seq // fold

    bre, bim = bblk_ref[0, :, :spb], bblk_ref[0, :, spb:]
    creb, cimb = creb_ref[0], cimb_ref[0]
    wcb = jnp.concatenate([creb, -cimb], axis=0).astype(BF16)
    lag = []
    for m in range(fold):
        wre, wim = _cplx_mul(bre, bim, pwr_ref[0, m:m + 1, :], pwi_ref[0, m:m + 1, :])
        rows = slice((fold - 1 - m) * LANES, (fold - m) * LANES)
        wbs[rows, :spb] = wre.astype(BF16)
        wbs[rows, spb:] = wim.astype(BF16)
        lag.append(_dot(wbs[rows, :], wcb))
    zero_blk = jnp.zeros((LANES, LANES), BF16)
    for r_in in range(fold):
        for r_out in range(fold):
            blk = lag[r_out - r_in].astype(BF16) if r_in <= r_out else zero_blk
            wts[r_in * LANES:(r_in + 1) * LANES, r_out * LANES:(r_out + 1) * LANES] = blk
    ctre, ctim = creb.T, cimb.T
    for r in range(fold):
        pr, pi = pwr_ref[0, r + 1:r + 2, :], pwi_ref[0, r + 1:r + 2, :]
        cols = slice(r * LANES, (r + 1) * LANES)
        wqs[:spb, cols] = (ctre * pr - ctim * pi).T.astype(BF16)
        wqs[spb:, cols] = (-(ctre * pi + ctim * pr)).T.astype(BF16)

    for r in range(fold):
        for b in range(nbat):
            yslab[r, pl.ds(b, nq, stride=nbat), :] = u_ref[b, pl.ds(r, nq, stride=fold), :]
        ufb[:, r * LANES:(r + 1) * LANES] = yslab[r].astype(BF16)

    buf[...] = _dot(ufb[...], wbs[...])

    ar, ai = pwr_ref[0, fold:fold + 1, :], pwi_ref[0, fold:fold + 1, :]
    groups_per_tile = SUBLANES // nbat
    z = jnp.zeros((nbat, spb), F32)

    def tile_step(i, g):
        gr, gi = g
        r0 = pl.multiple_of(i * SUBLANES, SUBLANES)
        xr, xi = buf[pl.ds(r0, SUBLANES), :spb], buf[pl.ds(r0, SUBLANES), spb:]
        hr, hi = [], []
        for k in range(groups_per_tile):
            hr.append(gr)
            hi.append(gi)
            gr, gi = _cplx_step(ar, ai, gr, gi, xr[k * nbat:(k + 1) * nbat], xi[k * nbat:(k + 1) * nbat])
        buf[pl.ds(r0, SUBLANES), :spb] = jnp.concatenate(hr, axis=0)
        buf[pl.ds(r0, SUBLANES), spb:] = jnp.concatenate(hi, axis=0)
        return gr, gi

    g_re, g_im = lax.fori_loop(0, nq // groups_per_tile, tile_step, (z, z), unroll=2)
    for b in range(nbat):
        nre_ref[b] = g_re[b:b + 1, :]
        nim_ref[b] = g_im[b:b + 1, :]

    hsb = buf[...].astype(BF16)
    per_tile = 2
    for n in range(fold // per_tile):
        cols = slice(n * per_tile * LANES, (n + 1) * per_tile * LANES)
        kk = (n + 1) * per_tile * LANES
        yn = _dot(hsb, wqs[:, cols]) + _dot(ufb[:, :kk], wts[:kk, cols])
        for i in range(per_tile):
            yslab[n * per_tile + i] = yn[:, i * LANES:(i + 1) * LANES]

    for b in range(nbat):
        for r in range(fold):
            ys = (yslab[r, pl.ds(b, nq, stride=nbat), :]
                  + d_ref[...] * u_ref[b, pl.ds(r, nq, stride=fold), :])
            out32[b, pl.ds(r, nq, stride=fold), :] = jax.nn.gelu(ys)
        gy_ref[b] = out32[b].astype(BF16)


def _s5_call(u3d, bblk, creb, cimb, pw_re, pw_im, d, c_all, n_sample, w_ada, b_ada, first_col, *, w):
    bsz, seq, _ = u3d.shape
    nblk = w // LANES
    spb = STATES_PER_BLOCK
    fold = S5_FOLD
    n_pow = fold + 1
    assert SUBLANES % bsz == 0 and seq % fold == 0
    rows = bsz * (seq // fold)
    n_c, dm = c_all.shape
    mod_cols = w_ada.shape[1] - first_col
    tn = mod_cols // nblk
    assert mod_cols % nblk == 0 and tn % LANES == 0 and first_col % tn == 0
    est = (2 * bsz * seq * LANES * 4 + 2 * bsz * seq * LANES * 2
           + 2 * dm * tn * 4 + dm * tn * 2 + 2 * n_c * dm * 4 + 3 * n_c * tn * 4
           + rows * fold * LANES * 2 + 2 * rows * 2 * spb * 4
           + 3 * fold * LANES * 2 * spb * 2
           + rows * fold * LANES * 4 + bsz * seq * LANES * 4
           + rows * 2 * spb * 2
           + 2 * (LANES * 2 * spb + 2 * spb * LANES) * 4)
    return pl.pallas_call(
        _s5_kernel,
        grid=(nblk,),
        in_specs=[pl.BlockSpec((bsz, seq, LANES), lambda c: (0, 0, nblk + c)),
                  pl.BlockSpec((1, LANES, 2 * spb), lambda c: (c, 0, 0)),
                  pl.BlockSpec((1, spb, LANES), lambda c: (c, 0, 0)),
                  pl.BlockSpec((1, spb, LANES), lambda c: (c, 0, 0)),
                  pl.BlockSpec((1, n_pow, spb), lambda c: (c, 0, 0)),
                  pl.BlockSpec((1, n_pow, spb), lambda c: (c, 0, 0)),
                  pl.BlockSpec((1, LANES), lambda c: (0, c)),
                  pl.BlockSpec((n_c, dm), lambda c: (0, 0)),
                  pl.BlockSpec((dm, tn), lambda c: (0, first_col // tn + c)),
                  pl.BlockSpec((1, tn), lambda c: (0, first_col // tn + c))],
        out_specs=[pl.BlockSpec((bsz, seq, LANES), lambda c: (0, 0, c)),
                   pl.BlockSpec((bsz, 1, spb), lambda c: (0, 0, c)),
                   pl.BlockSpec((bsz, 1, spb), lambda c: (0, 0, c)),
                   pl.BlockSpec((n_sample, tn), lambda c: (0, c)),
                   pl.BlockSpec((n_c - n_sample, tn), lambda c: (0, c))],
        out_shape=[jax.ShapeDtypeStruct((bsz, seq, w), BF16),
                   jax.ShapeDtypeStruct((bsz, 1, nblk * spb), F32),
                   jax.ShapeDtypeStruct((bsz, 1, nblk * spb), F32),
                   jax.ShapeDtypeStruct((n_sample, mod_cols), F32),
                   jax.ShapeDtypeStruct((n_c - n_sample, mod_cols), F32)],
        scratch_shapes=[pltpu.VMEM((rows, fold * LANES), BF16),
                        pltpu.VMEM((rows, 2 * spb), F32),
                        pltpu.VMEM((fold * LANES, 2 * spb), BF16),
                        pltpu.VMEM((2 * spb, fold * LANES), BF16),
                        pltpu.VMEM((fold * LANES, fold * LANES), BF16),
                        pltpu.VMEM((fold, rows, LANES), F32),
                        pltpu.VMEM((bsz, seq, LANES), F32)],
        compiler_params=pltpu.CompilerParams(dimension_semantics=("parallel",),
                                             vmem_limit_bytes=_vmem_limit(est)),
        name="s5_scan",
    )(u3d, bblk, creb, cimb, pw_re, pw_im, d, c_all, w_ada, b_ada)


def _s5_step_kernel(u_ref, h0r_ref, h0i_ref, bblk_ref, creb_ref, cimb_ref, pwr_ref, pwi_ref, d_ref,
                    gy_ref, nre_ref, nim_ref):
    w = u_ref.shape[1]
    nblk = w // LANES
    spb = STATES_PER_BLOCK
    for c in range(nblk):
        cols = slice(c * LANES, (c + 1) * LANES)
        scols = slice(c * spb, (c + 1) * spb)
        u = u_ref[:, cols]
        r = _dot(u.astype(BF16), bblk_ref[c].astype(BF16))
        hr, hi = _cplx_step(pwr_ref[c, 1:2, :], pwi_ref[c, 1:2, :], h0r_ref[:, scols], h0i_ref[:, scols],
                            r[:, :spb], r[:, spb:])
        nre_ref[:, scols] = hr
        nim_ref[:, scols] = hi
        y = (_dot(hr.astype(BF16), creb_ref[c].astype(BF16))
             - _dot(hi.astype(BF16), cimb_ref[c].astype(BF16)))
        gy_ref[:, cols] = jax.nn.gelu(y + d_ref[:, cols] * u).astype(BF16)


def _s5_step_call(u2d, h0r, h0i, bblk, creb, cimb, pw_re, pw_im, d, *, w):
    n, n_state = h0r.shape
    c2 = lambda i: (0, 0)
    c3 = lambda i: (0, 0, 0)
    return pl.pallas_call(
        _s5_step_kernel,
        grid=(1,),
        in_specs=[pl.BlockSpec((n, w), lambda i: (0, 1)),
                  pl.BlockSpec((n, n_state), c2),
                  pl.BlockSpec((n, n_state), c2),
                  pl.BlockSpec(bblk.shape, c3),
                  pl.BlockSpec(creb.shape, c3),
                  pl.BlockSpec(cimb.shape, c3),
                  pl.BlockSpec(pw_re.shape, c3),
                  pl.BlockSpec(pw_im.shape, c3),
                  pl.BlockSpec((1, w), c2)],
        out_specs=[pl.BlockSpec((n, w), c2),
                   pl.BlockSpec((n, n_state), c2),
                   pl.BlockSpec((n, n_state), c2)],
        out_shape=[jax.ShapeDtypeStruct((n, w), BF16),
                   jax.ShapeDtypeStruct((n, n_state), F32),
                   jax.ShapeDtypeStruct((n, n_state), F32)],
        name="s5_step",
    )(u2d, h0r, h0i, bblk, creb, cimb, pw_re, pw_im, d)


def _side_cast_specs(side, n_steps):
    in_specs, out_specs, out_shapes, nbytes = [], [], [], 0
    for arr, axis in side:
        blk = tuple(s // n_steps if a == axis else s for a, s in enumerate(arr.shape))
        assert arr.ndim == 2 and arr.shape[axis] % n_steps == 0 and blk[1] % LANES == 0 and blk[0] % BF16_ROWS == 0
        idx = (lambda i: (i, 0)) if axis == 0 else (lambda i: (0, i))
        in_specs.append(pl.BlockSpec(blk, idx))
        out_specs.append(pl.BlockSpec(blk, idx))
        out_shapes.append(jax.ShapeDtypeStruct(arr.shape, BF16))
        nbytes += 2 * blk[0] * blk[1] * (4 + 2)
    return in_specs, out_specs, out_shapes, nbytes


def _side_cast(src_refs, dst_refs):
    for src, dst in zip(src_refs, dst_refs, strict=True):
        dst[...] = src[...].astype(BF16)


def _merge_kernel(*refs, n_side):
    hs_ref, gy_ref, z_ref, x_ref, gt_ref, wp_ref, wg_ref, wo_ref = refs[:8]
    o_ref = refs[8 + n_side]
    _side_cast(refs[8:8 + n_side], refs[9 + n_side:])
    d = x_ref.shape[1]
    ya = _dot(hs_ref[...], wp_ref[...])
    merged = _sigmoid(z_ref[:, :d].astype(F32)) * ya
    glu = _dot(gy_ref[...], wg_ref[...])
    yb = glu[:, :d] * _sigmoid(glu[:, d:])
    merged = merged + _sigmoid(z_ref[:, d:].astype(F32)) * yb
    o = _dot(merged.astype(BF16), wo_ref[...])
    o_ref[...] = x_ref[...] + _rows(gt_ref) * o


def _merge_call(hs, gy, sz, x2d, mod, wp, wg, wo, *, tm, tiles_per_batch, side=()):
    m, d = x2d.shape
    w = hs.shape[1]
    side_in, side_out, side_shapes, side_bytes = _side_cast_specs(side, m // tm)
    est = (2 * (2 * tm * w * 2 + tm * 2 * d * 2 + 2 * tm * d * 4) + (wp.size + wg.size + wo.size) * 2
           + tm * d * 4 * 5 + side_bytes)
    one = pl.Buffered(1)
    return pl.pallas_call(
        functools.partial(_merge_kernel, n_side=len(side)),
        grid=(m // tm,),
        in_specs=[pl.BlockSpec((tm, w), lambda i: (i, 0)),
                  pl.BlockSpec((tm, w), lambda i: (i, 0)),
                  pl.BlockSpec((tm, 2 * d), lambda i: (i, 0)),
                  pl.BlockSpec((tm, d), lambda i: (i, 0)),
                  _mod_spec(mod, 2, d, tiles_per_batch, 1),
                  pl.BlockSpec(wp.shape, lambda i: (0, 0), pipeline_mode=one),
                  pl.BlockSpec(wg.shape, lambda i: (0, 0), pipeline_mode=one),
                  pl.BlockSpec(wo.shape, lambda i: (0, 0), pipeline_mode=one)] + side_in,
        out_specs=[pl.BlockSpec((tm, d), lambda i: (i, 0))] + side_out,
        out_shape=[jax.ShapeDtypeStruct((m, d), F32)] + side_shapes,
        compiler_params=pltpu.CompilerParams(dimension_semantics=("parallel",),
                                             vmem_limit_bytes=_vmem_limit(est)),
        name="merge",
    )(hs, gy, sz, x2d, mod, wp, wg, wo, *(a for a, _ in side))


def _mlp_kernel(x_ref, sc_ref, sh_ref, gt_ref, g2_ref, gf_ref, wu_ref, wd_ref, o_ref, h_scr, inv_scr, row_scr, *,
                final_norm):
    j = pl.program_id(1)
    nj = pl.num_programs(1)
    n_rows, d = x_ref.shape

    @pl.when(j == 0)
    def _():
        _norm_mod_bf16(x_ref, g2_ref, sc_ref, sh_ref, h_scr, inv_scr, row_scr)
        o_ref[...] = jnp.zeros(o_ref.shape, F32)

    up = _dot(h_scr[...], wu_ref[...])
    act = jnp.square(jnp.maximum(up, 0.0)).astype(BF16)
    nc = min(d, MLP_OUT_CHUNK)
    for c in range(d // nc):
        cols = slice(c * nc, (c + 1) * nc)
        o_ref[:, cols] += _dot(act, wd_ref[:, cols])

    @pl.when(j == nj - 1)
    def _():
        def chunk(rows):
            x2 = x_ref[rows, :] + _rows(gt_ref, rows) * o_ref[rows, :]
            if final_norm:
                ms = jnp.mean(x2 * x2, axis=-1, keepdims=True)
                x2 = (x2 * lax.rsqrt(ms + EPS)) * gf_ref[...]
            o_ref[rows, :] = x2

        _row_groups(n_rows, min(n_rows, EPILOGUE_ROWS), chunk, unroll=1)


def _mlp_call(x2d, mod, g2, gf, w_up, w_down, *, tm, tiles_per_batch, final_norm, tf):
    m, d = x2d.shape
    dff = w_up.shape[1]
    assert w_up.dtype == BF16 and w_down.dtype == BF16
    est = 3 * tm * d * 4 + tm * d * 2 + 4 * d * tf * 2 + tm * tf * 6 + tm * MLP_OUT_CHUNK * 4
    return pl.pallas_call(
        functools.partial(_mlp_kernel, final_norm=final_norm),
        grid=(m // tm, dff // tf),
        in_specs=[pl.BlockSpec((tm, d), lambda i, j: (i, 0), pipeline_mode=pl.Buffered(1)),
                  _mod_spec(mod, 4 - MLP_MOD_FIRST, d, tiles_per_batch, 2),
                  _mod_spec(mod, 3 - MLP_MOD_FIRST, d, tiles_per_batch, 2),
                  _mod_spec(mod, 5 - MLP_MOD_FIRST, d, tiles_per_batch, 2),
                  pl.BlockSpec((1, d), lambda i, j: (0, 0)),
                  pl.BlockSpec((1, d), lambda i, j: (0, 0)),
                  pl.BlockSpec((d, tf), lambda i, j: (0, j)),
                  pl.BlockSpec((tf, d), lambda i, j: (j, 0))],
        out_specs=pl.BlockSpec((tm, d), lambda i, j: (i, 0)),
        out_shape=jax.ShapeDtypeStruct((m, d), F32),
        scratch_shapes=[pltpu.VMEM((tm, d), BF16), pltpu.VMEM((tm, LANES), F32),
                        pltpu.VMEM((2 * SUBLANES, d), F32)],
        compiler_params=pltpu.CompilerParams(dimension_semantics=("parallel", "arbitrary"),
                                             vmem_limit_bytes=_vmem_limit(est)),
        name="mlp",
    )(x2d, mod, mod, mod, g2, gf, w_up, w_down)


def _pick_tile(n, pref):
    t = min(n, pref)
    assert n % t == 0, (n, t)
    return t


def kernel(x_prompt, x_sample, state_conv, state_lru, state_ssm_re, state_ssm_im, c_prompt, c_sample, w_ada, b_ada, g_norm1, g_norm2, w_in, conv_w, conv_b, w_rg_a, b_rg_a, w_rg_x, b_rg_x, lru_lambda, w_proj_a, ssm_a_re, ssm_a_im, ssm_log_dt, ssm_b_re, ssm_b_im, ssm_c_re, ssm_c_im, ssm_d, w_glu, w_out, w_up, w_down, g_final):
    depth = w_ada.shape[0]
    nb, seq, d = x_prompt.shape
    ns = x_sample.shape[0]
    assert x_sample.shape[1] == 1
    w = conv_w.shape[2]
    n_state = ssm_a_re.shape[1] * ssm_a_re.shape[2]
    assert ssm_b_re.shape[2:] == (SSM_STATE, SSM_GROUP) and w % LANES == 0
    assert w_in.shape[2] == 2 * w + 2 * d

    tl = _pick_tile(seq, 256)
    tm_in = _pick_tile(seq, 1024)
    tm_mg = _pick_tile(seq, 256)
    tm_mlp = _pick_tile(seq, 1024)

    xp = x_prompt.reshape(nb * seq, d)
    xs = x_sample.reshape(ns, d)
    assert ns % SUBLANES == 0
    pad = (-nb) % SUBLANES
    c_all = jnp.concatenate([c_sample, c_prompt, jnp.zeros((pad, d), F32)], axis=0)

    outs_p = [[] for _ in range(4)]
    outs_s = [[] for _ in range(4)]
    for l in range(depth):
        last = l == depth - 1
        b_ada_row = b_ada[l].reshape(1, -1)
        mod_s, mod_p = _mod_call(c_all, ns, w_ada[l], b_ada_row, MLP_MOD_FIRST * d)
        mod_p = mod_p.reshape(nb + pad, 1, MLP_MOD_FIRST * d)

        g1 = g_norm1[l].reshape(1, d)
        g2 = g_norm2[l].reshape(1, d)
        gf = g_final.reshape(1, d)
        cw, cb = conv_w[l], conv_b[l].reshape(1, w)
        wg = jnp.concatenate([w_rg_a[l], w_rg_x[l]], axis=2).astype(BF16)
        ba, bx, lam = b_rg_a[l].reshape(1, w), b_rg_x[l].reshape(1, w), lru_lambda[l].reshape(1, w)
        n_pow = S5_FOLD + 1
        bb, pwr, pwi = _s5_prep_call(ssm_a_re[l], ssm_a_im[l], ssm_log_dt[l], ssm_b_re[l], ssm_b_im[l], n_pow)
        bblk, creb, cimb = _s5_block_weights(bb, ssm_c_re[l], ssm_c_im[l])
        nblk = w // LANES
        pw_rows = [v.reshape(n_pow, nblk, STATES_PER_BLOCK).transpose(1, 0, 2) for v in (pwr, pwi)]
        dskip = ssm_d[l].reshape(1, w)

        u_s, sz_s, w_in_b = _inproj_call(xs, mod_s, g1, w_in[l], tm=ns, tiles_per_batch=1, w_mix=w)

        u_p, sz_p = _inproj_call(xp, mod_p, g1, w_in_b, tm=tm_in, tiles_per_batch=seq // tm_in, w_mix=w)
        u3 = u_p.reshape(nb, seq, 2 * w)
        hs_p, nconv_p, nlru_p, wp, wgl, wo = _lru_call(u3, cw, cb, wg, ba, bx, lam, tl=tl,
                                                       side=((w_proj_a[l], 1), (w_glu[l], 1), (w_out[l], 1)))
        gy_p, nre_p, nim_p, mlp_mod_s, mlp_mod_p = _s5_call(u3, bblk, creb, cimb, *pw_rows, dskip, c_all, ns,
                                                            w_ada[l], b_ada_row, MLP_MOD_FIRST * d, w=w)
        mlp_mod_p = mlp_mod_p.reshape(nb + pad, 1, (N_MOD - MLP_MOD_FIRST) * d)
        x1_p, w_up_b, w_down_b = _merge_call(hs_p.reshape(nb * seq, w), gy_p.reshape(nb * seq, w), sz_p, xp, mod_p,
                                             wp, wgl, wo, tm=tm_mg, tiles_per_batch=seq // tm_mg,
                                             side=((w_up[l], 1), (w_down[l], 0)))
        xp = _mlp_call(x1_p, mlp_mod_p, g2, gf, w_up_b, w_down_b, tm=tm_mlp, tiles_per_batch=seq // tm_mlp,
                       final_norm=last, tf=1024)

        buf = jnp.transpose(state_conv[l], (1, 0, 2))
        hs_s, nconv_s, nlru_s = _lru_step_call(u_s, buf, state_lru[l], cw, cb, wg, ba, bx, lam)
        gy_s, nre_s, nim_s = _s5_step_call(u_s, state_ssm_re[l].reshape(ns, n_state),
                                           state_ssm_im[l].reshape(ns, n_state), bblk, creb, cimb, *pw_rows, dskip,
                                           w=w)
        (x1_s,) = _merge_call(hs_s, gy_s, sz_s, xs, mod_s, wp, wgl, wo, tm=ns, tiles_per_batch=1)
        xs = _mlp_call(x1_s, mlp_mod_s, g2, gf, w_up_b, w_down_b, tm=ns, tiles_per_batch=1, final_norm=last,
                       tf=1024)

        gshape = ssm_a_re.shape[1:]
        for acc, v in zip(outs_p, (nconv_p, nlru_p.reshape(nb, w),
                                   nre_p.reshape((nb,) + gshape), nim_p.reshape((nb,) + gshape))):
            acc.append(v)
        for acc, v in zip(outs_s, (jnp.transpose(nconv_s, (1, 0, 2)), nlru_s,
                                   nre_s.reshape((ns,) + gshape), nim_s.reshape((ns,) + gshape))):
            acc.append(v)

    y_prompt = xp.reshape(nb, seq, d)
    y_sample = xs.reshape(ns, 1, d)
    stack = lambda a: a[0][None] if len(a) == 1 else jnp.stack(a)
    return (y_prompt, y_sample) + tuple(stack(a) for a in outs_p) + tuple(stack(a) for a in outs_s)
```

```python
import functools

import jax
import jax.numpy as jnp
from jax import lax
from jax.experimental import pallas as pl
from jax.experimental.pallas import tpu as pltpu

F32 = jnp.float32
BF16 = jnp.bfloat16

LANES = 128
SUBLANES = 8
BF16_ROWS = 2 * SUBLANES
VMEM_PHYSICAL_BYTES = 64 * 1024 * 1024
VMEM_LIMIT_CAP_BYTES = VMEM_PHYSICAL_BYTES - 6 * 1024 * 1024
VMEM_SLACK_BYTES = 8 * 1024 * 1024

LRU_C = 8.0
EPS = 1e-6
N_MOD = 6
MLP_MOD_FIRST = 3
SSM_GROUP = 16
SSM_STATE = 64
GROUPS_PER_BLOCK = LANES // SSM_GROUP
STATES_PER_BLOCK = GROUPS_PER_BLOCK * SSM_STATE

EPILOGUE_ROWS = 256
MLP_OUT_CHUNK = 512
INPROJ_CHUNK = 256
S5_FOLD = 8


def _vmem_limit(nbytes):
    return int(min(VMEM_LIMIT_CAP_BYTES, max(32 * 1024 * 1024, nbytes + VMEM_SLACK_BYTES)))


def _rows(ref, rows=None):
    if len(ref.shape) == 3:
        return ref[0]
    return ref[...] if rows is None else ref[rows, :]


def _row_groups(n_rows, group, fn, unroll):
    assert n_rows % group == 0

    def body(i, carry):
        fn(pl.ds(pl.multiple_of(i * group, group), group))
        return carry

    lax.fori_loop(0, n_rows // group, body, 0, unroll=unroll)


def _rms_scale_pass(src, inv_scr, d):
    def fn(rows):
        x = src(rows)
        xx = x * x
        part = xx[:, 0:LANES]
        for c in range(1, d // LANES):
            part = part + xx[:, c * LANES:(c + 1) * LANES]
        inv_scr[rows, :] = part

    _row_groups(inv_scr.shape[0], SUBLANES, fn, unroll=8)
    ss = jnp.sum(inv_scr[...], axis=-1, keepdims=True)
    inv_scr[...] = jnp.broadcast_to(lax.rsqrt(ss * (1.0 / d) + EPS), inv_scr.shape)


def _lanes(inv, d):
    return jnp.concatenate([inv] * (d // LANES), axis=1)


def _sublane_rows(scr, slot, row=None):
    rows = slice(slot * SUBLANES, (slot + 1) * SUBLANES)
    if row is None:
        return scr[rows, :]
    scr[rows, :] = jnp.broadcast_to(row, (SUBLANES, row.shape[1]))
    return None


def _norm_mod_bf16(x_ref, g_ref, sc_ref, sh_ref, h_scr, inv_scr, row_scr):
    n_rows, d = x_ref.shape
    _rms_scale_pass(lambda rows: x_ref[rows, :], inv_scr, d)
    per_token = len(sc_ref.shape) == 2
    if not per_token:
        _sublane_rows(row_scr, 0, g_ref[...] * (1.0 + sc_ref[0]))
        _sublane_rows(row_scr, 1, sh_ref[0])

    def fn(rows):
        halves = []
        for k in range(BF16_ROWS // SUBLANES):
            r8 = pl.ds(pl.multiple_of(rows.start + k * SUBLANES, SUBLANES), SUBLANES)
            gm = g_ref[...] * (1.0 + sc_ref[r8, :]) if per_token else _sublane_rows(row_scr, 0)
            sh = sh_ref[r8, :] if per_token else _sublane_rows(row_scr, 1)
            halves.append((x_ref[r8, :] * _lanes(inv_scr[r8, :], d)) * gm + sh)
        h_scr[rows, :] = jnp.concatenate(halves, axis=0).astype(BF16)

    _row_groups(n_rows, BF16_ROWS, fn, unroll=2)


def _dot(a, b):
    return jnp.dot(a, b, preferred_element_type=F32)


def _sigmoid(x):
    return 0.5 * jnp.tanh(0.5 * x) + 0.5


def _side_cast_specs(side, n_steps):
    in_specs, out_specs, out_shapes, nbytes = [], [], [], 0
    for arr, axis in side:
        blk = tuple(s // n_steps if a == axis else s for a, s in enumerate(arr.shape))
        assert arr.ndim == 2 and arr.shape[axis] % n_steps == 0 and blk[1] % LANES == 0 and blk[0] % BF16_ROWS == 0
        idx = (lambda i: (i, 0)) if axis == 0 else (lambda i: (0, i))
        in_specs.append(pl.BlockSpec(blk, idx))
        out_specs.append(pl.BlockSpec(blk, idx))
        out_shapes.append(jax.ShapeDtypeStruct(arr.shape, BF16))
        nbytes += 2 * blk[0] * blk[1] * (4 + 2)
    return in_specs, out_specs, out_shapes, nbytes


def _side_cast(src_refs, dst_refs):
    for src, dst in zip(src_refs, dst_refs, strict=True):
        dst[...] = src[...].astype(BF16)


def _mod_kernel(c_ref, w_ref, b_ref, os_ref, op_ref):
    c = c_ref[...]
    cs = (c * _sigmoid(c)).astype(BF16)
    mod = _dot(cs, w_ref[...].astype(BF16)) + b_ref[...]
    n_sample = os_ref.shape[0]
    os_ref[...] = mod[:n_sample]
    op_ref[...] = mod[n_sample:]


def _mod_call(c_all, n_sample, w_ada, b_ada, n, tn=1024):
    n_rows, d = c_all.shape
    assert n % tn == 0
    est = 2 * (d * tn * 4) + 3 * n_rows * tn * 4 + n_rows * d * 4 * 2 + d * tn * 2
    return pl.pallas_call(
        _mod_kernel,
        grid=(n // tn,),
        in_specs=[pl.BlockSpec((n_rows, d), lambda j: (0, 0)),
                  pl.BlockSpec((d, tn), lambda j: (0, j)),
                  pl.BlockSpec((1, tn), lambda j: (0, j))],
        out_specs=[pl.BlockSpec((n_sample, tn), lambda j: (0, j)),
                   pl.BlockSpec((n_rows - n_sample, tn), lambda j: (0, j))],
        out_shape=[jax.ShapeDtypeStruct((n_sample, n), F32),
                   jax.ShapeDtypeStruct((n_rows - n_sample, n), F32)],
        compiler_params=pltpu.CompilerParams(dimension_semantics=("parallel",),
                                             vmem_limit_bytes=_vmem_limit(est)),
        name="mod",
    )(c_all, w_ada, b_ada)


def _mod_spec(mod, piece, d, tiles_per_batch, ngrid):
    if mod.ndim == 3:
        if ngrid == 1:
            return pl.BlockSpec((1, 1, d), lambda i: (i // tiles_per_batch, 0, piece))
        return pl.BlockSpec((1, 1, d), lambda i, j: (i // tiles_per_batch, 0, piece))
    rows = mod.shape[0]
    if ngrid == 1:
        return pl.BlockSpec((rows, d), lambda i: (0, piece))
    return pl.BlockSpec((rows, d), lambda i, j: (0, piece))


def _inproj_kernel(*refs, n_u_tiles, emit_w):
    x_ref, sc_ref, sh_ref, g_ref, w_ref, u_ref, z_ref = refs[:7]
    h_scr, inv_scr, row_scr = refs[-3:]
    j = pl.program_id(1)

    @pl.when(j == 0)
    def _():
        _norm_mod_bf16(x_ref, g_ref, sc_ref, sh_ref, h_scr, inv_scr, row_scr)

    if emit_w:
        wb_ref = refs[7]
        wb_ref[...] = w_ref[...].astype(BF16)
        w_ref = wb_ref
    tn = w_ref.shape[1]
    nc = min(tn, INPROJ_CHUNK)

    @pl.when(j < n_u_tiles)
    def _():
        u_ref[...] = _dot(h_scr[...], w_ref[...])

    @pl.when(j >= n_u_tiles)
    def _():
        for c in range(tn // nc):
            cols = slice(c * nc, (c + 1) * nc)
            z_ref[:, cols] = _dot(h_scr[...], w_ref[:, cols]).astype(BF16)


def _inproj_call(x2d, mod, g1, w_in, *, tm, tiles_per_batch, w_mix, tn=1024):
    m, d = x2d.shape
    n = w_in.shape[1]
    n_u = 2 * w_mix
    n_u_tiles = n_u // tn
    emit_w = w_in.dtype != BF16
    assert not emit_w or m == tm
    wbytes = w_in.dtype.itemsize
    est = (2 * tm * d * 4 + 2 * d * tn * wbytes + 2 * tm * tn * 4 + 2 * tm * tn * 2 + tm * d * 2 + tm * tn * 4
           + (3 * d * tn * 2 if emit_w else 0))
    out_specs = [pl.BlockSpec((tm, tn), lambda i, j: (i, jnp.minimum(j, n_u_tiles - 1))),
                 pl.BlockSpec((tm, tn), lambda i, j: (i, jnp.maximum(j - n_u_tiles, 0)))]
    out_shape = [jax.ShapeDtypeStruct((m, n_u), F32),
                 jax.ShapeDtypeStruct((m, n - n_u), BF16)]
    if emit_w:
        out_specs.append(pl.BlockSpec((d, tn), lambda i, j: (0, j)))
        out_shape.append(jax.ShapeDtypeStruct((d, n), BF16))
    return pl.pallas_call(
        functools.partial(_inproj_kernel, n_u_tiles=n_u_tiles, emit_w=emit_w),
        grid=(m // tm, n // tn),
        in_specs=[pl.BlockSpec((tm, d), lambda i, j: (i, 0)),
                  _mod_spec(mod, 1, d, tiles_per_batch, 2),
                  _mod_spec(mod, 0, d, tiles_per_batch, 2),
                  pl.BlockSpec((1, d), lambda i, j: (0, 0)),
                  pl.BlockSpec((d, tn), lambda i, j: (0, j))],
        out_specs=out_specs,
        out_shape=out_shape,
        scratch_shapes=[pltpu.VMEM((tm, d), BF16), pltpu.VMEM((tm, LANES), F32),
                        pltpu.VMEM((2 * SUBLANES, d), F32)],
        compiler_params=pltpu.CompilerParams(dimension_semantics=("parallel", "arbitrary"),
                                             vmem_limit_bytes=_vmem_limit(est)),
        name="inproj",
    )(x2d, mod, mod, g1, w_in)


def _lru_gate_block(uc, g, ba, bx, sp):
    r = _sigmoid(g[:, :LANES] + ba)
    i = _sigmoid(g[:, LANES:] + bx)
    log_a = (-LRU_C * r) * sp
    a = jnp.exp(log_a)
    m2 = 1.0 - jnp.exp(2.0 * log_a)
    mult = jnp.where(m2 > 0.0, m2 * lax.rsqrt(m2), 0.0)
    return a, (mult * i) * uc


def _lru_kernel(*refs, n_side):
    u_ref, cw_ref, cb_ref, wg_ref, ba_ref, bx_ref, lam_ref = refs[:7]
    hs_ref, nconv_ref, nlru_ref = refs[7 + n_side:10 + n_side]
    ext, a_s, b_s, carry = refs[-4:]
    _side_cast(refs[7:7 + n_side], refs[10 + n_side:10 + 2 * n_side])
    t = pl.program_id(0)
    nt = pl.num_programs(0)
    nbat, tl, w = u_ref.shape
    nblk = w // LANES
    rows = nbat * tl
    kw = cw_ref.shape[0]
    hist = ext.shape[1] - rows
    steps_per_tile = SUBLANES // nbat

    @pl.when(t == 0)
    def _():
        ext[:, 0:hist, :] = jnp.zeros((nblk, hist, LANES), F32)
        carry[...] = jnp.zeros_like(carry)

    for b in range(nbat):
        for c in range(nblk):
            ext[c, pl.ds(hist + b, tl, stride=nbat), :] = u_ref[b, :, c * LANES:(c + 1) * LANES]

        @pl.when(t == nt - 1)
        def _(b=b):
            nconv_ref[b] = u_ref[b, tl - (kw - 1):tl, :]

    sp = jax.nn.softplus(-lam_ref[...])
    for c in range(nblk):
        cols = slice(c * LANES, (c + 1) * LANES)
        uc = cb_ref[:, cols]
        for k in range(kw):
            start = hist - (kw - 1 - k) * nbat
            uc = uc + ext[c, start:start + rows, :] * cw_ref[k:k + 1, cols]
        g = _dot(uc.astype(BF16), wg_ref[c])
        a, b = _lru_gate_block(uc, g, ba_ref[:, cols], bx_ref[:, cols], sp[:, cols])
        a_s[:, cols] = a
        b_s[:, cols] = b
        ext[c, hist - (kw - 1) * nbat:hist, :] = ext[c, hist + rows - (kw - 1) * nbat:hist + rows, :]

    def tile_step(i, hcur):
        r0 = pl.multiple_of(i * SUBLANES, SUBLANES)
        at, bt = a_s[pl.ds(r0, SUBLANES), :], b_s[pl.ds(r0, SUBLANES), :]
        hs = []
        for k in range(steps_per_tile):
            hcur = at[k * nbat:(k + 1) * nbat] * hcur + bt[k * nbat:(k + 1) * nbat]
            hs.append(hcur)
        htile = jnp.concatenate(hs, axis=0)
        for c in range(nblk):
            ext[c, pl.ds(pl.multiple_of(hist + r0, SUBLANES), SUBLANES), :] = htile[:, c * LANES:(c + 1) * LANES]
        return hcur

    h_end = lax.fori_loop(0, tl // steps_per_tile, tile_step, carry[...], unroll=2)
    carry[...] = h_end

    for b in range(nbat):
        for c in range(nblk):
            hs_ref[b, :, c * LANES:(c + 1) * LANES] = ext[c, pl.ds(hist + b, tl, stride=nbat), :].astype(BF16)

    @pl.when(t == nt - 1)
    def _():
        for b in range(nbat):
            nlru_ref[b] = h_end[b:b + 1, :]


def _lru_call(u3d, cw, cb, wg, ba, bx, lam, *, tl, side=()):
    bsz, seq, _ = u3d.shape
    w = cw.shape[1]
    kw = cw.shape[0]
    nblk = w // LANES
    assert SUBLANES % bsz == 0
    rows = bsz * tl
    hist = -(-(kw - 1) * bsz // SUBLANES) * SUBLANES
    side_in, side_out, side_shapes, side_bytes = _side_cast_specs(side, seq // tl)
    est = (2 * rows * w * 4 + 2 * rows * w * 2 + (hist + rows) * w * 4 + 2 * rows * w * 4
           + nblk * LANES * 2 * LANES * 2 * 2 + 2 * rows * 2 * LANES * 4 + side_bytes)
    const2 = lambda t: (0, 0)
    return pl.pallas_call(
        functools.partial(_lru_kernel, n_side=len(side)),
        grid=(seq // tl,),
        in_specs=[pl.BlockSpec((bsz, tl, w), lambda t: (0, t, 0)),
                  pl.BlockSpec((kw, w), const2),
                  pl.BlockSpec((1, w), const2),
                  pl.BlockSpec((nblk, LANES, 2 * LANES), lambda t: (0, 0, 0)),
                  pl.BlockSpec((1, w), const2),
                  pl.BlockSpec((1, w), const2),
                  pl.BlockSpec((1, w), const2)] + side_in,
        out_specs=[pl.BlockSpec((bsz, tl, w), lambda t: (0, t, 0)),
                   pl.BlockSpec((bsz, kw - 1, w), lambda t: (0, 0, 0)),
                   pl.BlockSpec((bsz, 1, w), lambda t: (0, 0, 0))] + side_out,
        out_shape=[jax.ShapeDtypeStruct((bsz, seq, w), BF16),
                   jax.ShapeDtypeStruct((bsz, kw - 1, w), F32),
                   jax.ShapeDtypeStruct((bsz, 1, w), F32)] + side_shapes,
        scratch_shapes=[pltpu.VMEM((nblk, hist + rows, LANES), F32),
                        pltpu.VMEM((rows, w), F32),
                        pltpu.VMEM((rows, w), F32),
                        pltpu.VMEM((bsz, w), F32)],
        compiler_params=pltpu.CompilerParams(dimension_semantics=("arbitrary",),
                                             vmem_limit_bytes=_vmem_limit(est)),
        name="lru_scan",
    )(u3d, cw, cb, wg, ba, bx, lam, *(a for a, _ in side))


def _lru_step_kernel(u_ref, buf_ref, h0_ref, cw_ref, cb_ref, wg_ref, ba_ref, bx_ref, lam_ref,
                     hs_ref, nconv_ref, nlru_ref):
    w = cw_ref.shape[1]
    kw = cw_ref.shape[0]
    nblk = w // LANES
    u = u_ref[...]
    uc = cb_ref[...] + buf_ref[0] * cw_ref[0:1, :]
    for k in range(1, kw - 1):
        uc = uc + buf_ref[k] * cw_ref[k:k + 1, :]
    uc = uc + u * cw_ref[kw - 1:kw, :]
    for k in range(kw - 2):
        nconv_ref[k] = buf_ref[k + 1]
    nconv_ref[kw - 2] = u
    sp = jax.nn.softplus(-lam_ref[...])
    for h in range(nblk):
        cols = slice(h * LANES, (h + 1) * LANES)
        uch = uc[:, cols]
        g = _dot(uch.astype(BF16), wg_ref[h])
        a, b = _lru_gate_block(uch, g, ba_ref[:, cols], bx_ref[:, cols], sp[:, cols])
        hn = a * h0_ref[:, cols] + b
        nlru_ref[:, cols] = hn
        hs_ref[:, cols] = hn.astype(BF16)


def _lru_step_call(u2d, buf, h0, cw, cb, wg, ba, bx, lam):
    n = h0.shape[0]
    w = cw.shape[1]
    kw = cw.shape[0]
    nblk = w // LANES
    c2 = lambda i: (0, 0)
    c3 = lambda i: (0, 0, 0)
    return pl.pallas_call(
        _lru_step_kernel,
        grid=(1,),
        in_specs=[pl.BlockSpec((n, w), c2),
                  pl.BlockSpec((kw - 1, n, w), c3),
                  pl.BlockSpec((n, w), c2),
                  pl.BlockSpec((kw, w), c2),
                  pl.BlockSpec((1, w), c2),
                  pl.BlockSpec((nblk, LANES, 2 * LANES), c3),
                  pl.BlockSpec((1, w), c2),
                  pl.BlockSpec((1, w), c2),
                  pl.BlockSpec((1, w), c2)],
        out_specs=[pl.BlockSpec((n, w), c2),
                   pl.BlockSpec((kw - 1, n, w), c3),
                   pl.BlockSpec((n, w), c2)],
        out_shape=[jax.ShapeDtypeStruct((n, w), BF16),
                   jax.ShapeDtypeStruct((kw - 1, n, w), F32),
                   jax.ShapeDtypeStruct((n, w), F32)],
        name="lru_step",
    )(u2d, buf, h0, cw, cb, wg, ba, bx, lam)


def _s5_prep_kernel(arx_ref, aix_ref, dt_ref, br_ref, bi_ref, ar_ref, ai_ref,
                    bb_ref, pwr_ref, pwi_ref):
    dt = jnp.exp(dt_ref[...])

    def abar(ar, ai):
        mag = jnp.exp(dt * ar)
        ang = dt * ai
        return mag * jnp.cos(ang), mag * jnp.sin(ang)

    ar, ai = arx_ref[...], aix_ref[...]
    abr, abi = abar(ar, ai)
    den = ar * ar + ai * ai
    nr, ni = abr - 1.0, abi
    q_re = (nr * ar + ni * ai) / den
    q_im = (ni * ar - nr * ai) / den
    br, bi = br_ref[...], bi_ref[...]
    bb_ref[0] = q_re * br - q_im * bi
    bb_ref[1] = q_re * bi + q_im * br

    pr, pi = abar(ar_ref[...], ai_ref[...])
    qr, qi = jnp.ones_like(pr), jnp.zeros_like(pr)
    for m in range(pwr_ref.shape[0]):
        pwr_ref[m] = qr
        pwi_ref[m] = qi
        qr, qi = qr * pr - qi * pi, qr * pi + qi * pr


def _s5_prep_call(a_re, a_im, log_dt, b_re, b_im, n_pow):
    g, p, k = b_re.shape
    arx = jnp.repeat(a_re, k, axis=1)
    aix = jnp.repeat(a_im, k, axis=1)
    bb, pwr, pwi = pl.pallas_call(
        _s5_prep_kernel,
        out_shape=[jax.ShapeDtypeStruct((2, g, p * k), F32),
                   jax.ShapeDtypeStruct((n_pow, g, p), F32), jax.ShapeDtypeStruct((n_pow, g, p), F32)],
        name="s5_prep",
    )(arx, aix, log_dt.reshape(g, 1), b_re.reshape(g, p * k), b_im.reshape(g, p * k), a_re, a_im)
    return bb.reshape(2, g, p, k), pwr, pwi


def _s5_block_weights(bb, c_re, c_im):
    _, g, p, k = bb.shape
    gb = GROUPS_PER_BLOCK
    nb = g // gb
    eye = jnp.eye(gb, dtype=F32)
    t = bb.reshape(2, nb, gb, p, k).transpose(1, 2, 4, 0, 3)
    bblk = jnp.einsum('cgkrp,gh->cgkrhp', t, eye).reshape(nb, gb * k, 2 * gb * p)
    cc = jnp.stack([c_re, c_im]).reshape(2, nb, gb, k, p)
    cblk = jnp.einsum('rcgkp,hg->rchpgk', cc, eye).reshape(2, nb, gb * p, gb * k)
    return bblk, cblk[0], cblk[1]


def _cplx_mul(ar, ai, br, bi):
    return ar * br - ai * bi, ar * bi + ai * br


def _cplx_step(ar, ai, hr, hi, br, bi):
    return ar * hr - ai * hi + br, ar * hi + ai * hr + bi


def _s5_kernel(u_ref, bblk_ref, creb_ref, cimb_ref, pwr_ref, pwi_ref, d_ref, c_ref, wa_ref, ba_ref,
               gy_ref, nre_ref, nim_ref, ms_ref, mp_ref,
               ufb, buf, wbs, wqs, wts, yslab, out32):
    _mod_kernel(c_ref, wa_ref, ba_ref, ms_ref, mp_ref)
    nbat, seq, _ = u_ref.shape
    spb = STATES_PER_BLOCK
    fold = wbs.shape[0] // LANES
    nq = seq // fold

    bre, bim = bblk_ref[0, :, :spb], bblk_ref[0, :, spb:]
    creb, cimb = creb_ref[0], cimb_ref[0]
    wcb = jnp.concatenate([creb, -cimb], axis=0).astype(BF16)
    lag = []
    for m in range(fold):
        wre, wim = _cplx_mul(bre, bim, pwr_ref[0, m:m + 1, :], pwi_ref[0, m:m + 1, :])
        rows = slice((fold - 1 - m) * LANES, (fold - m) * LANES)
        wbs[rows, :spb] = wre.astype(BF16)
        wbs[rows, spb:] = wim.astype(BF16)
        lag.append(_dot(wbs[rows, :], wcb))
    zero_blk = jnp.zeros((LANES, LANES), BF16)
    for r_in in range(fold):
        for r_out in range(fold):
            blk = lag[r_out - r_in].astype(BF16) if r_in <= r_out else zero_blk
            wts[r_in * LANES:(r_in + 1) * LANES, r_out * LANES:(r_out + 1) * LANES] = blk
    ctre, ctim = creb.T, cimb.T
    for r in range(fold):
        pr, pi = pwr_ref[0, r + 1:r + 2, :], pwi_ref[0, r + 1:r + 2, :]
        cols = slice(r * LANES, (r + 1) * LANES)
        wqs[:spb, cols] = (ctre * pr - ctim * pi).T.astype(BF16)
        wqs[spb:, cols] = (-(ctre * pi + ctim * pr)).T.astype(BF16)

    for r in range(fold):
        for b in range(nbat):
            yslab[r, pl.ds(b, nq, stride=nbat), :] = u_ref[b, pl.ds(r, nq, stride=fold), :]
        ufb[:, r * LANES:(r + 1) * LANES] = yslab[r].astype(BF16)

    buf[...] = _dot(ufb[...], wbs[...])

    ar, ai = pwr_ref[0, fold:fold + 1, :], pwi_ref[0, fold:fold + 1, :]
    groups_per_tile = SUBLANES // nbat
    z = jnp.zeros((nbat, spb), F32)

    def tile_step(i, g):
        gr, gi = g
        r0 = pl.multiple_of(i * SUBLANES, SUBLANES)
        xr, xi = buf[pl.ds(r0, SUBLANES), :spb], buf[pl.ds(r0, SUBLANES), spb:]
        hr, hi = [], []
        for k in range(groups_per_tile):
            hr.append(gr)
            hi.append(gi)
            gr, gi = _cplx_step(ar, ai, gr, gi, xr[k * nbat:(k + 1) * nbat], xi[k * nbat:(k + 1) * nbat])
        buf[pl.ds(r0, SUBLANES), :spb] = jnp.concatenate(hr, axis=0)
        buf[pl.ds(r0, SUBLANES), spb:] = jnp.concatenate(hi, axis=0)
        return gr, gi

    g_re, g_im = lax.fori_loop(0, nq // groups_per_tile, tile_step, (z, z), unroll=2)
    for b in range(nbat):
        nre_ref[b] = g_re[b:b + 1, :]
        nim_ref[b] = g_im[b:b + 1, :]

    hsb = buf[...].astype(BF16)
    per_tile = 2
    for n in range(fold // per_tile):
        cols = slice(n * per_tile * LANES, (n + 1) * per_tile * LANES)
        kk = (n + 1) * per_tile * LANES
        yn = _dot(hsb, wqs[:, cols]) + _dot(ufb[:, :kk], wts[:kk, cols])
        for i in range(per_tile):
            yslab[n * per_tile + i] = yn[:, i * LANES:(i + 1) * LANES]

    for b in range(nbat):
        for r in range(fold):
            ys = (yslab[r, pl.ds(b, nq, stride=nbat), :]
                  + d_ref[...] * u_ref[b, pl.ds(r, nq, stride=fold), :])
            out32[b, pl.ds(r, nq, stride=fold), :] = jax.nn.gelu(ys)
        gy_ref[b] = out32[b].astype(BF16)


def _s5_call(u3d, bblk, creb, cimb, pw_re, pw_im, d, c_all, n_sample, w_ada, b_ada, first_col, *, w):
    bsz, seq, _ = u3d.shape
    nblk = w // LANES
    spb = STATES_PER_BLOCK
    fold = S5_FOLD
    n_pow = fold + 1
    assert SUBLANES % bsz == 0 and seq % fold == 0
    rows = bsz * (seq // fold)
    n_c, dm = c_all.shape
    mod_cols = w_ada.shape[1] - first_col
    tn = mod_cols // nblk
    assert mod_cols % nblk == 0 and tn % LANES == 0 and first_col % tn == 0
    est = (2 * bsz * seq * LANES * 4 + 2 * bsz * seq * LANES * 2
           + 2 * dm * tn * 4 + dm * tn * 2 + 2 * n_c * dm * 4 + 3 * n_c * tn * 4
           + rows * fold * LANES * 2 + 2 * rows * 2 * spb * 4
           + 3 * fold * LANES * 2 * spb * 2
           + rows * fold * LANES * 4 + bsz * seq * LANES * 4
           + rows * 2 * spb * 2
           + 2 * (LANES * 2 * spb + 2 * spb * LANES) * 4)
    return pl.pallas_call(
        _s5_kernel,
        grid=(nblk,),
        in_specs=[pl.BlockSpec((bsz, seq, LANES), lambda c: (0, 0, nblk + c)),
                  pl.BlockSpec((1, LANES, 2 * spb), lambda c: (c, 0, 0)),
                  pl.BlockSpec((1, spb, LANES), lambda c: (c, 0, 0)),
                  pl.BlockSpec((1, spb, LANES), lambda c: (c, 0, 0)),
                  pl.BlockSpec((1, n_pow, spb), lambda c: (c, 0, 0)),
                  pl.BlockSpec((1, n_pow, spb), lambda c: (c, 0, 0)),
                  pl.BlockSpec((1, LANES), lambda c: (0, c)),
                  pl.BlockSpec((n_c, dm), lambda c: (0, 0)),
                  pl.BlockSpec((dm, tn), lambda c: (0, first_col // tn + c)),
                  pl.BlockSpec((1, tn), lambda c: (0, first_col // tn + c))],
        out_specs=[pl.BlockSpec((bsz, seq, LANES), lambda c: (0, 0, c)),
                   pl.BlockSpec((bsz, 1, spb), lambda c: (0, 0, c)),
                   pl.BlockSpec((bsz, 1, spb), lambda c: (0, 0, c)),
                   pl.BlockSpec((n_sample, tn), lambda c: (0, c)),
                   pl.BlockSpec((n_c - n_sample, tn), lambda c: (0, c))],
        out_shape=[jax.ShapeDtypeStruct((bsz, seq, w), BF16),
                   jax.ShapeDtypeStruct((bsz, 1, nblk * spb), F32),
                   jax.ShapeDtypeStruct((bsz, 1, nblk * spb), F32),
                   jax.ShapeDtypeStruct((n_sample, mod_cols), F32),
                   jax.ShapeDtypeStruct((n_c - n_sample, mod_cols), F32)],
        scratch_shapes=[pltpu.VMEM((rows, fold * LANES), BF16),
                        pltpu.VMEM((rows, 2 * spb), F32),
                        pltpu.VMEM((fold * LANES, 2 * spb), BF16),
                        pltpu.VMEM((2 * spb, fold * LANES), BF16),
                        pltpu.VMEM((fold * LANES, fold * LANES), BF16),
                        pltpu.VMEM((fold, rows, LANES), F32),
                        pltpu.VMEM((bsz, seq, LANES), F32)],
        compiler_params=pltpu.CompilerParams(dimension_semantics=("parallel",),
                                             vmem_limit_bytes=_vmem_limit(est)),
        name="s5_scan",
    )(u3d, bblk, creb, cimb, pw_re, pw_im, d, c_all, w_ada, b_ada)


def _s5_step_kernel(u_ref, h0r_ref, h0i_ref, bblk_ref, creb_ref, cimb_ref, pwr_ref, pwi_ref, d_ref,
                    gy_ref, nre_ref, nim_ref):
    w = u_ref.shape[1]
    nblk = w // LANES
    spb = STATES_PER_BLOCK
    for c in range(nblk):
        cols = slice(c * LANES, (c + 1) * LANES)
        scols = slice(c * spb, (c + 1) * spb)
        u = u_ref[:, cols]
        r = _dot(u.astype(BF16), bblk_ref[c].astype(BF16))
        hr, hi = _cplx_step(pwr_ref[c, 1:2, :], pwi_ref[c, 1:2, :], h0r_ref[:, scols], h0i_ref[:, scols],
                            r[:, :spb], r[:, spb:])
        nre_ref[:, scols] = hr
        nim_ref[:, scols] = hi
        y = (_dot(hr.astype(BF16), creb_ref[c].astype(BF16))
             - _dot(hi.astype(BF16), cimb_ref[c].astype(BF16)))
        gy_ref[:, cols] = jax.nn.gelu(y + d_ref[:, cols] * u).astype(BF16)


def _s5_step_call(u2d, h0r, h0i, bblk, creb, cimb, pw_re, pw_im, d, *, w):
    n, n_state = h0r.shape
    c2 = lambda i: (0, 0)
    c3 = lambda i: (0, 0, 0)
    return pl.pallas_call(
        _s5_step_kernel,
        grid=(1,),
        in_specs=[pl.BlockSpec((n, w), lambda i: (0, 1)),
                  pl.BlockSpec((n, n_state), c2),
                  pl.BlockSpec((n, n_state), c2),
                  pl.BlockSpec(bblk.shape, c3),
                  pl.BlockSpec(creb.shape, c3),
                  pl.BlockSpec(cimb.shape, c3),
                  pl.BlockSpec(pw_re.shape, c3),
                  pl.BlockSpec(pw_im.shape, c3),
                  pl.BlockSpec((1, w), c2)],
        out_specs=[pl.BlockSpec((n, w), c2),
                   pl.BlockSpec((n, n_state), c2),
                   pl.BlockSpec((n, n_state), c2)],
        out_shape=[jax.ShapeDtypeStruct((n, w), BF16),
                   jax.ShapeDtypeStruct((n, n_state), F32),
                   jax.ShapeDtypeStruct((n, n_state), F32)],
        name="s5_step",
    )(u2d, h0r, h0i, bblk, creb, cimb, pw_re, pw_im, d)


def _merge_kernel(*refs, n_side):
    hs_ref, gy_ref, z_ref, x_ref, gt_ref, wp_ref, wg_ref, wo_ref = refs[:8]
    o_ref = refs[8 + n_side]
    _side_cast(refs[8:8 + n_side], refs[9 + n_side:])
    d = x_ref.shape[1]
    ya = _dot(hs_ref[...], wp_ref[...])
    merged = _sigmoid(z_ref[:, :d].astype(F32)) * ya
    glu = _dot(gy_ref[...], wg_ref[...])
    yb = glu[:, :d] * _sigmoid(glu[:, d:])
    merged = merged + _sigmoid(z_ref[:, d:].astype(F32)) * yb
    o = _dot(merged.astype(BF16), wo_ref[...])
    o_ref[...] = x_ref[...] + _rows(gt_ref) * o


def _merge_call(hs, gy, sz, x2d, mod, wp, wg, wo, *, tm, tiles_per_batch, side=()):
    m, d = x2d.shape
    w = hs.shape[1]
    side_in, side_out, side_shapes, side_bytes = _side_cast_specs(side, m // tm)
    est = (2 * (2 * tm * w * 2 + tm * 2 * d * 2 + 2 * tm * d * 4) + (wp.size + wg.size + wo.size) * 2
           + tm * d * 4 * 5 + side_bytes)
    one = pl.Buffered(1)
    return pl.pallas_call(
        functools.partial(_merge_kernel, n_side=len(side)),
        grid=(m // tm,),
        in_specs=[pl.BlockSpec((tm, w), lambda i: (i, 0)),
                  pl.BlockSpec((tm, w), lambda i: (i, 0)),
                  pl.BlockSpec((tm, 2 * d), lambda i: (i, 0)),
                  pl.BlockSpec((tm, d), lambda i: (i, 0)),
                  _mod_spec(mod, 2, d, tiles_per_batch, 1),
                  pl.BlockSpec(wp.shape, lambda i: (0, 0), pipeline_mode=one),
                  pl.BlockSpec(wg.shape, lambda i: (0, 0), pipeline_mode=one),
                  pl.BlockSpec(wo.shape, lambda i: (0, 0), pipeline_mode=one)] + side_in,
        out_specs=[pl.BlockSpec((tm, d), lambda i: (i, 0))] + side_out,
        out_shape=[jax.ShapeDtypeStruct((m, d), F32)] + side_shapes,
        compiler_params=pltpu.CompilerParams(dimension_semantics=("parallel",),
                                             vmem_limit_bytes=_vmem_limit(est)),
        name="merge",
    )(hs, gy, sz, x2d, mod, wp, wg, wo, *(a for a, _ in side))


def _mlp_kernel(x_ref, sc_ref, sh_ref, gt_ref, g2_ref, gf_ref, wu_ref, wd_ref, o_ref, h_scr, inv_scr, row_scr, *,
                final_norm):
    j = pl.program_id(1)
    nj = pl.num_programs(1)
    n_rows, d = x_ref.shape

    @pl.when(j == 0)
    def _():
        _norm_mod_bf16(x_ref, g2_ref, sc_ref, sh_ref, h_scr, inv_scr, row_scr)
        o_ref[...] = jnp.zeros(o_ref.shape, F32)

    up = _dot(h_scr[...], wu_ref[...])
    act = jnp.square(jnp.maximum(up, 0.0)).astype(BF16)
    nc = min(d, MLP_OUT_CHUNK)
    for c in range(d // nc):
        cols = slice(c * nc, (c + 1) * nc)
        o_ref[:, cols] += _dot(act, wd_ref[:, cols])

    @pl.when(j == nj - 1)
    def _():
        def chunk(rows):
            x2 = x_ref[rows, :] + _rows(gt_ref, rows) * o_ref[rows, :]
            if final_norm:
                ms = jnp.mean(x2 * x2, axis=-1, keepdims=True)
                x2 = (x2 * lax.rsqrt(ms + EPS)) * gf_ref[...]
            o_ref[rows, :] = x2

        _row_groups(n_rows, min(n_rows, EPILOGUE_ROWS), chunk, unroll=1)


def _mlp_call(x2d, mod, g2, gf, w_up, w_down, *, tm, tiles_per_batch, final_norm, tf):
    m, d = x2d.shape
    dff = w_up.shape[1]
    assert w_up.dtype == BF16 and w_down.dtype == BF16
    est = 3 * tm * d * 4 + tm * d * 2 + 4 * d * tf * 2 + tm * tf * 6 + tm * MLP_OUT_CHUNK * 4
    return pl.pallas_call(
        functools.partial(_mlp_kernel, final_norm=final_norm),
        grid=(m // tm, dff // tf),
        in_specs=[pl.BlockSpec((tm, d), lambda i, j: (i, 0), pipeline_mode=pl.Buffered(1)),
                  _mod_spec(mod, 4 - MLP_MOD_FIRST, d, tiles_per_batch, 2),
                  _mod_spec(mod, 3 - MLP_MOD_FIRST, d, tiles_per_batch, 2),
                  _mod_spec(mod, 5 - MLP_MOD_FIRST, d, tiles_per_batch, 2),
                  pl.BlockSpec((1, d), lambda i, j: (0, 0)),
                  pl.BlockSpec((1, d), lambda i, j: (0, 0)),
                  pl.BlockSpec((d, tf), lambda i, j: (0, j)),
                  pl.BlockSpec((tf, d), lambda i, j: (j, 0))],
        out_specs=pl.BlockSpec((tm, d), lambda i, j: (i, 0)),
        out_shape=jax.ShapeDtypeStruct((m, d), F32),
        scratch_shapes=[pltpu.VMEM((tm, d), BF16), pltpu.VMEM((tm, LANES), F32),
                        pltpu.VMEM((2 * SUBLANES, d), F32)],
        compiler_params=pltpu.CompilerParams(dimension_semantics=("parallel", "arbitrary"),
                                             vmem_limit_bytes=_vmem_limit(est)),
        name="mlp",
    )(x2d, mod, mod, mod, g2, gf, w_up, w_down)


def _pick_tile(n, pref):
    t = min(n, pref)
    assert n % t == 0, (n, t)
    return t


def kernel(x_prompt, x_sample, state_conv, state_lru, state_ssm_re, state_ssm_im, c_prompt, c_sample, w_ada, b_ada, g_norm1, g_norm2, w_in, conv_w, conv_b, w_rg_a, b_rg_a, w_rg_x, b_rg_x, lru_lambda, w_proj_a, ssm_a_re, ssm_a_im, ssm_log_dt, ssm_b_re, ssm_b_im, ssm_c_re, ssm_c_im, ssm_d, w_glu, w_out, w_up, w_down, g_final):
    depth = w_ada.shape[0]
    nb, seq, d = x_prompt.shape
    ns = x_sample.shape[0]
    assert x_sample.shape[1] == 1
    w = conv_w.shape[2]
    n_state = ssm_a_re.shape[1] * ssm_a_re.shape[2]
    assert ssm_b_re.shape[2:] == (SSM_STATE, SSM_GROUP) and w % LANES == 0
    assert w_in.shape[2] == 2 * w + 2 * d

    tl = _pick_tile(seq, 256)
    tm_in = _pick_tile(seq, 1024)
    tm_mg = _pick_tile(seq, 256)
    tm_mlp = _pick_tile(seq, 1024)

    xp = x_prompt.reshape(nb * seq, d)
    xs = x_sample.reshape(ns, d)
    assert ns % SUBLANES == 0
    pad = (-nb) % SUBLANES
    c_all = jnp.concatenate([c_sample, c_prompt, jnp.zeros((pad, d), F32)], axis=0)

    outs_p = [[] for _ in range(4)]
    outs_s = [[] for _ in range(4)]
    for l in range(depth):
        last = l == depth - 1
        b_ada_row = b_ada[l].reshape(1, -1)
        mod_s, mod_p = _mod_call(c_all, ns, w_ada[l], b_ada_row, MLP_MOD_FIRST * d)
        mod_p = mod_p.reshape(nb + pad, 1, MLP_MOD_FIRST * d)

        g1 = g_norm1[l].reshape(1, d)
        g2 = g_norm2[l].reshape(1, d)
        gf = g_final.reshape(1, d)
        cw, cb = conv_w[l], conv_b[l].reshape(1, w)
        wg = jnp.concatenate([w_rg_a[l], w_rg_x[l]], axis=2).astype(BF16)
        ba, bx, lam = b_rg_a[l].reshape(1, w), b_rg_x[l].reshape(1, w), lru_lambda[l].reshape(1, w)
        n_pow = S5_FOLD + 1
        bb, pwr, pwi = _s5_prep_call(ssm_a_re[l], ssm_a_im[l], ssm_log_dt[l], ssm_b_re[l], ssm_b_im[l], n_pow)
        bblk, creb, cimb = _s5_block_weights(bb, ssm_c_re[l], ssm_c_im[l])
        nblk = w // LANES
        pw_rows = [v.reshape(n_pow, nblk, STATES_PER_BLOCK).transpose(1, 0, 2) for v in (pwr, pwi)]
        dskip = ssm_d[l].reshape(1, w)

        u_s, sz_s, w_in_b = _inproj_call(xs, mod_s, g1, w_in[l], tm=ns, tiles_per_batch=1, w_mix=w)

        u_p, sz_p = _inproj_call(xp, mod_p, g1, w_in_b, tm=tm_in, tiles_per_batch=seq // tm_in, w_mix=w)
        u3 = u_p.reshape(nb, seq, 2 * w)
        hs_p, nconv_p, nlru_p, wp, wgl, wo = _lru_call(u3, cw, cb, wg, ba, bx, lam, tl=tl,
                                                       side=((w_proj_a[l], 1), (w_glu[l], 1), (w_out[l], 1)))
        gy_p, nre_p, nim_p, mlp_mod_s, mlp_mod_p = _s5_call(u3, bblk, creb, cimb, *pw_rows, dskip, c_all, ns,
                                                            w_ada[l], b_ada_row, MLP_MOD_FIRST * d, w=w)
        mlp_mod_p = mlp_mod_p.reshape(nb + pad, 1, (N_MOD - MLP_MOD_FIRST) * d)
        x1_p, w_up_b, w_down_b = _merge_call(hs_p.reshape(nb * seq, w), gy_p.reshape(nb * seq, w), sz_p, xp, mod_p,
                                             wp, wgl, wo, tm=tm_mg, tiles_per_batch=seq // tm_mg,
                                             side=((w_up[l], 1), (w_down[l], 0)))
        xp = _mlp_call(x1_p, mlp_mod_p, g2, gf, w_up_b, w_down_b, tm=tm_mlp, tiles_per_batch=seq // tm_mlp,
                       final_norm=last, tf=1024)

        buf = jnp.transpose(state_conv[l], (1, 0, 2))
        hs_s, nconv_s, nlru_s = _lru_step_call(u_s, buf, state_lru[l], cw, cb, wg, ba, bx, lam)
        gy_s, nre_s, nim_s = _s5_step_call(u_s, state_ssm_re[l].reshape(ns, n_state),
                                           state_ssm_im[l].reshape(ns, n_state), bblk, creb, cimb, *pw_rows, dskip,
                                           w=w)
        (x1_s,) = _merge_call(hs_s, gy_s, sz_s, xs, mod_s, wp, wgl, wo, tm=ns, tiles_per_batch=1)
        xs = _mlp_call(x1_s, mlp_mod_s, g2, gf, w_up_b, w_down_b, tm=ns, tiles_per_batch=1, final_norm=last,
                       tf=1024)

        gshape = ssm_a_re.shape[1:]
        for acc, v in zip(outs_p, (nconv_p, nlru_p.reshape(nb, w),
                                   nre_p.reshape((nb,) + gshape), nim_p.reshape((nb,) + gshape))):
            acc.append(v)
        for acc, v in zip(outs_s, (jnp.transpose(nconv_s, (1, 0, 2)), nlru_s,
                                   nre_s.reshape((ns,) + gshape), nim_s.reshape((ns,) + gshape))):
            acc.append(v)

    y_prompt = xp.reshape(nb, seq, d)
    y_sample = xs.reshape(ns, 1, d)
    stack = lambda a: a[0][None] if len(a) == 1 else jnp.stack(a)
    return (y_prompt, y_sample) + tuple(stack(a) for a in outs_p) + tuple(stack(a) for a in outs_s)
```

```python
import functools

import jax
import jax.numpy as jnp
from jax import lax
from jax.experimental import pallas as pl
from jax.experimental.pallas import tpu as pltpu

F32 = jnp.float32
BF16 = jnp.bfloat16

LANES = 128
SUBLANES = 8
BF16_ROWS = 2 * SUBLANES
VMEM_PHYSICAL_BYTES = 64 * 1024 * 1024
VMEM_LIMIT_CAP_BYTES = VMEM_PHYSICAL_BYTES - 6 * 1024 * 1024
VMEM_SLACK_BYTES = 8 * 1024 * 1024

LRU_C = 8.0
EPS = 1e-6
N_MOD = 6
MLP_MOD_FIRST = 3
SSM_GROUP = 16
SSM_STATE = 64
GROUPS_PER_BLOCK = LANES // SSM_GROUP
STATES_PER_BLOCK = GROUPS_PER_BLOCK * SSM_STATE

EPILOGUE_ROWS = 256
MLP_OUT_CHUNK = 512
INPROJ_CHUNK = 256
S5_FOLD = 8


def _vmem_limit(nbytes):
    return int(min(VMEM_LIMIT_CAP_BYTES, max(32 * 1024 * 1024, nbytes + VMEM_SLACK_BYTES)))


def _rows(ref, rows=None):
    if len(ref.shape) == 3:
        return ref[0]
    return ref[...] if rows is None else ref[rows, :]


def _row_groups(n_rows, group, fn, unroll):
    assert n_rows % group == 0

    def body(i, carry):
        fn(pl.ds(pl.multiple_of(i * group, group), group))
        return carry

    lax.fori_loop(0, n_rows // group, body, 0, unroll=unroll)


def _rms_scale_pass(src, inv_scr, d):
    def fn(rows):
        x = src(rows)
        xx = x * x
        part = xx[:, 0:LANES]
        for c in range(1, d // LANES):
            part = part + xx[:, c * LANES:(c + 1) * LANES]
        inv_scr[rows, :] = part

    _row_groups(inv_scr.shape[0], SUBLANES, fn, unroll=8)
    ss = jnp.sum(inv_scr[...], axis=-1, keepdims=True)
    inv_scr[...] = jnp.broadcast_to(lax.rsqrt(ss * (1.0 / d) + EPS), inv_scr.shape)


def _lanes(inv, d):
    return jnp.concatenate([inv] * (d // LANES), axis=1)


def _sublane_rows(scr, slot, row=None):
    rows = slice(slot * SUBLANES, (slot + 1) * SUBLANES)
    if row is None:
        return scr[rows, :]
    scr[rows, :] = jnp.broadcast_to(row, (SUBLANES, row.shape[1]))
    return None


def _norm_mod_bf16(x_ref, g_ref, sc_ref, sh_ref, h_scr, inv_scr, row_scr):
    n_rows, d = x_ref.shape
    _rms_scale_pass(lambda rows: x_ref[rows, :], inv_scr, d)
    per_token = len(sc_ref.shape) == 2
    if not per_token:
        _sublane_rows(row_scr, 0, g_ref[...] * (1.0 + sc_ref[0]))
        _sublane_rows(row_scr, 1, sh_ref[0])

    def fn(rows):
        halves = []
        for k in range(BF16_ROWS // SUBLANES):
            r8 = pl.ds(pl.multiple_of(rows.start + k * SUBLANES, SUBLANES), SUBLANES)
            gm = g_ref[...] * (1.0 + sc_ref[r8, :]) if per_token else _sublane_rows(row_scr, 0)
            sh = sh_ref[r8, :] if per_token else _sublane_rows(row_scr, 1)
            halves.append((x_ref[r8, :] * _lanes(inv_scr[r8, :], d)) * gm + sh)
        h_scr[rows, :] = jnp.concatenate(halves, axis=0).astype(BF16)

    _row_groups(n_rows, BF16_ROWS, fn, unroll=2)


def _dot(a, b):
    return jnp.dot(a, b, preferred_element_type=F32)


def _sigmoid(x):
    return 0.5 * jnp.tanh(0.5 * x) + 0.5


def _side_cast_specs(side, n_steps):
    in_specs, out_specs, out_shapes, nbytes = [], [], [], 0
    for arr, axis in side:
        blk = tuple(s // n_steps if a == axis else s for a, s in enumerate(arr.shape))
        assert arr.ndim == 2 and arr.shape[axis] % n_steps == 0 and blk[1] % LANES == 0 and blk[0] % BF16_ROWS == 0
        idx = (lambda i: (i, 0)) if axis == 0 else (lambda i: (0, i))
        in_specs.append(pl.BlockSpec(blk, idx))
        out_specs.append(pl.BlockSpec(blk, idx))
        out_shapes.append(jax.ShapeDtypeStruct(arr.shape, BF16))
        nbytes += 2 * blk[0] * blk[1] * (4 + 2)
    return in_specs, out_specs, out_shapes, nbytes


def _side_cast(src_refs, dst_refs):
    for src, dst in zip(src_refs, dst_refs, strict=True):
        dst[...] = src[...].astype(BF16)


def _mod_kernel(c_ref, w_ref, b_ref, os_ref, op_ref):
    c = c_ref[...]
    cs = (c * _sigmoid(c)).astype(BF16)
    mod = _dot(cs, w_ref[...].astype(BF16)) + b_ref[...]
    n_sample = os_ref.shape[0]
    os_ref[...] = mod[:n_sample]
    op_ref[...] = mod[n_sample:]


def _mod_call(c_all, n_sample, w_ada, b_ada, n, tn=1024):
    n_rows, d = c_all.shape
    assert n % tn == 0
    est = 2 * (d * tn * 4) + 3 * n_rows * tn * 4 + n_rows * d * 4 * 2 + d * tn * 2
    return pl.pallas_call(
        _mod_kernel,
        grid=(n // tn,),
        in_specs=[pl.BlockSpec((n_rows, d), lambda j: (0, 0)),
                  pl.BlockSpec((d, tn), lambda j: (0, j)),
                  pl.BlockSpec((1, tn), lambda j: (0, j))],
        out_specs=[pl.BlockSpec((n_sample, tn), lambda j: (0, j)),
                   pl.BlockSpec((n_rows - n_sample, tn), lambda j: (0, j))],
        out_shape=[jax.ShapeDtypeStruct((n_sample, n), F32),
                   jax.ShapeDtypeStruct((n_rows - n_sample, n), F32)],
        compiler_params=pltpu.CompilerParams(dimension_semantics=("parallel",),
                                             vmem_limit_bytes=_vmem_limit(est)),
        name="mod",
    )(c_all, w_ada, b_ada)


def _mod_spec(mod, piece, d, tiles_per_batch, ngrid):
    if mod.ndim == 3:
        if ngrid == 1:
            return pl.BlockSpec((1, 1, d), lambda i: (i // tiles_per_batch, 0, piece))
        return pl.BlockSpec((1, 1, d), lambda i, j: (i // tiles_per_batch, 0, piece))
    rows = mod.shape[0]
    if ngrid == 1:
        return pl.BlockSpec((rows, d), lambda i: (0, piece))
    return pl.BlockSpec((rows, d), lambda i, j: (0, piece))


def _inproj_kernel(*refs, n_u_tiles, emit_w):
    x_ref, sc_ref, sh_ref, g_ref, w_ref, u_ref, z_ref = refs[:7]
    h_scr, inv_scr, row_scr = refs[-3:]
    j = pl.program_id(1)

    @pl.when(j == 0)
    def _():
        _norm_mod_bf16(x_ref, g_ref, sc_ref, sh_ref, h_scr, inv_scr, row_scr)

    if emit_w:
        wb_ref = refs[7]
        wb_ref[...] = w_ref[...].astype(BF16)
        w_ref = wb_ref
    tn = w_ref.shape[1]
    nc = min(tn, INPROJ_CHUNK)

    @pl.when(j < n_u_tiles)
    def _():
        u_ref[...] = _dot(h_scr[...], w_ref[...])

    @pl.when(j >= n_u_tiles)
    def _():
        for c in range(tn // nc):
            cols = slice(c * nc, (c + 1) * nc)
            z_ref[:, cols] = _dot(h_scr[...], w_ref[:, cols]).astype(BF16)


def _inproj_call(x2d, mod, g1, w_in, *, tm, tiles_per_batch, w_mix, tn=1024):
    m, d = x2d.shape
    n = w_in.shape[1]
    n_u = 2 * w_mix
    n_u_tiles = n_u // tn
    emit_w = w_in.dtype != BF16
    assert not emit_w or m == tm
    wbytes = w_in.dtype.itemsize
    est = (2 * tm * d * 4 + 2 * d * tn * wbytes + 2 * tm * tn * 4 + 2 * tm * tn * 2 + tm * d * 2 + tm * tn * 4
           + (3 * d * tn * 2 if emit_w else 0))
    out_specs = [pl.BlockSpec((tm, tn), lambda i, j: (i, jnp.minimum(j, n_u_tiles - 1))),
                 pl.BlockSpec((tm, tn), lambda i, j: (i, jnp.maximum(j - n_u_tiles, 0)))]
    out_shape = [jax.ShapeDtypeStruct((m, n_u), F32),
                 jax.ShapeDtypeStruct((m, n - n_u), BF16)]
    if emit_w:
        out_specs.append(pl.BlockSpec((d, tn), lambda i, j: (0, j)))
        out_shape.append(jax.ShapeDtypeStruct((d, n), BF16))
    return pl.pallas_call(
        functools.partial(_inproj_kernel, n_u_tiles=n_u_tiles, emit_w=emit_w),
        grid=(m // tm, n // tn),
        in_specs=[pl.BlockSpec((tm, d), lambda i, j: (i, 0)),
                  _mod_spec(mod, 1, d, tiles_per_batch, 2),
                  _mod_spec(mod, 0, d, tiles_per_batch, 2),
                  pl.BlockSpec((1, d), lambda i, j: (0, 0)),
                  pl.BlockSpec((d, tn), lambda i, j: (0, j))],
        out_specs=out_specs,
        out_shape=out_shape,
        scratch_shapes=[pltpu.VMEM((tm, d), BF16), pltpu.VMEM((tm, LANES), F32),
                        pltpu.VMEM((2 * SUBLANES, d), F32)],
        compiler_params=pltpu.CompilerParams(dimension_semantics=("parallel", "arbitrary"),
                                             vmem_limit_bytes=_vmem_limit(est)),
        name="inproj",
    )(x2d, mod, mod, g1, w_in)


def _lru_gate_block(uc, g, ba, bx, sp):
    r = _sigmoid(g[:, :LANES] + ba)
    i = _sigmoid(g[:, LANES:] + bx)
    log_a = (-LRU_C * r) * sp
    a = jnp.exp(log_a)
    m2 = 1.0 - jnp.exp(2.0 * log_a)
    mult = jnp.where(m2 > 0.0, m2 * lax.rsqrt(m2), 0.0)
    return a, (mult * i) * uc


def _lru_kernel(*refs, n_side):
    u_ref, cw_ref, cb_ref, wg_ref, ba_ref, bx_ref, lam_ref = refs[:7]
    hs_ref, nconv_ref, nlru_ref = refs[7 + n_side:10 + n_side]
    ext, a_s, b_s, carry = refs[-4:]
    _side_cast(refs[7:7 + n_side], refs[10 + n_side:10 + 2 * n_side])
    t = pl.program_id(0)
    nt = pl.num_programs(0)
    nbat, tl, w = u_ref.shape
    nblk = w // LANES
    rows = nbat * tl
    kw = cw_ref.shape[0]
    hist = ext.shape[1] - rows
    steps_per_tile = SUBLANES // nbat

    @pl.when(t == 0)
    def _():
        ext[:, 0:hist, :] = jnp.zeros((nblk, hist, LANES), F32)
        carry[...] = jnp.zeros_like(carry)

    for b in range(nbat):
        for c in range(nblk):
            ext[c, pl.ds(hist + b, tl, stride=nbat), :] = u_ref[b, :, c * LANES:(c + 1) * LANES]

        @pl.when(t == nt - 1)
        def _(b=b):
            nconv_ref[b] = u_ref[b, tl - (kw - 1):tl, :]

    sp = jax.nn.softplus(-lam_ref[...])
    for c in range(nblk):
        cols = slice(c * LANES, (c + 1) * LANES)
        uc = cb_ref[:, cols]
        for k in range(kw):
            start = hist - (kw - 1 - k) * nbat
            uc = uc + ext[c, start:start + rows, :] * cw_ref[k:k + 1, cols]
        g = _dot(uc.astype(BF16), wg_ref[c])
        a, b = _lru_gate_block(uc, g, ba_ref[:, cols], bx_ref[:, cols], sp[:, cols])
        a_s[:, cols] = a
        b_s[:, cols] = b
        ext[c, hist - (kw - 1) * nbat:hist, :] = ext[c, hist + rows - (kw - 1) * nbat:hist + rows, :]

    def tile_step(i, hcur):
        r0 = pl.multiple_of(i * SUBLANES, SUBLANES)
        at, bt = a_s[pl.ds(r0, SUBLANES), :], b_s[pl.ds(r0, SUBLANES), :]
        hs = []
        for k in range(steps_per_tile):
            hcur = at[k * nbat:(k + 1) * nbat] * hcur + bt[k * nbat:(k + 1) * nbat]
            hs.append(hcur)
        htile = jnp.concatenate(hs, axis=0)
        for c in range(nblk):
            ext[c, pl.ds(pl.multiple_of(hist + r0, SUBLANES), SUBLANES), :] = htile[:, c * LANES:(c + 1) * LANES]
        return hcur

    h_end = lax.fori_loop(0, tl // steps_per_tile, tile_step, carry[...], unroll=2)
    carry[...] = h_end

    for b in range(nbat):
        for c in range(nblk):
            hs_ref[b, :, c * LANES:(c + 1) * LANES] = ext[c, pl.ds(hist + b, tl, stride=nbat), :].astype(BF16)

    @pl.when(t == nt - 1)
    def _():
        for b in range(nbat):
            nlru_ref[b] = h_end[b:b + 1, :]


def _lru_call(u3d, cw, cb, wg, ba, bx, lam, *, tl, side=()):
    bsz, seq, _ = u3d.shape
    w = cw.shape[1]
    kw = cw.shape[0]
    nblk = w // LANES
    assert SUBLANES % bsz == 0
    rows = bsz * tl
    hist = -(-(kw - 1) * bsz // SUBLANES) * SUBLANES
    side_in, side_out, side_shapes, side_bytes = _side_cast_specs(side, seq // tl)
    est = (2 * rows * w * 4 + 2 * rows * w * 2 + (hist + rows) * w * 4 + 2 * rows * w * 4
           + nblk * LANES * 2 * LANES * 2 * 2 + 2 * rows * 2 * LANES * 4 + side_bytes)
    const2 = lambda t: (0, 0)
    return pl.pallas_call(
        functools.partial(_lru_kernel, n_side=len(side)),
        grid=(seq // tl,),
        in_specs=[pl.BlockSpec((bsz, tl, w), lambda t: (0, t, 0)),
                  pl.BlockSpec((kw, w), const2),
                  pl.BlockSpec((1, w), const2),
                  pl.BlockSpec((nblk, LANES, 2 * LANES), lambda t: (0, 0, 0)),
                  pl.BlockSpec((1, w), const2),
                  pl.BlockSpec((1, w), const2),
                  pl.BlockSpec((1, w), const2)] + side_in,
        out_specs=[pl.BlockSpec((bsz, tl, w), lambda t: (0, t, 0)),
                   pl.BlockSpec((bsz, kw - 1, w), lambda t: (0, 0, 0)),
                   pl.BlockSpec((bsz, 1, w), lambda t: (0, 0, 0))] + side_out,
        out_shape=[jax.ShapeDtypeStruct((bsz, seq, w), BF16),
                   jax.ShapeDtypeStruct((bsz, kw - 1, w), F32),
                   jax.ShapeDtypeStruct((bsz, 1, w), F32)] + side_shapes,
        scratch_shapes=[pltpu.VMEM((nblk, hist + rows, LANES), F32),
                        pltpu.VMEM((rows, w), F32),
                        pltpu.VMEM((rows, w), F32),
                        pltpu.VMEM((bsz, w), F32)],
        compiler_params=pltpu.CompilerParams(dimension_semantics=("arbitrary",),
                                             vmem_limit_bytes=_vmem_limit(est)),
        name="lru_scan",
    )(u3d, cw, cb, wg, ba, bx, lam, *(a for a, _ in side))


def _lru_step_kernel(u_ref, buf_ref, h0_ref, cw_ref, cb_ref, wg_ref, ba_ref, bx_ref, lam_ref,
                     hs_ref, nconv_ref, nlru_ref):
    w = cw_ref.shape[1]
    kw = cw_ref.shape[0]
    nblk = w // LANES
    u = u_ref[...]
    uc = cb_ref[...] + buf_ref[0] * cw_ref[0:1, :]
    for k in range(1, kw - 1):
        uc = uc + buf_ref[k] * cw_ref[k:k + 1, :]
    uc = uc + u * cw_ref[kw - 1:kw, :]
    for k in range(kw - 2):
        nconv_ref[k] = buf_ref[k + 1]
    nconv_ref[kw - 2] = u
    sp = jax.nn.softplus(-lam_ref[...])
    for h in range(nblk):
        cols = slice(h * LANES, (h + 1) * LANES)
        uch = uc[:, cols]
        g = _dot(uch.astype(BF16), wg_ref[h])
        a, b = _lru_gate_block(uch, g, ba_ref[:, cols], bx_ref[:, cols], sp[:, cols])
        hn = a * h0_ref[:, cols] + b
        nlru_ref[:, cols] = hn
        hs_ref[:, cols] = hn.astype(BF16)


def _lru_step_call(u2d, buf, h0, cw, cb, wg, ba, bx, lam):
    n = h0.shape[0]
    w = cw.shape[1]
    kw = cw.shape[0]
    nblk = w // LANES
    c2 = lambda i: (0, 0)
    c3 = lambda i: (0, 0, 0)
    return pl.pallas_call(
        _lru_step_kernel,
        grid=(1,),
        in_specs=[pl.BlockSpec((n, w), c2),
                  pl.BlockSpec((kw - 1, n, w), c3),
                  pl.BlockSpec((n, w), c2),
                  pl.BlockSpec((kw, w), c2),
                  pl.BlockSpec((1, w), c2),
                  pl.BlockSpec((nblk, LANES, 2 * LANES), c3),
                  pl.BlockSpec((1, w), c2),
                  pl.BlockSpec((1, w), c2),
                  pl.BlockSpec((1, w), c2)],
        out_specs=[pl.BlockSpec((n, w), c2),
                   pl.BlockSpec((kw - 1, n, w), c3),
                   pl.BlockSpec((n, w), c2)],
        out_shape=[jax.ShapeDtypeStruct((n, w), BF16),
                   jax.ShapeDtypeStruct((kw - 1, n, w), F32),
                   jax.ShapeDtypeStruct((n, w), F32)],
        name="lru_step",
    )(u2d, buf, h0, cw, cb, wg, ba, bx, lam)


def _abar(ar, ai, dt):
    mag = jnp.exp(dt * ar)
    ang = dt * ai
    return mag * jnp.cos(ang), mag * jnp.sin(ang)


def _s5_prep_kernel(ax_ref, dtx_ref, bt_ref, ct_ref, al_ref, dtl_ref, bb_ref, cc_ref, pw_ref):
    ar, ai = ax_ref[0], ax_ref[1]
    abr, abi = _abar(ar, ai, jnp.exp(dtx_ref[...]))
    den = ar * ar + ai * ai
    nr, ni = abr - 1.0, abi
    q_re = (nr * ar + ni * ai) / den
    q_im = (ni * ar - nr * ai) / den
    br, bi = bt_ref[0], bt_ref[1]
    for ref, pair in ((bb_ref, (q_re * br - q_im * bi, q_re * bi + q_im * br)), (cc_ref, (ct_ref[0], ct_ref[1]))):
        for part, v in enumerate(pair):
            ref[part] = jnp.concatenate([v, v], axis=1)

    pr, pi = _abar(al_ref[0], al_ref[1], jnp.exp(dtl_ref[...]))
    qr, qi = jnp.ones_like(pr), jnp.zeros_like(pr)
    for m in range(pw_ref.shape[1]):
        pw_ref[0, m:m + 1, :] = qr
        pw_ref[1, m:m + 1, :] = qi
        qr, qi = qr * pr - qi * pi, qr * pi + qi * pr


def _s5_prep_call(a_re, a_im, log_dt, b_re, b_im, c_re, c_im, n_pow):
    g, p, k = b_re.shape
    assert c_re.shape == (g, k, p) and 2 * p == LANES
    a = jnp.stack([a_re, a_im])
    return pl.pallas_call(
        _s5_prep_kernel,
        out_shape=[jax.ShapeDtypeStruct((2, g * k, 2 * p), F32), jax.ShapeDtypeStruct((2, g * k, 2 * p), F32),
                   jax.ShapeDtypeStruct((2, n_pow, g * p), F32)],
        name="s5_prep",
    )(jnp.repeat(a, k, axis=1), jnp.repeat(log_dt, k).reshape(g * k, 1),
      jnp.stack([b_re, b_im]).swapaxes(2, 3).reshape(2, g * k, p), jnp.stack([c_re, c_im]).reshape(2, g * k, p),
      a.reshape(2, 1, g * p), jnp.repeat(log_dt, p).reshape(1, g * p))


def _block_diag(stack):
    tiled = jnp.concatenate([stack] * (STATES_PER_BLOCK // stack.shape[1]), axis=1)
    own = (lax.broadcasted_iota(jnp.int32, tiled.shape, 0) // SSM_GROUP
           == lax.broadcasted_iota(jnp.int32, tiled.shape, 1) // SSM_STATE)
    return jnp.where(own, tiled, 0.0)


def _cplx_mul(ar, ai, br, bi):
    return ar * br - ai * bi, ar * bi + ai * br


def _cplx_step(ar, ai, hr, hi, br, bi):
    return ar * hr - ai * hi + br, ar * hi + ai * hr + bi


def _s5_kernel(u_ref, bb_ref, cc_ref, pw_ref, d_ref, c_ref, wa_ref, ba_ref,
               gy_ref, nre_ref, nim_ref, ms_ref, mp_ref,
               ufb, buf, wbs, wqs, wts, yslab, out32):
    _mod_kernel(c_ref, wa_ref, ba_ref, ms_ref, mp_ref)
    nbat, seq, _ = u_ref.shape
    spb = STATES_PER_BLOCK
    fold = wbs.shape[0] // LANES
    nq = seq // fold

    bre, bim = _block_diag(bb_ref[0]), _block_diag(bb_ref[1])
    ctre, ctim = _block_diag(cc_ref[0]), _block_diag(cc_ref[1])
    wcb = jnp.concatenate([ctre.T, -ctim.T], axis=0).astype(BF16)
    lag = []
    for m in range(fold):
        wre, wim = _cplx_mul(bre, bim, pw_ref[0, m:m + 1, :], pw_ref[1, m:m + 1, :])
        rows = slice((fold - 1 - m) * LANES, (fold - m) * LANES)
        wbs[rows, :spb] = wre.astype(BF16)
        wbs[rows, spb:] = wim.astype(BF16)
        lag.append(_dot(wbs[rows, :], wcb))
    zero_blk = jnp.zeros((LANES, LANES), BF16)
    for r_in in range(fold):
        for r_out in range(fold):
            blk = lag[r_out - r_in].astype(BF16) if r_in <= r_out else zero_blk
            wts[r_in * LANES:(r_in + 1) * LANES, r_out * LANES:(r_out + 1) * LANES] = blk
    for r in range(fold):
        pr, pi = pw_ref[0, r + 1:r + 2, :], pw_ref[1, r + 1:r + 2, :]
        cols = slice(r * LANES, (r + 1) * LANES)
        wqs[:spb, cols] = (ctre * pr - ctim * pi).T.astype(BF16)
        wqs[spb:, cols] = (-(ctre * pi + ctim * pr)).T.astype(BF16)

    for r in range(fold):
        for b in range(nbat):
            yslab[r, pl.ds(b, nq, stride=nbat), :] = u_ref[b, pl.ds(r, nq, stride=fold), :]
        ufb[:, r * LANES:(r + 1) * LANES] = yslab[r].astype(BF16)

    buf[...] = _dot(ufb[...], wbs[...])

    ar, ai = pw_ref[0, fold:fold + 1, :], pw_ref[1, fold:fold + 1, :]
    groups_per_tile = SUBLANES // nbat
    z = jnp.zeros((nbat, spb), F32)

    def tile_step(i, g):
        gr, gi = g
        r0 = pl.multiple_of(i * SUBLANES, SUBLANES)
        xr, xi = buf[pl.ds(r0, SUBLANES), :spb], buf[pl.ds(r0, SUBLANES), spb:]
        hr, hi = [], []
        for k in range(groups_per_tile):
            hr.append(gr)
            hi.append(gi)
            gr, gi = _cplx_step(ar, ai, gr, gi, xr[k * nbat:(k + 1) * nbat], xi[k * nbat:(k + 1) * nbat])
        buf[pl.ds(r0, SUBLANES), :spb] = jnp.concatenate(hr, axis=0)
        buf[pl.ds(r0, SUBLANES), spb:] = jnp.concatenate(hi, axis=0)
        return gr, gi

    g_re, g_im = lax.fori_loop(0, nq // groups_per_tile, tile_step, (z, z), unroll=2)
    for b in range(nbat):
        nre_ref[b] = g_re[b:b + 1, :]
        nim_ref[b] = g_im[b:b + 1, :]

    hsb = buf[...].astype(BF16)
    per_tile = 2
    for n in range(fold // per_tile):
        cols = slice(n * per_tile * LANES, (n + 1) * per_tile * LANES)
        kk = (n + 1) * per_tile * LANES
        yn = _dot(hsb, wqs[:, cols]) + _dot(ufb[:, :kk], wts[:kk, cols])
        for i in range(per_tile):
            yslab[n * per_tile + i] = yn[:, i * LANES:(i + 1) * LANES]

    for b in range(nbat):
        for r in range(fold):
            ys = (yslab[r, pl.ds(b, nq, stride=nbat), :]
                  + d_ref[...] * u_ref[b, pl.ds(r, nq, stride=fold), :])
            out32[b, pl.ds(r, nq, stride=fold), :] = jax.nn.gelu(ys)
        gy_ref[b] = out32[b].astype(BF16)


def _s5_call(u3d, bb, cc, pw, d, c_all, n_sample, w_ada, b_ada, first_col, *, w):
    bsz, seq, _ = u3d.shape
    nblk = w // LANES
    spb = STATES_PER_BLOCK
    fold = S5_FOLD
    assert SUBLANES % bsz == 0 and seq % fold == 0 and pw.shape[1] == fold + 1
    rows = bsz * (seq // fold)
    n_c, dm = c_all.shape
    mod_cols = w_ada.shape[1] - first_col
    tn = mod_cols // nblk
    assert mod_cols % nblk == 0 and tn % LANES == 0 and first_col % tn == 0
    est = (2 * bsz * seq * LANES * 4 + 2 * bsz * seq * LANES * 2
           + 2 * dm * tn * 4 + dm * tn * 2 + 2 * n_c * dm * 4 + 3 * n_c * tn * 4
           + rows * fold * LANES * 2 + 2 * rows * 2 * spb * 4
           + 3 * fold * LANES * 2 * spb * 2
           + rows * fold * LANES * 4 + bsz * seq * LANES * 4
           + rows * 2 * spb * 2
           + 2 * (LANES * 2 * spb + 2 * spb * LANES) * 4)
    return pl.pallas_call(
        _s5_kernel,
        grid=(nblk,),
        in_specs=[pl.BlockSpec((bsz, seq, LANES), lambda c: (0, 0, nblk + c)),
                  pl.BlockSpec((2, LANES, LANES), lambda c: (0, c, 0)),
                  pl.BlockSpec((2, LANES, LANES), lambda c: (0, c, 0)),
                  pl.BlockSpec((2, fold + 1, spb), lambda c: (0, 0, c)),
                  pl.BlockSpec((1, LANES), lambda c: (0, c)),
                  pl.BlockSpec((n_c, dm), lambda c: (0, 0)),
                  pl.BlockSpec((dm, tn), lambda c: (0, first_col // tn + c)),
                  pl.BlockSpec((1, tn), lambda c: (0, first_col // tn + c))],
        out_specs=[pl.BlockSpec((bsz, seq, LANES), lambda c: (0, 0, c)),
                   pl.BlockSpec((bsz, 1, spb), lambda c: (0, 0, c)),
                   pl.BlockSpec((bsz, 1, spb), lambda c: (0, 0, c)),
                   pl.BlockSpec((n_sample, tn), lambda c: (0, c)),
                   pl.BlockSpec((n_c - n_sample, tn), lambda c: (0, c))],
        out_shape=[jax.ShapeDtypeStruct((bsz, seq, w), BF16),
                   jax.ShapeDtypeStruct((bsz, 1, nblk * spb), F32),
                   jax.ShapeDtypeStruct((bsz, 1, nblk * spb), F32),
                   jax.ShapeDtypeStruct((n_sample, mod_cols), F32),
                   jax.ShapeDtypeStruct((n_c - n_sample, mod_cols), F32)],
        scratch_shapes=[pltpu.VMEM((rows, fold * LANES), BF16),
                        pltpu.VMEM((rows, 2 * spb), F32),
                        pltpu.VMEM((fold * LANES, 2 * spb), BF16),
                        pltpu.VMEM((2 * spb, fold * LANES), BF16),
                        pltpu.VMEM((fold * LANES, fold * LANES), BF16),
                        pltpu.VMEM((fold, rows, LANES), F32),
                        pltpu.VMEM((bsz, seq, LANES), F32)],
        compiler_params=pltpu.CompilerParams(dimension_semantics=("parallel",),
                                             vmem_limit_bytes=_vmem_limit(est)),
        name="s5_scan",
    )(u3d, bb, cc, pw, d, c_all, w_ada, b_ada)


def _s5_step_kernel(u_ref, h0r_ref, h0i_ref, bb_ref, cc_ref, pw_ref, d_ref, gy_ref, nre_ref, nim_ref):
    n = u_ref.shape[0]
    spb = h0r_ref.shape[0]
    u = u_ref[...]
    ut = u.T.astype(BF16)
    ar = jnp.broadcast_to(pw_ref[0, 1:2, :], (LANES, spb)).T
    ai = jnp.broadcast_to(pw_ref[1, 1:2, :], (LANES, spb)).T
    bu_re = _dot(_block_diag(bb_ref[0]).T.astype(BF16), ut)
    bu_im = _dot(_block_diag(bb_ref[1]).T.astype(BF16), ut)
    if n != LANES:
        ar, ai = _lanes(ar, n), _lanes(ai, n)
    hr, hi = _cplx_step(ar, ai, h0r_ref[...], h0i_ref[...], bu_re, bu_im)
    nre_ref[...] = hr
    nim_ref[...] = hi
    yt = (_dot(_block_diag(cc_ref[0]).astype(BF16), hr.astype(BF16))
          - _dot(_block_diag(cc_ref[1]).astype(BF16), hi.astype(BF16)))
    gy_ref[...] = jax.nn.gelu(yt.T + d_ref[...] * u).astype(BF16)


def _s5_step_call(u2d, h0r, h0i, bb, cc, pw, d, *, w):
    n_state, n = h0r.shape
    spb = STATES_PER_BLOCK
    nblk = w // LANES
    assert n_state == nblk * spb and n % LANES == 0
    state_spec = pl.BlockSpec((spb, n), lambda c: (c, 0))
    state_shape = jax.ShapeDtypeStruct((n_state, n), F32)
    return pl.pallas_call(
        _s5_step_kernel,
        grid=(nblk,),
        in_specs=[pl.BlockSpec((n, LANES), lambda c: (0, nblk + c)),
                  state_spec,
                  state_spec,
                  pl.BlockSpec((2, LANES, LANES), lambda c: (0, c, 0)),
                  pl.BlockSpec((2, LANES, LANES), lambda c: (0, c, 0)),
                  pl.BlockSpec((2, pw.shape[1], spb), lambda c: (0, 0, c)),
                  pl.BlockSpec((1, LANES), lambda c: (0, c))],
        out_specs=[pl.BlockSpec((n, LANES), lambda c: (0, c)), state_spec, state_spec],
        out_shape=[jax.ShapeDtypeStruct((n, w), BF16), state_shape, state_shape],
        compiler_params=pltpu.CompilerParams(dimension_semantics=("parallel",)),
        name="s5_step",
    )(u2d, h0r, h0i, bb, cc, pw, d)


def _merge_kernel(*refs, n_side):
    hs_ref, gy_ref, z_ref, x_ref, gt_ref, wp_ref, wg_ref, wo_ref = refs[:8]
    o_ref = refs[8 + n_side]
    _side_cast(refs[8:8 + n_side], refs[9 + n_side:])
    d = x_ref.shape[1]
    ya = _dot(hs_ref[...], wp_ref[...])
    merged = _sigmoid(z_ref[:, :d].astype(F32)) * ya
    glu = _dot(gy_ref[...], wg_ref[...])
    yb = glu[:, :d] * _sigmoid(glu[:, d:])
    merged = merged + _sigmoid(z_ref[:, d:].astype(F32)) * yb
    o = _dot(merged.astype(BF16), wo_ref[...])
    o_ref[...] = x_ref[...] + _rows(gt_ref) * o


def _merge_call(hs, gy, sz, x2d, mod, wp, wg, wo, *, tm, tiles_per_batch, side=()):
    m, d = x2d.shape
    w = hs.shape[1]
    side_in, side_out, side_shapes, side_bytes = _side_cast_specs(side, m // tm)
    est = (2 * (2 * tm * w * 2 + tm * 2 * d * 2 + 2 * tm * d * 4) + (wp.size + wg.size + wo.size) * 2
           + tm * d * 4 * 5 + side_bytes)
    one = pl.Buffered(1)
    return pl.pallas_call(
        functools.partial(_merge_kernel, n_side=len(side)),
        grid=(m // tm,),
        in_specs=[pl.BlockSpec((tm, w), lambda i: (i, 0)),
                  pl.BlockSpec((tm, w), lambda i: (i, 0)),
                  pl.BlockSpec((tm, 2 * d), lambda i: (i, 0)),
                  pl.BlockSpec((tm, d), lambda i: (i, 0)),
                  _mod_spec(mod, 2, d, tiles_per_batch, 1),
                  pl.BlockSpec(wp.shape, lambda i: (0, 0), pipeline_mode=one),
                  pl.BlockSpec(wg.shape, lambda i: (0, 0), pipeline_mode=one),
                  pl.BlockSpec(wo.shape, lambda i: (0, 0), pipeline_mode=one)] + side_in,
        out_specs=[pl.BlockSpec((tm, d), lambda i: (i, 0))] + side_out,
        out_shape=[jax.ShapeDtypeStruct((m, d), F32)] + side_shapes,
        compiler_params=pltpu.CompilerParams(dimension_semantics=("parallel",),
                                             vmem_limit_bytes=_vmem_limit(est)),
        name="merge",
    )(hs, gy, sz, x2d, mod, wp, wg, wo, *(a for a, _ in side))


def _mlp_kernel(x_ref, sc_ref, sh_ref, gt_ref, g2_ref, gf_ref, wu_ref, wd_ref, o_ref, h_scr, inv_scr, row_scr, *,
                final_norm):
    j = pl.program_id(1)
    nj = pl.num_programs(1)
    n_rows, d = x_ref.shape

    @pl.when(j == 0)
    def _():
        _norm_mod_bf16(x_ref, g2_ref, sc_ref, sh_ref, h_scr, inv_scr, row_scr)
        o_ref[...] = jnp.zeros(o_ref.shape, F32)

    up = _dot(h_scr[...], wu_ref[...])
    act = jnp.square(jnp.maximum(up, 0.0)).astype(BF16)
    nc = min(d, MLP_OUT_CHUNK)
    for c in range(d // nc):
        cols = slice(c * nc, (c + 1) * nc)
        o_ref[:, cols] += _dot(act, wd_ref[:, cols])

    @pl.when(j == nj - 1)
    def _():
        def chunk(rows):
            x2 = x_ref[rows, :] + _rows(gt_ref, rows) * o_ref[rows, :]
            if final_norm:
                ms = jnp.mean(x2 * x2, axis=-1, keepdims=True)
                x2 = (x2 * lax.rsqrt(ms + EPS)) * gf_ref[...]
            o_ref[rows, :] = x2

        _row_groups(n_rows, min(n_rows, EPILOGUE_ROWS), chunk, unroll=1)


def _mlp_call(x2d, mod, g2, gf, w_up, w_down, *, tm, tiles_per_batch, final_norm, tf):
    m, d = x2d.shape
    dff = w_up.shape[1]
    assert w_up.dtype == BF16 and w_down.dtype == BF16
    est = 3 * tm * d * 4 + tm * d * 2 + 4 * d * tf * 2 + tm * tf * 6 + tm * MLP_OUT_CHUNK * 4
    return pl.pallas_call(
        functools.partial(_mlp_kernel, final_norm=final_norm),
        grid=(m // tm, dff // tf),
        in_specs=[pl.BlockSpec((tm, d), lambda i, j: (i, 0), pipeline_mode=pl.Buffered(1)),
                  _mod_spec(mod, 4 - MLP_MOD_FIRST, d, tiles_per_batch, 2),
                  _mod_spec(mod, 3 - MLP_MOD_FIRST, d, tiles_per_batch, 2),
                  _mod_spec(mod, 5 - MLP_MOD_FIRST, d, tiles_per_batch, 2),
                  pl.BlockSpec((1, d), lambda i, j: (0, 0)),
                  pl.BlockSpec((1, d), lambda i, j: (0, 0)),
                  pl.BlockSpec((d, tf), lambda i, j: (0, j)),
                  pl.BlockSpec((tf, d), lambda i, j: (j, 0))],
        out_specs=pl.BlockSpec((tm, d), lambda i, j: (i, 0)),
        out_shape=jax.ShapeDtypeStruct((m, d), F32),
        scratch_shapes=[pltpu.VMEM((tm, d), BF16), pltpu.VMEM((tm, LANES), F32),
                        pltpu.VMEM((2 * SUBLANES, d), F32)],
        compiler_params=pltpu.CompilerParams(dimension_semantics=("parallel", "arbitrary"),
                                             vmem_limit_bytes=_vmem_limit(est)),
        name="mlp",
    )(x2d, mod, mod, mod, g2, gf, w_up, w_down)


def _pick_tile(n, pref):
    t = min(n, pref)
    assert n % t == 0, (n, t)
    return t


def kernel(x_prompt, x_sample, state_conv, state_lru, state_ssm_re, state_ssm_im, c_prompt, c_sample, w_ada, b_ada, g_norm1, g_norm2, w_in, conv_w, conv_b, w_rg_a, b_rg_a, w_rg_x, b_rg_x, lru_lambda, w_proj_a, ssm_a_re, ssm_a_im, ssm_log_dt, ssm_b_re, ssm_b_im, ssm_c_re, ssm_c_im, ssm_d, w_glu, w_out, w_up, w_down, g_final):
    depth = w_ada.shape[0]
    nb, seq, d = x_prompt.shape
    ns = x_sample.shape[0]
    assert x_sample.shape[1] == 1
    w = conv_w.shape[2]
    n_state = ssm_a_re.shape[1] * ssm_a_re.shape[2]
    assert ssm_b_re.shape[2:] == (SSM_STATE, SSM_GROUP) and w % LANES == 0
    assert w_in.shape[2] == 2 * w + 2 * d

    tl = _pick_tile(seq, 256)
    tm_in = _pick_tile(seq, 1024)
    tm_mg = _pick_tile(seq, 256)
    tm_mlp = _pick_tile(seq, 1024)

    xp = x_prompt.reshape(nb * seq, d)
    xs = x_sample.reshape(ns, d)
    assert ns % SUBLANES == 0
    pad = (-nb) % SUBLANES
    c_all = jnp.concatenate([c_sample, c_prompt, jnp.zeros((pad, d), F32)], axis=0)

    outs_p = [[] for _ in range(4)]
    outs_s = [[] for _ in range(4)]
    for l in range(depth):
        last = l == depth - 1
        b_ada_row = b_ada[l].reshape(1, -1)
        mod_s, mod_p = _mod_call(c_all, ns, w_ada[l], b_ada_row, MLP_MOD_FIRST * d)
        mod_p = mod_p.reshape(nb + pad, 1, MLP_MOD_FIRST * d)

        g1 = g_norm1[l].reshape(1, d)
        g2 = g_norm2[l].reshape(1, d)
        gf = g_final.reshape(1, d)
        cw, cb = conv_w[l], conv_b[l].reshape(1, w)
        wg = jnp.concatenate([w_rg_a[l], w_rg_x[l]], axis=2).astype(BF16)
        ba, bx, lam = b_rg_a[l].reshape(1, w), b_rg_x[l].reshape(1, w), lru_lambda[l].reshape(1, w)
        bb, cc, pw = _s5_prep_call(ssm_a_re[l], ssm_a_im[l], ssm_log_dt[l], ssm_b_re[l], ssm_b_im[l],
                                       ssm_c_re[l], ssm_c_im[l], S5_FOLD + 1)
        dskip = ssm_d[l].reshape(1, w)

        u_s, sz_s, w_in_b = _inproj_call(xs, mod_s, g1, w_in[l], tm=ns, tiles_per_batch=1, w_mix=w)

        u_p, sz_p = _inproj_call(xp, mod_p, g1, w_in_b, tm=tm_in, tiles_per_batch=seq // tm_in, w_mix=w)
        u3 = u_p.reshape(nb, seq, 2 * w)
        hs_p, nconv_p, nlru_p, wp, wgl, wo = _lru_call(u3, cw, cb, wg, ba, bx, lam, tl=tl,
                                                       side=((w_proj_a[l], 1), (w_glu[l], 1), (w_out[l], 1)))
        gy_p, nre_p, nim_p, mlp_mod_s, mlp_mod_p = _s5_call(u3, bb, cc, pw, dskip, c_all, ns,
                                                            w_ada[l], b_ada_row, MLP_MOD_FIRST * d, w=w)
        mlp_mod_p = mlp_mod_p.reshape(nb + pad, 1, (N_MOD - MLP_MOD_FIRST) * d)
        x1_p, w_up_b, w_down_b = _merge_call(hs_p.reshape(nb * seq, w), gy_p.reshape(nb * seq, w), sz_p, xp, mod_p,
                                             wp, wgl, wo, tm=tm_mg, tiles_per_batch=seq // tm_mg,
                                             side=((w_up[l], 1), (w_down[l], 0)))
        xp = _mlp_call(x1_p, mlp_mod_p, g2, gf, w_up_b, w_down_b, tm=tm_mlp, tiles_per_batch=seq // tm_mlp,
                       final_norm=last, tf=1024)

        buf = jnp.transpose(state_conv[l], (1, 0, 2))
        hs_s, nconv_s, nlru_s = _lru_step_call(u_s, buf, state_lru[l], cw, cb, wg, ba, bx, lam)
        by_token = lambda s: jnp.transpose(s, (1, 2, 0)).reshape(n_state, ns)
        gy_s, nre_s, nim_s = _s5_step_call(u_s, by_token(state_ssm_re[l]), by_token(state_ssm_im[l]), bb, cc, pw,
                                           dskip, w=w)
        (x1_s,) = _merge_call(hs_s, gy_s, sz_s, xs, mod_s, wp, wgl, wo, tm=ns, tiles_per_batch=1)
        xs = _mlp_call(x1_s, mlp_mod_s, g2, gf, w_up_b, w_down_b, tm=ns, tiles_per_batch=1, final_norm=last,
                       tf=1024)

        gshape = ssm_a_re.shape[1:]
        for acc, v in zip(outs_p, (nconv_p, nlru_p.reshape(nb, w),
                                   nre_p.reshape((nb,) + gshape), nim_p.reshape((nb,) + gshape))):
            acc.append(v)
        by_group = lambda s: jnp.transpose(s.reshape(gshape + (ns,)), (2, 0, 1))
        for acc, v in zip(outs_s, (jnp.transpose(nconv_s, (1, 0, 2)), nlru_s, by_group(nre_s), by_group(nim_s))):
            acc.append(v)

    y_prompt = xp.reshape(nb, seq, d)
    y_sample = xs.reshape(ns, 1, d)
    stack = lambda a: a[0][None] if len(a) == 1 else jnp.stack(a)
    return (y_prompt, y_sample) + tuple(stack(a) for a in outs_p) + tuple(stack(a) for a in outs_s)
```

```python
import functools

import jax
import jax.numpy as jnp
from jax import lax
from jax.experimental import pallas as pl
from jax.experimental.pallas import tpu as pltpu

F32 = jnp.float32
BF16 = jnp.bfloat16

LANES = 128
SUBLANES = 8
BF16_ROWS = 2 * SUBLANES
VMEM_PHYSICAL_BYTES = 64 * 1024 * 1024
VMEM_LIMIT_CAP_BYTES = VMEM_PHYSICAL_BYTES - 6 * 1024 * 1024
VMEM_SLACK_BYTES = 8 * 1024 * 1024

LRU_C = 8.0
EPS = 1e-6
N_MOD = 6
MLP_MOD_FIRST = 3
SSM_GROUP = 16
SSM_STATE = 64
GROUPS_PER_BLOCK = LANES // SSM_GROUP
STATES_PER_BLOCK = GROUPS_PER_BLOCK * SSM_STATE

EPILOGUE_ROWS = 256
MLP_OUT_CHUNK = 512
INPROJ_CHUNK = 256
S5_FOLD = 8


def _vmem_limit(nbytes):
    return int(min(VMEM_LIMIT_CAP_BYTES, max(32 * 1024 * 1024, nbytes + VMEM_SLACK_BYTES)))


def _rows(ref, rows=None):
    if len(ref.shape) == 3:
        return ref[0]
    return ref[...] if rows is None else ref[rows, :]


def _row_groups(n_rows, group, fn, unroll):
    assert n_rows % group == 0

    def body(i, carry):
        fn(pl.ds(pl.multiple_of(i * group, group), group))
        return carry

    lax.fori_loop(0, n_rows // group, body, 0, unroll=unroll)


def _rms_scale_pass(src, inv_scr, d):
    def fn(rows):
        x = src(rows)
        xx = x * x
        part = xx[:, 0:LANES]
        for c in range(1, d // LANES):
            part = part + xx[:, c * LANES:(c + 1) * LANES]
        inv_scr[rows, :] = part

    _row_groups(inv_scr.shape[0], SUBLANES, fn, unroll=8)
    ss = jnp.sum(inv_scr[...], axis=-1, keepdims=True)
    inv_scr[...] = jnp.broadcast_to(lax.rsqrt(ss * (1.0 / d) + EPS), inv_scr.shape)


def _lanes(inv, d):
    return jnp.concatenate([inv] * (d // LANES), axis=1)


def _sublane_rows(scr, slot, row=None):
    rows = slice(slot * SUBLANES, (slot + 1) * SUBLANES)
    if row is None:
        return scr[rows, :]
    scr[rows, :] = jnp.broadcast_to(row, (SUBLANES, row.shape[1]))
    return None


def _norm_mod_bf16(x_ref, g_ref, sc_ref, sh_ref, h_scr, inv_scr, row_scr, x_copy_ref=None):
    n_rows, d = x_ref.shape
    _rms_scale_pass(lambda rows: x_ref[rows, :], inv_scr, d)
    per_token = len(sc_ref.shape) == 2
    if not per_token:
        _sublane_rows(row_scr, 0, g_ref[...] * (1.0 + sc_ref[0]))
        _sublane_rows(row_scr, 1, sh_ref[0])

    def fn(rows):
        halves = []
        for k in range(BF16_ROWS // SUBLANES):
            r8 = pl.ds(pl.multiple_of(rows.start + k * SUBLANES, SUBLANES), SUBLANES)
            gm = g_ref[...] * (1.0 + sc_ref[r8, :]) if per_token else _sublane_rows(row_scr, 0)
            sh = sh_ref[r8, :] if per_token else _sublane_rows(row_scr, 1)
            x = x_ref[r8, :]
            if x_copy_ref is not None:
                x_copy_ref[r8, :] = x
            halves.append((x * _lanes(inv_scr[r8, :], d)) * gm + sh)
        h_scr[rows, :] = jnp.concatenate(halves, axis=0).astype(BF16)

    _row_groups(n_rows, BF16_ROWS, fn, unroll=2)


def _dot(a, b):
    return jnp.dot(a, b, preferred_element_type=F32)


def _sigmoid(x):
    return 0.5 * jnp.tanh(0.5 * x) + 0.5


def _side_cast_specs(side, n_steps):
    in_specs, out_specs, out_shapes, nbytes = [], [], [], 0
    for arr, axis in side:
        blk = tuple(s // n_steps if a == axis else s for a, s in enumerate(arr.shape))
        assert arr.ndim == 2 and arr.shape[axis] % n_steps == 0 and blk[1] % LANES == 0 and blk[0] % BF16_ROWS == 0
        idx = (lambda i: (i, 0)) if axis == 0 else (lambda i: (0, i))
        in_specs.append(pl.BlockSpec(blk, idx))
        out_specs.append(pl.BlockSpec(blk, idx))
        out_shapes.append(jax.ShapeDtypeStruct(arr.shape, BF16))
        nbytes += 2 * blk[0] * blk[1] * (4 + 2)
    return in_specs, out_specs, out_shapes, nbytes


def _side_cast(src_refs, dst_refs):
    for src, dst in zip(src_refs, dst_refs, strict=True):
        dst[...] = src[...].astype(BF16)


def _mod_kernel(c_ref, w_ref, b_ref, os_ref, op_ref):
    c = c_ref[...]
    cs = (c * _sigmoid(c)).astype(BF16)
    mod = _dot(cs, w_ref[...].astype(BF16)) + b_ref[...]
    n_sample = os_ref.shape[0]
    os_ref[...] = mod[:n_sample]
    op_ref[...] = mod[n_sample:]


def _mod_call(c_all, n_sample, w_ada, b_ada, n, tn=1024):
    n_rows, d = c_all.shape
    assert n % tn == 0
    est = 2 * (d * tn * 4) + 3 * n_rows * tn * 4 + n_rows * d * 4 * 2 + d * tn * 2
    return pl.pallas_call(
        _mod_kernel,
        grid=(n // tn,),
        in_specs=[pl.BlockSpec((n_rows, d), lambda j: (0, 0)),
                  pl.BlockSpec((d, tn), lambda j: (0, j)),
                  pl.BlockSpec((1, tn), lambda j: (0, j))],
        out_specs=[pl.BlockSpec((n_sample, tn), lambda j: (0, j)),
                   pl.BlockSpec((n_rows - n_sample, tn), lambda j: (0, j))],
        out_shape=[jax.ShapeDtypeStruct((n_sample, n), F32),
                   jax.ShapeDtypeStruct((n_rows - n_sample, n), F32)],
        compiler_params=pltpu.CompilerParams(dimension_semantics=("parallel",),
                                             vmem_limit_bytes=_vmem_limit(est)),
        name="mod",
    )(c_all, w_ada, b_ada)


def _mod_spec(mod, piece, d, tiles_per_batch, ngrid):
    if mod.ndim == 3:
        if ngrid == 1:
            return pl.BlockSpec((1, 1, d), lambda i: (i // tiles_per_batch, 0, piece))
        return pl.BlockSpec((1, 1, d), lambda i, j: (i // tiles_per_batch, 0, piece))
    rows = mod.shape[0]
    if ngrid == 1:
        return pl.BlockSpec((rows, d), lambda i: (0, piece))
    return pl.BlockSpec((rows, d), lambda i, j: (0, piece))


def _inproj_kernel(*refs, n_u_tiles, emit_w):
    x_ref, sc_ref, sh_ref, g_ref, w_ref, u_ref, z_ref = refs[:7]
    h_scr, inv_scr, row_scr = refs[-3:]
    j = pl.program_id(1)

    @pl.when(j == 0)
    def _():
        _norm_mod_bf16(x_ref, g_ref, sc_ref, sh_ref, h_scr, inv_scr, row_scr)

    if emit_w:
        wb_ref = refs[7]
        wb_ref[...] = w_ref[...].astype(BF16)
        w_ref = wb_ref
    tn = w_ref.shape[1]
    nc = min(tn, INPROJ_CHUNK)

    @pl.when(j < n_u_tiles)
    def _():
        u_ref[...] = _dot(h_scr[...], w_ref[...])

    @pl.when(j >= n_u_tiles)
    def _():
        for c in range(tn // nc):
            cols = slice(c * nc, (c + 1) * nc)
            z_ref[:, cols] = _dot(h_scr[...], w_ref[:, cols]).astype(BF16)


def _inproj_call(x2d, mod, g1, w_in, *, tm, tiles_per_batch, w_mix, tn=1024):
    m, d = x2d.shape
    n = w_in.shape[1]
    n_u = 2 * w_mix
    n_u_tiles = n_u // tn
    emit_w = w_in.dtype != BF16
    assert not emit_w or m == tm
    wbytes = w_in.dtype.itemsize
    est = (2 * tm * d * 4 + 2 * d * tn * wbytes + 2 * tm * tn * 4 + 2 * tm * tn * 2 + tm * d * 2 + tm * tn * 4
           + (3 * d * tn * 2 if emit_w else 0))
    out_specs = [pl.BlockSpec((tm, tn), lambda i, j: (i, jnp.minimum(j, n_u_tiles - 1))),
                 pl.BlockSpec((tm, tn), lambda i, j: (i, jnp.maximum(j - n_u_tiles, 0)))]
    out_shape = [jax.ShapeDtypeStruct((m, n_u), F32),
                 jax.ShapeDtypeStruct((m, n - n_u), BF16)]
    if emit_w:
        out_specs.append(pl.BlockSpec((d, tn), lambda i, j: (0, j)))
        out_shape.append(jax.ShapeDtypeStruct((d, n), BF16))
    return pl.pallas_call(
        functools.partial(_inproj_kernel, n_u_tiles=n_u_tiles, emit_w=emit_w),
        grid=(m // tm, n // tn),
        in_specs=[pl.BlockSpec((tm, d), lambda i, j: (i, 0)),
                  _mod_spec(mod, 1, d, tiles_per_batch, 2),
                  _mod_spec(mod, 0, d, tiles_per_batch, 2),
                  pl.BlockSpec((1, d), lambda i, j: (0, 0)),
                  pl.BlockSpec((d, tn), lambda i, j: (0, j))],
        out_specs=out_specs,
        out_shape=out_shape,
        scratch_shapes=[pltpu.VMEM((tm, d), BF16), pltpu.VMEM((tm, LANES), F32),
                        pltpu.VMEM((2 * SUBLANES, d), F32)],
        compiler_params=pltpu.CompilerParams(dimension_semantics=("parallel", "arbitrary"),
                                             vmem_limit_bytes=_vmem_limit(est)),
        name="inproj",
    )(x2d, mod, mod, g1, w_in)


def _lru_gate_block(uc, g, ba, bx, sp):
    r = _sigmoid(g[:, :LANES] + ba)
    i = _sigmoid(g[:, LANES:] + bx)
    log_a = (-LRU_C * r) * sp
    a = jnp.exp(log_a)
    m2 = 1.0 - jnp.exp(2.0 * log_a)
    mult = jnp.where(m2 > 0.0, m2 * lax.rsqrt(m2), 0.0)
    return a, (mult * i) * uc


def _lru_kernel(*refs, n_side):
    u_ref, cw_ref, cb_ref, wg_ref, ba_ref, bx_ref, lam_ref = refs[:7]
    hs_ref, nconv_ref, nlru_ref = refs[7 + n_side:10 + n_side]
    ext, a_s, b_s, carry = refs[-4:]
    _side_cast(refs[7:7 + n_side], refs[10 + n_side:10 + 2 * n_side])
    t = pl.program_id(0)
    nt = pl.num_programs(0)
    nbat, tl, w = u_ref.shape
    nblk = w // LANES
    rows = nbat * tl
    kw = cw_ref.shape[0]
    hist = ext.shape[1] - rows
    steps_per_tile = SUBLANES // nbat

    @pl.when(t == 0)
    def _():
        ext[:, 0:hist, :] = jnp.zeros((nblk, hist, LANES), F32)
        carry[...] = jnp.zeros_like(carry)

    for b in range(nbat):
        for c in range(nblk):
            ext[c, pl.ds(hist + b, tl, stride=nbat), :] = u_ref[b, :, c * LANES:(c + 1) * LANES]

        @pl.when(t == nt - 1)
        def _(b=b):
            nconv_ref[b] = u_ref[b, tl - (kw - 1):tl, :]

    sp = jax.nn.softplus(-lam_ref[...])
    for c in range(nblk):
        cols = slice(c * LANES, (c + 1) * LANES)
        uc = cb_ref[:, cols]
        for k in range(kw):
            start = hist - (kw - 1 - k) * nbat
            uc = uc + ext[c, start:start + rows, :] * cw_ref[k:k + 1, cols]
        g = _dot(uc.astype(BF16), wg_ref[c])
        a, b = _lru_gate_block(uc, g, ba_ref[:, cols], bx_ref[:, cols], sp[:, cols])
        a_s[:, cols] = a
        b_s[:, cols] = b
        ext[c, hist - (kw - 1) * nbat:hist, :] = ext[c, hist + rows - (kw - 1) * nbat:hist + rows, :]

    def tile_step(i, hcur):
        r0 = pl.multiple_of(i * SUBLANES, SUBLANES)
        at, bt = a_s[pl.ds(r0, SUBLANES), :], b_s[pl.ds(r0, SUBLANES), :]
        hs = []
        for k in range(steps_per_tile):
            hcur = at[k * nbat:(k + 1) * nbat] * hcur + bt[k * nbat:(k + 1) * nbat]
            hs.append(hcur)
        htile = jnp.concatenate(hs, axis=0)
        for c in range(nblk):
            ext[c, pl.ds(pl.multiple_of(hist + r0, SUBLANES), SUBLANES), :] = htile[:, c * LANES:(c + 1) * LANES]
        return hcur

    h_end = lax.fori_loop(0, tl // steps_per_tile, tile_step, carry[...], unroll=2)
    carry[...] = h_end

    for b in range(nbat):
        for c in range(nblk):
            hs_ref[b, :, c * LANES:(c + 1) * LANES] = ext[c, pl.ds(hist + b, tl, stride=nbat), :].astype(BF16)

    @pl.when(t == nt - 1)
    def _():
        for b in range(nbat):
            nlru_ref[b] = h_end[b:b + 1, :]


def _lru_call(u3d, cw, cb, wg, ba, bx, lam, *, tl, side=()):
    bsz, seq, _ = u3d.shape
    w = cw.shape[1]
    kw = cw.shape[0]
    nblk = w // LANES
    assert SUBLANES % bsz == 0
    rows = bsz * tl
    hist = -(-(kw - 1) * bsz // SUBLANES) * SUBLANES
    side_in, side_out, side_shapes, side_bytes = _side_cast_specs(side, seq // tl)
    est = (2 * rows * w * 4 + 2 * rows * w * 2 + (hist + rows) * w * 4 + 2 * rows * w * 4
           + nblk * LANES * 2 * LANES * 2 * 2 + 2 * rows * 2 * LANES * 4 + side_bytes)
    const2 = lambda t: (0, 0)
    return pl.pallas_call(
        functools.partial(_lru_kernel, n_side=len(side)),
        grid=(seq // tl,),
        in_specs=[pl.BlockSpec((bsz, tl, w), lambda t: (0, t, 0)),
                  pl.BlockSpec((kw, w), const2),
                  pl.BlockSpec((1, w), const2),
                  pl.BlockSpec((nblk, LANES, 2 * LANES), lambda t: (0, 0, 0)),
                  pl.BlockSpec((1, w), const2),
                  pl.BlockSpec((1, w), const2),
                  pl.BlockSpec((1, w), const2)] + side_in,
        out_specs=[pl.BlockSpec((bsz, tl, w), lambda t: (0, t, 0)),
                   pl.BlockSpec((bsz, kw - 1, w), lambda t: (0, 0, 0)),
                   pl.BlockSpec((bsz, 1, w), lambda t: (0, 0, 0))] + side_out,
        out_shape=[jax.ShapeDtypeStruct((bsz, seq, w), BF16),
                   jax.ShapeDtypeStruct((bsz, kw - 1, w), F32),
                   jax.ShapeDtypeStruct((bsz, 1, w), F32)] + side_shapes,
        scratch_shapes=[pltpu.VMEM((nblk, hist + rows, LANES), F32),
                        pltpu.VMEM((rows, w), F32),
                        pltpu.VMEM((rows, w), F32),
                        pltpu.VMEM((bsz, w), F32)],
        compiler_params=pltpu.CompilerParams(dimension_semantics=("arbitrary",),
                                             vmem_limit_bytes=_vmem_limit(est)),
        name="lru_scan",
    )(u3d, cw, cb, wg, ba, bx, lam, *(a for a, _ in side))


def _lru_step_kernel(u_ref, buf_ref, h0_ref, cw_ref, cb_ref, wg_ref, ba_ref, bx_ref, lam_ref,
                     hs_ref, nconv_ref, nlru_ref):
    w = cw_ref.shape[1]
    kw = cw_ref.shape[0]
    nblk = w // LANES
    u = u_ref[...]
    uc = cb_ref[...] + buf_ref[0] * cw_ref[0:1, :]
    for k in range(1, kw - 1):
        uc = uc + buf_ref[k] * cw_ref[k:k + 1, :]
    uc = uc + u * cw_ref[kw - 1:kw, :]
    for k in range(kw - 2):
        nconv_ref[k] = buf_ref[k + 1]
    nconv_ref[kw - 2] = u
    sp = jax.nn.softplus(-lam_ref[...])
    for h in range(nblk):
        cols = slice(h * LANES, (h + 1) * LANES)
        uch = uc[:, cols]
        g = _dot(uch.astype(BF16), wg_ref[h])
        a, b = _lru_gate_block(uch, g, ba_ref[:, cols], bx_ref[:, cols], sp[:, cols])
        hn = a * h0_ref[:, cols] + b
        nlru_ref[:, cols] = hn
        hs_ref[:, cols] = hn.astype(BF16)


def _lru_step_call(u2d, buf, h0, cw, cb, wg, ba, bx, lam):
    n = h0.shape[0]
    w = cw.shape[1]
    kw = cw.shape[0]
    nblk = w // LANES
    c2 = lambda i: (0, 0)
    c3 = lambda i: (0, 0, 0)
    return pl.pallas_call(
        _lru_step_kernel,
        grid=(1,),
        in_specs=[pl.BlockSpec((n, w), c2),
                  pl.BlockSpec((kw - 1, n, w), c3),
                  pl.BlockSpec((n, w), c2),
                  pl.BlockSpec((kw, w), c2),
                  pl.BlockSpec((1, w), c2),
                  pl.BlockSpec((nblk, LANES, 2 * LANES), c3),
                  pl.BlockSpec((1, w), c2),
                  pl.BlockSpec((1, w), c2),
                  pl.BlockSpec((1, w), c2)],
        out_specs=[pl.BlockSpec((n, w), c2),
                   pl.BlockSpec((kw - 1, n, w), c3),
                   pl.BlockSpec((n, w), c2)],
        out_shape=[jax.ShapeDtypeStruct((n, w), BF16),
                   jax.ShapeDtypeStruct((kw - 1, n, w), F32),
                   jax.ShapeDtypeStruct((n, w), F32)],
        name="lru_step",
    )(u2d, buf, h0, cw, cb, wg, ba, bx, lam)


def _abar(ar, ai, dt):
    mag = jnp.exp(dt * ar)
    ang = dt * ai
    return mag * jnp.cos(ang), mag * jnp.sin(ang)


def _s5_prep_kernel(ax_ref, dtx_ref, bt_ref, ct_ref, al_ref, dtl_ref, bb_ref, cc_ref, pw_ref):
    ar, ai = ax_ref[0], ax_ref[1]
    abr, abi = _abar(ar, ai, jnp.exp(dtx_ref[...]))
    den = ar * ar + ai * ai
    nr, ni = abr - 1.0, abi
    q_re = (nr * ar + ni * ai) / den
    q_im = (ni * ar - nr * ai) / den
    br, bi = bt_ref[0], bt_ref[1]
    for ref, pair in ((bb_ref, (q_re * br - q_im * bi, q_re * bi + q_im * br)), (cc_ref, (ct_ref[0], ct_ref[1]))):
        for part, v in enumerate(pair):
            ref[part] = jnp.concatenate([v, v], axis=1)

    pr, pi = _abar(al_ref[0], al_ref[1], jnp.exp(dtl_ref[...]))
    qr, qi = jnp.ones_like(pr), jnp.zeros_like(pr)
    for m in range(pw_ref.shape[1]):
        pw_ref[0, m:m + 1, :] = qr
        pw_ref[1, m:m + 1, :] = qi
        qr, qi = qr * pr - qi * pi, qr * pi + qi * pr


def _s5_prep_call(a_re, a_im, log_dt, b_re, b_im, c_re, c_im, n_pow):
    g, p, k = b_re.shape
    assert c_re.shape == (g, k, p) and 2 * p == LANES
    a = jnp.stack([a_re, a_im])
    return pl.pallas_call(
        _s5_prep_kernel,
        out_shape=[jax.ShapeDtypeStruct((2, g * k, 2 * p), F32), jax.ShapeDtypeStruct((2, g * k, 2 * p), F32),
                   jax.ShapeDtypeStruct((2, n_pow, g * p), F32)],
        name="s5_prep",
    )(jnp.repeat(a, k, axis=1), jnp.repeat(log_dt, k).reshape(g * k, 1),
      jnp.stack([b_re, b_im]).swapaxes(2, 3).reshape(2, g * k, p), jnp.stack([c_re, c_im]).reshape(2, g * k, p),
      a.reshape(2, 1, g * p), jnp.repeat(log_dt, p).reshape(1, g * p))


def _block_diag(stack):
    tiled = jnp.concatenate([stack] * (STATES_PER_BLOCK // stack.shape[1]), axis=1)
    own = (lax.broadcasted_iota(jnp.int32, tiled.shape, 0) // SSM_GROUP
           == lax.broadcasted_iota(jnp.int32, tiled.shape, 1) // SSM_STATE)
    return jnp.where(own, tiled, 0.0)


def _cplx_mul(ar, ai, br, bi):
    return ar * br - ai * bi, ar * bi + ai * br


def _cplx_step(ar, ai, hr, hi, br, bi):
    return ar * hr - ai * hi + br, ar * hi + ai * hr + bi


def _s5_kernel(u_ref, bb_ref, cc_ref, pw_ref, d_ref, c_ref, wa_ref, ba_ref,
               gy_ref, nre_ref, nim_ref, ms_ref, mp_ref,
               ufb, buf, wbs, wqs, wts, yslab, out32):
    _mod_kernel(c_ref, wa_ref, ba_ref, ms_ref, mp_ref)
    nbat, seq, _ = u_ref.shape
    spb = STATES_PER_BLOCK
    fold = wbs.shape[0] // LANES
    nq = seq // fold

    bre, bim = _block_diag(bb_ref[0]), _block_diag(bb_ref[1])
    ctre, ctim = _block_diag(cc_ref[0]), _block_diag(cc_ref[1])
    wcb = jnp.concatenate([ctre.T, -ctim.T], axis=0).astype(BF16)
    lag = []
    for m in range(fold):
        wre, wim = _cplx_mul(bre, bim, pw_ref[0, m:m + 1, :], pw_ref[1, m:m + 1, :])
        rows = slice((fold - 1 - m) * LANES, (fold - m) * LANES)
        wbs[rows, :spb] = wre.astype(BF16)
        wbs[rows, spb:] = wim.astype(BF16)
        lag.append(_dot(wbs[rows, :], wcb))
    zero_blk = jnp.zeros((LANES, LANES), BF16)
    for r_in in range(fold):
        for r_out in range(fold):
            blk = lag[r_out - r_in].astype(BF16) if r_in <= r_out else zero_blk
            wts[r_in * LANES:(r_in + 1) * LANES, r_out * LANES:(r_out + 1) * LANES] = blk
    for r in range(fold):
        pr, pi = pw_ref[0, r + 1:r + 2, :], pw_ref[1, r + 1:r + 2, :]
        cols = slice(r * LANES, (r + 1) * LANES)
        wqs[:spb, cols] = (ctre * pr - ctim * pi).T.astype(BF16)
        wqs[spb:, cols] = (-(ctre * pi + ctim * pr)).T.astype(BF16)

    for r in range(fold):
        for b in range(nbat):
            yslab[r, pl.ds(b, nq, stride=nbat), :] = u_ref[b, pl.ds(r, nq, stride=fold), :]
        ufb[:, r * LANES:(r + 1) * LANES] = yslab[r].astype(BF16)

    buf[...] = _dot(ufb[...], wbs[...])

    ar, ai = pw_ref[0, fold:fold + 1, :], pw_ref[1, fold:fold + 1, :]
    groups_per_tile = SUBLANES // nbat
    z = jnp.zeros((nbat, spb), F32)

    def tile_step(i, g):
        gr, gi = g
        r0 = pl.multiple_of(i * SUBLANES, SUBLANES)
        xr, xi = buf[pl.ds(r0, SUBLANES), :spb], buf[pl.ds(r0, SUBLANES), spb:]
        hr, hi = [], []
        for k in range(groups_per_tile):
            hr.append(gr)
            hi.append(gi)
            gr, gi = _cplx_step(ar, ai, gr, gi, xr[k * nbat:(k + 1) * nbat], xi[k * nbat:(k + 1) * nbat])
        buf[pl.ds(r0, SUBLANES), :spb] = jnp.concatenate(hr, axis=0)
        buf[pl.ds(r0, SUBLANES), spb:] = jnp.concatenate(hi, axis=0)
        return gr, gi

    g_re, g_im = lax.fori_loop(0, nq // groups_per_tile, tile_step, (z, z), unroll=2)
    for b in range(nbat):
        nre_ref[b] = g_re[b:b + 1, :]
        nim_ref[b] = g_im[b:b + 1, :]

    hsb = buf[...].astype(BF16)
    per_tile = 2
    for n in range(fold // per_tile):
        cols = slice(n * per_tile * LANES, (n + 1) * per_tile * LANES)
        kk = (n + 1) * per_tile * LANES
        yn = _dot(hsb, wqs[:, cols]) + _dot(ufb[:, :kk], wts[:kk, cols])
        for i in range(per_tile):
            yslab[n * per_tile + i] = yn[:, i * LANES:(i + 1) * LANES]

    for b in range(nbat):
        for r in range(fold):
            ys = (yslab[r, pl.ds(b, nq, stride=nbat), :]
                  + d_ref[...] * u_ref[b, pl.ds(r, nq, stride=fold), :])
            out32[b, pl.ds(r, nq, stride=fold), :] = jax.nn.gelu(ys)
        gy_ref[b] = out32[b].astype(BF16)


def _s5_call(u3d, bb, cc, pw, d, c_all, n_sample, w_ada, b_ada, first_col, *, w):
    bsz, seq, _ = u3d.shape
    nblk = w // LANES
    spb = STATES_PER_BLOCK
    fold = S5_FOLD
    assert SUBLANES % bsz == 0 and seq % fold == 0 and pw.shape[1] == fold + 1
    rows = bsz * (seq // fold)
    n_c, dm = c_all.shape
    mod_cols = w_ada.shape[1] - first_col
    tn = mod_cols // nblk
    assert mod_cols % nblk == 0 and tn % LANES == 0 and first_col % tn == 0
    est = (2 * bsz * seq * LANES * 4 + 2 * bsz * seq * LANES * 2
           + 2 * dm * tn * 4 + dm * tn * 2 + 2 * n_c * dm * 4 + 3 * n_c * tn * 4
           + rows * fold * LANES * 2 + 2 * rows * 2 * spb * 4
           + 3 * fold * LANES * 2 * spb * 2
           + rows * fold * LANES * 4 + bsz * seq * LANES * 4
           + rows * 2 * spb * 2
           + 2 * (LANES * 2 * spb + 2 * spb * LANES) * 4)
    return pl.pallas_call(
        _s5_kernel,
        grid=(nblk,),
        in_specs=[pl.BlockSpec((bsz, seq, LANES), lambda c: (0, 0, nblk + c)),
                  pl.BlockSpec((2, LANES, LANES), lambda c: (0, c, 0)),
                  pl.BlockSpec((2, LANES, LANES), lambda c: (0, c, 0)),
                  pl.BlockSpec((2, fold + 1, spb), lambda c: (0, 0, c)),
                  pl.BlockSpec((1, LANES), lambda c: (0, c)),
                  pl.BlockSpec((n_c, dm), lambda c: (0, 0)),
                  pl.BlockSpec((dm, tn), lambda c: (0, first_col // tn + c)),
                  pl.BlockSpec((1, tn), lambda c: (0, first_col // tn + c))],
        out_specs=[pl.BlockSpec((bsz, seq, LANES), lambda c: (0, 0, c)),
                   pl.BlockSpec((bsz, 1, spb), lambda c: (0, 0, c)),
                   pl.BlockSpec((bsz, 1, spb), lambda c: (0, 0, c)),
                   pl.BlockSpec((n_sample, tn), lambda c: (0, c)),
                   pl.BlockSpec((n_c - n_sample, tn), lambda c: (0, c))],
        out_shape=[jax.ShapeDtypeStruct((bsz, seq, w), BF16),
                   jax.ShapeDtypeStruct((bsz, 1, nblk * spb), F32),
                   jax.ShapeDtypeStruct((bsz, 1, nblk * spb), F32),
                   jax.ShapeDtypeStruct((n_sample, mod_cols), F32),
                   jax.ShapeDtypeStruct((n_c - n_sample, mod_cols), F32)],
        scratch_shapes=[pltpu.VMEM((rows, fold * LANES), BF16),
                        pltpu.VMEM((rows, 2 * spb), F32),
                        pltpu.VMEM((fold * LANES, 2 * spb), BF16),
                        pltpu.VMEM((2 * spb, fold * LANES), BF16),
                        pltpu.VMEM((fold * LANES, fold * LANES), BF16),
                        pltpu.VMEM((fold, rows, LANES), F32),
                        pltpu.VMEM((bsz, seq, LANES), F32)],
        compiler_params=pltpu.CompilerParams(dimension_semantics=("parallel",),
                                             vmem_limit_bytes=_vmem_limit(est)),
        name="s5_scan",
    )(u3d, bb, cc, pw, d, c_all, w_ada, b_ada)


def _s5_step_kernel(u_ref, h0r_ref, h0i_ref, bb_ref, cc_ref, pw_ref, d_ref, gy_ref, nre_ref, nim_ref):
    spb = h0r_ref.shape[0]
    u = u_ref[...]
    ut = u.T.astype(BF16)
    ar = jnp.broadcast_to(pw_ref[0, 1:2, :], (LANES, spb)).T
    ai = jnp.broadcast_to(pw_ref[1, 1:2, :], (LANES, spb)).T
    bu_re = _dot(_block_diag(bb_ref[0]).T.astype(BF16), ut)
    bu_im = _dot(_block_diag(bb_ref[1]).T.astype(BF16), ut)
    hr, hi = _cplx_step(ar, ai, h0r_ref[...], h0i_ref[...], bu_re, bu_im)
    nre_ref[...] = hr
    nim_ref[...] = hi
    yt = (_dot(_block_diag(cc_ref[0]).astype(BF16), hr.astype(BF16))
          - _dot(_block_diag(cc_ref[1]).astype(BF16), hi.astype(BF16)))
    gy_ref[...] = jax.nn.gelu(yt.T + d_ref[...] * u).astype(BF16)


def _s5_step_call(u2d, h0r, h0i, bb, cc, pw, d, *, w):
    n_state, n = h0r.shape
    spb = STATES_PER_BLOCK
    nblk = w // LANES
    assert n_state == nblk * spb and n == LANES
    state_spec = pl.BlockSpec((spb, n), lambda c: (c, 0))
    state_shape = jax.ShapeDtypeStruct((n_state, n), F32)
    return pl.pallas_call(
        _s5_step_kernel,
        grid=(nblk,),
        in_specs=[pl.BlockSpec((n, LANES), lambda c: (0, nblk + c)),
                  state_spec,
                  state_spec,
                  pl.BlockSpec((2, LANES, LANES), lambda c: (0, c, 0)),
                  pl.BlockSpec((2, LANES, LANES), lambda c: (0, c, 0)),
                  pl.BlockSpec((2, pw.shape[1], spb), lambda c: (0, 0, c)),
                  pl.BlockSpec((1, LANES), lambda c: (0, c))],
        out_specs=[pl.BlockSpec((n, LANES), lambda c: (0, c)), state_spec, state_spec],
        out_shape=[jax.ShapeDtypeStruct((n, w), BF16), state_shape, state_shape],
        compiler_params=pltpu.CompilerParams(dimension_semantics=("parallel",)),
        name="s5_step",
    )(u2d, h0r, h0i, bb, cc, pw, d)


def _merge_kernel(*refs, n_side):
    hs_ref, gy_ref, z_ref, x_ref, gt_ref, wp_ref, wg_ref, wo_ref = refs[:8]
    o_ref = refs[8 + n_side]
    _side_cast(refs[8:8 + n_side], refs[9 + n_side:])
    d = x_ref.shape[1]
    ya = _dot(hs_ref[...], wp_ref[...])
    merged = _sigmoid(z_ref[:, :d].astype(F32)) * ya
    glu = _dot(gy_ref[...], wg_ref[...])
    yb = glu[:, :d] * _sigmoid(glu[:, d:])
    merged = merged + _sigmoid(z_ref[:, d:].astype(F32)) * yb
    o = _dot(merged.astype(BF16), wo_ref[...])
    o_ref[...] = x_ref[...] + _rows(gt_ref) * o


def _merge_call(hs, gy, sz, x2d, mod, wp, wg, wo, *, tm, tiles_per_batch, side=()):
    m, d = x2d.shape
    w = hs.shape[1]
    side_in, side_out, side_shapes, side_bytes = _side_cast_specs(side, m // tm)
    est = (2 * (2 * tm * w * 2 + tm * 2 * d * 2 + 2 * tm * d * 4) + (wp.size + wg.size + wo.size) * 2
           + tm * d * 4 * 5 + side_bytes)
    one = pl.Buffered(1)
    return pl.pallas_call(
        functools.partial(_merge_kernel, n_side=len(side)),
        grid=(m // tm,),
        in_specs=[pl.BlockSpec((tm, w), lambda i: (i, 0)),
                  pl.BlockSpec((tm, w), lambda i: (i, 0)),
                  pl.BlockSpec((tm, 2 * d), lambda i: (i, 0)),
                  pl.BlockSpec((tm, d), lambda i: (i, 0)),
                  _mod_spec(mod, 2, d, tiles_per_batch, 1),
                  pl.BlockSpec(wp.shape, lambda i: (0, 0), pipeline_mode=one),
                  pl.BlockSpec(wg.shape, lambda i: (0, 0), pipeline_mode=one),
                  pl.BlockSpec(wo.shape, lambda i: (0, 0), pipeline_mode=one)] + side_in,
        out_specs=[pl.BlockSpec((tm, d), lambda i: (i, 0))] + side_out,
        out_shape=[jax.ShapeDtypeStruct((m, d), F32)] + side_shapes,
        compiler_params=pltpu.CompilerParams(dimension_semantics=("parallel",),
                                             vmem_limit_bytes=_vmem_limit(est)),
        name="merge",
    )(hs, gy, sz, x2d, mod, wp, wg, wo, *(a for a, _ in side))


def _mlp_kernel(x_hbm, sc_ref, sh_ref, gt_ref, g2_ref, gf_ref, wu_ref, wd_ref, o_ref,
                x_buf, x_sem, h_scr, inv_scr, row_scr, *, final_norm):
    i, j = pl.program_id(0), pl.program_id(1)
    ni, nj = pl.num_programs(0), pl.num_programs(1)
    n_rows, d = x_buf.shape

    def x_tile(tile):
        return pltpu.make_async_copy(x_hbm.at[pl.ds(tile * n_rows, n_rows), :], x_buf, x_sem)

    @pl.when(j == 0)
    def _():
        @pl.when(i == 0)
        def _():
            x_tile(0).start()

        x_tile(i).wait()
        _norm_mod_bf16(x_buf, g2_ref, sc_ref, sh_ref, h_scr, inv_scr, row_scr, x_copy_ref=o_ref)

        @pl.when(i + 1 < ni)
        def _():
            x_tile(i + 1).start()

    up = _dot(h_scr[...], wu_ref[...])
    act = jnp.square(jnp.maximum(up, 0.0)).astype(BF16)
    nc = min(d, MLP_OUT_CHUNK)
    for c in range(d // nc):
        cols = slice(c * nc, (c + 1) * nc)
        gate = gt_ref[0, :, cols] if len(gt_ref.shape) == 3 else gt_ref[:, cols]
        o_ref[:, cols] += gate * _dot(act, wd_ref[:, cols])

    if final_norm:
        @pl.when(j == nj - 1)
        def _():
            def chunk(rows):
                x2 = o_ref[rows, :]
                ms = jnp.mean(x2 * x2, axis=-1, keepdims=True)
                o_ref[rows, :] = (x2 * lax.rsqrt(ms + EPS)) * gf_ref[...]

            _row_groups(n_rows, min(n_rows, EPILOGUE_ROWS), chunk, unroll=1)


def _mlp_call(x2d, mod, g2, gf, w_up, w_down, *, tm, tiles_per_batch, final_norm, tf):
    m, d = x2d.shape
    dff = w_up.shape[1]
    assert w_up.dtype == BF16 and w_down.dtype == BF16
    est = 3 * tm * d * 4 + tm * d * 2 + 4 * d * tf * 2 + tm * tf * 6 + tm * MLP_OUT_CHUNK * 4
    return pl.pallas_call(
        functools.partial(_mlp_kernel, final_norm=final_norm),
        grid=(m // tm, dff // tf),
        in_specs=[pl.BlockSpec(memory_space=pl.ANY),
                  _mod_spec(mod, 4 - MLP_MOD_FIRST, d, tiles_per_batch, 2),
                  _mod_spec(mod, 3 - MLP_MOD_FIRST, d, tiles_per_batch, 2),
                  _mod_spec(mod, 5 - MLP_MOD_FIRST, d, tiles_per_batch, 2),
                  pl.BlockSpec((1, d), lambda i, j: (0, 0)),
                  pl.BlockSpec((1, d), lambda i, j: (0, 0)),
                  pl.BlockSpec((d, tf), lambda i, j: (0, j)),
                  pl.BlockSpec((tf, d), lambda i, j: (j, 0))],
        out_specs=pl.BlockSpec((tm, d), lambda i, j: (i, 0)),
        out_shape=jax.ShapeDtypeStruct((m, d), F32),
        scratch_shapes=[pltpu.VMEM((tm, d), F32), pltpu.SemaphoreType.DMA(()),
                        pltpu.VMEM((tm, d), BF16), pltpu.VMEM((tm, LANES), F32),
                        pltpu.VMEM((2 * SUBLANES, d), F32)],
        compiler_params=pltpu.CompilerParams(dimension_semantics=("arbitrary", "arbitrary"),
                                             vmem_limit_bytes=_vmem_limit(est)),
        name="mlp",
    )(x2d, mod, mod, mod, g2, gf, w_up, w_down)


def _pick_tile(n, pref):
    t = min(n, pref)
    assert n % t == 0, (n, t)
    return t


def kernel(x_prompt, x_sample, state_conv, state_lru, state_ssm_re, state_ssm_im, c_prompt, c_sample, w_ada, b_ada, g_norm1, g_norm2, w_in, conv_w, conv_b, w_rg_a, b_rg_a, w_rg_x, b_rg_x, lru_lambda, w_proj_a, ssm_a_re, ssm_a_im, ssm_log_dt, ssm_b_re, ssm_b_im, ssm_c_re, ssm_c_im, ssm_d, w_glu, w_out, w_up, w_down, g_final):
    depth = w_ada.shape[0]
    nb, seq, d = x_prompt.shape
    ns = x_sample.shape[0]
    assert x_sample.shape[1] == 1
    w = conv_w.shape[2]
    n_state = ssm_a_re.shape[1] * ssm_a_re.shape[2]
    assert ssm_b_re.shape[2:] == (SSM_STATE, SSM_GROUP) and w % LANES == 0
    assert w_in.shape[2] == 2 * w + 2 * d

    tl = _pick_tile(seq, 256)
    tm_in = _pick_tile(seq, 1024)
    tm_mg = _pick_tile(seq, 256)
    tm_mlp = _pick_tile(seq, 1024)

    xp = x_prompt.reshape(nb * seq, d)
    xs = x_sample.reshape(ns, d)
    assert ns % SUBLANES == 0
    pad = (-nb) % SUBLANES
    c_all = jnp.concatenate([c_sample, c_prompt, jnp.zeros((pad, d), F32)], axis=0)

    outs_p = [[] for _ in range(4)]
    outs_s = [[] for _ in range(4)]
    for l in range(depth):
        last = l == depth - 1
        b_ada_row = b_ada[l].reshape(1, -1)
        mod_s, mod_p = _mod_call(c_all, ns, w_ada[l], b_ada_row, MLP_MOD_FIRST * d)
        mod_p = mod_p.reshape(nb + pad, 1, MLP_MOD_FIRST * d)

        g1 = g_norm1[l].reshape(1, d)
        g2 = g_norm2[l].reshape(1, d)
        gf = g_final.reshape(1, d)
        cw, cb = conv_w[l], conv_b[l].reshape(1, w)
        wg = jnp.concatenate([w_rg_a[l], w_rg_x[l]], axis=2).astype(BF16)
        ba, bx, lam = b_rg_a[l].reshape(1, w), b_rg_x[l].reshape(1, w), lru_lambda[l].reshape(1, w)
        bb, cc, pw = _s5_prep_call(ssm_a_re[l], ssm_a_im[l], ssm_log_dt[l], ssm_b_re[l], ssm_b_im[l],
                                       ssm_c_re[l], ssm_c_im[l], S5_FOLD + 1)
        dskip = ssm_d[l].reshape(1, w)

        u_s, sz_s, w_in_b = _inproj_call(xs, mod_s, g1, w_in[l], tm=ns, tiles_per_batch=1, w_mix=w)

        u_p, sz_p = _inproj_call(xp, mod_p, g1, w_in_b, tm=tm_in, tiles_per_batch=seq // tm_in, w_mix=w)
        u3 = u_p.reshape(nb, seq, 2 * w)
        hs_p, nconv_p, nlru_p, wp, wgl, wo = _lru_call(u3, cw, cb, wg, ba, bx, lam, tl=tl,
                                                       side=((w_proj_a[l], 1), (w_glu[l], 1), (w_out[l], 1)))
        gy_p, nre_p, nim_p, mlp_mod_s, mlp_mod_p = _s5_call(u3, bb, cc, pw, dskip, c_all, ns,
                                                            w_ada[l], b_ada_row, MLP_MOD_FIRST * d, w=w)
        mlp_mod_p = mlp_mod_p.reshape(nb + pad, 1, (N_MOD - MLP_MOD_FIRST) * d)
        x1_p, w_up_b, w_down_b = _merge_call(hs_p.reshape(nb * seq, w), gy_p.reshape(nb * seq, w), sz_p, xp, mod_p,
                                             wp, wgl, wo, tm=tm_mg, tiles_per_batch=seq // tm_mg,
                                             side=((w_up[l], 1), (w_down[l], 0)))
        xp = _mlp_call(x1_p, mlp_mod_p, g2, gf, w_up_b, w_down_b, tm=tm_mlp, tiles_per_batch=seq // tm_mlp,
                       final_norm=last, tf=1024)

        buf = jnp.transpose(state_conv[l], (1, 0, 2))
        hs_s, nconv_s, nlru_s = _lru_step_call(u_s, buf, state_lru[l], cw, cb, wg, ba, bx, lam)
        by_token = lambda s: jnp.transpose(s, (1, 2, 0)).reshape(n_state, ns)
        gy_s, nre_s, nim_s = _s5_step_call(u_s, by_token(state_ssm_re[l]), by_token(state_ssm_im[l]), bb, cc, pw,
                                           dskip, w=w)
        (x1_s,) = _merge_call(hs_s, gy_s, sz_s, xs, mod_s, wp, wgl, wo, tm=ns, tiles_per_batch=1)
        xs = _mlp_call(x1_s, mlp_mod_s, g2, gf, w_up_b, w_down_b, tm=ns, tiles_per_batch=1, final_norm=last,
                       tf=1024)

        gshape = ssm_a_re.shape[1:]
        for acc, v in zip(outs_p, (nconv_p, nlru_p.reshape(nb, w),
                                   nre_p.reshape((nb,) + gshape), nim_p.reshape((nb,) + gshape))):
            acc.append(v)
        by_group = lambda s: jnp.transpose(s.reshape(gshape + (ns,)), (2, 0, 1))
        for acc, v in zip(outs_s, (jnp.transpose(nconv_s, (1, 0, 2)), nlru_s, by_group(nre_s), by_group(nim_s))):
            acc.append(v)

    y_prompt = xp.reshape(nb, seq, d)
    y_sample = xs.reshape(ns, 1, d)
    stack = lambda a: a[0][None] if len(a) == 1 else jnp.stack(a)
    return (y_prompt, y_sample) + tuple(stack(a) for a in outs_p) + tuple(stack(a) for a in outs_s)
```

```python
import functools

import jax
import jax.numpy as jnp
from jax import lax
from jax.experimental import pallas as pl
from jax.experimental.pallas import tpu as pltpu

F32 = jnp.float32
BF16 = jnp.bfloat16

LANES = 128
SUBLANES = 8
BF16_ROWS = 2 * SUBLANES
VMEM_PHYSICAL_BYTES = 64 * 1024 * 1024
VMEM_LIMIT_CAP_BYTES = VMEM_PHYSICAL_BYTES - 6 * 1024 * 1024
VMEM_SLACK_BYTES = 8 * 1024 * 1024

LRU_C = 8.0
EPS = 1e-6
N_MOD = 6
MLP_MOD_FIRST = 3
SSM_GROUP = 16
SSM_STATE = 64
GROUPS_PER_BLOCK = LANES // SSM_GROUP
STATES_PER_BLOCK = GROUPS_PER_BLOCK * SSM_STATE

EPILOGUE_ROWS = 256
MLP_OUT_CHUNK = 512
INPROJ_CHUNK = 256
S5_FOLD = 8


def _vmem_limit(nbytes):
    return int(min(VMEM_LIMIT_CAP_BYTES, max(32 * 1024 * 1024, nbytes + VMEM_SLACK_BYTES)))


def _rows(ref, rows=None):
    if len(ref.shape) == 3:
        return ref[0]
    return ref[...] if rows is None else ref[rows, :]


def _row_groups(n_rows, group, fn, unroll):
    assert n_rows % group == 0

    def body(i, carry):
        fn(pl.ds(pl.multiple_of(i * group, group), group))
        return carry

    lax.fori_loop(0, n_rows // group, body, 0, unroll=unroll)


def _rms_scale_pass(src, inv_scr, d):
    def fn(rows):
        x = src(rows)
        xx = x * x
        part = xx[:, 0:LANES]
        for c in range(1, d // LANES):
            part = part + xx[:, c * LANES:(c + 1) * LANES]
        inv_scr[rows, :] = part

    _row_groups(inv_scr.shape[0], SUBLANES, fn, unroll=8)
    ss = jnp.sum(inv_scr[...], axis=-1, keepdims=True)
    inv_scr[...] = jnp.broadcast_to(lax.rsqrt(ss * (1.0 / d) + EPS), inv_scr.shape)


def _lanes(inv, d):
    return jnp.concatenate([inv] * (d // LANES), axis=1)


def _sublane_rows(scr, slot, row=None):
    rows = slice(slot * SUBLANES, (slot + 1) * SUBLANES)
    if row is None:
        return scr[rows, :]
    scr[rows, :] = jnp.broadcast_to(row, (SUBLANES, row.shape[1]))
    return None


def _norm_mod_bf16(x_ref, g_ref, sc_ref, sh_ref, h_scr, inv_scr, row_scr, x_copy_ref=None):
    n_rows, d = x_ref.shape
    _rms_scale_pass(lambda rows: x_ref[rows, :], inv_scr, d)
    per_token = len(sc_ref.shape) == 2
    if not per_token:
        _sublane_rows(row_scr, 0, g_ref[...] * (1.0 + sc_ref[0]))
        _sublane_rows(row_scr, 1, sh_ref[0])

    def fn(rows):
        halves = []
        for k in range(BF16_ROWS // SUBLANES):
            r8 = pl.ds(pl.multiple_of(rows.start + k * SUBLANES, SUBLANES), SUBLANES)
            gm = g_ref[...] * (1.0 + sc_ref[r8, :]) if per_token else _sublane_rows(row_scr, 0)
            sh = sh_ref[r8, :] if per_token else _sublane_rows(row_scr, 1)
            x = x_ref[r8, :]
            if x_copy_ref is not None:
                x_copy_ref[r8, :] = x
            halves.append((x * _lanes(inv_scr[r8, :], d)) * gm + sh)
        h_scr[rows, :] = jnp.concatenate(halves, axis=0).astype(BF16)

    _row_groups(n_rows, BF16_ROWS, fn, unroll=2)


def _dot(a, b):
    return jnp.dot(a, b, preferred_element_type=F32)


def _sigmoid(x):
    return 0.5 * jnp.tanh(0.5 * x) + 0.5


def _side_cast_specs(side, n_steps):
    in_specs, out_specs, out_shapes, nbytes = [], [], [], 0
    for arr, axis in side:
        blk = tuple(s // n_steps if a == axis else s for a, s in enumerate(arr.shape))
        assert arr.ndim == 2 and arr.shape[axis] % n_steps == 0 and blk[1] % LANES == 0 and blk[0] % BF16_ROWS == 0
        idx = (lambda i: (i, 0)) if axis == 0 else (lambda i: (0, i))
        in_specs.append(pl.BlockSpec(blk, idx))
        out_specs.append(pl.BlockSpec(blk, idx))
        out_shapes.append(jax.ShapeDtypeStruct(arr.shape, BF16))
        nbytes += 2 * blk[0] * blk[1] * (4 + 2)
    return in_specs, out_specs, out_shapes, nbytes


def _side_cast(src_refs, dst_refs):
    for src, dst in zip(src_refs, dst_refs, strict=True):
        dst[...] = src[...].astype(BF16)


def _mod_kernel(c_ref, w_ref, b_ref, os_ref, op_ref):
    c = c_ref[...]
    cs = (c * _sigmoid(c)).astype(BF16)
    mod = _dot(cs, w_ref[...].astype(BF16)) + b_ref[...]
    n_sample = os_ref.shape[0]
    os_ref[...] = mod[:n_sample]
    op_ref[...] = mod[n_sample:]


def _mod_call(c_all, n_sample, w_ada, b_ada, n, tn=1024):
    n_rows, d = c_all.shape
    assert n % tn == 0
    est = 2 * (d * tn * 4) + 3 * n_rows * tn * 4 + n_rows * d * 4 * 2 + d * tn * 2
    return pl.pallas_call(
        _mod_kernel,
        grid=(n // tn,),
        in_specs=[pl.BlockSpec((n_rows, d), lambda j: (0, 0)),
                  pl.BlockSpec((d, tn), lambda j: (0, j)),
                  pl.BlockSpec((1, tn), lambda j: (0, j))],
        out_specs=[pl.BlockSpec((n_sample, tn), lambda j: (0, j)),
                   pl.BlockSpec((n_rows - n_sample, tn), lambda j: (0, j))],
        out_shape=[jax.ShapeDtypeStruct((n_sample, n), F32),
                   jax.ShapeDtypeStruct((n_rows - n_sample, n), F32)],
        compiler_params=pltpu.CompilerParams(dimension_semantics=("parallel",),
                                             vmem_limit_bytes=_vmem_limit(est)),
        name="mod",
    )(c_all, w_ada, b_ada)


def _mod_spec(mod, piece, d, tiles_per_batch, ngrid):
    if mod.ndim == 3:
        if ngrid == 1:
            return pl.BlockSpec((1, 1, d), lambda i: (i // tiles_per_batch, 0, piece))
        return pl.BlockSpec((1, 1, d), lambda i, j: (i // tiles_per_batch, 0, piece))
    rows = mod.shape[0]
    if ngrid == 1:
        return pl.BlockSpec((rows, d), lambda i: (0, piece))
    return pl.BlockSpec((rows, d), lambda i, j: (0, piece))


def _x_tile_prologue(x_hbm, x_buf, x_sem, prologue):
    i, j = pl.program_id(0), pl.program_id(1)
    n_rows = x_buf.shape[0]

    def x_tile(tile):
        return pltpu.make_async_copy(x_hbm.at[pl.ds(tile * n_rows, n_rows), :], x_buf, x_sem)

    @pl.when(j == 0)
    def _():
        @pl.when(i == 0)
        def _():
            x_tile(0).start()

        x_tile(i).wait()
        prologue()

        @pl.when(i + 1 < pl.num_programs(0))
        def _():
            x_tile(i + 1).start()


def _inproj_kernel(*refs, n_u_tiles, emit_w):
    x_hbm, sc_ref, sh_ref, g_ref, w_ref, u_ref, z_ref = refs[:7]
    x_buf, x_sem, h_scr, inv_scr, row_scr = refs[-5:]
    j = pl.program_id(1)
    _x_tile_prologue(x_hbm, x_buf, x_sem,
                     lambda: _norm_mod_bf16(x_buf, g_ref, sc_ref, sh_ref, h_scr, inv_scr, row_scr))

    if emit_w:
        wb_ref = refs[7]
        wb_ref[...] = w_ref[...].astype(BF16)
        w_ref = wb_ref
    tn = w_ref.shape[1]
    nc = min(tn, INPROJ_CHUNK)

    @pl.when(j < n_u_tiles)
    def _():
        u_ref[...] = _dot(h_scr[...], w_ref[...])

    @pl.when(j >= n_u_tiles)
    def _():
        for c in range(tn // nc):
            cols = slice(c * nc, (c + 1) * nc)
            z_ref[:, cols] = _dot(h_scr[...], w_ref[:, cols]).astype(BF16)


def _inproj_call(x2d, mod, g1, w_in, *, tm, tiles_per_batch, w_mix, tn=1024):
    m, d = x2d.shape
    n = w_in.shape[1]
    n_u = 2 * w_mix
    n_u_tiles = n_u // tn
    emit_w = w_in.dtype != BF16
    assert not emit_w or m == tm
    wbytes = w_in.dtype.itemsize
    est = (tm * d * 4 + 2 * d * tn * wbytes + 2 * tm * tn * 4 + 2 * tm * tn * 2 + tm * d * 2 + tm * tn * 4
           + (3 * d * tn * 2 if emit_w else 0))
    out_specs = [pl.BlockSpec((tm, tn), lambda i, j: (i, jnp.minimum(j, n_u_tiles - 1))),
                 pl.BlockSpec((tm, tn), lambda i, j: (i, jnp.maximum(j - n_u_tiles, 0)))]
    out_shape = [jax.ShapeDtypeStruct((m, n_u), F32),
                 jax.ShapeDtypeStruct((m, n - n_u), BF16)]
    if emit_w:
        out_specs.append(pl.BlockSpec((d, tn), lambda i, j: (0, j)))
        out_shape.append(jax.ShapeDtypeStruct((d, n), BF16))
    return pl.pallas_call(
        functools.partial(_inproj_kernel, n_u_tiles=n_u_tiles, emit_w=emit_w),
        grid=(m // tm, n // tn),
        in_specs=[pl.BlockSpec(memory_space=pl.ANY),
                  _mod_spec(mod, 1, d, tiles_per_batch, 2),
                  _mod_spec(mod, 0, d, tiles_per_batch, 2),
                  pl.BlockSpec((1, d), lambda i, j: (0, 0)),
                  pl.BlockSpec((d, tn), lambda i, j: (0, j))],
        out_specs=out_specs,
        out_shape=out_shape,
        scratch_shapes=[pltpu.VMEM((tm, d), F32), pltpu.SemaphoreType.DMA(()),
                        pltpu.VMEM((tm, d), BF16), pltpu.VMEM((tm, LANES), F32),
                        pltpu.VMEM((2 * SUBLANES, d), F32)],
        compiler_params=pltpu.CompilerParams(dimension_semantics=("arbitrary", "arbitrary"),
                                             vmem_limit_bytes=_vmem_limit(est)),
        name="inproj",
    )(x2d, mod, mod, g1, w_in)


def _lru_gate_block(uc, g, ba, bx, sp):
    r = _sigmoid(g[:, :LANES] + ba)
    i = _sigmoid(g[:, LANES:] + bx)
    log_a = (-LRU_C * r) * sp
    a = jnp.exp(log_a)
    m2 = 1.0 - jnp.exp(2.0 * log_a)
    mult = jnp.where(m2 > 0.0, m2 * lax.rsqrt(m2), 0.0)
    return a, (mult * i) * uc


def _lru_kernel(*refs, n_side):
    u_ref, cw_ref, cb_ref, wg_ref, ba_ref, bx_ref, lam_ref = refs[:7]
    hs_ref, nconv_ref, nlru_ref = refs[7 + n_side:10 + n_side]
    ext, a_s, b_s, carry = refs[-4:]
    _side_cast(refs[7:7 + n_side], refs[10 + n_side:10 + 2 * n_side])
    t = pl.program_id(0)
    nt = pl.num_programs(0)
    nbat, tl, w = u_ref.shape
    nblk = w // LANES
    rows = nbat * tl
    kw = cw_ref.shape[0]
    hist = ext.shape[1] - rows
    steps_per_tile = SUBLANES // nbat

    @pl.when(t == 0)
    def _():
        ext[:, 0:hist, :] = jnp.zeros((nblk, hist, LANES), F32)
        carry[...] = jnp.zeros_like(carry)

    for b in range(nbat):
        for c in range(nblk):
            ext[c, pl.ds(hist + b, tl, stride=nbat), :] = u_ref[b, :, c * LANES:(c + 1) * LANES]

        @pl.when(t == nt - 1)
        def _(b=b):
            nconv_ref[b] = u_ref[b, tl - (kw - 1):tl, :]

    sp = jax.nn.softplus(-lam_ref[...])
    for c in range(nblk):
        cols = slice(c * LANES, (c + 1) * LANES)
        uc = cb_ref[:, cols]
        for k in range(kw):
            start = hist - (kw - 1 - k) * nbat
            uc = uc + ext[c, start:start + rows, :] * cw_ref[k:k + 1, cols]
        g = _dot(uc.astype(BF16), wg_ref[c])
        a, b = _lru_gate_block(uc, g, ba_ref[:, cols], bx_ref[:, cols], sp[:, cols])
        a_s[:, cols] = a
        b_s[:, cols] = b
        ext[c, hist - (kw - 1) * nbat:hist, :] = ext[c, hist + rows - (kw - 1) * nbat:hist + rows, :]

    def tile_step(i, hcur):
        r0 = pl.multiple_of(i * SUBLANES, SUBLANES)
        at, bt = a_s[pl.ds(r0, SUBLANES), :], b_s[pl.ds(r0, SUBLANES), :]
        hs = []
        for k in range(steps_per_tile):
            hcur = at[k * nbat:(k + 1) * nbat] * hcur + bt[k * nbat:(k + 1) * nbat]
            hs.append(hcur)
        htile = jnp.concatenate(hs, axis=0)
        for c in range(nblk):
            ext[c, pl.ds(pl.multiple_of(hist + r0, SUBLANES), SUBLANES), :] = htile[:, c * LANES:(c + 1) * LANES]
        return hcur

    h_end = lax.fori_loop(0, tl // steps_per_tile, tile_step, carry[...], unroll=2)
    carry[...] = h_end

    for b in range(nbat):
        for c in range(nblk):
            hs_ref[b, :, c * LANES:(c + 1) * LANES] = ext[c, pl.ds(hist + b, tl, stride=nbat), :].astype(BF16)

    @pl.when(t == nt - 1)
    def _():
        for b in range(nbat):
            nlru_ref[b] = h_end[b:b + 1, :]


def _lru_call(u3d, cw, cb, wg, ba, bx, lam, *, tl, side=()):
    bsz, seq, _ = u3d.shape
    w = cw.shape[1]
    kw = cw.shape[0]
    nblk = w // LANES
    assert SUBLANES % bsz == 0
    rows = bsz * tl
    hist = -(-(kw - 1) * bsz // SUBLANES) * SUBLANES
    side_in, side_out, side_shapes, side_bytes = _side_cast_specs(side, seq // tl)
    est = (2 * rows * w * 4 + 2 * rows * w * 2 + (hist + rows) * w * 4 + 2 * rows * w * 4
           + nblk * LANES * 2 * LANES * 2 * 2 + 2 * rows * 2 * LANES * 4 + side_bytes)
    const2 = lambda t: (0, 0)
    return pl.pallas_call(
        functools.partial(_lru_kernel, n_side=len(side)),
        grid=(seq // tl,),
        in_specs=[pl.BlockSpec((bsz, tl, w), lambda t: (0, t, 0)),
                  pl.BlockSpec((kw, w), const2),
                  pl.BlockSpec((1, w), const2),
                  pl.BlockSpec((nblk, LANES, 2 * LANES), lambda t: (0, 0, 0)),
                  pl.BlockSpec((1, w), const2),
                  pl.BlockSpec((1, w), const2),
                  pl.BlockSpec((1, w), const2)] + side_in,
        out_specs=[pl.BlockSpec((bsz, tl, w), lambda t: (0, t, 0)),
                   pl.BlockSpec((bsz, kw - 1, w), lambda t: (0, 0, 0)),
                   pl.BlockSpec((bsz, 1, w), lambda t: (0, 0, 0))] + side_out,
        out_shape=[jax.ShapeDtypeStruct((bsz, seq, w), BF16),
                   jax.ShapeDtypeStruct((bsz, kw - 1, w), F32),
                   jax.ShapeDtypeStruct((bsz, 1, w), F32)] + side_shapes,
        scratch_shapes=[pltpu.VMEM((nblk, hist + rows, LANES), F32),
                        pltpu.VMEM((rows, w), F32),
                        pltpu.VMEM((rows, w), F32),
                        pltpu.VMEM((bsz, w), F32)],
        compiler_params=pltpu.CompilerParams(dimension_semantics=("arbitrary",),
                                             vmem_limit_bytes=_vmem_limit(est)),
        name="lru_scan",
    )(u3d, cw, cb, wg, ba, bx, lam, *(a for a, _ in side))


def _lru_step_kernel(u_ref, buf_ref, h0_ref, cw_ref, cb_ref, wg_ref, ba_ref, bx_ref, lam_ref,
                     hs_ref, nconv_ref, nlru_ref):
    w = cw_ref.shape[1]
    kw = cw_ref.shape[0]
    nblk = w // LANES
    u = u_ref[...]
    uc = cb_ref[...] + buf_ref[0] * cw_ref[0:1, :]
    for k in range(1, kw - 1):
        uc = uc + buf_ref[k] * cw_ref[k:k + 1, :]
    uc = uc + u * cw_ref[kw - 1:kw, :]
    for k in range(kw - 2):
        nconv_ref[k] = buf_ref[k + 1]
    nconv_ref[kw - 2] = u
    sp = jax.nn.softplus(-lam_ref[...])
    for h in range(nblk):
        cols = slice(h * LANES, (h + 1) * LANES)
        uch = uc[:, cols]
        g = _dot(uch.astype(BF16), wg_ref[h])
        a, b = _lru_gate_block(uch, g, ba_ref[:, cols], bx_ref[:, cols], sp[:, cols])
        hn = a * h0_ref[:, cols] + b
        nlru_ref[:, cols] = hn
        hs_ref[:, cols] = hn.astype(BF16)


def _lru_step_call(u2d, buf, h0, cw, cb, wg, ba, bx, lam):
    n = h0.shape[0]
    w = cw.shape[1]
    kw = cw.shape[0]
    nblk = w // LANES
    c2 = lambda i: (0, 0)
    c3 = lambda i: (0, 0, 0)
    return pl.pallas_call(
        _lru_step_kernel,
        grid=(1,),
        in_specs=[pl.BlockSpec((n, w), c2),
                  pl.BlockSpec((kw - 1, n, w), c3),
                  pl.BlockSpec((n, w), c2),
                  pl.BlockSpec((kw, w), c2),
                  pl.BlockSpec((1, w), c2),
                  pl.BlockSpec((nblk, LANES, 2 * LANES), c3),
                  pl.BlockSpec((1, w), c2),
                  pl.BlockSpec((1, w), c2),
                  pl.BlockSpec((1, w), c2)],
        out_specs=[pl.BlockSpec((n, w), c2),
                   pl.BlockSpec((kw - 1, n, w), c3),
                   pl.BlockSpec((n, w), c2)],
        out_shape=[jax.ShapeDtypeStruct((n, w), BF16),
                   jax.ShapeDtypeStruct((kw - 1, n, w), F32),
                   jax.ShapeDtypeStruct((n, w), F32)],
        name="lru_step",
    )(u2d, buf, h0, cw, cb, wg, ba, bx, lam)


def _abar(ar, ai, dt):
    mag = jnp.exp(dt * ar)
    ang = dt * ai
    return mag * jnp.cos(ang), mag * jnp.sin(ang)


def _s5_prep_kernel(ax_ref, dtx_ref, bt_ref, ct_ref, al_ref, dtl_ref, bb_ref, cc_ref, pw_ref):
    ar, ai = ax_ref[0], ax_ref[1]
    abr, abi = _abar(ar, ai, jnp.exp(dtx_ref[...]))
    den = ar * ar + ai * ai
    nr, ni = abr - 1.0, abi
    q_re = (nr * ar + ni * ai) / den
    q_im = (ni * ar - nr * ai) / den
    br, bi = bt_ref[0], bt_ref[1]
    for ref, pair in ((bb_ref, (q_re * br - q_im * bi, q_re * bi + q_im * br)), (cc_ref, (ct_ref[0], ct_ref[1]))):
        for part, v in enumerate(pair):
            ref[part] = jnp.concatenate([v, v], axis=1)

    pr, pi = _abar(al_ref[0], al_ref[1], jnp.exp(dtl_ref[...]))
    qr, qi = jnp.ones_like(pr), jnp.zeros_like(pr)
    for m in range(pw_ref.shape[1]):
        pw_ref[0, m:m + 1, :] = qr
        pw_ref[1, m:m + 1, :] = qi
        qr, qi = qr * pr - qi * pi, qr * pi + qi * pr


def _s5_prep_call(a_re, a_im, log_dt, b_re, b_im, c_re, c_im, n_pow):
    g, p, k = b_re.shape
    assert c_re.shape == (g, k, p) and 2 * p == LANES
    a = jnp.stack([a_re, a_im])
    return pl.pallas_call(
        _s5_prep_kernel,
        out_shape=[jax.ShapeDtypeStruct((2, g * k, 2 * p), F32), jax.ShapeDtypeStruct((2, g * k, 2 * p), F32),
                   jax.ShapeDtypeStruct((2, n_pow, g * p), F32)],
        name="s5_prep",
    )(jnp.repeat(a, k, axis=1), jnp.repeat(log_dt, k).reshape(g * k, 1),
      jnp.stack([b_re, b_im]).swapaxes(2, 3).reshape(2, g * k, p), jnp.stack([c_re, c_im]).reshape(2, g * k, p),
      a.reshape(2, 1, g * p), jnp.repeat(log_dt, p).reshape(1, g * p))


def _block_diag(stack):
    tiled = jnp.concatenate([stack] * (STATES_PER_BLOCK // stack.shape[1]), axis=1)
    own = (lax.broadcasted_iota(jnp.int32, tiled.shape, 0) // SSM_GROUP
           == lax.broadcasted_iota(jnp.int32, tiled.shape, 1) // SSM_STATE)
    return jnp.where(own, tiled, 0.0)


def _cplx_mul(ar, ai, br, bi):
    return ar * br - ai * bi, ar * bi + ai * br


def _cplx_step(ar, ai, hr, hi, br, bi):
    return ar * hr - ai * hi + br, ar * hi + ai * hr + bi


def _s5_kernel(u_ref, bb_ref, cc_ref, pw_ref, d_ref, c_ref, wa_ref, ba_ref,
               gy_ref, nre_ref, nim_ref, ms_ref, mp_ref,
               ufb, buf, wbs, wqs, wts, yslab, out32):
    _mod_kernel(c_ref, wa_ref, ba_ref, ms_ref, mp_ref)
    nbat, seq, _ = u_ref.shape
    spb = STATES_PER_BLOCK
    fold = wbs.shape[0] // LANES
    nq = seq // fold

    bre, bim = _block_diag(bb_ref[0]), _block_diag(bb_ref[1])
    ctre, ctim = _block_diag(cc_ref[0]), _block_diag(cc_ref[1])
    wcb = jnp.concatenate([ctre.T, -ctim.T], axis=0).astype(BF16)
    lag = []
    for m in range(fold):
        wre, wim = _cplx_mul(bre, bim, pw_ref[0, m:m + 1, :], pw_ref[1, m:m + 1, :])
        rows = slice((fold - 1 - m) * LANES, (fold - m) * LANES)
        wbs[rows, :spb] = wre.astype(BF16)
        wbs[rows, spb:] = wim.astype(BF16)
        lag.append(_dot(wbs[rows, :], wcb))
    zero_blk = jnp.zeros((LANES, LANES), BF16)
    for r_in in range(fold):
        for r_out in range(fold):
            blk = lag[r_out - r_in].astype(BF16) if r_in <= r_out else zero_blk
            wts[r_in * LANES:(r_in + 1) * LANES, r_out * LANES:(r_out + 1) * LANES] = blk
    for r in range(fold):
        pr, pi = pw_ref[0, r + 1:r + 2, :], pw_ref[1, r + 1:r + 2, :]
        cols = slice(r * LANES, (r + 1) * LANES)
        wqs[:spb, cols] = (ctre * pr - ctim * pi).T.astype(BF16)
        wqs[spb:, cols] = (-(ctre * pi + ctim * pr)).T.astype(BF16)

    for r in range(fold):
        for b in range(nbat):
            yslab[r, pl.ds(b, nq, stride=nbat), :] = u_ref[b, pl.ds(r, nq, stride=fold), :]
        ufb[:, r * LANES:(r + 1) * LANES] = yslab[r].astype(BF16)

    buf[...] = _dot(ufb[...], wbs[...])

    ar, ai = pw_ref[0, fold:fold + 1, :], pw_ref[1, fold:fold + 1, :]
    groups_per_tile = SUBLANES // nbat
    z = jnp.zeros((nbat, spb), F32)

    def tile_step(i, g):
        gr, gi = g
        r0 = pl.multiple_of(i * SUBLANES, SUBLANES)
        xr, xi = buf[pl.ds(r0, SUBLANES), :spb], buf[pl.ds(r0, SUBLANES), spb:]
        hr, hi = [], []
        for k in range(groups_per_tile):
            hr.append(gr)
            hi.append(gi)
            gr, gi = _cplx_step(ar, ai, gr, gi, xr[k * nbat:(k + 1) * nbat], xi[k * nbat:(k + 1) * nbat])
        buf[pl.ds(r0, SUBLANES), :spb] = jnp.concatenate(hr, axis=0)
        buf[pl.ds(r0, SUBLANES), spb:] = jnp.concatenate(hi, axis=0)
        return gr, gi

    g_re, g_im = lax.fori_loop(0, nq // groups_per_tile, tile_step, (z, z), unroll=2)
    for b in range(nbat):
        nre_ref[b] = g_re[b:b + 1, :]
        nim_ref[b] = g_im[b:b + 1, :]

    hsb = buf[...].astype(BF16)
    per_tile = 2
    for n in range(fold // per_tile):
        cols = slice(n * per_tile * LANES, (n + 1) * per_tile * LANES)
        kk = (n + 1) * per_tile * LANES
        yn = _dot(hsb, wqs[:, cols]) + _dot(ufb[:, :kk], wts[:kk, cols])
        for i in range(per_tile):
            yslab[n * per_tile + i] = yn[:, i * LANES:(i + 1) * LANES]

    for b in range(nbat):
        for r in range(fold):
            ys = (yslab[r, pl.ds(b, nq, stride=nbat), :]
                  + d_ref[...] * u_ref[b, pl.ds(r, nq, stride=fold), :])
            out32[b, pl.ds(r, nq, stride=fold), :] = jax.nn.gelu(ys)
        gy_ref[b] = out32[b].astype(BF16)


def _s5_call(u3d, bb, cc, pw, d, c_all, n_sample, w_ada, b_ada, first_col, *, w):
    bsz, seq, _ = u3d.shape
    nblk = w // LANES
    spb = STATES_PER_BLOCK
    fold = S5_FOLD
    assert SUBLANES % bsz == 0 and seq % fold == 0 and pw.shape[1] == fold + 1
    rows = bsz * (seq // fold)
    n_c, dm = c_all.shape
    mod_cols = w_ada.shape[1] - first_col
    tn = mod_cols // nblk
    assert mod_cols % nblk == 0 and tn % LANES == 0 and first_col % tn == 0
    est = (2 * bsz * seq * LANES * 4 + 2 * bsz * seq * LANES * 2
           + 2 * dm * tn * 4 + dm * tn * 2 + 2 * n_c * dm * 4 + 3 * n_c * tn * 4
           + rows * fold * LANES * 2 + 2 * rows * 2 * spb * 4
           + 3 * fold * LANES * 2 * spb * 2
           + rows * fold * LANES * 4 + bsz * seq * LANES * 4
           + rows * 2 * spb * 2
           + 2 * (LANES * 2 * spb + 2 * spb * LANES) * 4)
    return pl.pallas_call(
        _s5_kernel,
        grid=(nblk,),
        in_specs=[pl.BlockSpec((bsz, seq, LANES), lambda c: (0, 0, nblk + c)),
                  pl.BlockSpec((2, LANES, LANES), lambda c: (0, c, 0)),
                  pl.BlockSpec((2, LANES, LANES), lambda c: (0, c, 0)),
                  pl.BlockSpec((2, fold + 1, spb), lambda c: (0, 0, c)),
                  pl.BlockSpec((1, LANES), lambda c: (0, c)),
                  pl.BlockSpec((n_c, dm), lambda c: (0, 0)),
                  pl.BlockSpec((dm, tn), lambda c: (0, first_col // tn + c)),
                  pl.BlockSpec((1, tn), lambda c: (0, first_col // tn + c))],
        out_specs=[pl.BlockSpec((bsz, seq, LANES), lambda c: (0, 0, c)),
                   pl.BlockSpec((bsz, 1, spb), lambda c: (0, 0, c)),
                   pl.BlockSpec((bsz, 1, spb), lambda c: (0, 0, c)),
                   pl.BlockSpec((n_sample, tn), lambda c: (0, c)),
                   pl.BlockSpec((n_c - n_sample, tn), lambda c: (0, c))],
        out_shape=[jax.ShapeDtypeStruct((bsz, seq, w), BF16),
                   jax.ShapeDtypeStruct((bsz, 1, nblk * spb), F32),
                   jax.ShapeDtypeStruct((bsz, 1, nblk * spb), F32),
                   jax.ShapeDtypeStruct((n_sample, mod_cols), F32),
                   jax.ShapeDtypeStruct((n_c - n_sample, mod_cols), F32)],
        scratch_shapes=[pltpu.VMEM((rows, fold * LANES), BF16),
                        pltpu.VMEM((rows, 2 * spb), F32),
                        pltpu.VMEM((fold * LANES, 2 * spb), BF16),
                        pltpu.VMEM((2 * spb, fold * LANES), BF16),
                        pltpu.VMEM((fold * LANES, fold * LANES), BF16),
                        pltpu.VMEM((fold, rows, LANES), F32),
                        pltpu.VMEM((bsz, seq, LANES), F32)],
        compiler_params=pltpu.CompilerParams(dimension_semantics=("parallel",),
                                             vmem_limit_bytes=_vmem_limit(est)),
        name="s5_scan",
    )(u3d, bb, cc, pw, d, c_all, w_ada, b_ada)


def _s5_step_kernel(u_ref, h0r_ref, h0i_ref, bb_ref, cc_ref, pw_ref, d_ref, gy_ref, nre_ref, nim_ref):
    spb = h0r_ref.shape[0]
    u = u_ref[...]
    ut = u.T.astype(BF16)
    ar = jnp.broadcast_to(pw_ref[0, 1:2, :], (LANES, spb)).T
    ai = jnp.broadcast_to(pw_ref[1, 1:2, :], (LANES, spb)).T
    bu_re = _dot(_block_diag(bb_ref[0]).T.astype(BF16), ut)
    bu_im = _dot(_block_diag(bb_ref[1]).T.astype(BF16), ut)
    hr, hi = _cplx_step(ar, ai, h0r_ref[...], h0i_ref[...], bu_re, bu_im)
    nre_ref[...] = hr
    nim_ref[...] = hi
    yt = (_dot(_block_diag(cc_ref[0]).astype(BF16), hr.astype(BF16))
          - _dot(_block_diag(cc_ref[1]).astype(BF16), hi.astype(BF16)))
    gy_ref[...] = jax.nn.gelu(yt.T + d_ref[...] * u).astype(BF16)


def _s5_step_call(u2d, h0r, h0i, bb, cc, pw, d, *, w):
    n_state, n = h0r.shape
    spb = STATES_PER_BLOCK
    nblk = w // LANES
    assert n_state == nblk * spb and n == LANES
    state_spec = pl.BlockSpec((spb, n), lambda c: (c, 0))
    state_shape = jax.ShapeDtypeStruct((n_state, n), F32)
    return pl.pallas_call(
        _s5_step_kernel,
        grid=(nblk,),
        in_specs=[pl.BlockSpec((n, LANES), lambda c: (0, nblk + c)),
                  state_spec,
                  state_spec,
                  pl.BlockSpec((2, LANES, LANES), lambda c: (0, c, 0)),
                  pl.BlockSpec((2, LANES, LANES), lambda c: (0, c, 0)),
                  pl.BlockSpec((2, pw.shape[1], spb), lambda c: (0, 0, c)),
                  pl.BlockSpec((1, LANES), lambda c: (0, c))],
        out_specs=[pl.BlockSpec((n, LANES), lambda c: (0, c)), state_spec, state_spec],
        out_shape=[jax.ShapeDtypeStruct((n, w), BF16), state_shape, state_shape],
        compiler_params=pltpu.CompilerParams(dimension_semantics=("parallel",)),
        name="s5_step",
    )(u2d, h0r, h0i, bb, cc, pw, d)


def _merge_kernel(*refs, n_side):
    hs_ref, gy_ref, z_ref, x_ref, gt_ref, wp_ref, wg_ref, wo_ref = refs[:8]
    o_ref = refs[8 + n_side]
    _side_cast(refs[8:8 + n_side], refs[9 + n_side:])
    d = x_ref.shape[1]
    ya = _dot(hs_ref[...], wp_ref[...])
    merged = _sigmoid(z_ref[:, :d].astype(F32)) * ya
    glu = _dot(gy_ref[...], wg_ref[...])
    yb = glu[:, :d] * _sigmoid(glu[:, d:])
    merged = merged + _sigmoid(z_ref[:, d:].astype(F32)) * yb
    o = _dot(merged.astype(BF16), wo_ref[...])
    o_ref[...] = x_ref[...] + _rows(gt_ref) * o


def _merge_call(hs, gy, sz, x2d, mod, wp, wg, wo, *, tm, tiles_per_batch, side=()):
    m, d = x2d.shape
    w = hs.shape[1]
    side_in, side_out, side_shapes, side_bytes = _side_cast_specs(side, m // tm)
    est = (2 * (2 * tm * w * 2 + tm * 2 * d * 2 + 2 * tm * d * 4) + (wp.size + wg.size + wo.size) * 2
           + tm * d * 4 * 5 + side_bytes)
    one = pl.Buffered(1)
    return pl.pallas_call(
        functools.partial(_merge_kernel, n_side=len(side)),
        grid=(m // tm,),
        in_specs=[pl.BlockSpec((tm, w), lambda i: (i, 0)),
                  pl.BlockSpec((tm, w), lambda i: (i, 0)),
                  pl.BlockSpec((tm, 2 * d), lambda i: (i, 0)),
                  pl.BlockSpec((tm, d), lambda i: (i, 0)),
                  _mod_spec(mod, 2, d, tiles_per_batch, 1),
                  pl.BlockSpec(wp.shape, lambda i: (0, 0), pipeline_mode=one),
                  pl.BlockSpec(wg.shape, lambda i: (0, 0), pipeline_mode=one),
                  pl.BlockSpec(wo.shape, lambda i: (0, 0), pipeline_mode=one)] + side_in,
        out_specs=[pl.BlockSpec((tm, d), lambda i: (i, 0))] + side_out,
        out_shape=[jax.ShapeDtypeStruct((m, d), F32)] + side_shapes,
        compiler_params=pltpu.CompilerParams(dimension_semantics=("parallel",),
                                             vmem_limit_bytes=_vmem_limit(est)),
        name="merge",
    )(hs, gy, sz, x2d, mod, wp, wg, wo, *(a for a, _ in side))


def _mlp_kernel(x_hbm, sc_ref, sh_ref, gt_ref, g2_ref, gf_ref, wu_ref, wd_ref, o_ref,
                x_buf, x_sem, h_scr, inv_scr, row_scr, *, final_norm):
    n_rows, d = x_buf.shape
    _x_tile_prologue(x_hbm, x_buf, x_sem,
                     lambda: _norm_mod_bf16(x_buf, g2_ref, sc_ref, sh_ref, h_scr, inv_scr, row_scr, x_copy_ref=o_ref))

    up = _dot(h_scr[...], wu_ref[...])
    act = jnp.square(jnp.maximum(up, 0.0)).astype(BF16)
    nc = min(d, MLP_OUT_CHUNK)
    for c in range(d // nc):
        cols = slice(c * nc, (c + 1) * nc)
        gate = gt_ref[0, :, cols] if len(gt_ref.shape) == 3 else gt_ref[:, cols]
        o_ref[:, cols] += gate * _dot(act, wd_ref[:, cols])

    if final_norm:
        @pl.when(pl.program_id(1) == pl.num_programs(1) - 1)
        def _():
            def chunk(rows):
                x2 = o_ref[rows, :]
                ms = jnp.mean(x2 * x2, axis=-1, keepdims=True)
                o_ref[rows, :] = (x2 * lax.rsqrt(ms + EPS)) * gf_ref[...]

            _row_groups(n_rows, min(n_rows, EPILOGUE_ROWS), chunk, unroll=1)


def _mlp_call(x2d, mod, g2, gf, w_up, w_down, *, tm, tiles_per_batch, final_norm, tf):
    m, d = x2d.shape
    dff = w_up.shape[1]
    assert w_up.dtype == BF16 and w_down.dtype == BF16
    est = 3 * tm * d * 4 + tm * d * 2 + 4 * d * tf * 2 + tm * tf * 6 + tm * MLP_OUT_CHUNK * 4
    return pl.pallas_call(
        functools.partial(_mlp_kernel, final_norm=final_norm),
        grid=(m // tm, dff // tf),
        in_specs=[pl.BlockSpec(memory_space=pl.ANY),
                  _mod_spec(mod, 4 - MLP_MOD_FIRST, d, tiles_per_batch, 2),
                  _mod_spec(mod, 3 - MLP_MOD_FIRST, d, tiles_per_batch, 2),
                  _mod_spec(mod, 5 - MLP_MOD_FIRST, d, tiles_per_batch, 2),
                  pl.BlockSpec((1, d), lambda i, j: (0, 0)),
                  pl.BlockSpec((1, d), lambda i, j: (0, 0)),
                  pl.BlockSpec((d, tf), lambda i, j: (0, j)),
                  pl.BlockSpec((tf, d), lambda i, j: (j, 0))],
        out_specs=pl.BlockSpec((tm, d), lambda i, j: (i, 0)),
        out_shape=jax.ShapeDtypeStruct((m, d), F32),
        scratch_shapes=[pltpu.VMEM((tm, d), F32), pltpu.SemaphoreType.DMA(()),
                        pltpu.VMEM((tm, d), BF16), pltpu.VMEM((tm, LANES), F32),
                        pltpu.VMEM((2 * SUBLANES, d), F32)],
        compiler_params=pltpu.CompilerParams(dimension_semantics=("arbitrary", "arbitrary"),
                                             vmem_limit_bytes=_vmem_limit(est)),
        name="mlp",
    )(x2d, mod, mod, mod, g2, gf, w_up, w_down)


def _pick_tile(n, pref):
    t = min(n, pref)
    assert n % t == 0, (n, t)
    return t


def kernel(x_prompt, x_sample, state_conv, state_lru, state_ssm_re, state_ssm_im, c_prompt, c_sample, w_ada, b_ada, g_norm1, g_norm2, w_in, conv_w, conv_b, w_rg_a, b_rg_a, w_rg_x, b_rg_x, lru_lambda, w_proj_a, ssm_a_re, ssm_a_im, ssm_log_dt, ssm_b_re, ssm_b_im, ssm_c_re, ssm_c_im, ssm_d, w_glu, w_out, w_up, w_down, g_final):
    depth = w_ada.shape[0]
    nb, seq, d = x_prompt.shape
    ns = x_sample.shape[0]
    assert x_sample.shape[1] == 1
    w = conv_w.shape[2]
    n_state = ssm_a_re.shape[1] * ssm_a_re.shape[2]
    assert ssm_b_re.shape[2:] == (SSM_STATE, SSM_GROUP) and w % LANES == 0
    assert w_in.shape[2] == 2 * w + 2 * d

    tl = _pick_tile(seq, 256)
    tm_in = _pick_tile(seq, 1024)
    tm_mg = _pick_tile(seq, 256)
    tm_mlp = _pick_tile(seq, 1024)

    xp = x_prompt.reshape(nb * seq, d)
    xs = x_sample.reshape(ns, d)
    assert ns % SUBLANES == 0
    pad = (-nb) % SUBLANES
    c_all = jnp.concatenate([c_sample, c_prompt, jnp.zeros((pad, d), F32)], axis=0)

    outs_p = [[] for _ in range(4)]
    outs_s = [[] for _ in range(4)]
    for l in range(depth):
        last = l == depth - 1
        b_ada_row = b_ada[l].reshape(1, -1)
        mod_s, mod_p = _mod_call(c_all, ns, w_ada[l], b_ada_row, MLP_MOD_FIRST * d)
        mod_p = mod_p.reshape(nb + pad, 1, MLP_MOD_FIRST * d)

        g1 = g_norm1[l].reshape(1, d)
        g2 = g_norm2[l].reshape(1, d)
        gf = g_final.reshape(1, d)
        cw, cb = conv_w[l], conv_b[l].reshape(1, w)
        wg = jnp.concatenate([w_rg_a[l], w_rg_x[l]], axis=2).astype(BF16)
        ba, bx, lam = b_rg_a[l].reshape(1, w), b_rg_x[l].reshape(1, w), lru_lambda[l].reshape(1, w)
        bb, cc, pw = _s5_prep_call(ssm_a_re[l], ssm_a_im[l], ssm_log_dt[l], ssm_b_re[l], ssm_b_im[l],
                                       ssm_c_re[l], ssm_c_im[l], S5_FOLD + 1)
        dskip = ssm_d[l].reshape(1, w)

        u_s, sz_s, w_in_b = _inproj_call(xs, mod_s, g1, w_in[l], tm=ns, tiles_per_batch=1, w_mix=w)

        u_p, sz_p = _inproj_call(xp, mod_p, g1, w_in_b, tm=tm_in, tiles_per_batch=seq // tm_in, w_mix=w)
        u3 = u_p.reshape(nb, seq, 2 * w)
        hs_p, nconv_p, nlru_p, wp, wgl, wo = _lru_call(u3, cw, cb, wg, ba, bx, lam, tl=tl,
                                                       side=((w_proj_a[l], 1), (w_glu[l], 1), (w_out[l], 1)))
        gy_p, nre_p, nim_p, mlp_mod_s, mlp_mod_p = _s5_call(u3, bb, cc, pw, dskip, c_all, ns,
                                                            w_ada[l], b_ada_row, MLP_MOD_FIRST * d, w=w)
        mlp_mod_p = mlp_mod_p.reshape(nb + pad, 1, (N_MOD - MLP_MOD_FIRST) * d)
        x1_p, w_up_b, w_down_b = _merge_call(hs_p.reshape(nb * seq, w), gy_p.reshape(nb * seq, w), sz_p, xp, mod_p,
                                             wp, wgl, wo, tm=tm_mg, tiles_per_batch=seq // tm_mg,
                                             side=((w_up[l], 1), (w_down[l], 0)))
        xp = _mlp_call(x1_p, mlp_mod_p, g2, gf, w_up_b, w_down_b, tm=tm_mlp, tiles_per_batch=seq // tm_mlp,
                       final_norm=last, tf=1024)

        buf = jnp.transpose(state_conv[l], (1, 0, 2))
        hs_s, nconv_s, nlru_s = _lru_step_call(u_s, buf, state_lru[l], cw, cb, wg, ba, bx, lam)
        by_token = lambda s: jnp.transpose(s, (1, 2, 0)).reshape(n_state, ns)
        gy_s, nre_s, nim_s = _s5_step_call(u_s, by_token(state_ssm_re[l]), by_token(state_ssm_im[l]), bb, cc, pw,
                                           dskip, w=w)
        (x1_s,) = _merge_call(hs_s, gy_s, sz_s, xs, mod_s, wp, wgl, wo, tm=ns, tiles_per_batch=1)
        xs = _mlp_call(x1_s, mlp_mod_s, g2, gf, w_up_b, w_down_b, tm=ns, tiles_per_batch=1, final_norm=last,
                       tf=1024)

        gshape = ssm_a_re.shape[1:]
        for acc, v in zip(outs_p, (nconv_p, nlru_p.reshape(nb, w),
                                   nre_p.reshape((nb,) + gshape), nim_p.reshape((nb,) + gshape))):
            acc.append(v)
        by_group = lambda s: jnp.transpose(s.reshape(gshape + (ns,)), (2, 0, 1))
        for acc, v in zip(outs_s, (jnp.transpose(nconv_s, (1, 0, 2)), nlru_s, by_group(nre_s), by_group(nim_s))):
            acc.append(v)

    y_prompt = xp.reshape(nb, seq, d)
    y_sample = xs.reshape(ns, 1, d)
    stack = lambda a: a[0][None] if len(a) == 1 else jnp.stack(a)
    return (y_prompt, y_sample) + tuple(stack(a) for a in outs_p) + tuple(stack(a) for a in outs_s)
```

```python
import functools

import jax
import jax.numpy as jnp
from jax import lax
from jax.experimental import pallas as pl
from jax.experimental.pallas import tpu as pltpu

F32 = jnp.float32
BF16 = jnp.bfloat16

LANES = 128
SUBLANES = 8
BF16_ROWS = 2 * SUBLANES
VMEM_PHYSICAL_BYTES = 64 * 1024 * 1024
VMEM_LIMIT_CAP_BYTES = VMEM_PHYSICAL_BYTES - 6 * 1024 * 1024
VMEM_SLACK_BYTES = 8 * 1024 * 1024

LRU_C = 8.0
EPS = 1e-6
N_MOD = 6
MLP_MOD_FIRST = 3
SSM_GROUP = 16
SSM_STATE = 64
GROUPS_PER_BLOCK = LANES // SSM_GROUP
STATES_PER_BLOCK = GROUPS_PER_BLOCK * SSM_STATE

EPILOGUE_ROWS = 256
MLP_OUT_CHUNK = 512
INPROJ_CHUNK = 256
S5_FOLD = 8


def _vmem_limit(nbytes):
    return int(min(VMEM_LIMIT_CAP_BYTES, max(32 * 1024 * 1024, nbytes + VMEM_SLACK_BYTES)))


def _rows(ref, rows=None):
    if len(ref.shape) == 3:
        return ref[0]
    return ref[...] if rows is None else ref[rows, :]


def _row_groups(n_rows, group, fn, unroll):
    assert n_rows % group == 0

    def body(i, carry):
        fn(pl.ds(pl.multiple_of(i * group, group), group))
        return carry

    lax.fori_loop(0, n_rows // group, body, 0, unroll=unroll)


def _rms_scale_pass(src, inv_scr, d):
    def fn(rows):
        x = src(rows)
        xx = x * x
        part = xx[:, 0:LANES]
        for c in range(1, d // LANES):
            part = part + xx[:, c * LANES:(c + 1) * LANES]
        inv_scr[rows, :] = part

    _row_groups(inv_scr.shape[0], SUBLANES, fn, unroll=8)
    ss = jnp.sum(inv_scr[...], axis=-1, keepdims=True)
    inv_scr[...] = jnp.broadcast_to(lax.rsqrt(ss * (1.0 / d) + EPS), inv_scr.shape)


def _lanes(inv, d):
    return jnp.concatenate([inv] * (d // LANES), axis=1)


def _sublane_rows(scr, slot, row=None):
    rows = slice(slot * SUBLANES, (slot + 1) * SUBLANES)
    if row is None:
        return scr[rows, :]
    scr[rows, :] = jnp.broadcast_to(row, (SUBLANES, row.shape[1]))
    return None


def _norm_mod_bf16(x_ref, g_ref, sc_ref, sh_ref, h_scr, inv_scr, row_scr, x_copy_ref=None):
    n_rows, d = x_ref.shape
    _rms_scale_pass(lambda rows: x_ref[rows, :], inv_scr, d)
    per_token = len(sc_ref.shape) == 2
    if not per_token:
        _sublane_rows(row_scr, 0, g_ref[...] * (1.0 + sc_ref[0]))
        _sublane_rows(row_scr, 1, sh_ref[0])

    def fn(rows):
        halves = []
        for k in range(BF16_ROWS // SUBLANES):
            r8 = pl.ds(pl.multiple_of(rows.start + k * SUBLANES, SUBLANES), SUBLANES)
            gm = g_ref[...] * (1.0 + sc_ref[r8, :]) if per_token else _sublane_rows(row_scr, 0)
            sh = sh_ref[r8, :] if per_token else _sublane_rows(row_scr, 1)
            x = x_ref[r8, :]
            if x_copy_ref is not None:
                x_copy_ref[r8, :] = x
            halves.append((x * _lanes(inv_scr[r8, :], d)) * gm + sh)
        h_scr[rows, :] = jnp.concatenate(halves, axis=0).astype(BF16)

    _row_groups(n_rows, BF16_ROWS, fn, unroll=2)


def _dot(a, b):
    return jnp.dot(a, b, preferred_element_type=F32)


def _sigmoid(x):
    return 0.5 * jnp.tanh(0.5 * x) + 0.5


def _side_cast_specs(side, n_steps):
    in_specs, out_specs, out_shapes, nbytes = [], [], [], 0
    for arr, axis in side:
        blk = tuple(s // n_steps if a == axis else s for a, s in enumerate(arr.shape))
        assert arr.ndim == 2 and arr.shape[axis] % n_steps == 0 and blk[1] % LANES == 0 and blk[0] % BF16_ROWS == 0
        idx = (lambda i: (i, 0)) if axis == 0 else (lambda i: (0, i))
        in_specs.append(pl.BlockSpec(blk, idx))
        out_specs.append(pl.BlockSpec(blk, idx))
        out_shapes.append(jax.ShapeDtypeStruct(arr.shape, BF16))
        nbytes += 2 * blk[0] * blk[1] * (4 + 2)
    return in_specs, out_specs, out_shapes, nbytes


def _side_cast(src_refs, dst_refs):
    for src, dst in zip(src_refs, dst_refs, strict=True):
        dst[...] = src[...].astype(BF16)


def _mod_kernel(c_ref, w_ref, b_ref, os_ref, op_ref):
    c = c_ref[...]
    cs = (c * _sigmoid(c)).astype(BF16)
    mod = _dot(cs, w_ref[...].astype(BF16)) + b_ref[...]
    n_sample = os_ref.shape[0]
    os_ref[...] = mod[:n_sample]
    op_ref[...] = mod[n_sample:]


def _mod_call(c_all, n_sample, w_ada, b_ada, n, tn=1024):
    n_rows, d = c_all.shape
    assert n % tn == 0
    est = 2 * (d * tn * 4) + 3 * n_rows * tn * 4 + n_rows * d * 4 * 2 + d * tn * 2
    return pl.pallas_call(
        _mod_kernel,
        grid=(n // tn,),
        in_specs=[pl.BlockSpec((n_rows, d), lambda j: (0, 0)),
                  pl.BlockSpec((d, tn), lambda j: (0, j)),
                  pl.BlockSpec((1, tn), lambda j: (0, j))],
        out_specs=[pl.BlockSpec((n_sample, tn), lambda j: (0, j)),
                   pl.BlockSpec((n_rows - n_sample, tn), lambda j: (0, j))],
        out_shape=[jax.ShapeDtypeStruct((n_sample, n), F32),
                   jax.ShapeDtypeStruct((n_rows - n_sample, n), F32)],
        compiler_params=pltpu.CompilerParams(dimension_semantics=("parallel",),
                                             vmem_limit_bytes=_vmem_limit(est)),
        name="mod",
    )(c_all, w_ada, b_ada)


def _mod_spec(mod, piece, d, tiles_per_batch, ngrid):
    if mod.ndim == 3:
        if ngrid == 1:
            return pl.BlockSpec((1, 1, d), lambda i: (i // tiles_per_batch, 0, piece))
        return pl.BlockSpec((1, 1, d), lambda i, j: (i // tiles_per_batch, 0, piece))
    rows = mod.shape[0]
    if ngrid == 1:
        return pl.BlockSpec((rows, d), lambda i: (0, piece))
    return pl.BlockSpec((rows, d), lambda i, j: (0, piece))


def _x_tile_prologue(x_hbm, x_buf, x_sem, prologue):
    i, j = pl.program_id(0), pl.program_id(1)
    n_rows = x_buf.shape[0]

    def x_tile(tile):
        return pltpu.make_async_copy(x_hbm.at[pl.ds(tile * n_rows, n_rows), :], x_buf, x_sem)

    @pl.when(j == 0)
    def _():
        @pl.when(i == 0)
        def _():
            x_tile(0).start()

        x_tile(i).wait()
        prologue()

        @pl.when(i + 1 < pl.num_programs(0))
        def _():
            x_tile(i + 1).start()


def _inproj_kernel(*refs, n_u_tiles, emit_w):
    x_hbm, sc_ref, sh_ref, g_ref, w_ref, u_ref, z_ref = refs[:7]
    x_buf, x_sem, h_scr, inv_scr, row_scr = refs[-5:]
    j = pl.program_id(1)
    _x_tile_prologue(x_hbm, x_buf, x_sem,
                     lambda: _norm_mod_bf16(x_buf, g_ref, sc_ref, sh_ref, h_scr, inv_scr, row_scr))

    if emit_w:
        wb_ref = refs[7]
        wb_ref[...] = w_ref[...].astype(BF16)
        w_ref = wb_ref
    tn = w_ref.shape[1]
    nc = min(tn, INPROJ_CHUNK)

    @pl.when(j < n_u_tiles)
    def _():
        u_ref[...] = _dot(h_scr[...], w_ref[...])

    @pl.when(j >= n_u_tiles)
    def _():
        for c in range(tn // nc):
            cols = slice(c * nc, (c + 1) * nc)
            z_ref[:, cols] = _dot(h_scr[...], w_ref[:, cols]).astype(BF16)


def _inproj_call(x2d, mod, g1, w_in, *, tm, tiles_per_batch, w_mix, tn=1024):
    m, d = x2d.shape
    n = w_in.shape[1]
    n_u = 2 * w_mix
    n_u_tiles = n_u // tn
    emit_w = w_in.dtype != BF16
    assert not emit_w or m == tm
    wbytes = w_in.dtype.itemsize
    est = (tm * d * 4 + 2 * d * tn * wbytes + 2 * tm * tn * 4 + 2 * tm * tn * 2 + tm * d * 2 + tm * tn * 4
           + (3 * d * tn * 2 if emit_w else 0))
    out_specs = [pl.BlockSpec((tm, tn), lambda i, j: (i, jnp.minimum(j, n_u_tiles - 1))),
                 pl.BlockSpec((tm, tn), lambda i, j: (i, jnp.maximum(j - n_u_tiles, 0)))]
    out_shape = [jax.ShapeDtypeStruct((m, n_u), F32),
                 jax.ShapeDtypeStruct((m, n - n_u), BF16)]
    if emit_w:
        out_specs.append(pl.BlockSpec((d, tn), lambda i, j: (0, j)))
        out_shape.append(jax.ShapeDtypeStruct((d, n), BF16))
    return pl.pallas_call(
        functools.partial(_inproj_kernel, n_u_tiles=n_u_tiles, emit_w=emit_w),
        grid=(m // tm, n // tn),
        in_specs=[pl.BlockSpec(memory_space=pl.ANY),
                  _mod_spec(mod, 1, d, tiles_per_batch, 2),
                  _mod_spec(mod, 0, d, tiles_per_batch, 2),
                  pl.BlockSpec((1, d), lambda i, j: (0, 0)),
                  pl.BlockSpec((d, tn), lambda i, j: (0, j))],
        out_specs=out_specs,
        out_shape=out_shape,
        scratch_shapes=[pltpu.VMEM((tm, d), F32), pltpu.SemaphoreType.DMA(()),
                        pltpu.VMEM((tm, d), BF16), pltpu.VMEM((tm, LANES), F32),
                        pltpu.VMEM((2 * SUBLANES, d), F32)],
        compiler_params=pltpu.CompilerParams(dimension_semantics=("arbitrary", "arbitrary"),
                                             vmem_limit_bytes=_vmem_limit(est)),
        name="inproj",
    )(x2d, mod, mod, g1, w_in)


def _lru_gate_block(uc, g, ba, bx, sp):
    r = _sigmoid(g[:, :LANES] + ba)
    i = _sigmoid(g[:, LANES:] + bx)
    log_a = (-LRU_C * r) * sp
    a = jnp.exp(log_a)
    m2 = 1.0 - jnp.exp(2.0 * log_a)
    mult = jnp.where(m2 > 0.0, m2 * lax.rsqrt(m2), 0.0)
    return a, (mult * i) * uc


def _lru_kernel(*refs, n_side):
    u_ref, cw_ref, cb_ref, wg_ref, ba_ref, bx_ref, lam_ref = refs[:7]
    hs_ref, nconv_ref, nlru_ref = refs[7 + n_side:10 + n_side]
    ext, a_s, b_s, carry = refs[-4:]
    _side_cast(refs[7:7 + n_side], refs[10 + n_side:10 + 2 * n_side])
    t = pl.program_id(0)
    nt = pl.num_programs(0)
    nbat, tl, w = u_ref.shape
    nblk = w // LANES
    rows = nbat * tl
    kw = cw_ref.shape[0]
    hist = ext.shape[1] - rows
    steps_per_tile = SUBLANES // nbat

    @pl.when(t == 0)
    def _():
        ext[:, 0:hist, :] = jnp.zeros((nblk, hist, LANES), F32)
        carry[...] = jnp.zeros_like(carry)

    for b in range(nbat):
        for c in range(nblk):
            ext[c, pl.ds(hist + b, tl, stride=nbat), :] = u_ref[b, :, c * LANES:(c + 1) * LANES]

        @pl.when(t == nt - 1)
        def _(b=b):
            nconv_ref[b] = u_ref[b, tl - (kw - 1):tl, :]

    sp = jax.nn.softplus(-lam_ref[...])
    for c in range(nblk):
        cols = slice(c * LANES, (c + 1) * LANES)
        uc = cb_ref[:, cols]
        for k in range(kw):
            start = hist - (kw - 1 - k) * nbat
            uc = uc + ext[c, start:start + rows, :] * cw_ref[k:k + 1, cols]
        g = _dot(uc.astype(BF16), wg_ref[c])
        a, b = _lru_gate_block(uc, g, ba_ref[:, cols], bx_ref[:, cols], sp[:, cols])
        a_s[:, cols] = a
        b_s[:, cols] = b
        ext[c, hist - (kw - 1) * nbat:hist, :] = ext[c, hist + rows - (kw - 1) * nbat:hist + rows, :]

    def tile_step(i, hcur):
        r0 = pl.multiple_of(i * SUBLANES, SUBLANES)
        at, bt = a_s[pl.ds(r0, SUBLANES), :], b_s[pl.ds(r0, SUBLANES), :]
        hs = []
        for k in range(steps_per_tile):
            hcur = at[k * nbat:(k + 1) * nbat] * hcur + bt[k * nbat:(k + 1) * nbat]
            hs.append(hcur)
        htile = jnp.concatenate(hs, axis=0)
        for c in range(nblk):
            ext[c, pl.ds(pl.multiple_of(hist + r0, SUBLANES), SUBLANES), :] = htile[:, c * LANES:(c + 1) * LANES]
        return hcur

    h_end = lax.fori_loop(0, tl // steps_per_tile, tile_step, carry[...], unroll=2)
    carry[...] = h_end

    for b in range(nbat):
        for c in range(nblk):
            hs_ref[b, :, c * LANES:(c + 1) * LANES] = ext[c, pl.ds(hist + b, tl, stride=nbat), :].astype(BF16)

    @pl.when(t == nt - 1)
    def _():
        for b in range(nbat):
            nlru_ref[b] = h_end[b:b + 1, :]


def _lru_call(u3d, cw, cb, wg, ba, bx, lam, *, tl, side=()):
    bsz, seq, _ = u3d.shape
    w = cw.shape[1]
    kw = cw.shape[0]
    nblk = w // LANES
    assert SUBLANES % bsz == 0
    rows = bsz * tl
    hist = -(-(kw - 1) * bsz // SUBLANES) * SUBLANES
    side_in, side_out, side_shapes, side_bytes = _side_cast_specs(side, seq // tl)
    est = (2 * rows * w * 4 + 2 * rows * w * 2 + (hist + rows) * w * 4 + 2 * rows * w * 4
           + nblk * LANES * 2 * LANES * 2 * 2 + 2 * rows * 2 * LANES * 4 + side_bytes)
    const2 = lambda t: (0, 0)
    return pl.pallas_call(
        functools.partial(_lru_kernel, n_side=len(side)),
        grid=(seq // tl,),
        in_specs=[pl.BlockSpec((bsz, tl, w), lambda t: (0, t, 0)),
                  pl.BlockSpec((kw, w), const2),
                  pl.BlockSpec((1, w), const2),
                  pl.BlockSpec((nblk, LANES, 2 * LANES), lambda t: (0, 0, 0)),
                  pl.BlockSpec((1, w), const2),
                  pl.BlockSpec((1, w), const2),
                  pl.BlockSpec((1, w), const2)] + side_in,
        out_specs=[pl.BlockSpec((bsz, tl, w), lambda t: (0, t, 0)),
                   pl.BlockSpec((bsz, kw - 1, w), lambda t: (0, 0, 0)),
                   pl.BlockSpec((bsz, 1, w), lambda t: (0, 0, 0))] + side_out,
        out_shape=[jax.ShapeDtypeStruct((bsz, seq, w), BF16),
                   jax.ShapeDtypeStruct((bsz, kw - 1, w), F32),
                   jax.ShapeDtypeStruct((bsz, 1, w), F32)] + side_shapes,
        scratch_shapes=[pltpu.VMEM((nblk, hist + rows, LANES), F32),
                        pltpu.VMEM((rows, w), F32),
                        pltpu.VMEM((rows, w), F32),
                        pltpu.VMEM((bsz, w), F32)],
        compiler_params=pltpu.CompilerParams(dimension_semantics=("arbitrary",),
                                             vmem_limit_bytes=_vmem_limit(est)),
        name="lru_scan",
    )(u3d, cw, cb, wg, ba, bx, lam, *(a for a, _ in side))


def _lru_step_kernel(u_ref, buf_ref, h0_ref, cw_ref, cb_ref, wg_ref, ba_ref, bx_ref, lam_ref,
                     hs_ref, nconv_ref, nlru_ref):
    w = cw_ref.shape[1]
    kw = cw_ref.shape[0]
    nblk = w // LANES
    u = u_ref[...]
    uc = cb_ref[...] + buf_ref[0] * cw_ref[0:1, :]
    for k in range(1, kw - 1):
        uc = uc + buf_ref[k] * cw_ref[k:k + 1, :]
    uc = uc + u * cw_ref[kw - 1:kw, :]
    for k in range(kw - 2):
        nconv_ref[k] = buf_ref[k + 1]
    nconv_ref[kw - 2] = u
    sp = jax.nn.softplus(-lam_ref[...])
    for h in range(nblk):
        cols = slice(h * LANES, (h + 1) * LANES)
        uch = uc[:, cols]
        g = _dot(uch.astype(BF16), wg_ref[h])
        a, b = _lru_gate_block(uch, g, ba_ref[:, cols], bx_ref[:, cols], sp[:, cols])
        hn = a * h0_ref[:, cols] + b
        nlru_ref[:, cols] = hn
        hs_ref[:, cols] = hn.astype(BF16)


def _lru_step_call(u2d, buf, h0, cw, cb, wg, ba, bx, lam):
    n = h0.shape[0]
    w = cw.shape[1]
    kw = cw.shape[0]
    nblk = w // LANES
    c2 = lambda i: (0, 0)
    c3 = lambda i: (0, 0, 0)
    return pl.pallas_call(
        _lru_step_kernel,
        grid=(1,),
        in_specs=[pl.BlockSpec((n, w), c2),
                  pl.BlockSpec((kw - 1, n, w), c3),
                  pl.BlockSpec((n, w), c2),
                  pl.BlockSpec((kw, w), c2),
                  pl.BlockSpec((1, w), c2),
                  pl.BlockSpec((nblk, LANES, 2 * LANES), c3),
                  pl.BlockSpec((1, w), c2),
                  pl.BlockSpec((1, w), c2),
                  pl.BlockSpec((1, w), c2)],
        out_specs=[pl.BlockSpec((n, w), c2),
                   pl.BlockSpec((kw - 1, n, w), c3),
                   pl.BlockSpec((n, w), c2)],
        out_shape=[jax.ShapeDtypeStruct((n, w), BF16),
                   jax.ShapeDtypeStruct((kw - 1, n, w), F32),
                   jax.ShapeDtypeStruct((n, w), F32)],
        name="lru_step",
    )(u2d, buf, h0, cw, cb, wg, ba, bx, lam)


def _abar(ar, ai, dt):
    mag = jnp.exp(dt * ar)
    ang = dt * ai
    return mag * jnp.cos(ang), mag * jnp.sin(ang)


def _s5_prep_kernel(ax_ref, dtx_ref, bt_ref, ct_ref, al_ref, dtl_ref, bb_ref, cc_ref, pw_ref):
    ar, ai = ax_ref[0], ax_ref[1]
    abr, abi = _abar(ar, ai, jnp.exp(dtx_ref[...]))
    den = ar * ar + ai * ai
    nr, ni = abr - 1.0, abi
    q_re = (nr * ar + ni * ai) / den
    q_im = (ni * ar - nr * ai) / den
    br, bi = bt_ref[0], bt_ref[1]
    for ref, pair in ((bb_ref, (q_re * br - q_im * bi, q_re * bi + q_im * br)), (cc_ref, (ct_ref[0], ct_ref[1]))):
        for part, v in enumerate(pair):
            ref[part] = jnp.concatenate([v, v], axis=1)

    pr, pi = _abar(al_ref[0], al_ref[1], jnp.exp(dtl_ref[...]))
    qr, qi = jnp.ones_like(pr), jnp.zeros_like(pr)
    for m in range(pw_ref.shape[1]):
        pw_ref[0, m:m + 1, :] = qr
        pw_ref[1, m:m + 1, :] = qi
        qr, qi = qr * pr - qi * pi, qr * pi + qi * pr


def _s5_prep_call(a_re, a_im, log_dt, b_re, b_im, c_re, c_im, n_pow):
    g, p, k = b_re.shape
    assert c_re.shape == (g, k, p) and 2 * p == LANES
    a = jnp.stack([a_re, a_im])
    return pl.pallas_call(
        _s5_prep_kernel,
        out_shape=[jax.ShapeDtypeStruct((2, g * k, 2 * p), F32), jax.ShapeDtypeStruct((2, g * k, 2 * p), F32),
                   jax.ShapeDtypeStruct((2, n_pow, g * p), F32)],
        name="s5_prep",
    )(jnp.repeat(a, k, axis=1), jnp.repeat(log_dt, k).reshape(g * k, 1),
      jnp.stack([b_re, b_im]).swapaxes(2, 3).reshape(2, g * k, p), jnp.stack([c_re, c_im]).reshape(2, g * k, p),
      a.reshape(2, 1, g * p), jnp.repeat(log_dt, p).reshape(1, g * p))


def _block_diag(stack):
    tiled = jnp.concatenate([stack] * (STATES_PER_BLOCK // stack.shape[1]), axis=1)
    own = (lax.broadcasted_iota(jnp.int32, tiled.shape, 0) // SSM_GROUP
           == lax.broadcasted_iota(jnp.int32, tiled.shape, 1) // SSM_STATE)
    return jnp.where(own, tiled, 0.0)


def _cplx_mul(ar, ai, br, bi):
    return ar * br - ai * bi, ar * bi + ai * br


def _cplx_step(ar, ai, hr, hi, br, bi):
    return ar * hr - ai * hi + br, ar * hi + ai * hr + bi


def _s5_one_step(us_ref, h0r_ref, h0i_ref, bre, bim, ctre, ctim, ar, ai, d_ref, gys_ref, nres_ref, nims_ref):
    us = us_ref[...]
    ust = us.T.astype(BF16)
    ar = jnp.broadcast_to(ar, bre.shape).T
    ai = jnp.broadcast_to(ai, bre.shape).T
    hr, hi = _cplx_step(ar, ai, h0r_ref[...], h0i_ref[...],
                        _dot(bre.T.astype(BF16), ust), _dot(bim.T.astype(BF16), ust))
    nres_ref[...] = hr
    nims_ref[...] = hi
    yt = _dot(ctre.astype(BF16), hr.astype(BF16)) - _dot(ctim.astype(BF16), hi.astype(BF16))
    gys_ref[...] = jax.nn.gelu(yt.T + d_ref[...] * us).astype(BF16)


def _s5_kernel(u_ref, bb_ref, cc_ref, pw_ref, d_ref, c_ref, wa_ref, ba_ref, us_ref, h0r_ref, h0i_ref,
               gy_ref, nre_ref, nim_ref, ms_ref, mp_ref, gys_ref, nres_ref, nims_ref,
               ufb, buf, wbs, wqs, wts, yslab, out32):
    _mod_kernel(c_ref, wa_ref, ba_ref, ms_ref, mp_ref)
    nbat, seq, _ = u_ref.shape
    spb = STATES_PER_BLOCK
    fold = wbs.shape[0] // LANES
    nq = seq // fold

    bre, bim = _block_diag(bb_ref[0]), _block_diag(bb_ref[1])
    ctre, ctim = _block_diag(cc_ref[0]), _block_diag(cc_ref[1])
    _s5_one_step(us_ref, h0r_ref, h0i_ref, bre, bim, ctre, ctim, pw_ref[0, 1:2, :], pw_ref[1, 1:2, :], d_ref,
                 gys_ref, nres_ref, nims_ref)
    wcb = jnp.concatenate([ctre.T, -ctim.T], axis=0).astype(BF16)
    lag = []
    for m in range(fold):
        wre, wim = _cplx_mul(bre, bim, pw_ref[0, m:m + 1, :], pw_ref[1, m:m + 1, :])
        rows = slice((fold - 1 - m) * LANES, (fold - m) * LANES)
        wbs[rows, :spb] = wre.astype(BF16)
        wbs[rows, spb:] = wim.astype(BF16)
        lag.append(_dot(wbs[rows, :], wcb))
    zero_blk = jnp.zeros((LANES, LANES), BF16)
    for r_in in range(fold):
        for r_out in range(fold):
            blk = lag[r_out - r_in].astype(BF16) if r_in <= r_out else zero_blk
            wts[r_in * LANES:(r_in + 1) * LANES, r_out * LANES:(r_out + 1) * LANES] = blk
    for r in range(fold):
        pr, pi = pw_ref[0, r + 1:r + 2, :], pw_ref[1, r + 1:r + 2, :]
        cols = slice(r * LANES, (r + 1) * LANES)
        wqs[:spb, cols] = (ctre * pr - ctim * pi).T.astype(BF16)
        wqs[spb:, cols] = (-(ctre * pi + ctim * pr)).T.astype(BF16)

    for r in range(fold):
        for b in range(nbat):
            yslab[r, pl.ds(b, nq, stride=nbat), :] = u_ref[b, pl.ds(r, nq, stride=fold), :]
        ufb[:, r * LANES:(r + 1) * LANES] = yslab[r].astype(BF16)

    buf[...] = _dot(ufb[...], wbs[...])

    ar, ai = pw_ref[0, fold:fold + 1, :], pw_ref[1, fold:fold + 1, :]
    groups_per_tile = SUBLANES // nbat
    z = jnp.zeros((nbat, spb), F32)

    def tile_step(i, g):
        gr, gi = g
        r0 = pl.multiple_of(i * SUBLANES, SUBLANES)
        xr, xi = buf[pl.ds(r0, SUBLANES), :spb], buf[pl.ds(r0, SUBLANES), spb:]
        hr, hi = [], []
        for k in range(groups_per_tile):
            hr.append(gr)
            hi.append(gi)
            gr, gi = _cplx_step(ar, ai, gr, gi, xr[k * nbat:(k + 1) * nbat], xi[k * nbat:(k + 1) * nbat])
        buf[pl.ds(r0, SUBLANES), :spb] = jnp.concatenate(hr, axis=0)
        buf[pl.ds(r0, SUBLANES), spb:] = jnp.concatenate(hi, axis=0)
        return gr, gi

    g_re, g_im = lax.fori_loop(0, nq // groups_per_tile, tile_step, (z, z), unroll=2)
    for b in range(nbat):
        nre_ref[b] = g_re[b:b + 1, :]
        nim_ref[b] = g_im[b:b + 1, :]

    hsb = buf[...].astype(BF16)
    per_tile = 2
    for n in range(fold // per_tile):
        cols = slice(n * per_tile * LANES, (n + 1) * per_tile * LANES)
        kk = (n + 1) * per_tile * LANES
        yn = _dot(hsb, wqs[:, cols]) + _dot(ufb[:, :kk], wts[:kk, cols])
        for i in range(per_tile):
            yslab[n * per_tile + i] = yn[:, i * LANES:(i + 1) * LANES]

    for b in range(nbat):
        for r in range(fold):
            ys = (yslab[r, pl.ds(b, nq, stride=nbat), :]
                  + d_ref[...] * u_ref[b, pl.ds(r, nq, stride=fold), :])
            out32[b, pl.ds(r, nq, stride=fold), :] = jax.nn.gelu(ys)
        gy_ref[b] = out32[b].astype(BF16)


def _s5_call(u3d, bb, cc, pw, d, c_all, n_sample, w_ada, b_ada, first_col, u_step, h0r, h0i, *, w):
    bsz, seq, _ = u3d.shape
    nblk = w // LANES
    spb = STATES_PER_BLOCK
    fold = S5_FOLD
    assert SUBLANES % bsz == 0 and seq % fold == 0 and pw.shape[1] == fold + 1
    n_state, n_step = h0r.shape
    assert n_state == nblk * spb and n_step == LANES
    state_spec = pl.BlockSpec((spb, n_step), lambda c: (c, 0))
    state_shape = jax.ShapeDtypeStruct((n_state, n_step), F32)
    rows = bsz * (seq // fold)
    n_c, dm = c_all.shape
    mod_cols = w_ada.shape[1] - first_col
    tn = mod_cols // nblk
    assert mod_cols % nblk == 0 and tn % LANES == 0 and first_col % tn == 0
    est = (2 * bsz * seq * LANES * 4 + 2 * bsz * seq * LANES * 2
           + 2 * dm * tn * 4 + dm * tn * 2 + 2 * n_c * dm * 4 + 3 * n_c * tn * 4
           + rows * fold * LANES * 2 + 2 * rows * 2 * spb * 4
           + 3 * fold * LANES * 2 * spb * 2
           + rows * fold * LANES * 4 + bsz * seq * LANES * 4
           + rows * 2 * spb * 2
           + 2 * (LANES * 2 * spb + 2 * spb * LANES) * 4)
    return pl.pallas_call(
        _s5_kernel,
        grid=(nblk,),
        in_specs=[pl.BlockSpec((bsz, seq, LANES), lambda c: (0, 0, nblk + c)),
                  pl.BlockSpec((2, LANES, LANES), lambda c: (0, c, 0)),
                  pl.BlockSpec((2, LANES, LANES), lambda c: (0, c, 0)),
                  pl.BlockSpec((2, fold + 1, spb), lambda c: (0, 0, c)),
                  pl.BlockSpec((1, LANES), lambda c: (0, c)),
                  pl.BlockSpec((n_c, dm), lambda c: (0, 0)),
                  pl.BlockSpec((dm, tn), lambda c: (0, first_col // tn + c)),
                  pl.BlockSpec((1, tn), lambda c: (0, first_col // tn + c)),
                  pl.BlockSpec((n_step, LANES), lambda c: (0, nblk + c)),
                  state_spec,
                  state_spec],
        out_specs=[pl.BlockSpec((bsz, seq, LANES), lambda c: (0, 0, c)),
                   pl.BlockSpec((bsz, 1, spb), lambda c: (0, 0, c)),
                   pl.BlockSpec((bsz, 1, spb), lambda c: (0, 0, c)),
                   pl.BlockSpec((n_sample, tn), lambda c: (0, c)),
                   pl.BlockSpec((n_c - n_sample, tn), lambda c: (0, c)),
                   pl.BlockSpec((n_step, LANES), lambda c: (0, c)),
                   state_spec,
                   state_spec],
        out_shape=[jax.ShapeDtypeStruct((bsz, seq, w), BF16),
                   jax.ShapeDtypeStruct((bsz, 1, nblk * spb), F32),
                   jax.ShapeDtypeStruct((bsz, 1, nblk * spb), F32),
                   jax.ShapeDtypeStruct((n_sample, mod_cols), F32),
                   jax.ShapeDtypeStruct((n_c - n_sample, mod_cols), F32),
                   jax.ShapeDtypeStruct((n_step, w), BF16),
                   state_shape,
                   state_shape],
        scratch_shapes=[pltpu.VMEM((rows, fold * LANES), BF16),
                        pltpu.VMEM((rows, 2 * spb), F32),
                        pltpu.VMEM((fold * LANES, 2 * spb), BF16),
                        pltpu.VMEM((2 * spb, fold * LANES), BF16),
                        pltpu.VMEM((fold * LANES, fold * LANES), BF16),
                        pltpu.VMEM((fold, rows, LANES), F32),
                        pltpu.VMEM((bsz, seq, LANES), F32)],
        compiler_params=pltpu.CompilerParams(dimension_semantics=("parallel",),
                                             vmem_limit_bytes=_vmem_limit(est)),
        name="s5_scan",
    )(u3d, bb, cc, pw, d, c_all, w_ada, b_ada, u_step, h0r, h0i)


def _merge_kernel(*refs, n_side):
    hs_ref, gy_ref, z_ref, x_ref, gt_ref, wp_ref, wg_ref, wo_ref = refs[:8]
    o_ref = refs[8 + n_side]
    _side_cast(refs[8:8 + n_side], refs[9 + n_side:])
    d = x_ref.shape[1]
    ya = _dot(hs_ref[...], wp_ref[...])
    merged = _sigmoid(z_ref[:, :d].astype(F32)) * ya
    glu = _dot(gy_ref[...], wg_ref[...])
    yb = glu[:, :d] * _sigmoid(glu[:, d:])
    merged = merged + _sigmoid(z_ref[:, d:].astype(F32)) * yb
    o = _dot(merged.astype(BF16), wo_ref[...])
    o_ref[...] = x_ref[...] + _rows(gt_ref) * o


def _merge_call(hs, gy, sz, x2d, mod, wp, wg, wo, *, tm, tiles_per_batch, side=()):
    m, d = x2d.shape
    w = hs.shape[1]
    side_in, side_out, side_shapes, side_bytes = _side_cast_specs(side, m // tm)
    est = (2 * (2 * tm * w * 2 + tm * 2 * d * 2 + 2 * tm * d * 4) + (wp.size + wg.size + wo.size) * 2
           + tm * d * 4 * 5 + side_bytes)
    one = pl.Buffered(1)
    return pl.pallas_call(
        functools.partial(_merge_kernel, n_side=len(side)),
        grid=(m // tm,),
        in_specs=[pl.BlockSpec((tm, w), lambda i: (i, 0)),
                  pl.BlockSpec((tm, w), lambda i: (i, 0)),
                  pl.BlockSpec((tm, 2 * d), lambda i: (i, 0)),
                  pl.BlockSpec((tm, d), lambda i: (i, 0)),
                  _mod_spec(mod, 2, d, tiles_per_batch, 1),
                  pl.BlockSpec(wp.shape, lambda i: (0, 0), pipeline_mode=one),
                  pl.BlockSpec(wg.shape, lambda i: (0, 0), pipeline_mode=one),
                  pl.BlockSpec(wo.shape, lambda i: (0, 0), pipeline_mode=one)] + side_in,
        out_specs=[pl.BlockSpec((tm, d), lambda i: (i, 0))] + side_out,
        out_shape=[jax.ShapeDtypeStruct((m, d), F32)] + side_shapes,
        compiler_params=pltpu.CompilerParams(dimension_semantics=("parallel",),
                                             vmem_limit_bytes=_vmem_limit(est)),
        name="merge",
    )(hs, gy, sz, x2d, mod, wp, wg, wo, *(a for a, _ in side))


def _mlp_kernel(x_hbm, sc_ref, sh_ref, gt_ref, g2_ref, gf_ref, wu_ref, wd_ref, o_ref,
                x_buf, x_sem, h_scr, inv_scr, row_scr, *, final_norm):
    n_rows, d = x_buf.shape
    _x_tile_prologue(x_hbm, x_buf, x_sem,
                     lambda: _norm_mod_bf16(x_buf, g2_ref, sc_ref, sh_ref, h_scr, inv_scr, row_scr, x_copy_ref=o_ref))

    up = _dot(h_scr[...], wu_ref[...])
    act = jnp.square(jnp.maximum(up, 0.0)).astype(BF16)
    nc = min(d, MLP_OUT_CHUNK)
    for c in range(d // nc):
        cols = slice(c * nc, (c + 1) * nc)
        gate = gt_ref[0, :, cols] if len(gt_ref.shape) == 3 else gt_ref[:, cols]
        o_ref[:, cols] += gate * _dot(act, wd_ref[:, cols])

    if final_norm:
        @pl.when(pl.program_id(1) == pl.num_programs(1) - 1)
        def _():
            def chunk(rows):
                x2 = o_ref[rows, :]
                ms = jnp.mean(x2 * x2, axis=-1, keepdims=True)
                o_ref[rows, :] = (x2 * lax.rsqrt(ms + EPS)) * gf_ref[...]

            _row_groups(n_rows, min(n_rows, EPILOGUE_ROWS), chunk, unroll=1)


def _mlp_call(x2d, mod, g2, gf, w_up, w_down, *, tm, tiles_per_batch, final_norm, tf):
    m, d = x2d.shape
    dff = w_up.shape[1]
    assert w_up.dtype == BF16 and w_down.dtype == BF16
    est = 3 * tm * d * 4 + tm * d * 2 + 4 * d * tf * 2 + tm * tf * 6 + tm * MLP_OUT_CHUNK * 4
    return pl.pallas_call(
        functools.partial(_mlp_kernel, final_norm=final_norm),
        grid=(m // tm, dff // tf),
        in_specs=[pl.BlockSpec(memory_space=pl.ANY),
                  _mod_spec(mod, 4 - MLP_MOD_FIRST, d, tiles_per_batch, 2),
                  _mod_spec(mod, 3 - MLP_MOD_FIRST, d, tiles_per_batch, 2),
                  _mod_spec(mod, 5 - MLP_MOD_FIRST, d, tiles_per_batch, 2),
                  pl.BlockSpec((1, d), lambda i, j: (0, 0)),
                  pl.BlockSpec((1, d), lambda i, j: (0, 0)),
                  pl.BlockSpec((d, tf), lambda i, j: (0, j)),
                  pl.BlockSpec((tf, d), lambda i, j: (j, 0))],
        out_specs=pl.BlockSpec((tm, d), lambda i, j: (i, 0)),
        out_shape=jax.ShapeDtypeStruct((m, d), F32),
        scratch_shapes=[pltpu.VMEM((tm, d), F32), pltpu.SemaphoreType.DMA(()),
                        pltpu.VMEM((tm, d), BF16), pltpu.VMEM((tm, LANES), F32),
                        pltpu.VMEM((2 * SUBLANES, d), F32)],
        compiler_params=pltpu.CompilerParams(dimension_semantics=("arbitrary", "arbitrary"),
                                             vmem_limit_bytes=_vmem_limit(est)),
        name="mlp",
    )(x2d, mod, mod, mod, g2, gf, w_up, w_down)


def _pick_tile(n, pref):
    t = min(n, pref)
    assert n % t == 0, (n, t)
    return t


def kernel(x_prompt, x_sample, state_conv, state_lru, state_ssm_re, state_ssm_im, c_prompt, c_sample, w_ada, b_ada, g_norm1, g_norm2, w_in, conv_w, conv_b, w_rg_a, b_rg_a, w_rg_x, b_rg_x, lru_lambda, w_proj_a, ssm_a_re, ssm_a_im, ssm_log_dt, ssm_b_re, ssm_b_im, ssm_c_re, ssm_c_im, ssm_d, w_glu, w_out, w_up, w_down, g_final):
    depth = w_ada.shape[0]
    nb, seq, d = x_prompt.shape
    ns = x_sample.shape[0]
    assert x_sample.shape[1] == 1
    w = conv_w.shape[2]
    n_state = ssm_a_re.shape[1] * ssm_a_re.shape[2]
    assert ssm_b_re.shape[2:] == (SSM_STATE, SSM_GROUP) and w % LANES == 0
    assert w_in.shape[2] == 2 * w + 2 * d

    tl = _pick_tile(seq, 256)
    tm_in = _pick_tile(seq, 1024)
    tm_mg = _pick_tile(seq, 256)
    tm_mlp = _pick_tile(seq, 1024)

    xp = x_prompt.reshape(nb * seq, d)
    xs = x_sample.reshape(ns, d)
    assert ns % SUBLANES == 0
    pad = (-nb) % SUBLANES
    c_all = jnp.concatenate([c_sample, c_prompt, jnp.zeros((pad, d), F32)], axis=0)

    outs_p = [[] for _ in range(4)]
    outs_s = [[] for _ in range(4)]
    for l in range(depth):
        last = l == depth - 1
        b_ada_row = b_ada[l].reshape(1, -1)
        mod_s, mod_p = _mod_call(c_all, ns, w_ada[l], b_ada_row, MLP_MOD_FIRST * d)
        mod_p = mod_p.reshape(nb + pad, 1, MLP_MOD_FIRST * d)

        g1 = g_norm1[l].reshape(1, d)
        g2 = g_norm2[l].reshape(1, d)
        gf = g_final.reshape(1, d)
        cw, cb = conv_w[l], conv_b[l].reshape(1, w)
        wg = jnp.concatenate([w_rg_a[l], w_rg_x[l]], axis=2).astype(BF16)
        ba, bx, lam = b_rg_a[l].reshape(1, w), b_rg_x[l].reshape(1, w), lru_lambda[l].reshape(1, w)
        bb, cc, pw = _s5_prep_call(ssm_a_re[l], ssm_a_im[l], ssm_log_dt[l], ssm_b_re[l], ssm_b_im[l],
                                       ssm_c_re[l], ssm_c_im[l], S5_FOLD + 1)
        dskip = ssm_d[l].reshape(1, w)

        u_s, sz_s, w_in_b = _inproj_call(xs, mod_s, g1, w_in[l], tm=ns, tiles_per_batch=1, w_mix=w)

        u_p, sz_p = _inproj_call(xp, mod_p, g1, w_in_b, tm=tm_in, tiles_per_batch=seq // tm_in, w_mix=w)
        u3 = u_p.reshape(nb, seq, 2 * w)
        hs_p, nconv_p, nlru_p, wp, wgl, wo = _lru_call(u3, cw, cb, wg, ba, bx, lam, tl=tl,
                                                       side=((w_proj_a[l], 1), (w_glu[l], 1), (w_out[l], 1)))
        by_token = lambda s: jnp.transpose(s, (1, 2, 0)).reshape(n_state, ns)
        gy_p, nre_p, nim_p, mlp_mod_s, mlp_mod_p, gy_s, nre_s, nim_s = _s5_call(
            u3, bb, cc, pw, dskip, c_all, ns, w_ada[l], b_ada_row, MLP_MOD_FIRST * d,
            u_s, by_token(state_ssm_re[l]), by_token(state_ssm_im[l]), w=w)
        mlp_mod_p = mlp_mod_p.reshape(nb + pad, 1, (N_MOD - MLP_MOD_FIRST) * d)
        x1_p, w_up_b, w_down_b = _merge_call(hs_p.reshape(nb * seq, w), gy_p.reshape(nb * seq, w), sz_p, xp, mod_p,
                                             wp, wgl, wo, tm=tm_mg, tiles_per_batch=seq // tm_mg,
                                             side=((w_up[l], 1), (w_down[l], 0)))
        xp = _mlp_call(x1_p, mlp_mod_p, g2, gf, w_up_b, w_down_b, tm=tm_mlp, tiles_per_batch=seq // tm_mlp,
                       final_norm=last, tf=1024)

        buf = jnp.transpose(state_conv[l], (1, 0, 2))
        hs_s, nconv_s, nlru_s = _lru_step_call(u_s, buf, state_lru[l], cw, cb, wg, ba, bx, lam)
        (x1_s,) = _merge_call(hs_s, gy_s, sz_s, xs, mod_s, wp, wgl, wo, tm=ns, tiles_per_batch=1)
        xs = _mlp_call(x1_s, mlp_mod_s, g2, gf, w_up_b, w_down_b, tm=ns, tiles_per_batch=1, final_norm=last,
                       tf=1024)

        gshape = ssm_a_re.shape[1:]
        for acc, v in zip(outs_p, (nconv_p, nlru_p.reshape(nb, w),
                                   nre_p.reshape((nb,) + gshape), nim_p.reshape((nb,) + gshape))):
            acc.append(v)
        by_group = lambda s: jnp.transpose(s.reshape(gshape + (ns,)), (2, 0, 1))
        for acc, v in zip(outs_s, (jnp.transpose(nconv_s, (1, 0, 2)), nlru_s, by_group(nre_s), by_group(nim_s))):
            acc.append(v)

    y_prompt = xp.reshape(nb, seq, d)
    y_sample = xs.reshape(ns, 1, d)
    stack = lambda a: a[0][None] if len(a) == 1 else jnp.stack(a)
    return (y_prompt, y_sample) + tuple(stack(a) for a in outs_p) + tuple(stack(a) for a in outs_s)
```

```python
import functools

import jax
import jax.numpy as jnp
from jax import lax
from jax.experimental import pallas as pl
from jax.experimental.pallas import tpu as pltpu

F32 = jnp.float32
BF16 = jnp.bfloat16

LANES = 128
SUBLANES = 8
BF16_ROWS = 2 * SUBLANES
VMEM_PHYSICAL_BYTES = 64 * 1024 * 1024
VMEM_LIMIT_CAP_BYTES = VMEM_PHYSICAL_BYTES - 6 * 1024 * 1024
VMEM_SLACK_BYTES = 8 * 1024 * 1024

LRU_C = 8.0
EPS = 1e-6
N_MOD = 6
MLP_MOD_FIRST = 3
SSM_GROUP = 16
SSM_STATE = 64
GROUPS_PER_BLOCK = LANES // SSM_GROUP
STATES_PER_BLOCK = GROUPS_PER_BLOCK * SSM_STATE

EPILOGUE_ROWS = 256
MLP_OUT_CHUNK = 512
INPROJ_CHUNK = 256
INPROJ_W_TILE_BYTES = 8 * 1024 * 1024
S5_FOLD = 8


def _vmem_limit(nbytes):
    return int(min(VMEM_LIMIT_CAP_BYTES, max(32 * 1024 * 1024, nbytes + VMEM_SLACK_BYTES)))


def _rows(ref, rows=None):
    if len(ref.shape) == 3:
        return ref[0]
    return ref[...] if rows is None else ref[rows, :]


def _row_groups(n_rows, group, fn, unroll):
    assert n_rows % group == 0

    def body(i, carry):
        fn(pl.ds(pl.multiple_of(i * group, group), group))
        return carry

    lax.fori_loop(0, n_rows // group, body, 0, unroll=unroll)


def _rms_scale_pass(src, inv_scr, d):
    def fn(rows):
        x = src(rows)
        xx = x * x
        part = xx[:, 0:LANES]
        for c in range(1, d // LANES):
            part = part + xx[:, c * LANES:(c + 1) * LANES]
        inv_scr[rows, :] = part

    _row_groups(inv_scr.shape[0], SUBLANES, fn, unroll=8)
    ss = jnp.sum(inv_scr[...], axis=-1, keepdims=True)
    inv_scr[...] = jnp.broadcast_to(lax.rsqrt(ss * (1.0 / d) + EPS), inv_scr.shape)


def _lanes(inv, d):
    return jnp.concatenate([inv] * (d // LANES), axis=1)


def _sublane_rows(scr, slot, row=None):
    rows = slice(slot * SUBLANES, (slot + 1) * SUBLANES)
    if row is None:
        return scr[rows, :]
    scr[rows, :] = jnp.broadcast_to(row, (SUBLANES, row.shape[1]))
    return None


def _norm_mod_bf16(x_ref, g_ref, sc_ref, sh_ref, h_scr, inv_scr, row_scr, x_copy_ref=None):
    n_rows, d = x_ref.shape
    _rms_scale_pass(lambda rows: x_ref[rows, :], inv_scr, d)
    per_token = len(sc_ref.shape) == 2
    if not per_token:
        _sublane_rows(row_scr, 0, g_ref[...] * (1.0 + sc_ref[0]))
        _sublane_rows(row_scr, 1, sh_ref[0])

    def fn(rows):
        halves = []
        for k in range(BF16_ROWS // SUBLANES):
            r8 = pl.ds(pl.multiple_of(rows.start + k * SUBLANES, SUBLANES), SUBLANES)
            gm = g_ref[...] * (1.0 + sc_ref[r8, :]) if per_token else _sublane_rows(row_scr, 0)
            sh = sh_ref[r8, :] if per_token else _sublane_rows(row_scr, 1)
            x = x_ref[r8, :]
            if x_copy_ref is not None:
                x_copy_ref[r8, :] = x
            halves.append((x * _lanes(inv_scr[r8, :], d)) * gm + sh)
        h_scr[rows, :] = jnp.concatenate(halves, axis=0).astype(BF16)

    _row_groups(n_rows, BF16_ROWS, fn, unroll=2)


def _dot(a, b):
    return jnp.dot(a, b, preferred_element_type=F32)


def _sigmoid(x):
    return 0.5 * jnp.tanh(0.5 * x) + 0.5


def _side_cast_specs(side, n_steps):
    in_specs, out_specs, out_shapes, nbytes = [], [], [], 0
    for arr, axis in side:
        blk = tuple(s // n_steps if a == axis else s for a, s in enumerate(arr.shape))
        assert arr.ndim == 2 and arr.shape[axis] % n_steps == 0 and blk[1] % LANES == 0 and blk[0] % BF16_ROWS == 0
        idx = (lambda i: (i, 0)) if axis == 0 else (lambda i: (0, i))
        in_specs.append(pl.BlockSpec(blk, idx))
        out_specs.append(pl.BlockSpec(blk, idx))
        out_shapes.append(jax.ShapeDtypeStruct(arr.shape, BF16))
        nbytes += 2 * blk[0] * blk[1] * (4 + 2)
    return in_specs, out_specs, out_shapes, nbytes


def _side_cast(src_refs, dst_refs):
    for src, dst in zip(src_refs, dst_refs, strict=True):
        dst[...] = src[...].astype(BF16)


def _mod_kernel(c_ref, w_ref, b_ref, os_ref, op_ref):
    c = c_ref[...]
    cs = (c * _sigmoid(c)).astype(BF16)
    mod = _dot(cs, w_ref[...].astype(BF16)) + b_ref[...]
    n_sample = os_ref.shape[0]
    os_ref[...] = mod[:n_sample]
    op_ref[...] = mod[n_sample:]


def _mod_call(c_all, n_sample, w_ada, b_ada, n, tn=1024):
    n_rows, d = c_all.shape
    assert n % tn == 0
    est = 2 * (d * tn * 4) + 3 * n_rows * tn * 4 + n_rows * d * 4 * 2 + d * tn * 2
    return pl.pallas_call(
        _mod_kernel,
        grid=(n // tn,),
        in_specs=[pl.BlockSpec((n_rows, d), lambda j: (0, 0)),
                  pl.BlockSpec((d, tn), lambda j: (0, j)),
                  pl.BlockSpec((1, tn), lambda j: (0, j))],
        out_specs=[pl.BlockSpec((n_sample, tn), lambda j: (0, j)),
                   pl.BlockSpec((n_rows - n_sample, tn), lambda j: (0, j))],
        out_shape=[jax.ShapeDtypeStruct((n_sample, n), F32),
                   jax.ShapeDtypeStruct((n_rows - n_sample, n), F32)],
        compiler_params=pltpu.CompilerParams(dimension_semantics=("parallel",),
                                             vmem_limit_bytes=_vmem_limit(est)),
        name="mod",
    )(c_all, w_ada, b_ada)


def _mod_spec(mod, piece, d, tiles_per_batch, ngrid):
    if mod.ndim == 3:
        if ngrid == 1:
            return pl.BlockSpec((1, 1, d), lambda i: (i // tiles_per_batch, 0, piece))
        return pl.BlockSpec((1, 1, d), lambda i, j: (i // tiles_per_batch, 0, piece))
    rows = mod.shape[0]
    if ngrid == 1:
        return pl.BlockSpec((rows, d), lambda i: (0, piece))
    return pl.BlockSpec((rows, d), lambda i, j: (0, piece))


def _x_tile_prologue(x_hbm, x_buf, x_sem, prologue):
    i, j = pl.program_id(0), pl.program_id(1)
    n_rows = x_buf.shape[0]

    def x_tile(tile):
        return pltpu.make_async_copy(x_hbm.at[pl.ds(tile * n_rows, n_rows), :], x_buf, x_sem)

    @pl.when(j == 0)
    def _():
        @pl.when(i == 0)
        def _():
            x_tile(0).start()

        x_tile(i).wait()
        prologue()

        @pl.when(i + 1 < pl.num_programs(0))
        def _():
            x_tile(i + 1).start()


def _inproj_kernel(*refs, n_u_tiles, emit_w):
    x_hbm, sc_ref, sh_ref, g_ref, w_ref, u_ref, z_ref = refs[:7]
    x_buf, x_sem, h_scr, inv_scr, row_scr = refs[-5:]
    j = pl.program_id(1)
    _x_tile_prologue(x_hbm, x_buf, x_sem,
                     lambda: _norm_mod_bf16(x_buf, g_ref, sc_ref, sh_ref, h_scr, inv_scr, row_scr))

    if emit_w:
        wb_ref = refs[7]
        wb_ref[...] = w_ref[...].astype(BF16)
        w_ref = wb_ref
    tn = w_ref.shape[1]
    nc = min(tn, INPROJ_CHUNK)

    def project(out_ref):
        for c in range(tn // nc):
            cols = slice(c * nc, (c + 1) * nc)
            out_ref[:, cols] = _dot(h_scr[...], w_ref[:, cols]).astype(out_ref.dtype)

    pl.when(j < n_u_tiles)(lambda: project(u_ref))
    pl.when(j >= n_u_tiles)(lambda: project(z_ref))


def _inproj_call(x2d, mod, g1, w_in, *, tm, tiles_per_batch, w_mix):
    m, d = x2d.shape
    n = w_in.shape[1]
    n_u = 2 * w_mix
    emit_w = w_in.dtype != BF16
    assert not emit_w or m == tm
    wbytes = w_in.dtype.itemsize
    tn = INPROJ_W_TILE_BYTES // (d * wbytes)
    assert n_u % tn == 0 and n % tn == 0 and tn % INPROJ_CHUNK == 0
    n_u_tiles = n_u // tn
    est = (tm * d * 4 + 2 * d * tn * wbytes + 2 * tm * tn * 4 + 2 * tm * tn * 2 + tm * d * 2 + tm * tn * 4
           + (3 * d * tn * 2 if emit_w else 0))
    out_specs = [pl.BlockSpec((tm, tn), lambda i, j: (i, jnp.minimum(j, n_u_tiles - 1)),
                              pipeline_mode=pl.Buffered(1) if n_u_tiles == 1 else None),
                 pl.BlockSpec((tm, tn), lambda i, j: (i, jnp.maximum(j - n_u_tiles, 0)))]
    out_shape = [jax.ShapeDtypeStruct((m, n_u), F32),
                 jax.ShapeDtypeStruct((m, n - n_u), BF16)]
    if emit_w:
        out_specs.append(pl.BlockSpec((d, tn), lambda i, j: (0, j)))
        out_shape.append(jax.ShapeDtypeStruct((d, n), BF16))
    return pl.pallas_call(
        functools.partial(_inproj_kernel, n_u_tiles=n_u_tiles, emit_w=emit_w),
        grid=(m // tm, n // tn),
        in_specs=[pl.BlockSpec(memory_space=pl.ANY),
                  _mod_spec(mod, 1, d, tiles_per_batch, 2),
                  _mod_spec(mod, 0, d, tiles_per_batch, 2),
                  pl.BlockSpec((1, d), lambda i, j: (0, 0)),
                  pl.BlockSpec((d, tn), lambda i, j: (0, j))],
        out_specs=out_specs,
        out_shape=out_shape,
        scratch_shapes=[pltpu.VMEM((tm, d), F32), pltpu.SemaphoreType.DMA(()),
                        pltpu.VMEM((tm, d), BF16), pltpu.VMEM((tm, LANES), F32),
                        pltpu.VMEM((2 * SUBLANES, d), F32)],
        compiler_params=pltpu.CompilerParams(dimension_semantics=("arbitrary", "arbitrary"),
                                             vmem_limit_bytes=_vmem_limit(est)),
        name="inproj",
    )(x2d, mod, mod, g1, w_in)


def _lru_gate_block(uc, g, ba, bx, sp):
    r = _sigmoid(g[:, :LANES] + ba)
    i = _sigmoid(g[:, LANES:] + bx)
    log_a = (-LRU_C * r) * sp
    a = jnp.exp(log_a)
    m2 = 1.0 - jnp.exp(2.0 * log_a)
    mult = jnp.where(m2 > 0.0, m2 * lax.rsqrt(m2), 0.0)
    return a, (mult * i) * uc


def _lru_kernel(*refs, n_side):
    u_ref, cw_ref, cb_ref, wg_ref, ba_ref, bx_ref, lam_ref = refs[:7]
    hs_ref, nconv_ref, nlru_ref = refs[7 + n_side:10 + n_side]
    ext, a_s, b_s, carry = refs[-4:]
    _side_cast(refs[7:7 + n_side], refs[10 + n_side:10 + 2 * n_side])
    t = pl.program_id(0)
    nt = pl.num_programs(0)
    nbat, tl, w = u_ref.shape
    nblk = w // LANES
    rows = nbat * tl
    kw = cw_ref.shape[0]
    hist = ext.shape[1] - rows
    steps_per_tile = SUBLANES // nbat

    @pl.when(t == 0)
    def _():
        ext[:, 0:hist, :] = jnp.zeros((nblk, hist, LANES), F32)
        carry[...] = jnp.zeros_like(carry)

    for b in range(nbat):
        for c in range(nblk):
            ext[c, pl.ds(hist + b, tl, stride=nbat), :] = u_ref[b, :, c * LANES:(c + 1) * LANES]

        @pl.when(t == nt - 1)
        def _(b=b):
            nconv_ref[b] = u_ref[b, tl - (kw - 1):tl, :]

    sp = jax.nn.softplus(-lam_ref[...])
    for c in range(nblk):
        cols = slice(c * LANES, (c + 1) * LANES)
        uc = cb_ref[:, cols]
        for k in range(kw):
            start = hist - (kw - 1 - k) * nbat
            uc = uc + ext[c, start:start + rows, :] * cw_ref[k:k + 1, cols]
        g = _dot(uc.astype(BF16), wg_ref[c])
        a, b = _lru_gate_block(uc, g, ba_ref[:, cols], bx_ref[:, cols], sp[:, cols])
        a_s[:, cols] = a
        b_s[:, cols] = b
        ext[c, hist - (kw - 1) * nbat:hist, :] = ext[c, hist + rows - (kw - 1) * nbat:hist + rows, :]

    def tile_step(i, hcur):
        r0 = pl.multiple_of(i * SUBLANES, SUBLANES)
        at, bt = a_s[pl.ds(r0, SUBLANES), :], b_s[pl.ds(r0, SUBLANES), :]
        hs = []
        for k in range(steps_per_tile):
            hcur = at[k * nbat:(k + 1) * nbat] * hcur + bt[k * nbat:(k + 1) * nbat]
            hs.append(hcur)
        htile = jnp.concatenate(hs, axis=0)
        for c in range(nblk):
            ext[c, pl.ds(pl.multiple_of(hist + r0, SUBLANES), SUBLANES), :] = htile[:, c * LANES:(c + 1) * LANES]
        return hcur

    h_end = lax.fori_loop(0, tl // steps_per_tile, tile_step, carry[...], unroll=2)
    carry[...] = h_end

    for b in range(nbat):
        for c in range(nblk):
            hs_ref[b, :, c * LANES:(c + 1) * LANES] = ext[c, pl.ds(hist + b, tl, stride=nbat), :].astype(BF16)

    @pl.when(t == nt - 1)
    def _():
        for b in range(nbat):
            nlru_ref[b] = h_end[b:b + 1, :]


def _lru_call(u3d, cw, cb, wg, ba, bx, lam, *, tl, side=()):
    bsz, seq, _ = u3d.shape
    w = cw.shape[1]
    kw = cw.shape[0]
    nblk = w // LANES
    assert SUBLANES % bsz == 0
    rows = bsz * tl
    hist = -(-(kw - 1) * bsz // SUBLANES) * SUBLANES
    side_in, side_out, side_shapes, side_bytes = _side_cast_specs(side, seq // tl)
    est = (2 * rows * w * 4 + 2 * rows * w * 2 + (hist + rows) * w * 4 + 2 * rows * w * 4
           + nblk * LANES * 2 * LANES * 2 * 2 + 2 * rows * 2 * LANES * 4 + side_bytes)
    const2 = lambda t: (0, 0)
    return pl.pallas_call(
        functools.partial(_lru_kernel, n_side=len(side)),
        grid=(seq // tl,),
        in_specs=[pl.BlockSpec((bsz, tl, w), lambda t: (0, t, 0)),
                  pl.BlockSpec((kw, w), const2),
                  pl.BlockSpec((1, w), const2),
                  pl.BlockSpec((nblk, LANES, 2 * LANES), lambda t: (0, 0, 0)),
                  pl.BlockSpec((1, w), const2),
                  pl.BlockSpec((1, w), const2),
                  pl.BlockSpec((1, w), const2)] + side_in,
        out_specs=[pl.BlockSpec((bsz, tl, w), lambda t: (0, t, 0)),
                   pl.BlockSpec((bsz, kw - 1, w), lambda t: (0, 0, 0)),
                   pl.BlockSpec((bsz, 1, w), lambda t: (0, 0, 0))] + side_out,
        out_shape=[jax.ShapeDtypeStruct((bsz, seq, w), BF16),
                   jax.ShapeDtypeStruct((bsz, kw - 1, w), F32),
                   jax.ShapeDtypeStruct((bsz, 1, w), F32)] + side_shapes,
        scratch_shapes=[pltpu.VMEM((nblk, hist + rows, LANES), F32),
                        pltpu.VMEM((rows, w), F32),
                        pltpu.VMEM((rows, w), F32),
                        pltpu.VMEM((bsz, w), F32)],
        compiler_params=pltpu.CompilerParams(dimension_semantics=("arbitrary",),
                                             vmem_limit_bytes=_vmem_limit(est)),
        name="lru_scan",
    )(u3d, cw, cb, wg, ba, bx, lam, *(a for a, _ in side))


def _lru_step_kernel(u_ref, buf_ref, h0_ref, cw_ref, cb_ref, wg_ref, ba_ref, bx_ref, lam_ref,
                     hs_ref, nconv_ref, nlru_ref):
    w = cw_ref.shape[1]
    kw = cw_ref.shape[0]
    nblk = w // LANES
    u = u_ref[...]
    uc = cb_ref[...] + buf_ref[0] * cw_ref[0:1, :]
    for k in range(1, kw - 1):
        uc = uc + buf_ref[k] * cw_ref[k:k + 1, :]
    uc = uc + u * cw_ref[kw - 1:kw, :]
    for k in range(kw - 2):
        nconv_ref[k] = buf_ref[k + 1]
    nconv_ref[kw - 2] = u
    sp = jax.nn.softplus(-lam_ref[...])
    for h in range(nblk):
        cols = slice(h * LANES, (h + 1) * LANES)
        uch = uc[:, cols]
        g = _dot(uch.astype(BF16), wg_ref[h])
        a, b = _lru_gate_block(uch, g, ba_ref[:, cols], bx_ref[:, cols], sp[:, cols])
        hn = a * h0_ref[:, cols] + b
        nlru_ref[:, cols] = hn
        hs_ref[:, cols] = hn.astype(BF16)


def _lru_step_call(u2d, buf, h0, cw, cb, wg, ba, bx, lam):
    n = h0.shape[0]
    w = cw.shape[1]
    kw = cw.shape[0]
    nblk = w // LANES
    c2 = lambda i: (0, 0)
    c3 = lambda i: (0, 0, 0)
    return pl.pallas_call(
        _lru_step_kernel,
        grid=(1,),
        in_specs=[pl.BlockSpec((n, w), c2),
                  pl.BlockSpec((kw - 1, n, w), c3),
                  pl.BlockSpec((n, w), c2),
                  pl.BlockSpec((kw, w), c2),
                  pl.BlockSpec((1, w), c2),
                  pl.BlockSpec((nblk, LANES, 2 * LANES), c3),
                  pl.BlockSpec((1, w), c2),
                  pl.BlockSpec((1, w), c2),
                  pl.BlockSpec((1, w), c2)],
        out_specs=[pl.BlockSpec((n, w), c2),
                   pl.BlockSpec((kw - 1, n, w), c3),
                   pl.BlockSpec((n, w), c2)],
        out_shape=[jax.ShapeDtypeStruct((n, w), BF16),
                   jax.ShapeDtypeStruct((kw - 1, n, w), F32),
                   jax.ShapeDtypeStruct((n, w), F32)],
        name="lru_step",
    )(u2d, buf, h0, cw, cb, wg, ba, bx, lam)


def _abar(ar, ai, dt):
    mag = jnp.exp(dt * ar)
    ang = dt * ai
    return mag * jnp.cos(ang), mag * jnp.sin(ang)


def _s5_prep_kernel(ax_ref, dtx_ref, bt_ref, ct_ref, al_ref, dtl_ref, bb_ref, cc_ref, pw_ref):
    ar, ai = ax_ref[0], ax_ref[1]
    abr, abi = _abar(ar, ai, jnp.exp(dtx_ref[...]))
    den = ar * ar + ai * ai
    nr, ni = abr - 1.0, abi
    q_re = (nr * ar + ni * ai) / den
    q_im = (ni * ar - nr * ai) / den
    br, bi = bt_ref[0], bt_ref[1]
    for ref, pair in ((bb_ref, (q_re * br - q_im * bi, q_re * bi + q_im * br)), (cc_ref, (ct_ref[0], ct_ref[1]))):
        for part, v in enumerate(pair):
            ref[part] = jnp.concatenate([v, v], axis=1)

    pr, pi = _abar(al_ref[0], al_ref[1], jnp.exp(dtl_ref[...]))
    qr, qi = jnp.ones_like(pr), jnp.zeros_like(pr)
    for m in range(pw_ref.shape[1]):
        pw_ref[0, m:m + 1, :] = qr
        pw_ref[1, m:m + 1, :] = qi
        qr, qi = qr * pr - qi * pi, qr * pi + qi * pr


def _s5_prep_call(a_re, a_im, log_dt, b_re, b_im, c_re, c_im, n_pow):
    g, p, k = b_re.shape
    assert c_re.shape == (g, k, p) and 2 * p == LANES
    a = jnp.stack([a_re, a_im])
    return pl.pallas_call(
        _s5_prep_kernel,
        out_shape=[jax.ShapeDtypeStruct((2, g * k, 2 * p), F32), jax.ShapeDtypeStruct((2, g * k, 2 * p), F32),
                   jax.ShapeDtypeStruct((2, n_pow, g * p), F32)],
        name="s5_prep",
    )(jnp.repeat(a, k, axis=1), jnp.repeat(log_dt, k).reshape(g * k, 1),
      jnp.stack([b_re, b_im]).swapaxes(2, 3).reshape(2, g * k, p), jnp.stack([c_re, c_im]).reshape(2, g * k, p),
      a.reshape(2, 1, g * p), jnp.repeat(log_dt, p).reshape(1, g * p))


def _block_diag(stack):
    tiled = jnp.concatenate([stack] * (STATES_PER_BLOCK // stack.shape[1]), axis=1)
    own = (lax.broadcasted_iota(jnp.int32, tiled.shape, 0) // SSM_GROUP
           == lax.broadcasted_iota(jnp.int32, tiled.shape, 1) // SSM_STATE)
    return jnp.where(own, tiled, 0.0)


def _cplx_mul(ar, ai, br, bi):
    return ar * br - ai * bi, ar * bi + ai * br


def _cplx_step(ar, ai, hr, hi, br, bi):
    return ar * hr - ai * hi + br, ar * hi + ai * hr + bi


def _s5_one_step(us_ref, h0r_ref, h0i_ref, bre, bim, ctre, ctim, ar, ai, d_ref, gys_ref, nres_ref, nims_ref):
    us = us_ref[...]
    ust = us.T.astype(BF16)
    ar = jnp.broadcast_to(ar, bre.shape).T
    ai = jnp.broadcast_to(ai, bre.shape).T
    hr, hi = _cplx_step(ar, ai, h0r_ref[...], h0i_ref[...],
                        _dot(bre.T.astype(BF16), ust), _dot(bim.T.astype(BF16), ust))
    nres_ref[...] = hr
    nims_ref[...] = hi
    yt = _dot(ctre.astype(BF16), hr.astype(BF16)) - _dot(ctim.astype(BF16), hi.astype(BF16))
    gys_ref[...] = jax.nn.gelu(yt.T + d_ref[...] * us).astype(BF16)


def _s5_kernel(u_ref, bb_ref, cc_ref, pw_ref, d_ref, c_ref, wa_ref, ba_ref, us_ref, h0r_ref, h0i_ref,
               gy_ref, nre_ref, nim_ref, ms_ref, mp_ref, gys_ref, nres_ref, nims_ref,
               ufb, buf, wbs, wqs, wts, yslab, out32):
    _mod_kernel(c_ref, wa_ref, ba_ref, ms_ref, mp_ref)
    nbat, seq, _ = u_ref.shape
    spb = STATES_PER_BLOCK
    fold = wbs.shape[0] // LANES
    nq = seq // fold

    bre, bim = _block_diag(bb_ref[0]), _block_diag(bb_ref[1])
    ctre, ctim = _block_diag(cc_ref[0]), _block_diag(cc_ref[1])
    _s5_one_step(us_ref, h0r_ref, h0i_ref, bre, bim, ctre, ctim, pw_ref[0, 1:2, :], pw_ref[1, 1:2, :], d_ref,
                 gys_ref, nres_ref, nims_ref)
    wcb = jnp.concatenate([ctre.T, -ctim.T], axis=0).astype(BF16)
    lag = []
    for m in range(fold):
        wre, wim = _cplx_mul(bre, bim, pw_ref[0, m:m + 1, :], pw_ref[1, m:m + 1, :])
        rows = slice((fold - 1 - m) * LANES, (fold - m) * LANES)
        wbs[rows, :spb] = wre.astype(BF16)
        wbs[rows, spb:] = wim.astype(BF16)
        lag.append(_dot(wbs[rows, :], wcb))
    zero_blk = jnp.zeros((LANES, LANES), BF16)
    for r_in in range(fold):
        for r_out in range(fold):
            blk = lag[r_out - r_in].astype(BF16) if r_in <= r_out else zero_blk
            wts[r_in * LANES:(r_in + 1) * LANES, r_out * LANES:(r_out + 1) * LANES] = blk
    for r in range(fold):
        pr, pi = pw_ref[0, r + 1:r + 2, :], pw_ref[1, r + 1:r + 2, :]
        cols = slice(r * LANES, (r + 1) * LANES)
        wqs[:spb, cols] = (ctre * pr - ctim * pi).T.astype(BF16)
        wqs[spb:, cols] = (-(ctre * pi + ctim * pr)).T.astype(BF16)

    for r in range(fold):
        for b in range(nbat):
            yslab[r, pl.ds(b, nq, stride=nbat), :] = u_ref[b, pl.ds(r, nq, stride=fold), :]
        ufb[:, r * LANES:(r + 1) * LANES] = yslab[r].astype(BF16)

    buf[...] = _dot(ufb[...], wbs[...])

    ar, ai = pw_ref[0, fold:fold + 1, :], pw_ref[1, fold:fold + 1, :]
    groups_per_tile = SUBLANES // nbat
    z = jnp.zeros((nbat, spb), F32)

    def tile_step(i, g):
        gr, gi = g
        r0 = pl.multiple_of(i * SUBLANES, SUBLANES)
        xr, xi = buf[pl.ds(r0, SUBLANES), :spb], buf[pl.ds(r0, SUBLANES), spb:]
        hr, hi = [], []
        for k in range(groups_per_tile):
            hr.append(gr)
            hi.append(gi)
            gr, gi = _cplx_step(ar, ai, gr, gi, xr[k * nbat:(k + 1) * nbat], xi[k * nbat:(k + 1) * nbat])
        buf[pl.ds(r0, SUBLANES), :spb] = jnp.concatenate(hr, axis=0)
        buf[pl.ds(r0, SUBLANES), spb:] = jnp.concatenate(hi, axis=0)
        return gr, gi

    g_re, g_im = lax.fori_loop(0, nq // groups_per_tile, tile_step, (z, z), unroll=2)
    for b in range(nbat):
        nre_ref[b] = g_re[b:b + 1, :]
        nim_ref[b] = g_im[b:b + 1, :]

    hsb = buf[...].astype(BF16)
    per_tile = 2
    for n in range(fold // per_tile):
        cols = slice(n * per_tile * LANES, (n + 1) * per_tile * LANES)
        kk = (n + 1) * per_tile * LANES
        yn = _dot(hsb, wqs[:, cols]) + _dot(ufb[:, :kk], wts[:kk, cols])
        for i in range(per_tile):
            yslab[n * per_tile + i] = yn[:, i * LANES:(i + 1) * LANES]

    for b in range(nbat):
        for r in range(fold):
            ys = (yslab[r, pl.ds(b, nq, stride=nbat), :]
                  + d_ref[...] * u_ref[b, pl.ds(r, nq, stride=fold), :])
            out32[b, pl.ds(r, nq, stride=fold), :] = jax.nn.gelu(ys)
        gy_ref[b] = out32[b].astype(BF16)


def _s5_call(u3d, bb, cc, pw, d, c_all, n_sample, w_ada, b_ada, first_col, u_step, h0r, h0i, *, w):
    bsz, seq, _ = u3d.shape
    nblk = w // LANES
    spb = STATES_PER_BLOCK
    fold = S5_FOLD
    assert SUBLANES % bsz == 0 and seq % fold == 0 and pw.shape[1] == fold + 1
    n_state, n_step = h0r.shape
    assert n_state == nblk * spb and n_step == LANES
    state_spec = pl.BlockSpec((spb, n_step), lambda c: (c, 0))
    state_shape = jax.ShapeDtypeStruct((n_state, n_step), F32)
    rows = bsz * (seq // fold)
    n_c, dm = c_all.shape
    mod_cols = w_ada.shape[1] - first_col
    tn = mod_cols // nblk
    assert mod_cols % nblk == 0 and tn % LANES == 0 and first_col % tn == 0
    est = (2 * bsz * seq * LANES * 4 + 2 * bsz * seq * LANES * 2
           + 2 * dm * tn * 4 + dm * tn * 2 + 2 * n_c * dm * 4 + 3 * n_c * tn * 4
           + rows * fold * LANES * 2 + 2 * rows * 2 * spb * 4
           + 3 * fold * LANES * 2 * spb * 2
           + rows * fold * LANES * 4 + bsz * seq * LANES * 4
           + rows * 2 * spb * 2
           + 2 * (LANES * 2 * spb + 2 * spb * LANES) * 4)
    return pl.pallas_call(
        _s5_kernel,
        grid=(nblk,),
        in_specs=[pl.BlockSpec((bsz, seq, LANES), lambda c: (0, 0, nblk + c)),
                  pl.BlockSpec((2, LANES, LANES), lambda c: (0, c, 0)),
                  pl.BlockSpec((2, LANES, LANES), lambda c: (0, c, 0)),
                  pl.BlockSpec((2, fold + 1, spb), lambda c: (0, 0, c)),
                  pl.BlockSpec((1, LANES), lambda c: (0, c)),
                  pl.BlockSpec((n_c, dm), lambda c: (0, 0)),
                  pl.BlockSpec((dm, tn), lambda c: (0, first_col // tn + c)),
                  pl.BlockSpec((1, tn), lambda c: (0, first_col // tn + c)),
                  pl.BlockSpec((n_step, LANES), lambda c: (0, nblk + c)),
                  state_spec,
                  state_spec],
        out_specs=[pl.BlockSpec((bsz, seq, LANES), lambda c: (0, 0, c)),
                   pl.BlockSpec((bsz, 1, spb), lambda c: (0, 0, c)),
                   pl.BlockSpec((bsz, 1, spb), lambda c: (0, 0, c)),
                   pl.BlockSpec((n_sample, tn), lambda c: (0, c)),
                   pl.BlockSpec((n_c - n_sample, tn), lambda c: (0, c)),
                   pl.BlockSpec((n_step, LANES), lambda c: (0, c)),
                   state_spec,
                   state_spec],
        out_shape=[jax.ShapeDtypeStruct((bsz, seq, w), BF16),
                   jax.ShapeDtypeStruct((bsz, 1, nblk * spb), F32),
                   jax.ShapeDtypeStruct((bsz, 1, nblk * spb), F32),
                   jax.ShapeDtypeStruct((n_sample, mod_cols), F32),
                   jax.ShapeDtypeStruct((n_c - n_sample, mod_cols), F32),
                   jax.ShapeDtypeStruct((n_step, w), BF16),
                   state_shape,
                   state_shape],
        scratch_shapes=[pltpu.VMEM((rows, fold * LANES), BF16),
                        pltpu.VMEM((rows, 2 * spb), F32),
                        pltpu.VMEM((fold * LANES, 2 * spb), BF16),
                        pltpu.VMEM((2 * spb, fold * LANES), BF16),
                        pltpu.VMEM((fold * LANES, fold * LANES), BF16),
                        pltpu.VMEM((fold, rows, LANES), F32),
                        pltpu.VMEM((bsz, seq, LANES), F32)],
        compiler_params=pltpu.CompilerParams(dimension_semantics=("parallel",),
                                             vmem_limit_bytes=_vmem_limit(est)),
        name="s5_scan",
    )(u3d, bb, cc, pw, d, c_all, w_ada, b_ada, u_step, h0r, h0i)


def _merge_kernel(*refs, n_side):
    hs_ref, gy_ref, z_ref, x_ref, gt_ref, wp_ref, wg_ref, wo_ref = refs[:8]
    o_ref = refs[8 + n_side]
    _side_cast(refs[8:8 + n_side], refs[9 + n_side:])
    d = x_ref.shape[1]
    ya = _dot(hs_ref[...], wp_ref[...])
    merged = _sigmoid(z_ref[:, :d].astype(F32)) * ya
    glu = _dot(gy_ref[...], wg_ref[...])
    yb = glu[:, :d] * _sigmoid(glu[:, d:])
    merged = merged + _sigmoid(z_ref[:, d:].astype(F32)) * yb
    o = _dot(merged.astype(BF16), wo_ref[...])
    o_ref[...] = x_ref[...] + _rows(gt_ref) * o


def _merge_call(hs, gy, sz, x2d, mod, wp, wg, wo, *, tm, tiles_per_batch, side=()):
    m, d = x2d.shape
    w = hs.shape[1]
    side_in, side_out, side_shapes, side_bytes = _side_cast_specs(side, m // tm)
    est = (2 * (2 * tm * w * 2 + tm * 2 * d * 2 + 2 * tm * d * 4) + (wp.size + wg.size + wo.size) * 2
           + tm * d * 4 * 5 + side_bytes)
    one = pl.Buffered(1)
    return pl.pallas_call(
        functools.partial(_merge_kernel, n_side=len(side)),
        grid=(m // tm,),
        in_specs=[pl.BlockSpec((tm, w), lambda i: (i, 0)),
                  pl.BlockSpec((tm, w), lambda i: (i, 0)),
                  pl.BlockSpec((tm, 2 * d), lambda i: (i, 0)),
                  pl.BlockSpec((tm, d), lambda i: (i, 0)),
                  _mod_spec(mod, 2, d, tiles_per_batch, 1),
                  pl.BlockSpec(wp.shape, lambda i: (0, 0), pipeline_mode=one),
                  pl.BlockSpec(wg.shape, lambda i: (0, 0), pipeline_mode=one),
                  pl.BlockSpec(wo.shape, lambda i: (0, 0), pipeline_mode=one)] + side_in,
        out_specs=[pl.BlockSpec((tm, d), lambda i: (i, 0))] + side_out,
        out_shape=[jax.ShapeDtypeStruct((m, d), F32)] + side_shapes,
        compiler_params=pltpu.CompilerParams(dimension_semantics=("parallel",),
                                             vmem_limit_bytes=_vmem_limit(est)),
        name="merge",
    )(hs, gy, sz, x2d, mod, wp, wg, wo, *(a for a, _ in side))


def _mlp_kernel(x_hbm, sc_ref, sh_ref, gt_ref, g2_ref, gf_ref, wu_ref, wd_ref, o_ref,
                x_buf, x_sem, h_scr, inv_scr, row_scr, *, final_norm):
    n_rows, d = x_buf.shape
    _x_tile_prologue(x_hbm, x_buf, x_sem,
                     lambda: _norm_mod_bf16(x_buf, g2_ref, sc_ref, sh_ref, h_scr, inv_scr, row_scr, x_copy_ref=o_ref))

    up = _dot(h_scr[...], wu_ref[...])
    act = jnp.square(jnp.maximum(up, 0.0)).astype(BF16)
    nc = min(d, MLP_OUT_CHUNK)
    for c in range(d // nc):
        cols = slice(c * nc, (c + 1) * nc)
        gate = gt_ref[0, :, cols] if len(gt_ref.shape) == 3 else gt_ref[:, cols]
        o_ref[:, cols] += gate * _dot(act, wd_ref[:, cols])

    if final_norm:
        @pl.when(pl.program_id(1) == pl.num_programs(1) - 1)
        def _():
            def chunk(rows):
                x2 = o_ref[rows, :]
                ms = jnp.mean(x2 * x2, axis=-1, keepdims=True)
                o_ref[rows, :] = (x2 * lax.rsqrt(ms + EPS)) * gf_ref[...]

            _row_groups(n_rows, min(n_rows, EPILOGUE_ROWS), chunk, unroll=1)


def _mlp_call(x2d, mod, g2, gf, w_up, w_down, *, tm, tiles_per_batch, final_norm, tf):
    m, d = x2d.shape
    dff = w_up.shape[1]
    assert w_up.dtype == BF16 and w_down.dtype == BF16
    est = 3 * tm * d * 4 + tm * d * 2 + 4 * d * tf * 2 + tm * tf * 6 + tm * MLP_OUT_CHUNK * 4
    return pl.pallas_call(
        functools.partial(_mlp_kernel, final_norm=final_norm),
        grid=(m // tm, dff // tf),
        in_specs=[pl.BlockSpec(memory_space=pl.ANY),
                  _mod_spec(mod, 4 - MLP_MOD_FIRST, d, tiles_per_batch, 2),
                  _mod_spec(mod, 3 - MLP_MOD_FIRST, d, tiles_per_batch, 2),
                  _mod_spec(mod, 5 - MLP_MOD_FIRST, d, tiles_per_batch, 2),
                  pl.BlockSpec((1, d), lambda i, j: (0, 0)),
                  pl.BlockSpec((1, d), lambda i, j: (0, 0)),
                  pl.BlockSpec((d, tf), lambda i, j: (0, j)),
                  pl.BlockSpec((tf, d), lambda i, j: (j, 0))],
        out_specs=pl.BlockSpec((tm, d), lambda i, j: (i, 0)),
        out_shape=jax.ShapeDtypeStruct((m, d), F32),
        scratch_shapes=[pltpu.VMEM((tm, d), F32), pltpu.SemaphoreType.DMA(()),
                        pltpu.VMEM((tm, d), BF16), pltpu.VMEM((tm, LANES), F32),
                        pltpu.VMEM((2 * SUBLANES, d), F32)],
        compiler_params=pltpu.CompilerParams(dimension_semantics=("arbitrary", "arbitrary"),
                                             vmem_limit_bytes=_vmem_limit(est)),
        name="mlp",
    )(x2d, mod, mod, mod, g2, gf, w_up, w_down)


def _pick_tile(n, pref):
    t = min(n, pref)
    assert n % t == 0, (n, t)
    return t


def kernel(x_prompt, x_sample, state_conv, state_lru, state_ssm_re, state_ssm_im, c_prompt, c_sample, w_ada, b_ada, g_norm1, g_norm2, w_in, conv_w, conv_b, w_rg_a, b_rg_a, w_rg_x, b_rg_x, lru_lambda, w_proj_a, ssm_a_re, ssm_a_im, ssm_log_dt, ssm_b_re, ssm_b_im, ssm_c_re, ssm_c_im, ssm_d, w_glu, w_out, w_up, w_down, g_final):
    depth = w_ada.shape[0]
    nb, seq, d = x_prompt.shape
    ns = x_sample.shape[0]
    assert x_sample.shape[1] == 1
    w = conv_w.shape[2]
    n_state = ssm_a_re.shape[1] * ssm_a_re.shape[2]
    assert ssm_b_re.shape[2:] == (SSM_STATE, SSM_GROUP) and w % LANES == 0
    assert w_in.shape[2] == 2 * w + 2 * d

    tl = _pick_tile(seq, 256)
    tm_in = _pick_tile(seq, 1024)
    tm_mg = _pick_tile(seq, 256)
    tm_mlp = _pick_tile(seq, 1024)

    xp = x_prompt.reshape(nb * seq, d)
    xs = x_sample.reshape(ns, d)
    assert ns % SUBLANES == 0
    pad = (-nb) % SUBLANES
    c_all = jnp.concatenate([c_sample, c_prompt, jnp.zeros((pad, d), F32)], axis=0)

    outs_p = [[] for _ in range(4)]
    outs_s = [[] for _ in range(4)]
    for l in range(depth):
        last = l == depth - 1
        b_ada_row = b_ada[l].reshape(1, -1)
        mod_s, mod_p = _mod_call(c_all, ns, w_ada[l], b_ada_row, MLP_MOD_FIRST * d)
        mod_p = mod_p.reshape(nb + pad, 1, MLP_MOD_FIRST * d)

        g1 = g_norm1[l].reshape(1, d)
        g2 = g_norm2[l].reshape(1, d)
        gf = g_final.reshape(1, d)
        cw, cb = conv_w[l], conv_b[l].reshape(1, w)
        wg = jnp.concatenate([w_rg_a[l], w_rg_x[l]], axis=2).astype(BF16)
        ba, bx, lam = b_rg_a[l].reshape(1, w), b_rg_x[l].reshape(1, w), lru_lambda[l].reshape(1, w)
        bb, cc, pw = _s5_prep_call(ssm_a_re[l], ssm_a_im[l], ssm_log_dt[l], ssm_b_re[l], ssm_b_im[l],
                                       ssm_c_re[l], ssm_c_im[l], S5_FOLD + 1)
        dskip = ssm_d[l].reshape(1, w)

        u_s, sz_s, w_in_b = _inproj_call(xs, mod_s, g1, w_in[l], tm=ns, tiles_per_batch=1, w_mix=w)

        u_p, sz_p = _inproj_call(xp, mod_p, g1, w_in_b, tm=tm_in, tiles_per_batch=seq // tm_in, w_mix=w)
        u3 = u_p.reshape(nb, seq, 2 * w)
        hs_p, nconv_p, nlru_p, wp, wgl, wo = _lru_call(u3, cw, cb, wg, ba, bx, lam, tl=tl,
                                                       side=((w_proj_a[l], 1), (w_glu[l], 1), (w_out[l], 1)))
        by_token = lambda s: jnp.transpose(s, (1, 2, 0)).reshape(n_state, ns)
        gy_p, nre_p, nim_p, mlp_mod_s, mlp_mod_p, gy_s, nre_s, nim_s = _s5_call(
            u3, bb, cc, pw, dskip, c_all, ns, w_ada[l], b_ada_row, MLP_MOD_FIRST * d,
            u_s, by_token(state_ssm_re[l]), by_token(state_ssm_im[l]), w=w)
        mlp_mod_p = mlp_mod_p.reshape(nb + pad, 1, (N_MOD - MLP_MOD_FIRST) * d)
        x1_p, w_up_b, w_down_b = _merge_call(hs_p.reshape(nb * seq, w), gy_p.reshape(nb * seq, w), sz_p, xp, mod_p,
                                             wp, wgl, wo, tm=tm_mg, tiles_per_batch=seq // tm_mg,
                                             side=((w_up[l], 1), (w_down[l], 0)))
        xp = _mlp_call(x1_p, mlp_mod_p, g2, gf, w_up_b, w_down_b, tm=tm_mlp, tiles_per_batch=seq // tm_mlp,
                       final_norm=last, tf=1024)

        buf = jnp.transpose(state_conv[l], (1, 0, 2))
        hs_s, nconv_s, nlru_s = _lru_step_call(u_s, buf, state_lru[l], cw, cb, wg, ba, bx, lam)
        (x1_s,) = _merge_call(hs_s, gy_s, sz_s, xs, mod_s, wp, wgl, wo, tm=ns, tiles_per_batch=1)
        xs = _mlp_call(x1_s, mlp_mod_s, g2, gf, w_up_b, w_down_b, tm=ns, tiles_per_batch=1, final_norm=last,
                       tf=1024)

        gshape = ssm_a_re.shape[1:]
        for acc, v in zip(outs_p, (nconv_p, nlru_p.reshape(nb, w),
                                   nre_p.reshape((nb,) + gshape), nim_p.reshape((nb,) + gshape))):
            acc.append(v)
        by_group = lambda s: jnp.transpose(s.reshape(gshape + (ns,)), (2, 0, 1))
        for acc, v in zip(outs_s, (jnp.transpose(nconv_s, (1, 0, 2)), nlru_s, by_group(nre_s), by_group(nim_s))):
            acc.append(v)

    y_prompt = xp.reshape(nb, seq, d)
    y_sample = xs.reshape(ns, 1, d)
    stack = lambda a: a[0][None] if len(a) == 1 else jnp.stack(a)
    return (y_prompt, y_sample) + tuple(stack(a) for a in outs_p) + tuple(stack(a) for a in outs_s)
```

```python
import functools

import jax
import jax.numpy as jnp
from jax import lax
from jax.experimental import pallas as pl
from jax.experimental.pallas import tpu as pltpu

F32 = jnp.float32
BF16 = jnp.bfloat16

LANES = 128
SUBLANES = 8
BF16_ROWS = 2 * SUBLANES
VMEM_PHYSICAL_BYTES = 64 * 1024 * 1024
VMEM_LIMIT_CAP_BYTES = VMEM_PHYSICAL_BYTES - 6 * 1024 * 1024
VMEM_SLACK_BYTES = 8 * 1024 * 1024

LRU_C = 8.0
EPS = 1e-6
N_MOD = 6
MLP_MOD_FIRST = 3
SSM_GROUP = 16
SSM_STATE = 64
GROUPS_PER_BLOCK = LANES // SSM_GROUP
STATES_PER_BLOCK = GROUPS_PER_BLOCK * SSM_STATE

EPILOGUE_ROWS = 256
MLP_OUT_CHUNK = 512
INPROJ_CHUNK = 256
S5_FOLD = 8


def _vmem_limit(nbytes):
    return int(min(VMEM_LIMIT_CAP_BYTES, max(32 * 1024 * 1024, nbytes + VMEM_SLACK_BYTES)))


def _rows(ref, rows=None):
    if len(ref.shape) == 3:
        return ref[0]
    return ref[...] if rows is None else ref[rows, :]


def _row_groups(n_rows, group, fn, unroll):
    assert n_rows % group == 0

    def body(i, carry):
        fn(pl.ds(pl.multiple_of(i * group, group), group))
        return carry

    lax.fori_loop(0, n_rows // group, body, 0, unroll=unroll)


def _rms_scale_pass(src, inv_scr, d):
    def fn(rows):
        x = src(rows)
        xx = x * x
        part = xx[:, 0:LANES]
        for c in range(1, d // LANES):
            part = part + xx[:, c * LANES:(c + 1) * LANES]
        inv_scr[rows, :] = part

    _row_groups(inv_scr.shape[0], SUBLANES, fn, unroll=8)
    ss = jnp.sum(inv_scr[...], axis=-1, keepdims=True)
    inv_scr[...] = jnp.broadcast_to(lax.rsqrt(ss * (1.0 / d) + EPS), inv_scr.shape)


def _lanes(inv, d):
    return jnp.concatenate([inv] * (d // LANES), axis=1)


def _sublane_rows(scr, slot, row=None):
    rows = slice(slot * SUBLANES, (slot + 1) * SUBLANES)
    if row is None:
        return scr[rows, :]
    scr[rows, :] = jnp.broadcast_to(row, (SUBLANES, row.shape[1]))
    return None


def _norm_mod_bf16(x_ref, g_ref, sc_ref, sh_ref, h_scr, inv_scr, row_scr, x_copy_ref=None):
    n_rows, d = x_ref.shape
    _rms_scale_pass(lambda rows: x_ref[rows, :], inv_scr, d)
    per_token = len(sc_ref.shape) == 2
    if not per_token:
        _sublane_rows(row_scr, 0, g_ref[...] * (1.0 + sc_ref[0]))
        _sublane_rows(row_scr, 1, sh_ref[0])

    def fn(rows):
        halves = []
        for k in range(BF16_ROWS // SUBLANES):
            r8 = pl.ds(pl.multiple_of(rows.start + k * SUBLANES, SUBLANES), SUBLANES)
            gm = g_ref[...] * (1.0 + sc_ref[r8, :]) if per_token else _sublane_rows(row_scr, 0)
            sh = sh_ref[r8, :] if per_token else _sublane_rows(row_scr, 1)
            x = x_ref[r8, :]
            if x_copy_ref is not None:
                x_copy_ref[r8, :] = x
            halves.append((x * _lanes(inv_scr[r8, :], d)) * gm + sh)
        h_scr[rows, :] = jnp.concatenate(halves, axis=0).astype(BF16)

    _row_groups(n_rows, BF16_ROWS, fn, unroll=2)


def _dot(a, b):
    return jnp.dot(a, b, preferred_element_type=F32)


def _sigmoid(x):
    return 0.5 * jnp.tanh(0.5 * x) + 0.5


def _side_cast_specs(side, n_steps):
    in_specs, out_specs, out_shapes, nbytes = [], [], [], 0
    for arr, axis in side:
        blk = tuple(s // n_steps if a == axis else s for a, s in enumerate(arr.shape))
        assert arr.ndim == 2 and arr.shape[axis] % n_steps == 0 and blk[1] % LANES == 0 and blk[0] % BF16_ROWS == 0
        idx = (lambda i: (i, 0)) if axis == 0 else (lambda i: (0, i))
        in_specs.append(pl.BlockSpec(blk, idx))
        out_specs.append(pl.BlockSpec(blk, idx))
        out_shapes.append(jax.ShapeDtypeStruct(arr.shape, BF16))
        nbytes += 2 * blk[0] * blk[1] * (4 + 2)
    return in_specs, out_specs, out_shapes, nbytes


def _side_cast(src_refs, dst_refs):
    for src, dst in zip(src_refs, dst_refs, strict=True):
        dst[...] = src[...].astype(BF16)


def _mod_kernel(c_ref, w_ref, b_ref, os_ref, op_ref):
    c = c_ref[...]
    cs = (c * _sigmoid(c)).astype(BF16)
    mod = _dot(cs, w_ref[...].astype(BF16)) + b_ref[...]
    n_sample = os_ref.shape[0]
    os_ref[...] = mod[:n_sample]
    op_ref[...] = mod[n_sample:]


def _mod_call(c_all, n_sample, w_ada, b_ada, n, tn=2048):
    n_rows, d = c_all.shape
    assert n % tn == 0
    est = 2 * (d * tn * 4) + 3 * n_rows * tn * 4 + n_rows * d * 4 * 2 + d * tn * 2
    return pl.pallas_call(
        _mod_kernel,
        grid=(n // tn,),
        in_specs=[pl.BlockSpec((n_rows, d), lambda j: (0, 0)),
                  pl.BlockSpec((d, tn), lambda j: (0, j)),
                  pl.BlockSpec((1, tn), lambda j: (0, j))],
        out_specs=[pl.BlockSpec((n_sample, tn), lambda j: (0, j)),
                   pl.BlockSpec((n_rows - n_sample, tn), lambda j: (0, j))],
        out_shape=[jax.ShapeDtypeStruct((n_sample, n), F32),
                   jax.ShapeDtypeStruct((n_rows - n_sample, n), F32)],
        compiler_params=pltpu.CompilerParams(dimension_semantics=("parallel",),
                                             vmem_limit_bytes=_vmem_limit(est)),
        name="mod",
    )(c_all, w_ada, b_ada)


def _mod_spec(mod, piece, d, tiles_per_batch, ngrid):
    if mod.ndim == 3:
        if ngrid == 1:
            return pl.BlockSpec((1, 1, d), lambda i: (i // tiles_per_batch, 0, piece))
        return pl.BlockSpec((1, 1, d), lambda i, j: (i // tiles_per_batch, 0, piece))
    rows = mod.shape[0]
    if ngrid == 1:
        return pl.BlockSpec((rows, d), lambda i: (0, piece))
    return pl.BlockSpec((rows, d), lambda i, j: (0, piece))


def _x_tile_prologue(x_hbm, x_buf, x_sem, prologue):
    i, j = pl.program_id(0), pl.program_id(1)
    n_rows = x_buf.shape[0]

    def x_tile(tile):
        return pltpu.make_async_copy(x_hbm.at[pl.ds(tile * n_rows, n_rows), :], x_buf, x_sem)

    @pl.when(j == 0)
    def _():
        @pl.when(i == 0)
        def _():
            x_tile(0).start()

        x_tile(i).wait()
        prologue()

        @pl.when(i + 1 < pl.num_programs(0))
        def _():
            x_tile(i + 1).start()


def _inproj_kernel(*refs, n_u_tiles, emit_w):
    x_hbm, sc_ref, sh_ref, g_ref, w_ref, u_ref, z_ref = refs[:7]
    x_buf, x_sem, h_scr, inv_scr, row_scr = refs[-5:]
    j = pl.program_id(1)
    _x_tile_prologue(x_hbm, x_buf, x_sem,
                     lambda: _norm_mod_bf16(x_buf, g_ref, sc_ref, sh_ref, h_scr, inv_scr, row_scr))

    if emit_w:
        wb_ref = refs[7]
        wb_ref[...] = w_ref[...].astype(BF16)
        w_ref = wb_ref
    tn = w_ref.shape[1]
    nc = min(tn, INPROJ_CHUNK)

    def project(out_ref):
        for c in range(tn // nc):
            cols = slice(c * nc, (c + 1) * nc)
            out_ref[:, cols] = _dot(h_scr[...], w_ref[:, cols]).astype(out_ref.dtype)

    pl.when(j < n_u_tiles)(lambda: project(u_ref))
    pl.when(j >= n_u_tiles)(lambda: project(z_ref))


def _inproj_call(x2d, mod, g1, w_in, *, tm, tiles_per_batch, w_mix, tn=1024):
    m, d = x2d.shape
    n = w_in.shape[1]
    n_u = 2 * w_mix
    n_u_tiles = n_u // tn
    emit_w = w_in.dtype != BF16
    assert not emit_w or m == tm
    wbytes = w_in.dtype.itemsize
    est = (tm * d * 4 + 2 * d * tn * wbytes + 2 * tm * tn * 4 + 2 * tm * tn * 2 + tm * d * 2 + tm * tn * 4
           + (3 * d * tn * 2 if emit_w else 0))
    out_specs = [pl.BlockSpec((tm, tn), lambda i, j: (i, jnp.minimum(j, n_u_tiles - 1))),
                 pl.BlockSpec((tm, tn), lambda i, j: (i, jnp.maximum(j - n_u_tiles, 0)))]
    out_shape = [jax.ShapeDtypeStruct((m, n_u), F32),
                 jax.ShapeDtypeStruct((m, n - n_u), BF16)]
    if emit_w:
        out_specs.append(pl.BlockSpec((d, tn), lambda i, j: (0, j)))
        out_shape.append(jax.ShapeDtypeStruct((d, n), BF16))
    return pl.pallas_call(
        functools.partial(_inproj_kernel, n_u_tiles=n_u_tiles, emit_w=emit_w),
        grid=(m // tm, n // tn),
        in_specs=[pl.BlockSpec(memory_space=pl.ANY),
                  _mod_spec(mod, 1, d, tiles_per_batch, 2),
                  _mod_spec(mod, 0, d, tiles_per_batch, 2),
                  pl.BlockSpec((1, d), lambda i, j: (0, 0)),
                  pl.BlockSpec((d, tn), lambda i, j: (0, j))],
        out_specs=out_specs,
        out_shape=out_shape,
        scratch_shapes=[pltpu.VMEM((tm, d), F32), pltpu.SemaphoreType.DMA(()),
                        pltpu.VMEM((tm, d), BF16), pltpu.VMEM((tm, LANES), F32),
                        pltpu.VMEM((2 * SUBLANES, d), F32)],
        compiler_params=pltpu.CompilerParams(dimension_semantics=("arbitrary", "arbitrary"),
                                             vmem_limit_bytes=_vmem_limit(est)),
        name="inproj",
    )(x2d, mod, mod, g1, w_in)


def _lru_gate_block(uc, g, ba, bx, sp):
    r = _sigmoid(g[:, :LANES] + ba)
    i = _sigmoid(g[:, LANES:] + bx)
    log_a = (-LRU_C * r) * sp
    a = jnp.exp(log_a)
    m2 = 1.0 - jnp.exp(2.0 * log_a)
    mult = jnp.where(m2 > 0.0, m2 * lax.rsqrt(m2), 0.0)
    return a, (mult * i) * uc


def _lru_kernel(*refs, n_side):
    u_ref, cw_ref, cb_ref, wg_ref, ba_ref, bx_ref, lam_ref = refs[:7]
    hs_ref, nconv_ref, nlru_ref = refs[7 + n_side:10 + n_side]
    ext, a_s, b_s, carry = refs[-4:]
    _side_cast(refs[7:7 + n_side], refs[10 + n_side:10 + 2 * n_side])
    t = pl.program_id(0)
    nt = pl.num_programs(0)
    nbat, tl, w = u_ref.shape
    nblk = w // LANES
    rows = nbat * tl
    kw = cw_ref.shape[0]
    hist = ext.shape[1] - rows
    steps_per_tile = SUBLANES // nbat

    @pl.when(t == 0)
    def _():
        ext[:, 0:hist, :] = jnp.zeros((nblk, hist, LANES), F32)
        carry[...] = jnp.zeros_like(carry)

    for b in range(nbat):
        for c in range(nblk):
            ext[c, pl.ds(hist + b, tl, stride=nbat), :] = u_ref[b, :, c * LANES:(c + 1) * LANES]

        @pl.when(t == nt - 1)
        def _(b=b):
            nconv_ref[b] = u_ref[b, tl - (kw - 1):tl, :]

    sp = jax.nn.softplus(-lam_ref[...])
    for c in range(nblk):
        cols = slice(c * LANES, (c + 1) * LANES)
        uc = cb_ref[:, cols]
        for k in range(kw):
            start = hist - (kw - 1 - k) * nbat
            uc = uc + ext[c, start:start + rows, :] * cw_ref[k:k + 1, cols]
        g = _dot(uc.astype(BF16), wg_ref[c])
        a, b = _lru_gate_block(uc, g, ba_ref[:, cols], bx_ref[:, cols], sp[:, cols])
        a_s[:, cols] = a
        b_s[:, cols] = b
        ext[c, hist - (kw - 1) * nbat:hist, :] = ext[c, hist + rows - (kw - 1) * nbat:hist + rows, :]

    def tile_step(i, hcur):
        r0 = pl.multiple_of(i * SUBLANES, SUBLANES)
        at, bt = a_s[pl.ds(r0, SUBLANES), :], b_s[pl.ds(r0, SUBLANES), :]
        hs = []
        for k in range(steps_per_tile):
            hcur = at[k * nbat:(k + 1) * nbat] * hcur + bt[k * nbat:(k + 1) * nbat]
            hs.append(hcur)
        htile = jnp.concatenate(hs, axis=0)
        for c in range(nblk):
            ext[c, pl.ds(pl.multiple_of(hist + r0, SUBLANES), SUBLANES), :] = htile[:, c * LANES:(c + 1) * LANES]
        return hcur

    h_end = lax.fori_loop(0, tl // steps_per_tile, tile_step, carry[...], unroll=2)
    carry[...] = h_end

    for b in range(nbat):
        for c in range(nblk):
            hs_ref[b, :, c * LANES:(c + 1) * LANES] = ext[c, pl.ds(hist + b, tl, stride=nbat), :].astype(BF16)

    @pl.when(t == nt - 1)
    def _():
        for b in range(nbat):
            nlru_ref[b] = h_end[b:b + 1, :]


def _lru_call(u3d, cw, cb, wg, ba, bx, lam, *, tl, side=()):
    bsz, seq, _ = u3d.shape
    w = cw.shape[1]
    kw = cw.shape[0]
    nblk = w // LANES
    assert SUBLANES % bsz == 0
    rows = bsz * tl
    hist = -(-(kw - 1) * bsz // SUBLANES) * SUBLANES
    side_in, side_out, side_shapes, side_bytes = _side_cast_specs(side, seq // tl)
    est = (2 * rows * w * 4 + 2 * rows * w * 2 + (hist + rows) * w * 4 + 2 * rows * w * 4
           + nblk * LANES * 2 * LANES * 2 * 2 + 2 * rows * 2 * LANES * 4 + side_bytes)
    const2 = lambda t: (0, 0)
    return pl.pallas_call(
        functools.partial(_lru_kernel, n_side=len(side)),
        grid=(seq // tl,),
        in_specs=[pl.BlockSpec((bsz, tl, w), lambda t: (0, t, 0)),
                  pl.BlockSpec((kw, w), const2),
                  pl.BlockSpec((1, w), const2),
                  pl.BlockSpec((nblk, LANES, 2 * LANES), lambda t: (0, 0, 0)),
                  pl.BlockSpec((1, w), const2),
                  pl.BlockSpec((1, w), const2),
                  pl.BlockSpec((1, w), const2)] + side_in,
        out_specs=[pl.BlockSpec((bsz, tl, w), lambda t: (0, t, 0)),
                   pl.BlockSpec((bsz, kw - 1, w), lambda t: (0, 0, 0)),
                   pl.BlockSpec((bsz, 1, w), lambda t: (0, 0, 0))] + side_out,
        out_shape=[jax.ShapeDtypeStruct((bsz, seq, w), BF16),
                   jax.ShapeDtypeStruct((bsz, kw - 1, w), F32),
                   jax.ShapeDtypeStruct((bsz, 1, w), F32)] + side_shapes,
        scratch_shapes=[pltpu.VMEM((nblk, hist + rows, LANES), F32),
                        pltpu.VMEM((rows, w), F32),
                        pltpu.VMEM((rows, w), F32),
                        pltpu.VMEM((bsz, w), F32)],
        compiler_params=pltpu.CompilerParams(dimension_semantics=("arbitrary",),
                                             vmem_limit_bytes=_vmem_limit(est)),
        name="lru_scan",
    )(u3d, cw, cb, wg, ba, bx, lam, *(a for a, _ in side))


def _lru_step_kernel(u_ref, buf_ref, h0_ref, cw_ref, cb_ref, wg_ref, ba_ref, bx_ref, lam_ref,
                     hs_ref, nconv_ref, nlru_ref):
    w = cw_ref.shape[1]
    kw = cw_ref.shape[0]
    nblk = w // LANES
    u = u_ref[...]
    uc = cb_ref[...] + buf_ref[0] * cw_ref[0:1, :]
    for k in range(1, kw - 1):
        uc = uc + buf_ref[k] * cw_ref[k:k + 1, :]
    uc = uc + u * cw_ref[kw - 1:kw, :]
    for k in range(kw - 2):
        nconv_ref[k] = buf_ref[k + 1]
    nconv_ref[kw - 2] = u
    sp = jax.nn.softplus(-lam_ref[...])
    for h in range(nblk):
        cols = slice(h * LANES, (h + 1) * LANES)
        uch = uc[:, cols]
        g = _dot(uch.astype(BF16), wg_ref[h])
        a, b = _lru_gate_block(uch, g, ba_ref[:, cols], bx_ref[:, cols], sp[:, cols])
        hn = a * h0_ref[:, cols] + b
        nlru_ref[:, cols] = hn
        hs_ref[:, cols] = hn.astype(BF16)


def _lru_step_call(u2d, buf, h0, cw, cb, wg, ba, bx, lam):
    n = h0.shape[0]
    w = cw.shape[1]
    kw = cw.shape[0]
    nblk = w // LANES
    c2 = lambda i: (0, 0)
    c3 = lambda i: (0, 0, 0)
    return pl.pallas_call(
        _lru_step_kernel,
        grid=(1,),
        in_specs=[pl.BlockSpec((n, w), c2),
                  pl.BlockSpec((kw - 1, n, w), c3),
                  pl.BlockSpec((n, w), c2),
                  pl.BlockSpec((kw, w), c2),
                  pl.BlockSpec((1, w), c2),
                  pl.BlockSpec((nblk, LANES, 2 * LANES), c3),
                  pl.BlockSpec((1, w), c2),
                  pl.BlockSpec((1, w), c2),
                  pl.BlockSpec((1, w), c2)],
        out_specs=[pl.BlockSpec((n, w), c2),
                   pl.BlockSpec((kw - 1, n, w), c3),
                   pl.BlockSpec((n, w), c2)],
        out_shape=[jax.ShapeDtypeStruct((n, w), BF16),
                   jax.ShapeDtypeStruct((kw - 1, n, w), F32),
                   jax.ShapeDtypeStruct((n, w), F32)],
        name="lru_step",
    )(u2d, buf, h0, cw, cb, wg, ba, bx, lam)


def _abar(ar, ai, dt):
    mag = jnp.exp(dt * ar)
    ang = dt * ai
    return mag * jnp.cos(ang), mag * jnp.sin(ang)


def _s5_prep_kernel(ag_ref, dtg_ref, bt_ref, ct_ref, al_ref, dtl_ref, bb_ref, cc_ref, pw_ref):
    ar, ai = ag_ref[0], ag_ref[1]
    abr, abi = _abar(ar, ai, jnp.exp(dtg_ref[...]))
    den = ar * ar + ai * ai
    nr, ni = abr - 1.0, abi
    n_k = bt_ref.shape[1] // ar.shape[0]
    per_channel = lambda q: jnp.concatenate(
        [jnp.broadcast_to(q[g:g + 1, :], (n_k, q.shape[1])) for g in range(q.shape[0])], axis=0)
    q_re = per_channel((nr * ar + ni * ai) / den)
    q_im = per_channel((ni * ar - nr * ai) / den)
    br, bi = bt_ref[0], bt_ref[1]
    for ref, pair in ((bb_ref, (q_re * br - q_im * bi, q_re * bi + q_im * br)), (cc_ref, (ct_ref[0], ct_ref[1]))):
        for part, v in enumerate(pair):
            ref[part] = jnp.concatenate([v, v], axis=1)

    pr, pi = _abar(al_ref[0], al_ref[1], jnp.exp(dtl_ref[...]))
    qr, qi = jnp.ones_like(pr), jnp.zeros_like(pr)
    for m in range(pw_ref.shape[1]):
        pw_ref[0, m:m + 1, :] = qr
        pw_ref[1, m:m + 1, :] = qi
        qr, qi = qr * pr - qi * pi, qr * pi + qi * pr


def _s5_prep_call(a_re, a_im, log_dt, b_re, b_im, c_re, c_im, n_pow):
    g, p, k = b_re.shape
    assert c_re.shape == (g, k, p) and 2 * p == LANES
    a = jnp.stack([a_re, a_im])
    return pl.pallas_call(
        _s5_prep_kernel,
        out_shape=[jax.ShapeDtypeStruct((2, g * k, 2 * p), F32), jax.ShapeDtypeStruct((2, g * k, 2 * p), F32),
                   jax.ShapeDtypeStruct((2, n_pow, g * p), F32)],
        name="s5_prep",
    )(a, log_dt.reshape(g, 1),
      jnp.stack([b_re, b_im]).swapaxes(2, 3).reshape(2, g * k, p), jnp.stack([c_re, c_im]).reshape(2, g * k, p),
      a.reshape(2, 1, g * p), jnp.repeat(log_dt, p).reshape(1, g * p))


def _block_diag(stack):
    tiled = jnp.concatenate([stack] * (STATES_PER_BLOCK // stack.shape[1]), axis=1)
    own = (lax.broadcasted_iota(jnp.int32, tiled.shape, 0) // SSM_GROUP
           == lax.broadcasted_iota(jnp.int32, tiled.shape, 1) // SSM_STATE)
    return jnp.where(own, tiled, 0.0)


def _cplx_mul(ar, ai, br, bi):
    return ar * br - ai * bi, ar * bi + ai * br


def _cplx_step(ar, ai, hr, hi, br, bi):
    return ar * hr - ai * hi + br, ar * hi + ai * hr + bi


def _s5_one_step(us_ref, h0r_ref, h0i_ref, bre, bim, ctre, ctim, ar, ai, d_ref, gys_ref, nres_ref, nims_ref):
    us = us_ref[...]
    ust = us.T.astype(BF16)
    ar = jnp.broadcast_to(ar, bre.shape).T
    ai = jnp.broadcast_to(ai, bre.shape).T
    hr, hi = _cplx_step(ar, ai, h0r_ref[...], h0i_ref[...],
                        _dot(bre.T.astype(BF16), ust), _dot(bim.T.astype(BF16), ust))
    nres_ref[...] = hr
    nims_ref[...] = hi
    yt = _dot(ctre.astype(BF16), hr.astype(BF16)) - _dot(ctim.astype(BF16), hi.astype(BF16))
    gys_ref[...] = jax.nn.gelu(yt.T + d_ref[...] * us).astype(BF16)


def _s5_kernel(u_ref, bb_ref, cc_ref, pw_ref, d_ref, c_ref, wa_ref, ba_ref, us_ref, h0r_ref, h0i_ref,
               gy_ref, nre_ref, nim_ref, ms_ref, mp_ref, gys_ref, nres_ref, nims_ref,
               ufb, buf, wbs, wqs, wts, yslab, out32):
    _mod_kernel(c_ref, wa_ref, ba_ref, ms_ref, mp_ref)
    nbat, seq, _ = u_ref.shape
    spb = STATES_PER_BLOCK
    fold = wbs.shape[0] // LANES
    nq = seq // fold

    bre, bim = _block_diag(bb_ref[0]), _block_diag(bb_ref[1])
    ctre, ctim = _block_diag(cc_ref[0]), _block_diag(cc_ref[1])
    _s5_one_step(us_ref, h0r_ref, h0i_ref, bre, bim, ctre, ctim, pw_ref[0, 1:2, :], pw_ref[1, 1:2, :], d_ref,
                 gys_ref, nres_ref, nims_ref)
    wcb = jnp.concatenate([ctre.T, -ctim.T], axis=0).astype(BF16)
    lag = []
    for m in range(fold):
        wre, wim = _cplx_mul(bre, bim, pw_ref[0, m:m + 1, :], pw_ref[1, m:m + 1, :])
        rows = slice((fold - 1 - m) * LANES, (fold - m) * LANES)
        wbs[rows, :spb] = wre.astype(BF16)
        wbs[rows, spb:] = wim.astype(BF16)
        lag.append(_dot(wbs[rows, :], wcb))
    zero_blk = jnp.zeros((LANES, LANES), BF16)
    for r_in in range(fold):
        for r_out in range(fold):
            blk = lag[r_out - r_in].astype(BF16) if r_in <= r_out else zero_blk
            wts[r_in * LANES:(r_in + 1) * LANES, r_out * LANES:(r_out + 1) * LANES] = blk
    for r in range(fold):
        pr, pi = pw_ref[0, r + 1:r + 2, :], pw_ref[1, r + 1:r + 2, :]
        cols = slice(r * LANES, (r + 1) * LANES)
        wqs[:spb, cols] = (ctre * pr - ctim * pi).T.astype(BF16)
        wqs[spb:, cols] = (-(ctre * pi + ctim * pr)).T.astype(BF16)

    for r in range(fold):
        for b in range(nbat):
            yslab[r, pl.ds(b, nq, stride=nbat), :] = u_ref[b, pl.ds(r, nq, stride=fold), :]
        ufb[:, r * LANES:(r + 1) * LANES] = yslab[r].astype(BF16)

    buf[...] = _dot(ufb[...], wbs[...])

    ar, ai = pw_ref[0, fold:fold + 1, :], pw_ref[1, fold:fold + 1, :]
    groups_per_tile = SUBLANES // nbat
    z = jnp.zeros((nbat, spb), F32)

    def tile_step(i, g):
        gr, gi = g
        r0 = pl.multiple_of(i * SUBLANES, SUBLANES)
        xr, xi = buf[pl.ds(r0, SUBLANES), :spb], buf[pl.ds(r0, SUBLANES), spb:]
        hr, hi = [], []
        for k in range(groups_per_tile):
            hr.append(gr)
            hi.append(gi)
            gr, gi = _cplx_step(ar, ai, gr, gi, xr[k * nbat:(k + 1) * nbat], xi[k * nbat:(k + 1) * nbat])
        buf[pl.ds(r0, SUBLANES), :spb] = jnp.concatenate(hr, axis=0)
        buf[pl.ds(r0, SUBLANES), spb:] = jnp.concatenate(hi, axis=0)
        return gr, gi

    g_re, g_im = lax.fori_loop(0, nq // groups_per_tile, tile_step, (z, z), unroll=2)
    for b in range(nbat):
        nre_ref[b] = g_re[b:b + 1, :]
        nim_ref[b] = g_im[b:b + 1, :]

    hsb = buf[...].astype(BF16)
    per_tile = 2
    for n in range(fold // per_tile):
        cols = slice(n * per_tile * LANES, (n + 1) * per_tile * LANES)
        kk = (n + 1) * per_tile * LANES
        yn = _dot(hsb, wqs[:, cols]) + _dot(ufb[:, :kk], wts[:kk, cols])
        for i in range(per_tile):
            yslab[n * per_tile + i] = yn[:, i * LANES:(i + 1) * LANES]

    for b in range(nbat):
        for r in range(fold):
            ys = (yslab[r, pl.ds(b, nq, stride=nbat), :]
                  + d_ref[...] * u_ref[b, pl.ds(r, nq, stride=fold), :])
            out32[b, pl.ds(r, nq, stride=fold), :] = jax.nn.gelu(ys)
        gy_ref[b] = out32[b].astype(BF16)


def _s5_call(u3d, bb, cc, pw, d, c_all, n_sample, w_ada, b_ada, first_col, u_step, h0r, h0i, *, w):
    bsz, seq, _ = u3d.shape
    nblk = w // LANES
    spb = STATES_PER_BLOCK
    fold = S5_FOLD
    assert SUBLANES % bsz == 0 and seq % fold == 0 and pw.shape[1] == fold + 1
    n_state, n_step = h0r.shape
    assert n_state == nblk * spb and n_step == LANES
    state_spec = pl.BlockSpec((spb, n_step), lambda c: (c, 0))
    state_shape = jax.ShapeDtypeStruct((n_state, n_step), F32)
    rows = bsz * (seq // fold)
    n_c, dm = c_all.shape
    mod_cols = w_ada.shape[1] - first_col
    tn = mod_cols // nblk
    assert mod_cols % nblk == 0 and tn % LANES == 0 and first_col % tn == 0
    est = (2 * bsz * seq * LANES * 4 + 2 * bsz * seq * LANES * 2
           + 2 * dm * tn * 4 + dm * tn * 2 + 2 * n_c * dm * 4 + 3 * n_c * tn * 4
           + rows * fold * LANES * 2 + 2 * rows * 2 * spb * 4
           + 3 * fold * LANES * 2 * spb * 2
           + rows * fold * LANES * 4 + bsz * seq * LANES * 4
           + rows * 2 * spb * 2
           + 2 * (LANES * 2 * spb + 2 * spb * LANES) * 4)
    return pl.pallas_call(
        _s5_kernel,
        grid=(nblk,),
        in_specs=[pl.BlockSpec((bsz, seq, LANES), lambda c: (0, 0, nblk + c)),
                  pl.BlockSpec((2, LANES, LANES), lambda c: (0, c, 0)),
                  pl.BlockSpec((2, LANES, LANES), lambda c: (0, c, 0)),
                  pl.BlockSpec((2, fold + 1, spb), lambda c: (0, 0, c)),
                  pl.BlockSpec((1, LANES), lambda c: (0, c)),
                  pl.BlockSpec((n_c, dm), lambda c: (0, 0)),
                  pl.BlockSpec((dm, tn), lambda c: (0, first_col // tn + c)),
                  pl.BlockSpec((1, tn), lambda c: (0, first_col // tn + c)),
                  pl.BlockSpec((n_step, LANES), lambda c: (0, nblk + c)),
                  state_spec,
                  state_spec],
        out_specs=[pl.BlockSpec((bsz, seq, LANES), lambda c: (0, 0, c)),
                   pl.BlockSpec((bsz, 1, spb), lambda c: (0, 0, c)),
                   pl.BlockSpec((bsz, 1, spb), lambda c: (0, 0, c)),
                   pl.BlockSpec((n_sample, tn), lambda c: (0, c)),
                   pl.BlockSpec((n_c - n_sample, tn), lambda c: (0, c)),
                   pl.BlockSpec((n_step, LANES), lambda c: (0, c)),
                   state_spec,
                   state_spec],
        out_shape=[jax.ShapeDtypeStruct((bsz, seq, w), BF16),
                   jax.ShapeDtypeStruct((bsz, 1, nblk * spb), F32),
                   jax.ShapeDtypeStruct((bsz, 1, nblk * spb), F32),
                   jax.ShapeDtypeStruct((n_sample, mod_cols), F32),
                   jax.ShapeDtypeStruct((n_c - n_sample, mod_cols), F32),
                   jax.ShapeDtypeStruct((n_step, w), BF16),
                   state_shape,
                   state_shape],
        scratch_shapes=[pltpu.VMEM((rows, fold * LANES), BF16),
                        pltpu.VMEM((rows, 2 * spb), F32),
                        pltpu.VMEM((fold * LANES, 2 * spb), BF16),
                        pltpu.VMEM((2 * spb, fold * LANES), BF16),
                        pltpu.VMEM((fold * LANES, fold * LANES), BF16),
                        pltpu.VMEM((fold, rows, LANES), F32),
                        pltpu.VMEM((bsz, seq, LANES), F32)],
        compiler_params=pltpu.CompilerParams(dimension_semantics=("parallel",),
                                             vmem_limit_bytes=_vmem_limit(est)),
        name="s5_scan",
    )(u3d, bb, cc, pw, d, c_all, w_ada, b_ada, u_step, h0r, h0i)


def _merge_kernel(*refs, n_side):
    hs_ref, gy_ref, z_ref, x_ref, gt_ref, wp_ref, wg_ref, wo_ref = refs[:8]
    o_ref = refs[8 + n_side]
    _side_cast(refs[8:8 + n_side], refs[9 + n_side:])
    d = x_ref.shape[1]
    ya = _dot(hs_ref[...], wp_ref[...])
    merged = _sigmoid(z_ref[:, :d].astype(F32)) * ya
    glu = _dot(gy_ref[...], wg_ref[...])
    yb = glu[:, :d] * _sigmoid(glu[:, d:])
    merged = merged + _sigmoid(z_ref[:, d:].astype(F32)) * yb
    o = _dot(merged.astype(BF16), wo_ref[...])
    o_ref[...] = x_ref[...] + _rows(gt_ref) * o


def _merge_call(hs, gy, sz, x2d, mod, wp, wg, wo, *, tm, tiles_per_batch, side=()):
    m, d = x2d.shape
    w = hs.shape[1]
    side_in, side_out, side_shapes, side_bytes = _side_cast_specs(side, m // tm)
    est = (2 * (2 * tm * w * 2 + tm * 2 * d * 2 + 2 * tm * d * 4) + (wp.size + wg.size + wo.size) * 2
           + tm * d * 4 * 5 + side_bytes)
    one = pl.Buffered(1)
    return pl.pallas_call(
        functools.partial(_merge_kernel, n_side=len(side)),
        grid=(m // tm,),
        in_specs=[pl.BlockSpec((tm, w), lambda i: (i, 0)),
                  pl.BlockSpec((tm, w), lambda i: (i, 0)),
                  pl.BlockSpec((tm, 2 * d), lambda i: (i, 0)),
                  pl.BlockSpec((tm, d), lambda i: (i, 0)),
                  _mod_spec(mod, 2, d, tiles_per_batch, 1),
                  pl.BlockSpec(wp.shape, lambda i: (0, 0), pipeline_mode=one),
                  pl.BlockSpec(wg.shape, lambda i: (0, 0), pipeline_mode=one),
                  pl.BlockSpec(wo.shape, lambda i: (0, 0), pipeline_mode=one)] + side_in,
        out_specs=[pl.BlockSpec((tm, d), lambda i: (i, 0))] + side_out,
        out_shape=[jax.ShapeDtypeStruct((m, d), F32)] + side_shapes,
        compiler_params=pltpu.CompilerParams(dimension_semantics=("parallel",),
                                             vmem_limit_bytes=_vmem_limit(est)),
        name="merge",
    )(hs, gy, sz, x2d, mod, wp, wg, wo, *(a for a, _ in side))


def _mlp_kernel(x_hbm, sc_ref, sh_ref, gt_ref, g2_ref, gf_ref, wu_ref, wd_ref, o_ref,
                x_buf, x_sem, h_scr, inv_scr, row_scr, *, final_norm):
    n_rows, d = x_buf.shape
    _x_tile_prologue(x_hbm, x_buf, x_sem,
                     lambda: _norm_mod_bf16(x_buf, g2_ref, sc_ref, sh_ref, h_scr, inv_scr, row_scr, x_copy_ref=o_ref))

    up = _dot(h_scr[...], wu_ref[...])
    act = jnp.square(jnp.maximum(up, 0.0)).astype(BF16)
    nc = min(d, MLP_OUT_CHUNK)
    for c in range(d // nc):
        cols = slice(c * nc, (c + 1) * nc)
        gate = gt_ref[0, :, cols] if len(gt_ref.shape) == 3 else gt_ref[:, cols]
        o_ref[:, cols] += gate * _dot(act, wd_ref[:, cols])

    if final_norm:
        @pl.when(pl.program_id(1) == pl.num_programs(1) - 1)
        def _():
            def chunk(rows):
                x2 = o_ref[rows, :]
                ms = jnp.mean(x2 * x2, axis=-1, keepdims=True)
                o_ref[rows, :] = (x2 * lax.rsqrt(ms + EPS)) * gf_ref[...]

            _row_groups(n_rows, min(n_rows, EPILOGUE_ROWS), chunk, unroll=1)


def _mlp_call(x2d, mod, g2, gf, w_up, w_down, *, tm, tiles_per_batch, final_norm, tf):
    m, d = x2d.shape
    dff = w_up.shape[1]
    assert w_up.dtype == BF16 and w_down.dtype == BF16
    est = 3 * tm * d * 4 + tm * d * 2 + 4 * d * tf * 2 + tm * tf * 6 + tm * MLP_OUT_CHUNK * 4
    return pl.pallas_call(
        functools.partial(_mlp_kernel, final_norm=final_norm),
        grid=(m // tm, dff // tf),
        in_specs=[pl.BlockSpec(memory_space=pl.ANY),
                  _mod_spec(mod, 4 - MLP_MOD_FIRST, d, tiles_per_batch, 2),
                  _mod_spec(mod, 3 - MLP_MOD_FIRST, d, tiles_per_batch, 2),
                  _mod_spec(mod, 5 - MLP_MOD_FIRST, d, tiles_per_batch, 2),
                  pl.BlockSpec((1, d), lambda i, j: (0, 0)),
                  pl.BlockSpec((1, d), lambda i, j: (0, 0)),
                  pl.BlockSpec((d, tf), lambda i, j: (0, j)),
                  pl.BlockSpec((tf, d), lambda i, j: (j, 0))],
        out_specs=pl.BlockSpec((tm, d), lambda i, j: (i, 0)),
        out_shape=jax.ShapeDtypeStruct((m, d), F32),
        scratch_shapes=[pltpu.VMEM((tm, d), F32), pltpu.SemaphoreType.DMA(()),
                        pltpu.VMEM((tm, d), BF16), pltpu.VMEM((tm, LANES), F32),
                        pltpu.VMEM((2 * SUBLANES, d), F32)],
        compiler_params=pltpu.CompilerParams(dimension_semantics=("arbitrary", "arbitrary"),
                                             vmem_limit_bytes=_vmem_limit(est)),
        name="mlp",
    )(x2d, mod, mod, mod, g2, gf, w_up, w_down)


def _pick_tile(n, pref):
    t = min(n, pref)
    assert n % t == 0, (n, t)
    return t


def kernel(x_prompt, x_sample, state_conv, state_lru, state_ssm_re, state_ssm_im, c_prompt, c_sample, w_ada, b_ada, g_norm1, g_norm2, w_in, conv_w, conv_b, w_rg_a, b_rg_a, w_rg_x, b_rg_x, lru_lambda, w_proj_a, ssm_a_re, ssm_a_im, ssm_log_dt, ssm_b_re, ssm_b_im, ssm_c_re, ssm_c_im, ssm_d, w_glu, w_out, w_up, w_down, g_final):
    depth = w_ada.shape[0]
    nb, seq, d = x_prompt.shape
    ns = x_sample.shape[0]
    assert x_sample.shape[1] == 1
    w = conv_w.shape[2]
    n_state = ssm_a_re.shape[1] * ssm_a_re.shape[2]
    assert ssm_b_re.shape[2:] == (SSM_STATE, SSM_GROUP) and w % LANES == 0
    assert w_in.shape[2] == 2 * w + 2 * d

    tl = _pick_tile(seq, 256)
    tm_in = _pick_tile(seq, 1024)
    tm_mg = _pick_tile(seq, 256)
    tm_mlp = _pick_tile(seq, 1024)
    tf_prompt = _pick_tile(w_up.shape[2], 1024)
    tf_sample = _pick_tile(w_up.shape[2], 2048)

    xp = x_prompt.reshape(nb * seq, d)
    xs = x_sample.reshape(ns, d)
    assert ns % SUBLANES == 0
    pad = (-nb) % SUBLANES
    c_all = jnp.concatenate([c_sample, c_prompt, jnp.zeros((pad, d), F32)], axis=0)

    outs_p = [[] for _ in range(4)]
    outs_s = [[] for _ in range(4)]
    for l in range(depth):
        last = l == depth - 1
        b_ada_row = b_ada[l].reshape(1, -1)
        mod_s, mod_p = _mod_call(c_all, ns, w_ada[l], b_ada_row, MLP_MOD_FIRST * d)
        mod_p = mod_p.reshape(nb + pad, 1, MLP_MOD_FIRST * d)

        g1 = g_norm1[l].reshape(1, d)
        g2 = g_norm2[l].reshape(1, d)
        gf = g_final.reshape(1, d)
        cw, cb = conv_w[l], conv_b[l].reshape(1, w)
        wg = jnp.concatenate([w_rg_a[l], w_rg_x[l]], axis=2).astype(BF16)
        ba, bx, lam = b_rg_a[l].reshape(1, w), b_rg_x[l].reshape(1, w), lru_lambda[l].reshape(1, w)
        bb, cc, pw = _s5_prep_call(ssm_a_re[l], ssm_a_im[l], ssm_log_dt[l], ssm_b_re[l], ssm_b_im[l],
                                       ssm_c_re[l], ssm_c_im[l], S5_FOLD + 1)
        dskip = ssm_d[l].reshape(1, w)

        u_s, sz_s, w_in_b = _inproj_call(xs, mod_s, g1, w_in[l], tm=ns, tiles_per_batch=1, w_mix=w)

        u_p, sz_p = _inproj_call(xp, mod_p, g1, w_in_b, tm=tm_in, tiles_per_batch=seq // tm_in, w_mix=w)
        u3 = u_p.reshape(nb, seq, 2 * w)
        hs_p, nconv_p, nlru_p, wp, wgl, wo = _lru_call(u3, cw, cb, wg, ba, bx, lam, tl=tl,
                                                       side=((w_proj_a[l], 1), (w_glu[l], 1), (w_out[l], 1)))
        by_token = lambda s: jnp.transpose(s, (1, 2, 0)).reshape(n_state, ns)
        gy_p, nre_p, nim_p, mlp_mod_s, mlp_mod_p, gy_s, nre_s, nim_s = _s5_call(
            u3, bb, cc, pw, dskip, c_all, ns, w_ada[l], b_ada_row, MLP_MOD_FIRST * d,
            u_s, by_token(state_ssm_re[l]), by_token(state_ssm_im[l]), w=w)
        mlp_mod_p = mlp_mod_p.reshape(nb + pad, 1, (N_MOD - MLP_MOD_FIRST) * d)
        x1_p, w_up_b, w_down_b = _merge_call(hs_p.reshape(nb * seq, w), gy_p.reshape(nb * seq, w), sz_p, xp, mod_p,
                                             wp, wgl, wo, tm=tm_mg, tiles_per_batch=seq // tm_mg,
                                             side=((w_up[l], 1), (w_down[l], 0)))
        xp = _mlp_call(x1_p, mlp_mod_p, g2, gf, w_up_b, w_down_b, tm=tm_mlp, tiles_per_batch=seq // tm_mlp,
                       final_norm=last, tf=tf_prompt)

        buf = jnp.transpose(state_conv[l], (1, 0, 2))
        hs_s, nconv_s, nlru_s = _lru_step_call(u_s, buf, state_lru[l], cw, cb, wg, ba, bx, lam)
        (x1_s,) = _merge_call(hs_s, gy_s, sz_s, xs, mod_s, wp, wgl, wo, tm=ns, tiles_per_batch=1)
        xs = _mlp_call(x1_s, mlp_mod_s, g2, gf, w_up_b, w_down_b, tm=ns, tiles_per_batch=1, final_norm=last,
                       tf=tf_sample)

        gshape = ssm_a_re.shape[1:]
        for acc, v in zip(outs_p, (nconv_p, nlru_p.reshape(nb, w),
                                   nre_p.reshape((nb,) + gshape), nim_p.reshape((nb,) + gshape))):
            acc.append(v)
        by_group = lambda s: jnp.transpose(s.reshape(gshape + (ns,)), (2, 0, 1))
        for acc, v in zip(outs_s, (jnp.transpose(nconv_s, (1, 0, 2)), nlru_s, by_group(nre_s), by_group(nim_s))):
            acc.append(v)

    y_prompt = xp.reshape(nb, seq, d)
    y_sample = xs.reshape(ns, 1, d)
    stack = lambda a: a[0][None] if len(a) == 1 else jnp.stack(a)
    return (y_prompt, y_sample) + tuple(stack(a) for a in outs_p) + tuple(stack(a) for a in outs_s)
```

```python
import functools

import jax
import jax.numpy as jnp
from jax import lax
from jax.experimental import pallas as pl
from jax.experimental.pallas import tpu as pltpu

F32 = jnp.float32
BF16 = jnp.bfloat16

LANES = 128
SUBLANES = 8
BF16_ROWS = 2 * SUBLANES
VMEM_PHYSICAL_BYTES = 64 * 1024 * 1024
VMEM_LIMIT_CAP_BYTES = VMEM_PHYSICAL_BYTES - 6 * 1024 * 1024
VMEM_SLACK_BYTES = 8 * 1024 * 1024

LRU_C = 8.0
EPS = 1e-6
N_MOD = 6
MLP_MOD_FIRST = 3
SSM_GROUP = 16
SSM_STATE = 64
GROUPS_PER_BLOCK = LANES // SSM_GROUP
STATES_PER_BLOCK = GROUPS_PER_BLOCK * SSM_STATE

EPILOGUE_ROWS = 256
MLP_OUT_CHUNK = 512
INPROJ_CHUNK = 256
S5_FOLD = 8


def _vmem_limit(nbytes):
    return int(min(VMEM_LIMIT_CAP_BYTES, max(32 * 1024 * 1024, nbytes + VMEM_SLACK_BYTES)))


def _rows(ref, rows=None):
    if len(ref.shape) == 3:
        return ref[0]
    return ref[...] if rows is None else ref[rows, :]


def _row_groups(n_rows, group, fn, unroll):
    assert n_rows % group == 0

    def body(i, carry):
        fn(pl.ds(pl.multiple_of(i * group, group), group))
        return carry

    lax.fori_loop(0, n_rows // group, body, 0, unroll=unroll)


def _rms_scale_pass(src, inv_scr, d):
    def fn(rows):
        x = src(rows)
        xx = x * x
        part = xx[:, 0:LANES]
        for c in range(1, d // LANES):
            part = part + xx[:, c * LANES:(c + 1) * LANES]
        inv_scr[rows, :] = part

    _row_groups(inv_scr.shape[0], SUBLANES, fn, unroll=8)
    ss = jnp.sum(inv_scr[...], axis=-1, keepdims=True)
    inv_scr[...] = jnp.broadcast_to(lax.rsqrt(ss * (1.0 / d) + EPS), inv_scr.shape)


def _lanes(inv, d):
    return jnp.concatenate([inv] * (d // LANES), axis=1)


def _sublane_rows(scr, slot, row=None):
    rows = slice(slot * SUBLANES, (slot + 1) * SUBLANES)
    if row is None:
        return scr[rows, :]
    scr[rows, :] = jnp.broadcast_to(row, (SUBLANES, row.shape[1]))
    return None


def _norm_mod_bf16(x_ref, g_ref, sc_ref, sh_ref, h_scr, inv_scr, row_scr, x_copy_ref=None):
    n_rows, d = x_ref.shape
    _rms_scale_pass(lambda rows: x_ref[rows, :], inv_scr, d)
    per_token = len(sc_ref.shape) == 2
    if not per_token:
        _sublane_rows(row_scr, 0, g_ref[...] * (1.0 + sc_ref[0]))
        _sublane_rows(row_scr, 1, sh_ref[0])

    def fn(rows):
        halves = []
        for k in range(BF16_ROWS // SUBLANES):
            r8 = pl.ds(pl.multiple_of(rows.start + k * SUBLANES, SUBLANES), SUBLANES)
            gm = g_ref[...] * (1.0 + sc_ref[r8, :]) if per_token else _sublane_rows(row_scr, 0)
            sh = sh_ref[r8, :] if per_token else _sublane_rows(row_scr, 1)
            x = x_ref[r8, :]
            if x_copy_ref is not None:
                x_copy_ref[r8, :] = x
            halves.append((x * _lanes(inv_scr[r8, :], d)) * gm + sh)
        h_scr[rows, :] = jnp.concatenate(halves, axis=0).astype(BF16)

    _row_groups(n_rows, BF16_ROWS, fn, unroll=2)


def _dot(a, b):
    return jnp.dot(a, b, preferred_element_type=F32)


def _sigmoid(x):
    return 0.5 * jnp.tanh(0.5 * x) + 0.5


def _side_cast_specs(side, n_steps):
    in_specs, out_specs, out_shapes, nbytes = [], [], [], 0
    for arr, axis in side:
        blk = tuple(s // n_steps if a == axis else s for a, s in enumerate(arr.shape))
        assert arr.ndim == 2 and arr.shape[axis] % n_steps == 0 and blk[1] % LANES == 0 and blk[0] % BF16_ROWS == 0
        idx = (lambda i: (i, 0)) if axis == 0 else (lambda i: (0, i))
        in_specs.append(pl.BlockSpec(blk, idx))
        out_specs.append(pl.BlockSpec(blk, idx))
        out_shapes.append(jax.ShapeDtypeStruct(arr.shape, BF16))
        nbytes += 2 * blk[0] * blk[1] * (4 + 2)
    return in_specs, out_specs, out_shapes, nbytes


def _side_cast(src_refs, dst_refs):
    for src, dst in zip(src_refs, dst_refs, strict=True):
        dst[...] = src[...].astype(BF16)


def _mod_kernel(c_ref, w_ref, b_ref, os_ref, op_ref):
    c = c_ref[...]
    cs = (c * _sigmoid(c)).astype(BF16)
    mod = _dot(cs, w_ref[...].astype(BF16)) + b_ref[...]
    n_sample = os_ref.shape[0]
    os_ref[...] = mod[:n_sample]
    op_ref[...] = mod[n_sample:]


def _mod_call(c_all, n_sample, w_ada, b_ada, n, tn=1024):
    n_rows, d = c_all.shape
    assert n % tn == 0
    est = 2 * (d * tn * 4) + 3 * n_rows * tn * 4 + n_rows * d * 4 * 2 + d * tn * 2
    return pl.pallas_call(
        _mod_kernel,
        grid=(n // tn,),
        in_specs=[pl.BlockSpec((n_rows, d), lambda j: (0, 0)),
                  pl.BlockSpec((d, tn), lambda j: (0, j)),
                  pl.BlockSpec((1, tn), lambda j: (0, j))],
        out_specs=[pl.BlockSpec((n_sample, tn), lambda j: (0, j)),
                   pl.BlockSpec((n_rows - n_sample, tn), lambda j: (0, j))],
        out_shape=[jax.ShapeDtypeStruct((n_sample, n), F32),
                   jax.ShapeDtypeStruct((n_rows - n_sample, n), F32)],
        compiler_params=pltpu.CompilerParams(dimension_semantics=("parallel",),
                                             vmem_limit_bytes=_vmem_limit(est)),
        name="mod",
    )(c_all, w_ada, b_ada)


def _mod_spec(mod, piece, d, tiles_per_batch, ngrid):
    if mod.ndim == 3:
        if ngrid == 1:
            return pl.BlockSpec((1, 1, d), lambda i: (i // tiles_per_batch, 0, piece))
        return pl.BlockSpec((1, 1, d), lambda i, j: (i // tiles_per_batch, 0, piece))
    rows = mod.shape[0]
    if ngrid == 1:
        return pl.BlockSpec((rows, d), lambda i: (0, piece))
    return pl.BlockSpec((rows, d), lambda i, j: (0, piece))


def _x_tile_prologue(x_hbm, x_buf, x_sem, prologue):
    i, j = pl.program_id(0), pl.program_id(1)
    n_rows = x_buf.shape[0]

    def x_tile(tile):
        return pltpu.make_async_copy(x_hbm.at[pl.ds(tile * n_rows, n_rows), :], x_buf, x_sem)

    @pl.when(j == 0)
    def _():
        @pl.when(i == 0)
        def _():
            x_tile(0).start()

        x_tile(i).wait()
        prologue()

        @pl.when(i + 1 < pl.num_programs(0))
        def _():
            x_tile(i + 1).start()


def _inproj_kernel(*refs, n_u_tiles, emit_w):
    x_hbm, sc_ref, sh_ref, g_ref, w_ref, u_ref, z_ref = refs[:7]
    x_buf, x_sem, h_scr, inv_scr, row_scr = refs[-5:]
    j = pl.program_id(1)
    _x_tile_prologue(x_hbm, x_buf, x_sem,
                     lambda: _norm_mod_bf16(x_buf, g_ref, sc_ref, sh_ref, h_scr, inv_scr, row_scr))

    if emit_w:
        wb_ref = refs[7]
        wb_ref[...] = w_ref[...].astype(BF16)
        w_ref = wb_ref
    tn = w_ref.shape[1]
    nc = min(tn, INPROJ_CHUNK)

    def project(out_ref):
        for c in range(tn // nc):
            cols = slice(c * nc, (c + 1) * nc)
            out_ref[:, cols] = _dot(h_scr[...], w_ref[:, cols]).astype(out_ref.dtype)

    pl.when(j < n_u_tiles)(lambda: project(u_ref))
    pl.when(j >= n_u_tiles)(lambda: project(z_ref))


def _inproj_call(x2d, mod, g1, w_in, *, tm, tiles_per_batch, w_mix, tn=1024):
    m, d = x2d.shape
    n = w_in.shape[1]
    n_u = 2 * w_mix
    n_u_tiles = n_u // tn
    emit_w = w_in.dtype != BF16
    assert not emit_w or m == tm
    wbytes = w_in.dtype.itemsize
    est = (tm * d * 4 + 2 * d * tn * wbytes + 2 * tm * tn * 4 + 2 * tm * tn * 2 + tm * d * 2 + tm * tn * 4
           + (3 * d * tn * 2 if emit_w else 0))
    out_specs = [pl.BlockSpec((tm, tn), lambda i, j: (i, jnp.minimum(j, n_u_tiles - 1))),
                 pl.BlockSpec((tm, tn), lambda i, j: (i, jnp.maximum(j - n_u_tiles, 0)))]
    out_shape = [jax.ShapeDtypeStruct((m, n_u), F32),
                 jax.ShapeDtypeStruct((m, n - n_u), BF16)]
    if emit_w:
        out_specs.append(pl.BlockSpec((d, tn), lambda i, j: (0, j)))
        out_shape.append(jax.ShapeDtypeStruct((d, n), BF16))
    return pl.pallas_call(
        functools.partial(_inproj_kernel, n_u_tiles=n_u_tiles, emit_w=emit_w),
        grid=(m // tm, n // tn),
        in_specs=[pl.BlockSpec(memory_space=pl.ANY),
                  _mod_spec(mod, 1, d, tiles_per_batch, 2),
                  _mod_spec(mod, 0, d, tiles_per_batch, 2),
                  pl.BlockSpec((1, d), lambda i, j: (0, 0)),
                  pl.BlockSpec((d, tn), lambda i, j: (0, j))],
        out_specs=out_specs,
        out_shape=out_shape,
        scratch_shapes=[pltpu.VMEM((tm, d), F32), pltpu.SemaphoreType.DMA(()),
                        pltpu.VMEM((tm, d), BF16), pltpu.VMEM((tm, LANES), F32),
                        pltpu.VMEM((2 * SUBLANES, d), F32)],
        compiler_params=pltpu.CompilerParams(dimension_semantics=("arbitrary", "arbitrary"),
                                             vmem_limit_bytes=_vmem_limit(est)),
        name="inproj",
    )(x2d, mod, mod, g1, w_in)


def _lru_gate_block(uc, g, ba, bx, sp):
    r = _sigmoid(g[:, :LANES] + ba)
    i = _sigmoid(g[:, LANES:] + bx)
    log_a = (-LRU_C * r) * sp
    a = jnp.exp(log_a)
    m2 = 1.0 - jnp.exp(2.0 * log_a)
    mult = jnp.where(m2 > 0.0, m2 * lax.rsqrt(m2), 0.0)
    return a, (mult * i) * uc


def _lru_kernel(*refs, n_side):
    u_ref, cw_ref, cb_ref, wg_ref, ba_ref, bx_ref, lam_ref = refs[:7]
    hs_ref, nconv_ref, nlru_ref = refs[7 + n_side:10 + n_side]
    ext, a_s, b_s, carry = refs[-4:]
    _side_cast(refs[7:7 + n_side], refs[10 + n_side:10 + 2 * n_side])
    t = pl.program_id(0)
    nt = pl.num_programs(0)
    nbat, tl, w = u_ref.shape
    nblk = w // LANES
    rows = nbat * tl
    kw = cw_ref.shape[0]
    hist = ext.shape[1] - rows
    steps_per_tile = SUBLANES // nbat

    @pl.when(t == 0)
    def _():
        ext[:, 0:hist, :] = jnp.zeros((nblk, hist, LANES), F32)
        carry[...] = jnp.zeros_like(carry)

    for b in range(nbat):
        for c in range(nblk):
            ext[c, pl.ds(hist + b, tl, stride=nbat), :] = u_ref[b, :, c * LANES:(c + 1) * LANES]

        @pl.when(t == nt - 1)
        def _(b=b):
            nconv_ref[b] = u_ref[b, tl - (kw - 1):tl, :]

    sp = jax.nn.softplus(-lam_ref[...])
    for c in range(nblk):
        cols = slice(c * LANES, (c + 1) * LANES)
        uc = cb_ref[:, cols]
        for k in range(kw):
            start = hist - (kw - 1 - k) * nbat
            uc = uc + ext[c, start:start + rows, :] * cw_ref[k:k + 1, cols]
        g = _dot(uc.astype(BF16), wg_ref[c])
        a, b = _lru_gate_block(uc, g, ba_ref[:, cols], bx_ref[:, cols], sp[:, cols])
        a_s[:, cols] = a
        b_s[:, cols] = b
        ext[c, hist - (kw - 1) * nbat:hist, :] = ext[c, hist + rows - (kw - 1) * nbat:hist + rows, :]

    def tile_step(i, hcur):
        r0 = pl.multiple_of(i * SUBLANES, SUBLANES)
        at, bt = a_s[pl.ds(r0, SUBLANES), :], b_s[pl.ds(r0, SUBLANES), :]
        hs = []
        for k in range(steps_per_tile):
            hcur = at[k * nbat:(k + 1) * nbat] * hcur + bt[k * nbat:(k + 1) * nbat]
            hs.append(hcur)
        htile = jnp.concatenate(hs, axis=0)
        for c in range(nblk):
            ext[c, pl.ds(pl.multiple_of(hist + r0, SUBLANES), SUBLANES), :] = htile[:, c * LANES:(c + 1) * LANES]
        return hcur

    h_end = lax.fori_loop(0, tl // steps_per_tile, tile_step, carry[...], unroll=2)
    carry[...] = h_end

    for b in range(nbat):
        for c in range(nblk):
            hs_ref[b, :, c * LANES:(c + 1) * LANES] = ext[c, pl.ds(hist + b, tl, stride=nbat), :].astype(BF16)

    @pl.when(t == nt - 1)
    def _():
        for b in range(nbat):
            nlru_ref[b] = h_end[b:b + 1, :]


def _lru_call(u3d, cw, cb, wg, ba, bx, lam, *, tl, side=()):
    bsz, seq, _ = u3d.shape
    w = cw.shape[1]
    kw = cw.shape[0]
    nblk = w // LANES
    assert SUBLANES % bsz == 0
    rows = bsz * tl
    hist = -(-(kw - 1) * bsz // SUBLANES) * SUBLANES
    side_in, side_out, side_shapes, side_bytes = _side_cast_specs(side, seq // tl)
    est = (2 * rows * w * 4 + 2 * rows * w * 2 + (hist + rows) * w * 4 + 2 * rows * w * 4
           + nblk * LANES * 2 * LANES * 2 * 2 + 2 * rows * 2 * LANES * 4 + side_bytes)
    const2 = lambda t: (0, 0)
    return pl.pallas_call(
        functools.partial(_lru_kernel, n_side=len(side)),
        grid=(seq // tl,),
        in_specs=[pl.BlockSpec((bsz, tl, w), lambda t: (0, t, 0)),
                  pl.BlockSpec((kw, w), const2),
                  pl.BlockSpec((1, w), const2),
                  pl.BlockSpec((nblk, LANES, 2 * LANES), lambda t: (0, 0, 0)),
                  pl.BlockSpec((1, w), const2),
                  pl.BlockSpec((1, w), const2),
                  pl.BlockSpec((1, w), const2)] + side_in,
        out_specs=[pl.BlockSpec((bsz, tl, w), lambda t: (0, t, 0)),
                   pl.BlockSpec((bsz, kw - 1, w), lambda t: (0, 0, 0)),
                   pl.BlockSpec((bsz, 1, w), lambda t: (0, 0, 0))] + side_out,
        out_shape=[jax.ShapeDtypeStruct((bsz, seq, w), BF16),
                   jax.ShapeDtypeStruct((bsz, kw - 1, w), F32),
                   jax.ShapeDtypeStruct((bsz, 1, w), F32)] + side_shapes,
        scratch_shapes=[pltpu.VMEM((nblk, hist + rows, LANES), F32),
                        pltpu.VMEM((rows, w), F32),
                        pltpu.VMEM((rows, w), F32),
                        pltpu.VMEM((bsz, w), F32)],
        compiler_params=pltpu.CompilerParams(dimension_semantics=("arbitrary",),
                                             vmem_limit_bytes=_vmem_limit(est)),
        name="lru_scan",
    )(u3d, cw, cb, wg, ba, bx, lam, *(a for a, _ in side))


def _lru_step_kernel(u_ref, buf_ref, h0_ref, cw_ref, cb_ref, wg_ref, ba_ref, bx_ref, lam_ref,
                     hs_ref, nconv_ref, nlru_ref):
    w = cw_ref.shape[1]
    kw = cw_ref.shape[0]
    nblk = w // LANES
    u = u_ref[...]
    uc = cb_ref[...] + buf_ref[0] * cw_ref[0:1, :]
    for k in range(1, kw - 1):
        uc = uc + buf_ref[k] * cw_ref[k:k + 1, :]
    uc = uc + u * cw_ref[kw - 1:kw, :]
    for k in range(kw - 2):
        nconv_ref[k] = buf_ref[k + 1]
    nconv_ref[kw - 2] = u
    sp = jax.nn.softplus(-lam_ref[...])
    for h in range(nblk):
        cols = slice(h * LANES, (h + 1) * LANES)
        uch = uc[:, cols]
        g = _dot(uch.astype(BF16), wg_ref[h])
        a, b = _lru_gate_block(uch, g, ba_ref[:, cols], bx_ref[:, cols], sp[:, cols])
        hn = a * h0_ref[:, cols] + b
        nlru_ref[:, cols] = hn
        hs_ref[:, cols] = hn.astype(BF16)


def _lru_step_call(u2d, buf, h0, cw, cb, wg, ba, bx, lam):
    n = h0.shape[0]
    w = cw.shape[1]
    kw = cw.shape[0]
    nblk = w // LANES
    c2 = lambda i: (0, 0)
    c3 = lambda i: (0, 0, 0)
    return pl.pallas_call(
        _lru_step_kernel,
        grid=(1,),
        in_specs=[pl.BlockSpec((n, w), c2),
                  pl.BlockSpec((kw - 1, n, w), c3),
                  pl.BlockSpec((n, w), c2),
                  pl.BlockSpec((kw, w), c2),
                  pl.BlockSpec((1, w), c2),
                  pl.BlockSpec((nblk, LANES, 2 * LANES), c3),
                  pl.BlockSpec((1, w), c2),
                  pl.BlockSpec((1, w), c2),
                  pl.BlockSpec((1, w), c2)],
        out_specs=[pl.BlockSpec((n, w), c2),
                   pl.BlockSpec((kw - 1, n, w), c3),
                   pl.BlockSpec((n, w), c2)],
        out_shape=[jax.ShapeDtypeStruct((n, w), BF16),
                   jax.ShapeDtypeStruct((kw - 1, n, w), F32),
                   jax.ShapeDtypeStruct((n, w), F32)],
        name="lru_step",
    )(u2d, buf, h0, cw, cb, wg, ba, bx, lam)


def _abar(ar, ai, dt):
    mag = jnp.exp(dt * ar)
    ang = dt * ai
    return mag * jnp.cos(ang), mag * jnp.sin(ang)


def _s5_prep_kernel(ag_ref, dtg_ref, bt_ref, ct_ref, al_ref, dtl_ref, bb_ref, cc_ref, pw_ref):
    ar, ai = ag_ref[0], ag_ref[1]
    abr, abi = _abar(ar, ai, jnp.exp(dtg_ref[...]))
    den = ar * ar + ai * ai
    nr, ni = abr - 1.0, abi
    n_k = bt_ref.shape[1] // ar.shape[0]
    per_channel = lambda q: jnp.concatenate(
        [jnp.broadcast_to(q[g:g + 1, :], (n_k, q.shape[1])) for g in range(q.shape[0])], axis=0)
    q_re = per_channel((nr * ar + ni * ai) / den)
    q_im = per_channel((ni * ar - nr * ai) / den)
    br, bi = bt_ref[0], bt_ref[1]
    for ref, pair in ((bb_ref, (q_re * br - q_im * bi, q_re * bi + q_im * br)), (cc_ref, (ct_ref[0], ct_ref[1]))):
        for part, v in enumerate(pair):
            ref[part] = jnp.concatenate([v, v], axis=1)

    pr, pi = _abar(al_ref[0], al_ref[1], jnp.exp(dtl_ref[...]))
    qr, qi = jnp.ones_like(pr), jnp.zeros_like(pr)
    for m in range(pw_ref.shape[1]):
        pw_ref[0, m:m + 1, :] = qr
        pw_ref[1, m:m + 1, :] = qi
        qr, qi = qr * pr - qi * pi, qr * pi + qi * pr


def _s5_prep_call(a_re, a_im, log_dt, b_re, b_im, c_re, c_im, n_pow):
    g, p, k = b_re.shape
    assert c_re.shape == (g, k, p) and 2 * p == LANES
    a = jnp.stack([a_re, a_im])
    return pl.pallas_call(
        _s5_prep_kernel,
        out_shape=[jax.ShapeDtypeStruct((2, g * k, 2 * p), F32), jax.ShapeDtypeStruct((2, g * k, 2 * p), F32),
                   jax.ShapeDtypeStruct((2, n_pow, g * p), F32)],
        name="s5_prep",
    )(a, log_dt.reshape(g, 1),
      jnp.stack([b_re, b_im]).swapaxes(2, 3).reshape(2, g * k, p), jnp.stack([c_re, c_im]).reshape(2, g * k, p),
      a.reshape(2, 1, g * p), jnp.repeat(log_dt, p).reshape(1, g * p))


def _block_diag(stack):
    tiled = jnp.concatenate([stack] * (STATES_PER_BLOCK // stack.shape[1]), axis=1)
    own = (lax.broadcasted_iota(jnp.int32, tiled.shape, 0) // SSM_GROUP
           == lax.broadcasted_iota(jnp.int32, tiled.shape, 1) // SSM_STATE)
    return jnp.where(own, tiled, 0.0)


def _cplx_mul(ar, ai, br, bi):
    return ar * br - ai * bi, ar * bi + ai * br


def _cplx_step(ar, ai, hr, hi, br, bi):
    return ar * hr - ai * hi + br, ar * hi + ai * hr + bi


def _s5_one_step(us_ref, h0r_ref, h0i_ref, bre, bim, ctre, ctim, ar, ai, d_ref, gys_ref, nres_ref, nims_ref):
    us = us_ref[...]
    ust = us.T.astype(BF16)
    ar = jnp.broadcast_to(ar, bre.shape).T
    ai = jnp.broadcast_to(ai, bre.shape).T
    hr, hi = _cplx_step(ar, ai, h0r_ref[...], h0i_ref[...],
                        _dot(bre.T.astype(BF16), ust), _dot(bim.T.astype(BF16), ust))
    nres_ref[...] = hr
    nims_ref[...] = hi
    yt = _dot(ctre.astype(BF16), hr.astype(BF16)) - _dot(ctim.astype(BF16), hi.astype(BF16))
    gys_ref[...] = jax.nn.gelu(yt.T + d_ref[...] * us).astype(BF16)


def _s5_kernel(u_ref, bb_ref, cc_ref, pw_ref, d_ref, c_ref, wa_ref, ba_ref, us_ref, h0r_ref, h0i_ref,
               gy_ref, nre_ref, nim_ref, ms_ref, mp_ref, gys_ref, nres_ref, nims_ref,
               ufb, buf, wbs, wqs, wts, yslab, out32):
    _mod_kernel(c_ref, wa_ref, ba_ref, ms_ref, mp_ref)
    nbat, seq, _ = u_ref.shape
    spb = STATES_PER_BLOCK
    fold = wbs.shape[0] // LANES
    nq = seq // fold

    bre, bim = _block_diag(bb_ref[0]), _block_diag(bb_ref[1])
    ctre, ctim = _block_diag(cc_ref[0]), _block_diag(cc_ref[1])
    _s5_one_step(us_ref, h0r_ref, h0i_ref, bre, bim, ctre, ctim, pw_ref[0, 1:2, :], pw_ref[1, 1:2, :], d_ref,
                 gys_ref, nres_ref, nims_ref)
    wcb = jnp.concatenate([ctre.T, -ctim.T], axis=0).astype(BF16)
    lag = []
    for m in range(fold):
        wre, wim = _cplx_mul(bre, bim, pw_ref[0, m:m + 1, :], pw_ref[1, m:m + 1, :])
        rows = slice((fold - 1 - m) * LANES, (fold - m) * LANES)
        wbs[rows, :spb] = wre.astype(BF16)
        wbs[rows, spb:] = wim.astype(BF16)
        lag.append(_dot(wbs[rows, :], wcb))
    zero_blk = jnp.zeros((LANES, LANES), BF16)
    for r_in in range(fold):
        for r_out in range(fold):
            blk = lag[r_out - r_in].astype(BF16) if r_in <= r_out else zero_blk
            wts[r_in * LANES:(r_in + 1) * LANES, r_out * LANES:(r_out + 1) * LANES] = blk
    for r in range(fold):
        pr, pi = pw_ref[0, r + 1:r + 2, :], pw_ref[1, r + 1:r + 2, :]
        cols = slice(r * LANES, (r + 1) * LANES)
        wqs[:spb, cols] = (ctre * pr - ctim * pi).T.astype(BF16)
        wqs[spb:, cols] = (-(ctre * pi + ctim * pr)).T.astype(BF16)

    for r in range(fold):
        for b in range(nbat):
            yslab[r, pl.ds(b, nq, stride=nbat), :] = u_ref[b, pl.ds(r, nq, stride=fold), :]
        ufb[:, r * LANES:(r + 1) * LANES] = yslab[r].astype(BF16)

    buf[...] = _dot(ufb[...], wbs[...])

    ar, ai = pw_ref[0, fold:fold + 1, :], pw_ref[1, fold:fold + 1, :]
    groups_per_tile = SUBLANES // nbat
    z = jnp.zeros((nbat, spb), F32)

    def tile_step(i, g):
        gr, gi = g
        r0 = pl.multiple_of(i * SUBLANES, SUBLANES)
        xr, xi = buf[pl.ds(r0, SUBLANES), :spb], buf[pl.ds(r0, SUBLANES), spb:]
        hr, hi = [], []
        for k in range(groups_per_tile):
            hr.append(gr)
            hi.append(gi)
            gr, gi = _cplx_step(ar, ai, gr, gi, xr[k * nbat:(k + 1) * nbat], xi[k * nbat:(k + 1) * nbat])
        buf[pl.ds(r0, SUBLANES), :spb] = jnp.concatenate(hr, axis=0)
        buf[pl.ds(r0, SUBLANES), spb:] = jnp.concatenate(hi, axis=0)
        return gr, gi

    g_re, g_im = lax.fori_loop(0, nq // groups_per_tile, tile_step, (z, z), unroll=2)
    for b in range(nbat):
        nre_ref[b] = g_re[b:b + 1, :]
        nim_ref[b] = g_im[b:b + 1, :]

    hsb = buf[...].astype(BF16)
    per_tile = 2
    for n in range(fold // per_tile):
        cols = slice(n * per_tile * LANES, (n + 1) * per_tile * LANES)
        kk = (n + 1) * per_tile * LANES
        yn = _dot(hsb, wqs[:, cols]) + _dot(ufb[:, :kk], wts[:kk, cols])
        for i in range(per_tile):
            yslab[n * per_tile + i] = yn[:, i * LANES:(i + 1) * LANES]

    for b in range(nbat):
        for r in range(fold):
            ys = (yslab[r, pl.ds(b, nq, stride=nbat), :]
                  + d_ref[...] * u_ref[b, pl.ds(r, nq, stride=fold), :])
            out32[b, pl.ds(r, nq, stride=fold), :] = jax.nn.gelu(ys)
        gy_ref[b] = out32[b].astype(BF16)


def _s5_call(u3d, bb, cc, pw, d, c_all, n_sample, w_ada, b_ada, first_col, u_step, h0r, h0i, *, w):
    bsz, seq, _ = u3d.shape
    nblk = w // LANES
    spb = STATES_PER_BLOCK
    fold = S5_FOLD
    assert SUBLANES % bsz == 0 and seq % fold == 0 and pw.shape[1] == fold + 1
    n_state, n_step = h0r.shape
    assert n_state == nblk * spb and n_step == LANES
    state_spec = pl.BlockSpec((spb, n_step), lambda c: (c, 0))
    state_shape = jax.ShapeDtypeStruct((n_state, n_step), F32)
    rows = bsz * (seq // fold)
    n_c, dm = c_all.shape
    mod_cols = w_ada.shape[1] - first_col
    tn = mod_cols // nblk
    assert mod_cols % nblk == 0 and tn % LANES == 0 and first_col % tn == 0
    est = (2 * bsz * seq * LANES * 4 + 2 * bsz * seq * LANES * 2
           + 2 * dm * tn * 4 + dm * tn * 2 + 2 * n_c * dm * 4 + 3 * n_c * tn * 4
           + rows * fold * LANES * 2 + 2 * rows * 2 * spb * 4
           + 3 * fold * LANES * 2 * spb * 2
           + rows * fold * LANES * 4 + bsz * seq * LANES * 4
           + rows * 2 * spb * 2
           + 2 * (LANES * 2 * spb + 2 * spb * LANES) * 4)
    return pl.pallas_call(
        _s5_kernel,
        grid=(nblk,),
        in_specs=[pl.BlockSpec((bsz, seq, LANES), lambda c: (0, 0, nblk + c)),
                  pl.BlockSpec((2, LANES, LANES), lambda c: (0, c, 0)),
                  pl.BlockSpec((2, LANES, LANES), lambda c: (0, c, 0)),
                  pl.BlockSpec((2, fold + 1, spb), lambda c: (0, 0, c)),
                  pl.BlockSpec((1, LANES), lambda c: (0, c)),
                  pl.BlockSpec((n_c, dm), lambda c: (0, 0)),
                  pl.BlockSpec((dm, tn), lambda c: (0, first_col // tn + c)),
                  pl.BlockSpec((1, tn), lambda c: (0, first_col // tn + c)),
                  pl.BlockSpec((n_step, LANES), lambda c: (0, nblk + c)),
                  state_spec,
                  state_spec],
        out_specs=[pl.BlockSpec((bsz, seq, LANES), lambda c: (0, 0, c)),
                   pl.BlockSpec((bsz, 1, spb), lambda c: (0, 0, c)),
                   pl.BlockSpec((bsz, 1, spb), lambda c: (0, 0, c)),
                   pl.BlockSpec((n_sample, tn), lambda c: (0, c)),
                   pl.BlockSpec((n_c - n_sample, tn), lambda c: (0, c)),
                   pl.BlockSpec((n_step, LANES), lambda c: (0, c)),
                   state_spec,
                   state_spec],
        out_shape=[jax.ShapeDtypeStruct((bsz, seq, w), BF16),
                   jax.ShapeDtypeStruct((bsz, 1, nblk * spb), F32),
                   jax.ShapeDtypeStruct((bsz, 1, nblk * spb), F32),
                   jax.ShapeDtypeStruct((n_sample, mod_cols), F32),
                   jax.ShapeDtypeStruct((n_c - n_sample, mod_cols), F32),
                   jax.ShapeDtypeStruct((n_step, w), BF16),
                   state_shape,
                   state_shape],
        scratch_shapes=[pltpu.VMEM((rows, fold * LANES), BF16),
                        pltpu.VMEM((rows, 2 * spb), F32),
                        pltpu.VMEM((fold * LANES, 2 * spb), BF16),
                        pltpu.VMEM((2 * spb, fold * LANES), BF16),
                        pltpu.VMEM((fold * LANES, fold * LANES), BF16),
                        pltpu.VMEM((fold, rows, LANES), F32),
                        pltpu.VMEM((bsz, seq, LANES), F32)],
        compiler_params=pltpu.CompilerParams(dimension_semantics=("parallel",),
                                             vmem_limit_bytes=_vmem_limit(est)),
        name="s5_scan",
    )(u3d, bb, cc, pw, d, c_all, w_ada, b_ada, u_step, h0r, h0i)


def _merge_kernel(*refs, n_side):
    hs_ref, gy_ref, z_ref, x_ref, gt_ref, wp_ref, wg_ref, wo_ref = refs[:8]
    o_ref = refs[8 + n_side]
    _side_cast(refs[8:8 + n_side], refs[9 + n_side:])
    d = x_ref.shape[1]
    ya = _dot(hs_ref[...], wp_ref[...])
    merged = _sigmoid(z_ref[:, :d].astype(F32)) * ya
    glu = _dot(gy_ref[...], wg_ref[...])
    yb = glu[:, :d] * _sigmoid(glu[:, d:])
    merged = merged + _sigmoid(z_ref[:, d:].astype(F32)) * yb
    o = _dot(merged.astype(BF16), wo_ref[...])
    o_ref[...] = x_ref[...] + _rows(gt_ref) * o


def _merge_call(hs, gy, sz, x2d, mod, wp, wg, wo, *, tm, tiles_per_batch, side=()):
    m, d = x2d.shape
    w = hs.shape[1]
    side_in, side_out, side_shapes, side_bytes = _side_cast_specs(side, m // tm)
    est = (2 * (2 * tm * w * 2 + tm * 2 * d * 2 + 2 * tm * d * 4) + (wp.size + wg.size + wo.size) * 2
           + tm * d * 4 * 5 + side_bytes)
    one = pl.Buffered(1)
    return pl.pallas_call(
        functools.partial(_merge_kernel, n_side=len(side)),
        grid=(m // tm,),
        in_specs=[pl.BlockSpec((tm, w), lambda i: (i, 0)),
                  pl.BlockSpec((tm, w), lambda i: (i, 0)),
                  pl.BlockSpec((tm, 2 * d), lambda i: (i, 0)),
                  pl.BlockSpec((tm, d), lambda i: (i, 0)),
                  _mod_spec(mod, 2, d, tiles_per_batch, 1),
                  pl.BlockSpec(wp.shape, lambda i: (0, 0), pipeline_mode=one),
                  pl.BlockSpec(wg.shape, lambda i: (0, 0), pipeline_mode=one),
                  pl.BlockSpec(wo.shape, lambda i: (0, 0), pipeline_mode=one)] + side_in,
        out_specs=[pl.BlockSpec((tm, d), lambda i: (i, 0))] + side_out,
        out_shape=[jax.ShapeDtypeStruct((m, d), F32)] + side_shapes,
        compiler_params=pltpu.CompilerParams(dimension_semantics=("parallel",),
                                             vmem_limit_bytes=_vmem_limit(est)),
        name="merge",
    )(hs, gy, sz, x2d, mod, wp, wg, wo, *(a for a, _ in side))


def _mlp_kernel(x_hbm, sc_ref, sh_ref, gt_ref, g2_ref, gf_ref, wu_ref, wd_ref, o_ref,
                x_buf, x_sem, h_scr, inv_scr, row_scr, *, final_norm):
    n_rows, d = x_buf.shape
    _x_tile_prologue(x_hbm, x_buf, x_sem,
                     lambda: _norm_mod_bf16(x_buf, g2_ref, sc_ref, sh_ref, h_scr, inv_scr, row_scr, x_copy_ref=o_ref))

    up = _dot(h_scr[...], wu_ref[...])
    act = jnp.square(jnp.maximum(up, 0.0)).astype(BF16)
    nc = min(d, MLP_OUT_CHUNK)
    for c in range(d // nc):
        cols = slice(c * nc, (c + 1) * nc)
        gate = gt_ref[0, :, cols] if len(gt_ref.shape) == 3 else gt_ref[:, cols]
        o_ref[:, cols] += gate * _dot(act, wd_ref[:, cols])

    if final_norm:
        @pl.when(pl.program_id(1) == pl.num_programs(1) - 1)
        def _():
            def chunk(rows):
                x2 = o_ref[rows, :]
                ms = jnp.mean(x2 * x2, axis=-1, keepdims=True)
                o_ref[rows, :] = (x2 * lax.rsqrt(ms + EPS)) * gf_ref[...]

            _row_groups(n_rows, min(n_rows, EPILOGUE_ROWS), chunk, unroll=1)


def _mlp_call(x2d, mod, g2, gf, w_up, w_down, *, tm, tiles_per_batch, final_norm, tf):
    m, d = x2d.shape
    dff = w_up.shape[1]
    assert w_up.dtype == BF16 and w_down.dtype == BF16
    est = 3 * tm * d * 4 + tm * d * 2 + 4 * d * tf * 2 + tm * tf * 6 + tm * MLP_OUT_CHUNK * 4
    return pl.pallas_call(
        functools.partial(_mlp_kernel, final_norm=final_norm),
        grid=(m // tm, dff // tf),
        in_specs=[pl.BlockSpec(memory_space=pl.ANY),
                  _mod_spec(mod, 4 - MLP_MOD_FIRST, d, tiles_per_batch, 2),
                  _mod_spec(mod, 3 - MLP_MOD_FIRST, d, tiles_per_batch, 2),
                  _mod_spec(mod, 5 - MLP_MOD_FIRST, d, tiles_per_batch, 2),
                  pl.BlockSpec((1, d), lambda i, j: (0, 0)),
                  pl.BlockSpec((1, d), lambda i, j: (0, 0)),
                  pl.BlockSpec((d, tf), lambda i, j: (0, j)),
                  pl.BlockSpec((tf, d), lambda i, j: (j, 0))],
        out_specs=pl.BlockSpec((tm, d), lambda i, j: (i, 0)),
        out_shape=jax.ShapeDtypeStruct((m, d), F32),
        scratch_shapes=[pltpu.VMEM((tm, d), F32), pltpu.SemaphoreType.DMA(()),
                        pltpu.VMEM((tm, d), BF16), pltpu.VMEM((tm, LANES), F32),
                        pltpu.VMEM((2 * SUBLANES, d), F32)],
        compiler_params=pltpu.CompilerParams(dimension_semantics=("arbitrary", "arbitrary"),
                                             vmem_limit_bytes=_vmem_limit(est)),
        name="mlp",
    )(x2d, mod, mod, mod, g2, gf, w_up, w_down)


def _pick_tile(n, pref):
    t = min(n, pref)
    assert n % t == 0, (n, t)
    return t


def kernel(x_prompt, x_sample, state_conv, state_lru, state_ssm_re, state_ssm_im, c_prompt, c_sample, w_ada, b_ada, g_norm1, g_norm2, w_in, conv_w, conv_b, w_rg_a, b_rg_a, w_rg_x, b_rg_x, lru_lambda, w_proj_a, ssm_a_re, ssm_a_im, ssm_log_dt, ssm_b_re, ssm_b_im, ssm_c_re, ssm_c_im, ssm_d, w_glu, w_out, w_up, w_down, g_final):
    depth = w_ada.shape[0]
    nb, seq, d = x_prompt.shape
    ns = x_sample.shape[0]
    assert x_sample.shape[1] == 1
    w = conv_w.shape[2]
    n_state = ssm_a_re.shape[1] * ssm_a_re.shape[2]
    assert ssm_b_re.shape[2:] == (SSM_STATE, SSM_GROUP) and w % LANES == 0
    assert w_in.shape[2] == 2 * w + 2 * d

    tl = _pick_tile(seq, 256)
    tm_in = _pick_tile(seq, 1024)
    tm_mg = _pick_tile(seq, 256)
    tm_mlp = _pick_tile(seq, 1024)

    xp = x_prompt.reshape(nb * seq, d)
    xs = x_sample.reshape(ns, d)
    assert ns % SUBLANES == 0
    pad = (-nb) % SUBLANES
    c_all = jnp.concatenate([c_sample, c_prompt, jnp.zeros((pad, d), F32)], axis=0)

    outs_p = [[] for _ in range(4)]
    outs_s = [[] for _ in range(4)]
    for l in range(depth):
        last = l == depth - 1
        b_ada_row = b_ada[l].reshape(1, -1)
        mod_s, mod_p = _mod_call(c_all, ns, w_ada[l], b_ada_row, MLP_MOD_FIRST * d)
        mod_p = mod_p.reshape(nb + pad, 1, MLP_MOD_FIRST * d)

        g1 = g_norm1[l].reshape(1, d)
        g2 = g_norm2[l].reshape(1, d)
        gf = g_final.reshape(1, d)
        cw, cb = conv_w[l], conv_b[l].reshape(1, w)
        wg = jnp.concatenate([w_rg_a[l], w_rg_x[l]], axis=2).astype(BF16)
        ba, bx, lam = b_rg_a[l].reshape(1, w), b_rg_x[l].reshape(1, w), lru_lambda[l].reshape(1, w)
        bb, cc, pw = _s5_prep_call(ssm_a_re[l], ssm_a_im[l], ssm_log_dt[l], ssm_b_re[l], ssm_b_im[l],
                                       ssm_c_re[l], ssm_c_im[l], S5_FOLD + 1)
        dskip = ssm_d[l].reshape(1, w)

        u_s, sz_s, w_in_b = _inproj_call(xs, mod_s, g1, w_in[l], tm=ns, tiles_per_batch=1, w_mix=w)

        u_p, sz_p = _inproj_call(xp, mod_p, g1, w_in_b, tm=tm_in, tiles_per_batch=seq // tm_in, w_mix=w)
        u3 = u_p.reshape(nb, seq, 2 * w)
        hs_p, nconv_p, nlru_p, wp, wgl, wo = _lru_call(u3, cw, cb, wg, ba, bx, lam, tl=tl,
                                                       side=((w_proj_a[l], 1), (w_glu[l], 1), (w_out[l], 1)))
        by_token = lambda s: jnp.transpose(s, (1, 2, 0)).reshape(n_state, ns)
        gy_p, nre_p, nim_p, mlp_mod_s, mlp_mod_p, gy_s, nre_s, nim_s = _s5_call(
            u3, bb, cc, pw, dskip, c_all, ns, w_ada[l], b_ada_row, MLP_MOD_FIRST * d,
            u_s, by_token(state_ssm_re[l]), by_token(state_ssm_im[l]), w=w)
        mlp_mod_p = mlp_mod_p.reshape(nb + pad, 1, (N_MOD - MLP_MOD_FIRST) * d)
        x1_p, w_up_b, w_down_b = _merge_call(hs_p.reshape(nb * seq, w), gy_p.reshape(nb * seq, w), sz_p, xp, mod_p,
                                             wp, wgl, wo, tm=tm_mg, tiles_per_batch=seq // tm_mg,
                                             side=((w_up[l], 1), (w_down[l], 0)))
        xp = _mlp_call(x1_p, mlp_mod_p, g2, gf, w_up_b, w_down_b, tm=tm_mlp, tiles_per_batch=seq // tm_mlp,
                       final_norm=last, tf=1024)

        buf = jnp.transpose(state_conv[l], (1, 0, 2))
        hs_s, nconv_s, nlru_s = _lru_step_call(u_s, buf, state_lru[l], cw, cb, wg, ba, bx, lam)
        (x1_s,) = _merge_call(hs_s, gy_s, sz_s, xs, mod_s, wp, wgl, wo, tm=ns, tiles_per_batch=1)
        xs = _mlp_call(x1_s, mlp_mod_s, g2, gf, w_up_b, w_down_b, tm=ns, tiles_per_batch=1, final_norm=last,
                       tf=1024)

        gshape = ssm_a_re.shape[1:]
        for acc, v in zip(outs_p, (nconv_p, nlru_p.reshape(nb, w),
                                   nre_p.reshape((nb,) + gshape), nim_p.reshape((nb,) + gshape))):
            acc.append(v)
        by_group = lambda s: jnp.transpose(s.reshape(gshape + (ns,)), (2, 0, 1))
        for acc, v in zip(outs_s, (jnp.transpose(nconv_s, (1, 0, 2)), nlru_s, by_group(nre_s), by_group(nim_s))):
            acc.append(v)

    y_prompt = xp.reshape(nb, seq, d)
    y_sample = xs.reshape(ns, 1, d)
    stack = lambda a: a[0][None] if len(a) == 1 else jnp.stack(a)
    return (y_prompt, y_sample) + tuple(stack(a) for a in outs_p) + tuple(stack(a) for a in outs_s)
```
